```python
import math
import jax, jax.numpy as jnp
from jax import lax
import numpy as np

D_MODEL = 1024
BATCH = 8
SEQ = 4096
DEPTH = 2

F32 = jnp.float32
EPS = 1e-6
Q_BLOCK = 128
ROPE_THETA = 500000.0
ROT_FRACTION = 4

SSD_HEADS = 4
SSD_HEAD_DIM = 64
SSD_D_INNER = SSD_HEADS * SSD_HEAD_DIM
SSD_GROUPS = 2
SSD_STATE = 128
SSD_CONV = 4
SSD_CONV_DIM = SSD_D_INNER + 2 * SSD_GROUPS * SSD_STATE
SSD_CHUNK = 128

DSA_HEADS = 4
DSA_HEAD_DIM = 64
IDX_HEADS = 8
IDX_DIM = 64
IDX_TOPK_MAX = 256

MLA_HEADS = 4
MLA_Q_RANK = 256
MLA_KV_RANK = 128
MLA_NOPE = 64
MLA_ROPE = 32
MLA_V = 64

FOX_HEADS = 4
FOX_HEAD_DIM = 64
FOX_FORGET_BIAS_INIT = 2.0

MIX_WIDTH = SSD_D_INNER + DSA_HEADS * DSA_HEAD_DIM + MLA_HEADS * MLA_V + FOX_HEADS * FOX_HEAD_DIM

IN_SPLITS = (SSD_D_INNER, SSD_CONV_DIM, SSD_HEADS,
             DSA_HEADS * DSA_HEAD_DIM, DSA_HEAD_DIM, DSA_HEAD_DIM, IDX_HEADS * IDX_DIM, IDX_DIM, IDX_HEADS,
             MLA_Q_RANK, MLA_KV_RANK, MLA_ROPE,
             FOX_HEADS * FOX_HEAD_DIM, FOX_HEADS * FOX_HEAD_DIM, FOX_HEADS * FOX_HEAD_DIM, FOX_HEADS)
IN_COLS = sum(IN_SPLITS)

MOE_GROUPS = 4
MOE_EXPERTS_PER_GROUP = 8
MOE_EXPERTS = MOE_GROUPS * MOE_EXPERTS_PER_GROUP
MOE_TOPK = 2
MOE_FF = 256
MOE_BLOCK = 128

kernel_name = 'hybrid_ssd_dsa_mla_fox_hmoe'


def rmsnorm(x, g):
    xf = x.astype(F32)
    y = xf * lax.rsqrt(jnp.mean(xf * xf, axis=-1, keepdims=True) + EPS) * g.astype(F32)
    return y.astype(x.dtype)


def rope(x, positions, rot_dim):
    half = rot_dim // 2
    inv = ROPE_THETA ** (-jnp.arange(half, dtype=F32) / half)
    ang = positions.astype(F32)[..., None] * inv
    cos = jnp.cos(ang)[:, :, None, :]
    sin = jnp.sin(ang)[:, :, None, :]
    xr = x[..., :rot_dim].astype(F32)
    x1, x2 = xr[..., :half], xr[..., half:]
    rot = jnp.concatenate([x1 * cos - x2 * sin, x2 * cos + x1 * sin], axis=-1).astype(x.dtype)
    return jnp.concatenate([rot, x[..., rot_dim:]], axis=-1)


def split_points():
    pts, acc = [], 0
    for w in IN_SPLITS[:-1]:
        acc += w
        pts.append(acc)
    return pts


def ssd_scan(xh, dt, a, bh, ch):
    b, s, h, p = xh.shape
    n = bh.shape[-1]
    c = s // SSD_CHUNK
    xh = xh.astype(F32).reshape(b, c, SSD_CHUNK, h, p)
    bh = bh.astype(F32).reshape(b, c, SSD_CHUNK, h, n)
    ch = ch.astype(F32).reshape(b, c, SSD_CHUNK, h, n)
    dt = dt.reshape(b, c, SSD_CHUNK, h)
    a_cs = jnp.cumsum(jnp.moveaxis(dt * a, -1, 1), axis=-1)
    tri = jnp.tril(jnp.ones((SSD_CHUNK, SSD_CHUNK), dtype=bool))
    seg = jnp.exp(jnp.where(tri, a_cs[..., :, None] - a_cs[..., None, :], -jnp.inf))
    xdt = xh * dt[..., None]
    y_diag = jnp.einsum('bcqhn,bckhn,bhcqk,bckhp->bcqhp', ch, bh, seg, xdt)
    decay_to_end = jnp.exp(a_cs[..., -1:] - a_cs)
    chunk_states = jnp.einsum('bckhn,bhck,bckhp->cbhpn', bh, decay_to_end, xdt)
    chunk_decay = jnp.moveaxis(jnp.exp(a_cs[..., -1]), -1, 0)

    def step(state, inp):
        st, dec = inp
        return state * dec[..., None, None] + st, state

    _, prev = lax.scan(step, jnp.zeros((b, h, p, n), F32), (chunk_states, chunk_decay))
    y_off = jnp.einsum('bcqhn,cbhpn,bhcq->bcqhp', ch, prev, jnp.exp(a_cs))
    return (y_diag + y_off).reshape(b, s, h, p)


def ssd_mixer(z, xbc, dt_raw, conv_w, conv_b, dt_bias, a_log, d_skip, norm_w):
    b, s, _ = z.shape
    xbc = lax.conv_general_dilated(xbc, conv_w[:, None, :].astype(xbc.dtype), window_strides=(1,),
                                   padding=[(SSD_CONV - 1, 0)], dimension_numbers=('NWC', 'WIO', 'NWC'),
                                   feature_group_count=SSD_CONV_DIM) + conv_b
    xbc = jax.nn.silu(xbc)
    xs = xbc[..., :SSD_D_INNER]
    bm = xbc[..., SSD_D_INNER:SSD_D_INNER + SSD_GROUPS * SSD_STATE]
    cm = xbc[..., SSD_D_INNER + SSD_GROUPS * SSD_STATE:]
    rep = SSD_HEADS // SSD_GROUPS
    xh = xs.reshape(b, s, SSD_HEADS, SSD_HEAD_DIM)
    bh = jnp.repeat(bm.reshape(b, s, SSD_GROUPS, SSD_STATE), rep, axis=2)
    ch = jnp.repeat(cm.reshape(b, s, SSD_GROUPS, SSD_STATE), rep, axis=2)
    dt = jax.nn.softplus(dt_raw.astype(F32) + dt_bias.astype(F32))
    a = -jnp.exp(a_log.astype(F32))
    y = ssd_scan(xh, dt, a, bh, ch) + d_skip.astype(F32)[:, None] * xh.astype(F32)
    y = y.reshape(b, s, SSD_D_INNER)
    return rmsnorm(y * jax.nn.silu(z.astype(F32)), norm_w)


def blocked_causal_attention(q, k, v, fcum=None):
    b, s, h, dq = q.shape
    dv = v.shape[-1]
    scale = dq ** -0.5
    k_pos = jnp.arange(s)
    f_t = None if fcum is None else jnp.moveaxis(fcum, 1, 2)

    def one(i):
        start = i * Q_BLOCK
        q_blk = lax.dynamic_slice_in_dim(q, start, Q_BLOCK, axis=1)
        logits = jnp.einsum('bqhd,bkhd->bhqk', q_blk, k).astype(F32) * scale
        if f_t is not None:
            f_q = lax.dynamic_slice_in_dim(f_t, start, Q_BLOCK, axis=2)
            logits = logits + f_q[..., :, None] - f_t[:, :, None, :]
        q_pos = start + jnp.arange(Q_BLOCK)
        mask = k_pos[None, :] <= q_pos[:, None]
        p = jax.nn.softmax(jnp.where(mask, logits, -jnp.inf), axis=-1)
        return jnp.einsum('bhqk,bkhd->bqhd', p.astype(v.dtype), v)

    out = lax.map(one, jnp.arange(s // Q_BLOCK))
    return jnp.moveaxis(out, 0, 1).reshape(b, s, h * dv)


def dsa_mixer(q, k, v, q_idx, k_idx, w_idx, positions, n_sel):
    b, s, _ = q.shape
    q = rope(q.reshape(b, s, DSA_HEADS, DSA_HEAD_DIM), positions, DSA_HEAD_DIM // ROT_FRACTION)
    k = rope(k[:, :, None, :], positions, DSA_HEAD_DIM // ROT_FRACTION)[:, :, 0]
    q_idx = rope(q_idx.reshape(b, s, IDX_HEADS, IDX_DIM), positions, IDX_DIM // ROT_FRACTION)
    k_idx = rope(k_idx[:, :, None, :], positions, IDX_DIM // ROT_FRACTION)[:, :, 0]
    w_idx = w_idx.astype(F32) * (IDX_HEADS * IDX_DIM) ** -0.5
    scale = DSA_HEAD_DIM ** -0.5
    k_pos = jnp.arange(s)
    bidx = jnp.arange(b)[:, None, None]

    def one(i):
        start = i * Q_BLOCK
        q_blk = lax.dynamic_slice_in_dim(q, start, Q_BLOCK, axis=1)
        qi_blk = lax.dynamic_slice_in_dim(q_idx, start, Q_BLOCK, axis=1)
        wi_blk = lax.dynamic_slice_in_dim(w_idx, start, Q_BLOCK, axis=1)
        q_pos = start + jnp.arange(Q_BLOCK)
        rel = jax.nn.relu(jnp.einsum('bqjd,bkd->bqjk', qi_blk, k_idx).astype(F32))
        score = jnp.einsum('bqjk,bqj->bqk', rel, wi_blk)
        score = jnp.where((k_pos[None, :] <= q_pos[:, None])[None], score, -jnp.inf)
        _, sel = lax.top_k(score, n_sel)
        k_sel = k[bidx, sel]
        v_sel = v[bidx, sel]
        logits = jnp.einsum('bqhd,bqnd->bhqn', q_blk, k_sel).astype(F32) * scale
        valid = (sel <= q_pos[None, :, None])[:, None]
        p = jax.nn.softmax(jnp.where(valid, logits, -jnp.inf), axis=-1)
        return jnp.einsum('bhqn,bqnd->bqhd', p.astype(v.dtype), v_sel)

    out = lax.map(one, jnp.arange(s // Q_BLOCK))
    return jnp.moveaxis(out, 0, 1).reshape(b, s, DSA_HEADS * DSA_HEAD_DIM)


def mla_mixer(c_q, c_kv, k_rope, positions, q_norm, w_uq, kv_norm, w_ukv):
    b, s, _ = c_q.shape
    qf = (rmsnorm(c_q, q_norm) @ w_uq).reshape(b, s, MLA_HEADS, MLA_NOPE + MLA_ROPE)
    kvf = (rmsnorm(c_kv, kv_norm) @ w_ukv).reshape(b, s, MLA_HEADS, MLA_NOPE + MLA_V)
    q_pe = rope(qf[..., MLA_NOPE:], positions, MLA_ROPE)
    k_pe = rope(k_rope[:, :, None, :], positions, MLA_ROPE)
    q = jnp.concatenate([qf[..., :MLA_NOPE], q_pe], axis=-1)
    k = jnp.concatenate([kvf[..., :MLA_NOPE], jnp.broadcast_to(k_pe, (b, s, MLA_HEADS, MLA_ROPE))], axis=-1)
    v = kvf[..., MLA_NOPE:]
    return blocked_causal_attention(q, k, v)


def fox_mixer(q, k, v, f_raw, f_bias):
    b, s, _ = q.shape
    shp = (b, s, FOX_HEADS, FOX_HEAD_DIM)
    log_f = jax.nn.log_sigmoid(f_raw.astype(F32) + f_bias.astype(F32))
    fcum = jnp.cumsum(log_f, axis=1)
    return blocked_causal_attention(q.reshape(shp), k.reshape(shp), v.reshape(shp), fcum)


def hier_moe(h, w_rg, b_rg, w_re, b_re, w_gate, w_up, w_down):
    t, d = h.shape
    g_prob = jax.nn.softmax((h @ w_rg + b_rg).astype(F32), axis=-1)
    g_p, g_idx = lax.top_k(g_prob, 1)
    e_logits = (h @ w_re + b_re).astype(F32).reshape(t, MOE_GROUPS, MOE_EXPERTS_PER_GROUP)
    e_in = jnp.take_along_axis(e_logits, g_idx[:, :, None], axis=1)[:, 0]
    e_top, e_loc = lax.top_k(e_in, MOE_TOPK)
    e_w = jax.nn.softmax(e_top, axis=-1) * g_p
    e_id = g_idx * MOE_EXPERTS_PER_GROUP + e_loc
    n_assign = t * MOE_TOPK
    flat_e = e_id.reshape(-1)
    flat_tok = jnp.repeat(jnp.arange(t, dtype=jnp.int32), MOE_TOPK)
    flat_w = e_w.reshape(-1)
    order = jnp.argsort(flat_e)
    se, stok, sw = flat_e[order], flat_tok[order], flat_w[order]
    counts = jnp.bincount(flat_e, length=MOE_EXPERTS)
    starts = jnp.cumsum(counts) - counts
    pcounts = ((counts + MOE_BLOCK - 1) // MOE_BLOCK) * MOE_BLOCK
    pends = jnp.cumsum(pcounts)
    pstarts = pends - pcounts
    dest = pstarts[se] + (jnp.arange(n_assign) - starts[se])
    n_slots = n_assign + MOE_EXPERTS * MOE_BLOCK
    n_slots = ((n_slots + MOE_BLOCK - 1) // MOE_BLOCK) * MOE_BLOCK
    slot_tok = jnp.zeros((n_slots,), jnp.int32).at[dest].set(stok)
    slot_w = jnp.zeros((n_slots,), F32).at[dest].set(sw)
    n_blocks = n_slots // MOE_BLOCK
    block_e = jnp.minimum(jnp.searchsorted(pends, jnp.arange(n_blocks) * MOE_BLOCK, side='right'), MOE_EXPERTS - 1)
    xs = h[slot_tok].reshape(n_blocks, MOE_BLOCK, d)

    def expert_block(args):
        xb, e = args
        return (jax.nn.silu(xb @ w_gate[e]) * (xb @ w_up[e])) @ w_down[e]

    ys = lax.map(expert_block, (xs, block_e)).reshape(n_slots, d)
    return jax.ops.segment_sum(ys * slot_w[:, None].astype(ys.dtype), slot_tok, num_segments=t)


def setup_inputs(seed: int = 0) -> dict:
    key = jax.random.key(seed)
    ks = jax.random.split(key, 28)
    L, D = DEPTH, D_MODEL

    def nrm(k, shape, scale):
        return jax.random.normal(k, shape, F32) * scale

    x = jax.random.normal(ks[0], (BATCH, SEQ, D), F32)
    positions = jnp.broadcast_to(jnp.arange(SEQ, dtype=jnp.int32), (BATCH, SEQ))
    norm_mix = 1.0 + nrm(ks[1], (L, D), 0.02)
    w_in = nrm(ks[2], (L, D, IN_COLS), D ** -0.5)
    ssd_conv_w = nrm(ks[3], (L, SSD_CONV, SSD_CONV_DIM), SSD_CONV ** -0.5)
    ssd_conv_b = nrm(ks[4], (L, SSD_CONV_DIM), 0.02)
    dt0 = jnp.exp(jax.random.uniform(ks[5], (L, SSD_HEADS), F32, math.log(1e-3), math.log(1e-1)))
    ssd_dt_bias = dt0 + jnp.log(-jnp.expm1(-dt0))
    ssd_a_log = jnp.log(jax.random.uniform(ks[6], (L, SSD_HEADS), F32, 1.0, 16.0))
    ssd_d = 1.0 + nrm(ks[7], (L, SSD_HEADS), 0.1)
    ssd_norm = 1.0 + nrm(ks[8], (L, SSD_D_INNER), 0.02)
    mla_q_norm = 1.0 + nrm(ks[9], (L, MLA_Q_RANK), 0.02)
    mla_w_uq = nrm(ks[10], (L, MLA_Q_RANK, MLA_HEADS * (MLA_NOPE + MLA_ROPE)), MLA_Q_RANK ** -0.5)
    mla_kv_norm = 1.0 + nrm(ks[11], (L, MLA_KV_RANK), 0.02)
    mla_w_ukv = nrm(ks[12], (L, MLA_KV_RANK, MLA_HEADS * (MLA_NOPE + MLA_V)), MLA_KV_RANK ** -0.5)
    fox_f_bias = FOX_FORGET_BIAS_INIT + nrm(ks[13], (L, FOX_HEADS), 0.1)
    w_out = nrm(ks[14], (L, MIX_WIDTH, D), 0.5 * MIX_WIDTH ** -0.5)
    norm_ffn = 1.0 + nrm(ks[15], (L, D), 0.02)
    router_group_w = nrm(ks[16], (L, D, MOE_GROUPS), D ** -0.5)
    router_group_b = nrm(ks[17], (L, MOE_GROUPS), 0.01)
    router_expert_w = nrm(ks[18], (L, D, MOE_EXPERTS), D ** -0.5)
    router_expert_b = nrm(ks[19], (L, MOE_EXPERTS), 0.01)
    expert_w_gate = nrm(ks[20], (L, MOE_EXPERTS, D, MOE_FF), D ** -0.5)
    expert_w_up = nrm(ks[21], (L, MOE_EXPERTS, D, MOE_FF), D ** -0.5)
    expert_w_down = nrm(ks[22], (L, MOE_EXPERTS, MOE_FF, D), MOE_FF ** -0.5)
    final_norm = 1.0 + nrm(ks[23], (D,), 0.02)
    return {'x': x, 'positions': positions, 'norm_mix': norm_mix, 'w_in': w_in,
            'ssd_conv_w': ssd_conv_w, 'ssd_conv_b': ssd_conv_b, 'ssd_dt_bias': ssd_dt_bias,
            'ssd_a_log': ssd_a_log, 'ssd_d': ssd_d, 'ssd_norm': ssd_norm,
            'mla_q_norm': mla_q_norm, 'mla_w_uq': mla_w_uq, 'mla_kv_norm': mla_kv_norm, 'mla_w_ukv': mla_w_ukv,
            'fox_f_bias': fox_f_bias, 'w_out': w_out, 'norm_ffn': norm_ffn,
            'router_group_w': router_group_w, 'router_group_b': router_group_b,
            'router_expert_w': router_expert_w, 'router_expert_b': router_expert_b,
            'expert_w_gate': expert_w_gate, 'expert_w_up': expert_w_up, 'expert_w_down': expert_w_down,
            'final_norm': final_norm}


def reference(x, positions, norm_mix, w_in, ssd_conv_w, ssd_conv_b, ssd_dt_bias, ssd_a_log, ssd_d, ssd_norm,
              mla_q_norm, mla_w_uq, mla_kv_norm, mla_w_ukv, fox_f_bias, w_out, norm_ffn,
              router_group_w, router_group_b, router_expert_w, router_expert_b,
              expert_w_gate, expert_w_up, expert_w_down, final_norm):
    b, s, _ = x.shape
    n_sel = min(IDX_TOPK_MAX, s // 4)
    pts = split_points()
    for l in range(DEPTH):
        h = rmsnorm(x, norm_mix[l])
        proj = h @ w_in[l]
        (ssd_z, ssd_xbc, ssd_dt, dsa_q, dsa_k, dsa_v, idx_q, idx_k, idx_w,
         mla_cq, mla_ckv, mla_kr, fox_q, fox_k, fox_v, fox_f) = jnp.split(proj, pts, axis=-1)
        y_ssd = ssd_mixer(ssd_z, ssd_xbc, ssd_dt, ssd_conv_w[l], ssd_conv_b[l], ssd_dt_bias[l],
                          ssd_a_log[l], ssd_d[l], ssd_norm[l])
        y_dsa = dsa_mixer(dsa_q, dsa_k, dsa_v, idx_q, idx_k, idx_w, positions, n_sel)
        y_mla = mla_mixer(mla_cq, mla_ckv, mla_kr, positions, mla_q_norm[l], mla_w_uq[l], mla_kv_norm[l], mla_w_ukv[l])
        y_fox = fox_mixer(fox_q, fox_k, fox_v, fox_f, fox_f_bias[l])
        heads = jnp.concatenate([y_ssd.astype(x.dtype), y_dsa.astype(x.dtype),
                                 y_mla.astype(x.dtype), y_fox.astype(x.dtype)], axis=-1)
        x = x + heads @ w_out[l]
        h2 = rmsnorm(x, norm_ffn[l]).reshape(b * s, D_MODEL)
        y_ffn = hier_moe(h2, router_group_w[l], router_group_b[l], router_expert_w[l], router_expert_b[l],
                         expert_w_gate[l], expert_w_up[l], expert_w_down[l])
        x = x + y_ffn.reshape(b, s, D_MODEL).astype(x.dtype)
    return rmsnorm(x, final_norm)
```

```python
import functools
import math

import jax
import jax.numpy as jnp
from jax import lax
from jax.experimental import pallas as pl
from jax.experimental.pallas import tpu as pltpu

F32 = jnp.float32
BF16 = jnp.bfloat16
I32 = jnp.int32
EPS = 1e-6
ROPE_THETA = 500000.0
HIGHEST = lax.Precision.HIGHEST

D_MODEL = 1024
SSD_HEADS, SSD_HEAD_DIM, SSD_GROUPS, SSD_STATE, SSD_CONV = 4, 64, 2, 128, 4
SSD_D_INNER = SSD_HEADS * SSD_HEAD_DIM
SSD_CONV_DIM = SSD_D_INNER + 2 * SSD_GROUPS * SSD_STATE
DSA_HEADS, DSA_HEAD_DIM, IDX_HEADS, IDX_DIM, IDX_TOPK_MAX = 4, 64, 8, 64, 256
MLA_HEADS, MLA_Q_RANK, MLA_KV_RANK, MLA_NOPE, MLA_ROPE, MLA_V = 4, 256, 128, 64, 32, 64
FOX_HEADS, FOX_HEAD_DIM = 4, 64
MOE_GROUPS, MOE_EPG, MOE_EXPERTS, MOE_FF = 4, 8, 32, 256

LANES = 128
VMEM_LIMIT = 56 * 1024 * 1024
INT_MIN = -(2 ** 31)
NEG_BIG = -1e30

TM = 512
SSD_Q = 128
DSA_BQ = 128
DSA_SC = 512
FA_B = 512
MOE_N = 1024
MOE_ALIGN = 16
MOE_CH = 128
MOE_NS = 2 * MOE_N + MOE_EXPERTS * MOE_ALIGN + MOE_CH

C_ZX, C_GT, C_DQ, C_KV, C_IK, C_IQ, C_MC, C_FQ, C_FK, C_FV, C_END = (
    0, 1024, 1152, 1408, 1536, 1664, 2176, 2688, 3200, 3712, 4224)
G_FOX, G_DT, G_IW = 0, 4, 8
LOG2E = math.log2(math.e)
HEAD_W = 128
FA_HEADS = 4
FOX_ONE_Q, FOX_F_Q, FOX_F_K, FOX_ONE_K = 64, 67, 64, 67


def _cparams(sem):
    return pltpu.CompilerParams(dimension_semantics=sem, vmem_limit_bytes=VMEM_LIMIT)


def _rms(xf, g):
    return xf * lax.rsqrt(jnp.mean(xf * xf, axis=-1, keepdims=True) + EPS) * g


def _softplus(x):
    return jnp.maximum(x, 0.0) + jnp.log1p(jnp.exp(-jnp.abs(x)))


def _silu(x):
    return x * jax.nn.sigmoid(x)


def _rope_apply(a, c, sa, sb, half):
    return a * c + pltpu.roll(a, LANES - half, 1) * sa + pltpu.roll(a, half, 1) * sb


def _dot_nt(a, b):
    return lax.dot_general(a, b, (((1,), (1,)), ((), ())), preferred_element_type=F32)


def _rope_tab_kernel(pos_ref, inv_ref, o_ref):
    pos = pos_ref[...].astype(F32)
    lane = lax.broadcasted_iota(I32, (1, LANES), 1)
    ang = pos * inv_ref[0:1, :]
    c, s = jnp.cos(ang), jnp.sin(ang)
    jd = lane % 64
    o_ref[:, 0:128] = c
    o_ref[:, 128:256] = jnp.where(jd < 8, -s, 0.0)
    o_ref[:, 256:384] = jnp.where((jd >= 8) & (jd < 16), s, 0.0)
    ang = pos * inv_ref[1:2, :]
    c, s = jnp.cos(ang), jnp.sin(ang)
    jm = lane - 64
    o_ref[:, 384:512] = c
    o_ref[:, 512:640] = jnp.where((jm >= 0) & (jm < 16), -s, 0.0)
    o_ref[:, 640:768] = jnp.where((jm >= 16) & (jm < 32), s, 0.0)


def _rope_tables(positions):
    t = positions.size
    half_d, half_m = DSA_HEAD_DIM // 4 // 2, MLA_ROPE // 2
    inv_d = ROPE_THETA ** (-jnp.arange(half_d, dtype=F32) / half_d)
    inv_m = ROPE_THETA ** (-jnp.arange(half_m, dtype=F32) / half_m)
    lane = jnp.arange(LANES)
    jd = lane % 64
    row_d = jnp.where(jd < 2 * half_d, inv_d[jd % half_d], 0.0)
    jm = lane - 64
    row_m = jnp.where((jm >= 0) & (jm < 2 * half_m), inv_m[jm % half_m], 0.0)
    inv = jnp.zeros((8, LANES), F32).at[0].set(row_d).at[1].set(row_m)
    tm = 1024
    return pl.pallas_call(
        _rope_tab_kernel,
        grid=(t // tm,),
        in_specs=[pl.BlockSpec((tm, 1), lambda i: (i, 0)), pl.BlockSpec((8, LANES), lambda i: (0, 0))],
        out_specs=pl.BlockSpec((tm, 768), lambda i: (i, 0)),
        out_shape=jax.ShapeDtypeStruct((t, 768), F32),
        compiler_params=_cparams(("parallel",)),
        name="rope_tables",
    )(positions.reshape(t, 1), inv)


def _inproj_kernel(*refs, has_add):
    if has_add:
        (x_ref, yp_ref, g_ref, w_ref, tab_ref, xo_ref, zx_ref, gt_ref, dq_ref, dk_ref, dv_ref,
         ik_ref, iq_ref, mc_ref, fq_ref, fk_ref, fv_ref) = refs
        x = x_ref[...] + yp_ref[...].astype(F32)
        xo_ref[...] = x
    else:
        (x_ref, g_ref, w_ref, tab_ref, zx_ref, gt_ref, dq_ref, dk_ref, dv_ref,
         ik_ref, iq_ref, mc_ref, fq_ref, fk_ref, fv_ref) = refs
        x = x_ref[...]
    h = _rms(x, g_ref[...]).astype(BF16)

    def proj(a, b):
        return jnp.dot(h, w_ref[:, a:b], preferred_element_type=F32)

    zx_ref[...] = proj(C_ZX, C_GT)
    gt_ref[...] = proj(C_GT, C_DQ)
    c_d, sa_d, sb_d = tab_ref[:, 0:128], tab_ref[:, 128:256], tab_ref[:, 256:384]
    c_m, sa_m, sb_m = tab_ref[:, 384:512], tab_ref[:, 512:640], tab_ref[:, 640:768]
    q = proj(C_DQ, C_KV)
    for i in range(2):
        dq_ref[:, i * 128:(i + 1) * 128] = _rope_apply(q[:, i * 128:(i + 1) * 128], c_d, sa_d, sb_d, 8).astype(BF16)
    kv = proj(C_KV, C_IK)
    is_k = lax.broadcasted_iota(I32, (1, LANES), 1) < 64
    kvr = _rope_apply(kv, jnp.where(is_k, c_d, 1.0), jnp.where(is_k, sa_d, 0.0), jnp.where(is_k, sb_d, 0.0), 8)
    dk_ref[...] = kvr[:, 0:64].astype(BF16)
    dv_ref[...] = kvr[:, 64:128].astype(BF16)
    ikb = proj(C_IK, C_IQ)
    ik_ref[...] = _rope_apply(ikb, c_d, sa_d, sb_d, 8)[:, 0:64].astype(BF16)
    iq = proj(C_IQ, C_MC)
    for i in range(4):
        iq_ref[:, i * 128:(i + 1) * 128] = _rope_apply(iq[:, i * 128:(i + 1) * 128], c_d, sa_d, sb_d, 8).astype(BF16)
    mc = proj(C_MC, C_FQ)
    mc_ref[:, 0:384] = mc[:, 0:384]
    mc_ref[:, 384:512] = _rope_apply(mc[:, 384:512], c_m, sa_m, sb_m, 16)
    fq_ref[...] = proj(C_FQ, C_FK).astype(BF16)
    fk_ref[...] = proj(C_FK, C_FV).astype(BF16)
    ones_hi = jnp.where(lax.broadcasted_iota(I32, (1, FA_HEADS * HEAD_W), 1) % HEAD_W >= 64, 1.0, 0.0)
    fv_ref[...] = (proj(C_FV, C_END) + ones_hi).astype(BF16)


def _pack_w_in(w):
    d = w.shape[0]
    z = lambda n: jnp.zeros((d, n), F32)
    o = 0
    parts = {}
    for name, width in (("z", 256), ("xbc", 768), ("dt", 4), ("dq", 256), ("dk", 64), ("dv", 64), ("iq", 512),
                        ("ik", 64), ("iw", 8), ("cq", 256), ("ckv", 128), ("kr", 32), ("fq", 256), ("fk", 256),
                        ("fv", 256), ("ff", 4)):
        parts[name] = w[:, o:o + width]
        o += width
    scale = DSA_HEAD_DIM ** -0.5

    def per_head(a):
        return jnp.pad(a.reshape(d, FA_HEADS, 64), ((0, 0), (0, 0), (0, HEAD_W - 64))).reshape(d, FA_HEADS * HEAD_W)

    cat = jnp.concatenate([
        parts["z"], parts["xbc"],
        parts["ff"], parts["dt"], parts["iw"], z(112),
        parts["dq"] * scale,
        parts["dk"], parts["dv"],
        parts["ik"], z(64),
        parts["iq"],
        parts["cq"], parts["ckv"], z(64), parts["kr"], z(32),
        per_head(parts["fq"] * (FOX_HEAD_DIM ** -0.5 * LOG2E)), per_head(parts["fk"]), per_head(parts["fv"])], axis=1)
    return cat.astype(BF16)


def _inproj(x, y_prev, g, w_cat, tab):
    t = x.shape[0]
    has_add = y_prev is not None
    tok = lambda n: pl.BlockSpec((TM, n), lambda i: (i, 0))
    full = lambda a: pl.BlockSpec(a.shape, lambda i: (0,) * a.ndim)
    ins = [x] + ([y_prev] if has_add else []) + [g, w_cat, tab]
    in_specs = [tok(D_MODEL)] + ([tok(D_MODEL)] if has_add else []) + [full(g), full(w_cat), tok(768)]
    outs = ([("x", D_MODEL, F32)] if has_add else []) + [
        ("zx", 1024, F32), ("gt", 128, F32), ("dq", 256, BF16), ("dk", 64, BF16), ("dv", 64, BF16),
        ("ik", 64, BF16), ("iq", 512, BF16), ("mc", 512, F32), ("fq", 512, BF16), ("fk", 512, BF16), ("fv", 512, BF16)]
    res = pl.pallas_call(
        functools.partial(_inproj_kernel, has_add=has_add),
        grid=(t // TM,),
        in_specs=in_specs,
        out_specs=[tok(n) for _, n, _ in outs],
        out_shape=[jax.ShapeDtypeStruct((t, n), dt) for _, n, dt in outs],
        compiler_params=_cparams(("parallel",)),
        name="inproj",
    )(*ins)
    res = list(res)
    x_new = res.pop(0) if has_add else x
    return x_new, dict(zip([n for n, _, _ in outs if n != "x"], res))


def _mla_prep_kernel(mc_ref, qn_ref, kn_ref, wq_ref, wk_ref, wv_ref, tab_ref, q_ref, k_ref, v_ref):
    mc = mc_ref[...]
    cq = _rms(mc[:, 0:256], qn_ref[...]).astype(BF16)
    ckv = _rms(mc[:, 256:384], kn_ref[...]).astype(BF16)
    kr = mc[:, 384:512]
    c_m, sa_m, sb_m = tab_ref[:, 384:512], tab_ref[:, 512:640], tab_ref[:, 640:768]
    q = jnp.dot(cq, wq_ref[...], preferred_element_type=F32)
    k = jnp.dot(ckv, wk_ref[...], preferred_element_type=F32)
    for h in range(MLA_HEADS):
        sl = slice(h * 128, (h + 1) * 128)
        q_ref[:, sl] = _rope_apply(q[:, sl], c_m, sa_m, sb_m, 16).astype(BF16)
        k_ref[:, sl] = (k[:, sl] + kr).astype(BF16)
    ones_hi = jnp.where(lax.broadcasted_iota(I32, (1, FA_HEADS * HEAD_W), 1) % HEAD_W >= 64, 1.0, 0.0)
    v_ref[...] = (jnp.dot(ckv, wv_ref[...], preferred_element_type=F32) + ones_hi).astype(BF16)


def _mla_prep(mc, q_norm, kv_norm, w_uq, w_ukv, tab):
    t = mc.shape[0]
    dqk = MLA_NOPE + MLA_ROPE
    wq = jnp.pad(w_uq.reshape(MLA_Q_RANK, MLA_HEADS, dqk) * (dqk ** -0.5 * LOG2E), ((0, 0), (0, 0), (0, 128 - dqk)))
    wq = wq.reshape(MLA_Q_RANK, MLA_HEADS * 128).astype(BF16)
    wkv = w_ukv.reshape(MLA_KV_RANK, MLA_HEADS, MLA_NOPE + MLA_V)
    wk = jnp.pad(wkv[:, :, :MLA_NOPE], ((0, 0), (0, 0), (0, 128 - MLA_NOPE))).reshape(MLA_KV_RANK, MLA_HEADS * 128).astype(BF16)
    wv = jnp.pad(wkv[:, :, MLA_NOPE:], ((0, 0), (0, 0), (0, 128 - MLA_V))).reshape(MLA_KV_RANK, MLA_HEADS * 128).astype(BF16)
    tok = lambda n: pl.BlockSpec((TM, n), lambda i: (i, 0))
    full = lambda a: pl.BlockSpec(a.shape, lambda i: (0,) * a.ndim)
    qn, kn = q_norm.reshape(1, -1), kv_norm.reshape(1, -1)
    return pl.pallas_call(
        _mla_prep_kernel,
        grid=(t // TM,),
        in_specs=[tok(512), full(qn), full(kn), full(wq), full(wk), full(wv), tok(768)],
        out_specs=[tok(512), tok(512), tok(512)],
        out_shape=[jax.ShapeDtypeStruct((t, 512), BF16)] * 3,
        compiler_params=_cparams(("parallel",)),
        name="mla_prep",
    )(mc, qn, kn, wq, wk, wv, tab)


def _fox_cum_kernel(gt_ref, b_ref, q_ref, k_ref, qo_ref, ko_ref, carry_s, *, blk):
    @pl.when(pl.program_id(1) == 0)
    def _():
        carry_s[...] = jnp.zeros_like(carry_s)

    x = gt_ref[...] + b_ref[...]
    lf = -_softplus(-x)
    ri = lax.broadcasted_iota(I32, (blk, blk), 0)
    ci = lax.broadcasted_iota(I32, (blk, blk), 1)
    cs = jnp.dot((ri >= ci).astype(F32), lf, precision=HIGHEST, preferred_element_type=F32) + carry_s[0:1, :]
    carry_s[0:1, :] = cs[blk - 1:blk, :]
    f2 = cs * LOG2E
    lane = lax.broadcasted_iota(I32, (1, HEAD_W), 1)

    def pieces(f, base):
        hi = f.astype(BF16).astype(F32)
        mid = (f - hi).astype(BF16).astype(F32)
        lo = f - hi - mid
        return jnp.where(lane == base, hi, jnp.where(lane == base + 1, mid, jnp.where(lane == base + 2, lo, 0.0)))

    def ones(base):
        return jnp.where((lane >= base) & (lane < base + 3), 1.0, 0.0)

    for h in range(FOX_HEADS):
        fh = f2[:, G_FOX + h:G_FOX + h + 1]
        sl = slice(h * HEAD_W, (h + 1) * HEAD_W)
        qo_ref[:, sl] = (q_ref[:, sl].astype(F32) + pieces(fh, FOX_F_Q) + ones(FOX_ONE_Q)).astype(BF16)
        ko_ref[:, sl] = (k_ref[:, sl].astype(F32) - pieces(fh, FOX_F_K) + ones(FOX_ONE_K)).astype(BF16)


def _fox_cum(gt, f_bias, fq, fk, b, s):
    blk = 512
    bias = jnp.zeros((1, LANES), F32).at[0, G_FOX:G_FOX + FOX_HEADS].set(f_bias)
    nb = s // blk
    tok = lambda n: pl.BlockSpec((blk, n), lambda i, j: (i * nb + j, 0))
    w = FOX_HEADS * HEAD_W
    return pl.pallas_call(
        functools.partial(_fox_cum_kernel, blk=blk),
        grid=(b, nb),
        in_specs=[tok(LANES), pl.BlockSpec((1, LANES), lambda i, j: (0, 0)), tok(w), tok(w)],
        out_specs=[tok(w), tok(w)],
        out_shape=[jax.ShapeDtypeStruct((b * s, w), BF16)] * 2,
        scratch_shapes=[pltpu.VMEM((8, LANES), F32)],
        compiler_params=_cparams(("parallel", "arbitrary")),
        name="fox_cum",
    )(gt, bias, fq, fk)


def _ssd_kernel(zx_ref, gt_ref, cw_ref, cb_ref, pr_ref, pc_ref, drow_ref, nw_ref, o_ref, ext_s, st_s, y_s, *, q):
    @pl.when(pl.program_id(1) == 0)
    def _():
        ext_s[0:8, :] = jnp.zeros((8, SSD_CONV_DIM), F32)
        st_s[...] = jnp.zeros_like(st_s)

    raw = zx_ref[:, 256:1024]
    ext_s[8:8 + q, :] = raw
    acc = jnp.broadcast_to(cb_ref[...], (q, SSD_CONV_DIM))
    for j in range(SSD_CONV):
        acc = acc + cw_ref[j:j + 1, :] * ext_s[5 + j:5 + j + q, :]
    ext_s[0:8, :] = raw[q - 8:q, :]
    xbc = _silu(acc)
    xs = xbc[:, 0:SSD_D_INNER]

    g = gt_ref[...]
    lane = lax.broadcasted_iota(I32, (1, LANES), 1)
    dtc = _softplus(g + pr_ref[0:1, :])
    a_r = jnp.where((lane >= G_DT) & (lane < G_DT + SSD_HEADS), -jnp.exp(pr_ref[1:2, :]), 0.0)
    ri = lax.broadcasted_iota(I32, (q, q), 0)
    ci = lax.broadcasted_iota(I32, (q, q), 1)
    tri = ri >= ci
    acs_c = jnp.dot(tri.astype(F32), dtc * a_r, precision=HIGHEST, preferred_element_type=F32)
    sub = lax.broadcasted_iota(I32, (LANES, 1), 0)
    dtr = _softplus(g.T + pc_ref[:, 0:1])
    a_c = jnp.where((sub >= G_DT) & (sub < G_DT + SSD_HEADS), -jnp.exp(pc_ref[:, 1:2]), 0.0)
    acs_r = jnp.dot(dtr * a_c, (ri <= ci).astype(F32), precision=HIGHEST, preferred_element_type=F32)

    rep = SSD_HEADS // SSD_GROUPS
    for gi in range(SSD_GROUPS):
        bg = xbc[:, SSD_D_INNER + gi * SSD_STATE:SSD_D_INNER + (gi + 1) * SSD_STATE]
        cg = xbc[:, SSD_D_INNER + (SSD_GROUPS + gi) * SSD_STATE:SSD_D_INNER + (SSD_GROUPS + gi + 1) * SSD_STATE]
        bt = bg.T.astype(BF16)
        cb16 = cg.astype(BF16)
        cbm = jnp.dot(cb16, bt, preferred_element_type=F32)
        for hh in range(rep):
            h = gi * rep + hh
            ac = acs_c[:, G_DT + h:G_DT + h + 1]
            ar = acs_r[G_DT + h:G_DT + h + 1, :]
            seg = jnp.where(tri, jnp.exp(ac - ar), 0.0)
            xh = xs[:, h * SSD_HEAD_DIM:(h + 1) * SSD_HEAD_DIM]
            xdt = xh * dtc[:, G_DT + h:G_DT + h + 1]
            yd = jnp.dot((cbm * seg).astype(BF16), xdt.astype(BF16), preferred_element_type=F32)
            aend = ac[q - 1:q, :]
            st = st_s[h]
            yo = jnp.dot(cb16, st.astype(BF16), preferred_element_type=F32) * jnp.exp(ac)
            st_s[h] = st * jnp.exp(aend) + jnp.dot(bt, (xdt * jnp.exp(aend - ac)).astype(BF16),
                                                   preferred_element_type=F32)
            y_s[:, h * SSD_HEAD_DIM:(h + 1) * SSD_HEAD_DIM] = yd + yo
    y = y_s[...] + drow_ref[...] * xs
    y = y * _silu(zx_ref[:, 0:SSD_D_INNER])
    o_ref[...] = _rms(y, nw_ref[...]).astype(BF16)


def _ssd(zx, gt, conv_w, conv_b, dt_bias, a_log, d_skip, norm_w, b, s):
    q = SSD_Q
    nc = s // q
    pr = jnp.zeros((8, LANES), F32).at[0, G_DT:G_DT + SSD_HEADS].set(dt_bias).at[1, G_DT:G_DT + SSD_HEADS].set(a_log)
    pc = pr.T
    drow = jnp.repeat(d_skip, SSD_HEAD_DIM).reshape(1, SSD_D_INNER)
    cb = conv_b.reshape(1, -1)
    nw = norm_w.reshape(1, -1)
    full = lambda a: pl.BlockSpec(a.shape, lambda i, j: (0,) * a.ndim)
    return pl.pallas_call(
        functools.partial(_ssd_kernel, q=q),
        grid=(b, nc),
        in_specs=[pl.BlockSpec((q, 1024), lambda i, j: (i * nc + j, 0)), pl.BlockSpec((q, LANES), lambda i, j: (i * nc + j, 0)),
                  full(conv_w), full(cb), full(pr), full(pc), full(drow), full(nw)],
        out_specs=pl.BlockSpec((q, SSD_D_INNER), lambda i, j: (i * nc + j, 0)),
        out_shape=jax.ShapeDtypeStruct((b * s, SSD_D_INNER), BF16),
        scratch_shapes=[pltpu.VMEM((8 + q, SSD_CONV_DIM), F32), pltpu.VMEM((SSD_HEADS, SSD_STATE, SSD_HEAD_DIM), F32),
                        pltpu.VMEM((q, SSD_D_INNER), F32)],
        compiler_params=_cparams(("parallel", "arbitrary")),
        name="ssd",
    )(zx, gt, conv_w, cb, pr, pc, drow, nw)


def _dsa_kernel(iq_ref, gt_ref, q_ref, ik_ref, k_ref, v_ref, o_ref,
                keys_s, qi_s, qs_s, thr_s, m_s, l_s, acc_s, *, bq, sc_w, n_sel):
    qb = pl.program_id(1)
    n_sc = ((qb + 1) * bq + sc_w - 1) // sc_w
    nsub = sc_w // LANES
    for j in range(IDX_HEADS):
        qi_s[j] = iq_ref[:, j * IDX_DIM:(j + 1) * IDX_DIM]
    for h in range(DSA_HEADS):
        qs_s[h * bq:(h + 1) * bq, :] = q_ref[:, h * DSA_HEAD_DIM:(h + 1) * DSA_HEAD_DIM]
    w = gt_ref[...] * ((IDX_HEADS * IDX_DIM) ** -0.5)
    qpos = qb * bq + lax.broadcasted_iota(I32, (bq, 1), 0)

    def score_body(sc, carry):
        k0 = pl.multiple_of(sc * sc_w, sc_w)
        kidx = ik_ref[pl.ds(k0, sc_w), :]
        sco = jnp.zeros((bq, sc_w), F32)
        for j in range(IDX_HEADS):
            sco = sco + jnp.maximum(_dot_nt(qi_s[j], kidx), 0.0) * w[:, G_IW + j:G_IW + j + 1]
        bits = pltpu.bitcast(sco, I32)
        key = jnp.where(bits < 0, (bits ^ jnp.int32(0x7FFFFFFF)) + 1, bits)
        kpos = k0 + lax.broadcasted_iota(I32, (1, sc_w), 1)
        keys_s[sc] = jnp.where(kpos <= qpos, key, jnp.int32(INT_MIN))
        return carry

    lax.fori_loop(0, n_sc, score_body, 0)

    def count_ge(cand):
        def body(sc, acc):
            m = jnp.where(keys_s[sc] >= cand, 1, 0)
            for u in range(nsub):
                acc = acc + m[:, u * LANES:(u + 1) * LANES]
            return acc
        acc = lax.fori_loop(0, n_sc, body, jnp.zeros((bq, LANES), I32))
        return jnp.sum(acc.astype(F32), axis=1, keepdims=True)

    thr_s[...] = jnp.full((bq, 1), INT_MIN + 1, I32)

    @pl.when((qb + 1) * bq > n_sel)
    def _():
        def bit_body(it, carry):
            lo, cnt_lo = carry
            cand = lo + jnp.left_shift(jnp.int32(1), 31 - it)
            cnt = count_ge(cand)
            ok = cnt >= n_sel
            return jnp.where(ok, cand, lo), jnp.where(ok, cnt, cnt_lo)

        lo, cnt_lo = lax.fori_loop(0, 32, bit_body,
                                   (jnp.full((bq, 1), INT_MIN, I32), jnp.full((bq, 1), 3.0e38, F32)))
        tie = (cnt_lo > n_sel) & (lo > INT_MIN)

        @pl.when(jnp.max(jnp.where(tie, 1.0, 0.0)) > 0.0)
        def _():
            need = n_sel - count_ge(lo + 1)
            ri = lax.broadcasted_iota(I32, (LANES, LANES), 0)
            ci = lax.broadcasted_iota(I32, (LANES, LANES), 1)
            upper = jnp.where(ri <= ci, 1.0, 0.0).astype(BF16)

            def tie_body(sc, run):
                t = keys_s[sc]
                for u in range(nsub):
                    tu = t[:, u * LANES:(u + 1) * LANES]
                    eq = (tu == lo) & tie
                    pre = jnp.dot(jnp.where(eq, 1.0, 0.0).astype(BF16), upper, preferred_element_type=F32)
                    drop = eq & (run + pre > need)
                    keys_s[sc, :, u * LANES:(u + 1) * LANES] = jnp.where(drop, jnp.int32(INT_MIN), tu)
                    run = run + pre[:, LANES - 1:LANES]
                return run

            lax.fori_loop(0, n_sc, tie_body, jnp.zeros((bq, 1), F32))

        thr_s[...] = jnp.maximum(lo, INT_MIN + 1)

    thr = thr_s[...]
    m_s[...] = jnp.full(m_s.shape, NEG_BIG, F32)
    l_s[...] = jnp.zeros_like(l_s)
    acc_s[...] = jnp.zeros_like(acc_s)

    def att_body(sc, carry):
        k0 = pl.multiple_of(sc * sc_w, sc_w)
        kk = k_ref[pl.ds(k0, sc_w), :]
        vv = v_ref[pl.ds(k0, sc_w), :]
        sel = (keys_s[sc] >= thr)[None]
        s3 = _dot_nt(qs_s[...], kk).reshape(DSA_HEADS, bq, sc_w)
        s3 = jnp.where(sel, s3, NEG_BIG)
        m_old = m_s[...]
        m_new = jnp.maximum(m_old, jnp.max(s3, axis=2, keepdims=True))
        alpha = jnp.exp(m_old - m_new)
        p = jnp.where(sel, jnp.exp(s3 - m_new), 0.0)
        l_s[...] = alpha * l_s[...] + jnp.sum(p, axis=2, keepdims=True)
        pv = jnp.dot(p.reshape(DSA_HEADS * bq, sc_w).astype(BF16), vv, preferred_element_type=F32)
        acc_s[...] = alpha * acc_s[...] + pv.reshape(DSA_HEADS, bq, DSA_HEAD_DIM)
        m_s[...] = m_new
        return carry

    lax.fori_loop(0, n_sc, att_body, 0)
    out = acc_s[...] / l_s[...]
    for h in range(DSA_HEADS):
        o_ref[:, h * DSA_HEAD_DIM:(h + 1) * DSA_HEAD_DIM] = out[h].astype(BF16)


def _dsa(iq, gt, dq, ik, dk, dv, b, s, n_sel):
    bq, sc_w = DSA_BQ, DSA_SC
    nq = s // bq
    qblk = lambda n: pl.BlockSpec((bq, n), lambda i, j: (i * nq + j, 0))
    kblk = pl.BlockSpec((s, 64), lambda i, j: (i, 0))
    return pl.pallas_call(
        functools.partial(_dsa_kernel, bq=bq, sc_w=sc_w, n_sel=n_sel),
        grid=(b, nq),
        in_specs=[qblk(512), qblk(LANES), qblk(256), kblk, kblk, kblk],
        out_specs=qblk(256),
        out_shape=jax.ShapeDtypeStruct((b * s, 256), BF16),
        scratch_shapes=[pltpu.VMEM((s // sc_w, bq, sc_w), I32), pltpu.VMEM((IDX_HEADS, bq, IDX_DIM), BF16),
                        pltpu.VMEM((DSA_HEADS * bq, DSA_HEAD_DIM), BF16), pltpu.VMEM((bq, 1), I32),
                        pltpu.VMEM((DSA_HEADS, bq, 1), F32), pltpu.VMEM((DSA_HEADS, bq, 1), F32),
                        pltpu.VMEM((DSA_HEADS, bq, DSA_HEAD_DIM), F32)],
        compiler_params=_cparams(("parallel", "arbitrary")),
        name="dsa",
    )(iq, gt, dq, ik, dk, dv)


def _flash_kernel(qt_ref, kt_ref, q_ref, k_ref, v_ref, o_ref, m_s, acc_s, *, blk):
    t = pl.program_id(1)
    qi, ki = qt_ref[t], kt_ref[t]

    @pl.when(ki == 0)
    def _():
        m_s[...] = jnp.full(m_s.shape, NEG_BIG, F32)
        acc_s[...] = jnp.zeros_like(acc_s)

    def step(diagonal):
        if diagonal:
            keep = lax.broadcasted_iota(I32, (blk, blk), 0) >= lax.broadcasted_iota(I32, (blk, blk), 1)
        for h in range(FA_HEADS):
            sl = slice(h * HEAD_W, (h + 1) * HEAD_W)
            s = _dot_nt(q_ref[:, sl], k_ref[:, sl])
            if diagonal:
                s = jnp.where(keep, s, 2 * NEG_BIG)
            m_prev = m_s[h]
            m_next = jnp.maximum(m_prev, jnp.max(s, axis=1, keepdims=True))
            p = jnp.exp2(s - jnp.tile(m_next, (1, blk // LANES)))
            pv = jnp.dot(p.astype(BF16), v_ref[:, sl], preferred_element_type=F32)
            acc_s[h] = jnp.exp2(m_prev - m_next) * acc_s[h] + pv
            m_s[h] = m_next

    @pl.when(ki < qi)
    def _():
        step(False)

    @pl.when(ki == qi)
    def _():
        step(True)
        for h in range(FA_HEADS):
            acc = acc_s[h]
            o_ref[:, h * 64:(h + 1) * 64] = (acc / pltpu.roll(acc, 64, 1))[:, 0:64].astype(BF16)


def _flash(q, k, v, b, s, name):
    blk = FA_B
    nb = s // blk
    pairs = [(i, j) for i in range(nb) for j in range(i + 1)]
    qt = jnp.asarray([p[0] for p in pairs], I32)
    kt = jnp.asarray([p[1] for p in pairs], I32)
    w = FA_HEADS * HEAD_W
    grid_spec = pltpu.PrefetchScalarGridSpec(
        num_scalar_prefetch=2,
        grid=(b, len(pairs)),
        in_specs=[pl.BlockSpec((blk, w), lambda i, t, qt, kt: (i * nb + qt[t], 0)),
                  pl.BlockSpec((blk, w), lambda i, t, qt, kt: (i * nb + kt[t], 0)),
                  pl.BlockSpec((blk, w), lambda i, t, qt, kt: (i * nb + kt[t], 0))],
        out_specs=pl.BlockSpec((blk, FA_HEADS * 64), lambda i, t, qt, kt: (i * nb + qt[t], 0)),
        scratch_shapes=[pltpu.VMEM((FA_HEADS, blk, LANES), F32), pltpu.VMEM((FA_HEADS, blk, HEAD_W), F32)],
    )
    return pl.pallas_call(
        functools.partial(_flash_kernel, blk=blk),
        grid_spec=grid_spec,
        out_shape=jax.ShapeDtypeStruct((b * s, FA_HEADS * 64), BF16),
        compiler_params=_cparams(("parallel", "arbitrary")),
        name=name,
    )(qt, kt, q, k, v)


def _outproj_kernel(x_ref, y0_ref, y1_ref, y2_ref, y3_ref, wo_ref, g_ref, wr_ref, br_ref,
                    xo_ref, h_ref, cw_ref, rs_ref):
    acc = x_ref[...]
    for i, y_ref in enumerate((y0_ref, y1_ref, y2_ref, y3_ref)):
        acc = acc + jnp.dot(y_ref[...], wo_ref[i * 256:(i + 1) * 256, :], preferred_element_type=F32)
    xo_ref[...] = acc
    h2 = _rms(acc, g_ref[...])
    h_ref[...] = h2.astype(BF16)
    lg = jnp.dot(h2, wr_ref[...], precision=HIGHEST, preferred_element_type=F32) + br_ref[...]
    lane = lax.broadcasted_iota(I32, (1, LANES), 1)
    lanef = lane.astype(F32)
    is_g = (lane >= MOE_EXPERTS) & (lane < MOE_EXPERTS + MOE_GROUPS)
    gl = jnp.where(is_g, lg, -jnp.inf)
    gmax = jnp.max(gl, axis=1, keepdims=True)
    gidx = jnp.min(jnp.where(gl == gmax, lanef, 999.0), axis=1, keepdims=True) - MOE_EXPERTS
    g_p = 1.0 / jnp.sum(jnp.where(is_g, jnp.exp(gl - gmax), 0.0), axis=1, keepdims=True)
    in_g = (lane < MOE_EXPERTS) & (jnp.floor(lanef * (1.0 / MOE_EPG)) == gidx)
    el = jnp.where(in_g, lg, -jnp.inf)
    m1 = jnp.max(el, axis=1, keepdims=True)
    i1 = jnp.min(jnp.where(el == m1, lanef, 999.0), axis=1, keepdims=True)
    el2 = jnp.where(lanef == i1, -jnp.inf, el)
    m2 = jnp.max(el2, axis=1, keepdims=True)
    i2 = jnp.min(jnp.where(el2 == m2, lanef, 999.0), axis=1, keepdims=True)
    t = jnp.exp(m2 - m1)
    w1 = 1.0 / (1.0 + t)
    cw_ref[...] = jnp.where(lanef == i1, w1 * g_p, jnp.where(lanef == i2, t * w1 * g_p, 0.0))
    rs_ref[...] = jnp.where((lanef == i1) | (lanef == i2), 1.0, 0.0).astype(BF16)


def _outproj(x, ys, w_out, g, w_rg, b_rg, w_re, b_re):
    t = x.shape[0]
    wo = w_out.astype(BF16)
    wr = jnp.zeros((D_MODEL, LANES), F32).at[:, 0:MOE_EXPERTS].set(w_re).at[:, MOE_EXPERTS:MOE_EXPERTS + MOE_GROUPS].set(w_rg)
    br = jnp.zeros((1, LANES), F32).at[0, 0:MOE_EXPERTS].set(b_re).at[0, MOE_EXPERTS:MOE_EXPERTS + MOE_GROUPS].set(b_rg)
    tok = lambda n: pl.BlockSpec((TM, n), lambda i: (i, 0))
    full = lambda a: pl.BlockSpec(a.shape, lambda i: (0,) * a.ndim)
    return pl.pallas_call(
        _outproj_kernel,
        grid=(t // TM,),
        in_specs=[tok(D_MODEL)] + [tok(256)] * 4 + [full(wo), full(g), full(wr), full(br)],
        out_specs=[tok(D_MODEL), tok(D_MODEL), tok(LANES), tok(LANES)],
        out_shape=[jax.ShapeDtypeStruct((t, D_MODEL), F32), jax.ShapeDtypeStruct((t, D_MODEL), BF16),
                   jax.ShapeDtypeStruct((t, LANES), F32), jax.ShapeDtypeStruct((t, LANES), BF16)],
        compiler_params=_cparams(("parallel",)),
        name="outproj_router",
    )(x, *ys, wo, g, wr, br)


def _moe_plan_kernel(rs_ref, cw_ref, meta_ref, col_ref, row_ref, *, n):
    sel = rs_ref[...]
    self32 = sel.astype(F32)
    ti = lax.broadcasted_iota(I32, (n, n), 0)
    tj = lax.broadcasted_iota(I32, (n, n), 1)
    rank = jnp.dot(jnp.where(ti > tj, 1.0, 0.0).astype(BF16), sel, preferred_element_type=F32)
    cnt = jnp.sum(self32, axis=0, keepdims=True).astype(I32)
    cpad = ((cnt + (MOE_ALIGN - 1)) // MOE_ALIGN) * MOE_ALIGN
    li = lax.broadcasted_iota(I32, (LANES, LANES), 0)
    lj = lax.broadcasted_iota(I32, (LANES, LANES), 1)
    off = jnp.dot(jnp.broadcast_to(cpad.astype(F32), (8, LANES)), (li < lj).astype(F32),
                  precision=HIGHEST, preferred_element_type=F32)[0:1, :]
    dest = off + rank
    lane = lax.broadcasted_iota(I32, (1, LANES), 1)
    lanef = lane.astype(F32)
    on = self32 > 0.5
    e_lo = jnp.min(jnp.where(on, lanef, 999.0), axis=1, keepdims=True)
    e_hi = jnp.max(jnp.where(on, lanef, -1.0), axis=1, keepdims=True)
    is_lo, is_hi = lanef == e_lo, lanef == e_hi
    cw = cw_ref[...]
    pick = lambda m, a: jnp.sum(jnp.where(m, a, 0.0), axis=1, keepdims=True)
    d_lo, d_hi, w_lo, w_hi = pick(is_lo, dest), pick(is_hi, dest), pick(is_lo, cw), pick(is_hi, cw)
    colv = jnp.where(lane == 0, d_lo, jnp.where(lane == 1, d_hi, jnp.where(lane == 2, w_lo, jnp.where(lane == 3, w_hi, 0.0))))
    col_ref[...] = colv
    row_ref[0] = colv.T[0:8, :]
    sub = lax.broadcasted_iota(I32, (8, LANES), 0)
    meta_ref[0] = jnp.where(sub == 0, off.astype(I32), jnp.where(sub == 1, cnt, 0))


def _moe_kernel(meta_ref, h_ref, col_ref, row_ref, wg_ref, wu_ref, wd_ref, o_ref, s_s, y_s, *, n, ns):
    t, e = pl.program_id(0), pl.program_id(1)

    @pl.when(e == 0)
    def _():
        d_lo, d_hi = row_ref[0, 0:1, :], row_ref[0, 1:2, :]
        for r in range(ns // LANES):
            si = (r * LANES + lax.broadcasted_iota(I32, (LANES, 1), 0)).astype(F32)
            perm = jnp.where((si == d_lo) | (si == d_hi), 1.0, 0.0).astype(BF16)
            s_s[r * LANES:(r + 1) * LANES, :] = jnp.dot(perm, h_ref[...], preferred_element_type=F32).astype(BF16)
        y_s[...] = jnp.zeros_like(y_s)

    off = meta_ref[t * 2 * MOE_EXPERTS + e]
    cnt = meta_ref[t * 2 * MOE_EXPERTS + MOE_EXPERTS + e]

    def body(i, carry):
        r0 = pl.multiple_of(off + i * MOE_CH, MOE_ALIGN)
        xs = s_s[pl.ds(r0, MOE_CH), :]
        gate = jnp.dot(xs, wg_ref[0], preferred_element_type=F32)
        up = jnp.dot(xs, wu_ref[0], preferred_element_type=F32)
        hid = (_silu(gate) * up).astype(BF16)
        y_s[pl.ds(r0, MOE_CH), :] = jnp.dot(hid, wd_ref[0], preferred_element_type=F32).astype(BF16)
        return carry

    lax.fori_loop(0, (cnt + MOE_CH - 1) // MOE_CH, body, 0)

    @pl.when(e == pl.num_programs(1) - 1)
    def _():
        rows, kc = 256, ns // 3
        for c in range(n // rows):
            cv = col_ref[c * rows:(c + 1) * rows, :]
            d_lo, d_hi, w_lo, w_hi = cv[:, 0:1], cv[:, 1:2], cv[:, 2:3], cv[:, 3:4]
            acc = jnp.zeros((rows, D_MODEL), F32)
            for r in range(ns // kc):
                si = (r * kc + lax.broadcasted_iota(I32, (1, kc), 1)).astype(F32)
                pw = (jnp.where(si == d_lo, w_lo, 0.0) + jnp.where(si == d_hi, w_hi, 0.0)).astype(BF16)
                acc = acc + jnp.dot(pw, y_s[r * kc:(r + 1) * kc, :], preferred_element_type=F32)
            o_ref[c * rows:(c + 1) * rows, :] = acc.astype(BF16)


def _moe(h2, cw, rs, w_gate, w_up, w_down):
    t = h2.shape[0]
    n, ns = MOE_N, MOE_NS
    nt = t // n
    meta, col, row = pl.pallas_call(
        functools.partial(_moe_plan_kernel, n=n),
        grid=(nt,),
        in_specs=[pl.BlockSpec((n, LANES), lambda i: (i, 0)), pl.BlockSpec((n, LANES), lambda i: (i, 0))],
        out_specs=[pl.BlockSpec((1, 8, LANES), lambda i: (i, 0, 0)), pl.BlockSpec((n, LANES), lambda i: (i, 0)),
                   pl.BlockSpec((1, 8, n), lambda i: (i, 0, 0))],
        out_shape=[jax.ShapeDtypeStruct((nt, 8, LANES), I32), jax.ShapeDtypeStruct((t, LANES), F32),
                   jax.ShapeDtypeStruct((nt, 8, n), F32)],
        compiler_params=_cparams(("parallel",)),
        name="moe_plan",
    )(rs, cw)
    meta_flat = meta[:, 0:2, 0:MOE_EXPERTS].reshape(-1)
    wg, wu, wd = w_gate.astype(BF16), w_up.astype(BF16), w_down.astype(BF16)
    grid_spec = pltpu.PrefetchScalarGridSpec(
        num_scalar_prefetch=1,
        grid=(nt, MOE_EXPERTS),
        in_specs=[pl.BlockSpec((n, D_MODEL), lambda i, e, m: (i, 0)),
                  pl.BlockSpec((n, LANES), lambda i, e, m: (i, 0)),
                  pl.BlockSpec((1, 8, n), lambda i, e, m: (i, 0, 0)),
                  pl.BlockSpec((1, D_MODEL, MOE_FF), lambda i, e, m: (e, 0, 0)),
                  pl.BlockSpec((1, D_MODEL, MOE_FF), lambda i, e, m: (e, 0, 0)),
                  pl.BlockSpec((1, MOE_FF, D_MODEL), lambda i, e, m: (e, 0, 0))],
        out_specs=pl.BlockSpec((n, D_MODEL), lambda i, e, m: (i, 0)),
        scratch_shapes=[pltpu.VMEM((ns, D_MODEL), BF16), pltpu.VMEM((ns, D_MODEL), BF16)],
    )
    return pl.pallas_call(
        functools.partial(_moe_kernel, n=n, ns=ns),
        grid_spec=grid_spec,
        out_shape=jax.ShapeDtypeStruct((t, D_MODEL), BF16),
        compiler_params=_cparams(("parallel", "arbitrary")),
        name="moe_experts",
    )(meta_flat, h2, col, row, wg, wu, wd)


def _final_kernel(x_ref, y_ref, g_ref, o_ref):
    o_ref[...] = _rms(x_ref[...] + y_ref[...].astype(F32), g_ref[...])


def _final(x, y, g):
    t = x.shape[0]
    tok = pl.BlockSpec((TM, D_MODEL), lambda i: (i, 0))
    return pl.pallas_call(
        _final_kernel,
        grid=(t // TM,),
        in_specs=[tok, tok, pl.BlockSpec((1, D_MODEL), lambda i: (0, 0))],
        out_specs=tok,
        out_shape=jax.ShapeDtypeStruct((t, D_MODEL), F32),
        compiler_params=_cparams(("parallel",)),
        name="final_norm",
    )(x, y, g)


def kernel(x, positions, norm_mix, w_in, ssd_conv_w, ssd_conv_b, ssd_dt_bias, ssd_a_log, ssd_d, ssd_norm, mla_q_norm, mla_w_uq, mla_kv_norm, mla_w_ukv, fox_f_bias, w_out, norm_ffn, router_group_w, router_group_b, router_expert_w, router_expert_b, expert_w_gate, expert_w_up, expert_w_down, final_norm):
    b, s, d = x.shape
    assert d == D_MODEL and s % FA_B == 0 and (b * s) % MOE_N == 0
    depth = w_in.shape[0]
    n_sel = min(IDX_TOPK_MAX, s // 4)
    tab = _rope_tables(positions)
    xf = x.reshape(b * s, d)
    y_ffn = None
    for l in range(depth):
        xf, p = _inproj(xf, y_ffn, norm_mix[l].reshape(1, d), _pack_w_in(w_in[l]), tab)
        mq, mk, mv = _mla_prep(p["mc"], mla_q_norm[l], mla_kv_norm[l], mla_w_uq[l], mla_w_ukv[l], tab)
        fq, fk = _fox_cum(p["gt"], fox_f_bias[l], p["fq"], p["fk"], b, s)
        y_ssd = _ssd(p["zx"], p["gt"], ssd_conv_w[l], ssd_conv_b[l], ssd_dt_bias[l], ssd_a_log[l], ssd_d[l],
                     ssd_norm[l], b, s)
        y_dsa = _dsa(p["iq"], p["gt"], p["dq"], p["ik"], p["dk"], p["dv"], b, s, n_sel)
        y_mla = _flash(mq, mk, mv, b, s, "flash_mla")
        y_fox = _flash(fq, fk, p["fv"], b, s, "flash_fox")
        xf, h2, cw, rs = _outproj(xf, (y_ssd, y_dsa, y_mla, y_fox), w_out[l], norm_ffn[l].reshape(1, d),
                                  router_group_w[l], router_group_b[l], router_expert_w[l], router_expert_b[l])
        y_ffn = _moe(h2, cw, rs, expert_w_gate[l], expert_w_up[l], expert_w_down[l])
    return _final(xf, y_ffn, final_norm.reshape(1, d)).reshape(b, s, d)
```

```python
import functools
import math

import jax
import jax.numpy as jnp
from jax import lax
from jax.experimental import pallas as pl
from jax.experimental.pallas import tpu as pltpu

F32 = jnp.float32
BF16 = jnp.bfloat16
I32 = jnp.int32
EPS = 1e-6
ROPE_THETA = 500000.0
HIGHEST = lax.Precision.HIGHEST

D_MODEL = 1024
SSD_HEADS, SSD_HEAD_DIM, SSD_GROUPS, SSD_STATE, SSD_CONV = 4, 64, 2, 128, 4
SSD_D_INNER = SSD_HEADS * SSD_HEAD_DIM
SSD_CONV_DIM = SSD_D_INNER + 2 * SSD_GROUPS * SSD_STATE
DSA_HEADS, DSA_HEAD_DIM, IDX_HEADS, IDX_DIM, IDX_TOPK_MAX = 4, 64, 8, 64, 256
MLA_HEADS, MLA_Q_RANK, MLA_KV_RANK, MLA_NOPE, MLA_ROPE, MLA_V = 4, 256, 128, 64, 32, 64
FOX_HEADS, FOX_HEAD_DIM = 4, 64
MOE_GROUPS, MOE_EPG, MOE_EXPERTS, MOE_FF = 4, 8, 32, 256

LANES = 128
VMEM_LIMIT = 56 * 1024 * 1024
INT_MIN = -(2 ** 31)
NEG_BIG = -1e30

TM = 512
SSD_Q = 128
DSA_BQ = 128
DSA_SC = 512
FA_B = 512
MOE_N = 1024
MOE_ALIGN = 16
MOE_CH = 128
MOE_NS = 2 * MOE_N + MOE_EXPERTS * MOE_ALIGN + MOE_CH

C_ZX, C_GT, C_DQ, C_KV, C_IK, C_IQ, C_MC, C_FQ, C_FK, C_FV, C_END = (
    0, 1024, 1152, 1408, 1536, 1664, 2176, 2688, 3200, 3712, 4224)
G_FOX, G_DT, G_IW = 0, 4, 8
LOG2E = math.log2(math.e)
HEAD_W = 128
FA_HEADS = 4
FOX_ONE_Q, FOX_F_Q, FOX_F_K, FOX_ONE_K = 64, 67, 64, 67


def _cparams(sem):
    return pltpu.CompilerParams(dimension_semantics=sem, vmem_limit_bytes=VMEM_LIMIT)


def _rms(xf, g):
    return xf * lax.rsqrt(jnp.mean(xf * xf, axis=-1, keepdims=True) + EPS) * g


def _softplus(x):
    return jnp.maximum(x, 0.0) + jnp.log1p(jnp.exp(-jnp.abs(x)))


def _silu(x):
    return x * jax.nn.sigmoid(x)


def _rope_apply(a, c, sa, sb, half):
    return a * c + pltpu.roll(a, LANES - half, 1) * sa + pltpu.roll(a, half, 1) * sb


def _dot_nt(a, b):
    return lax.dot_general(a, b, (((1,), (1,)), ((), ())), preferred_element_type=F32)


def _rope_tab_kernel(pos_ref, inv_ref, o_ref):
    pos = pos_ref[...].astype(F32)
    lane = lax.broadcasted_iota(I32, (1, LANES), 1)
    ang = pos * inv_ref[0:1, :]
    c, s = jnp.cos(ang), jnp.sin(ang)
    jd = lane % 64
    o_ref[:, 0:128] = c
    o_ref[:, 128:256] = jnp.where(jd < 8, -s, 0.0)
    o_ref[:, 256:384] = jnp.where((jd >= 8) & (jd < 16), s, 0.0)
    ang = pos * inv_ref[1:2, :]
    c, s = jnp.cos(ang), jnp.sin(ang)
    jm = lane - 64
    o_ref[:, 384:512] = c
    o_ref[:, 512:640] = jnp.where((jm >= 0) & (jm < 16), -s, 0.0)
    o_ref[:, 640:768] = jnp.where((jm >= 16) & (jm < 32), s, 0.0)


def _rope_tables(positions):
    t = positions.size
    half_d, half_m = DSA_HEAD_DIM // 4 // 2, MLA_ROPE // 2
    inv_d = ROPE_THETA ** (-jnp.arange(half_d, dtype=F32) / half_d)
    inv_m = ROPE_THETA ** (-jnp.arange(half_m, dtype=F32) / half_m)
    lane = jnp.arange(LANES)
    jd = lane % 64
    row_d = jnp.where(jd < 2 * half_d, inv_d[jd % half_d], 0.0)
    jm = lane - 64
    row_m = jnp.where((jm >= 0) & (jm < 2 * half_m), inv_m[jm % half_m], 0.0)
    inv = jnp.zeros((8, LANES), F32).at[0].set(row_d).at[1].set(row_m)
    tm = 1024
    return pl.pallas_call(
        _rope_tab_kernel,
        grid=(t // tm,),
        in_specs=[pl.BlockSpec((tm, 1), lambda i: (i, 0)), pl.BlockSpec((8, LANES), lambda i: (0, 0))],
        out_specs=pl.BlockSpec((tm, 768), lambda i: (i, 0)),
        out_shape=jax.ShapeDtypeStruct((t, 768), F32),
        compiler_params=_cparams(("parallel",)),
        name="rope_tables",
    )(positions.reshape(t, 1), inv)


def _inproj_kernel(*refs, has_add):
    if has_add:
        (x_ref, yp_ref, g_ref, w_ref, tab_ref, xo_ref, zx_ref, gt_ref, dq_ref, dk_ref, dv_ref,
         ik_ref, iq_ref, mc_ref, fq_ref, fk_ref, fv_ref) = refs
        x = x_ref[...] + yp_ref[...].astype(F32)
        xo_ref[...] = x
    else:
        (x_ref, g_ref, w_ref, tab_ref, zx_ref, gt_ref, dq_ref, dk_ref, dv_ref,
         ik_ref, iq_ref, mc_ref, fq_ref, fk_ref, fv_ref) = refs
        x = x_ref[...]
    h = _rms(x, g_ref[...]).astype(BF16)

    def proj(a, b):
        return jnp.dot(h, w_ref[:, a:b], preferred_element_type=F32)

    zx_ref[...] = proj(C_ZX, C_GT)
    gt_ref[...] = proj(C_GT, C_DQ)
    c_d, sa_d, sb_d = tab_ref[:, 0:128], tab_ref[:, 128:256], tab_ref[:, 256:384]
    c_m, sa_m, sb_m = tab_ref[:, 384:512], tab_ref[:, 512:640], tab_ref[:, 640:768]
    q = proj(C_DQ, C_KV)
    for i in range(2):
        dq_ref[:, i * 128:(i + 1) * 128] = _rope_apply(q[:, i * 128:(i + 1) * 128], c_d, sa_d, sb_d, 8).astype(BF16)
    kv = proj(C_KV, C_IK)
    is_k = lax.broadcasted_iota(I32, (1, LANES), 1) < 64
    kvr = _rope_apply(kv, jnp.where(is_k, c_d, 1.0), jnp.where(is_k, sa_d, 0.0), jnp.where(is_k, sb_d, 0.0), 8)
    dk_ref[...] = kvr[:, 0:64].astype(BF16)
    dv_ref[0] = kvr.T[64:128, :].astype(BF16)
    ikb = proj(C_IK, C_IQ)
    ik_ref[...] = _rope_apply(ikb, c_d, sa_d, sb_d, 8)[:, 0:64].astype(BF16)
    iq = proj(C_IQ, C_MC)
    for i in range(4):
        iq_ref[:, i * 128:(i + 1) * 128] = _rope_apply(iq[:, i * 128:(i + 1) * 128], c_d, sa_d, sb_d, 8).astype(BF16)
    mc = proj(C_MC, C_FQ)
    mc_ref[:, 0:384] = mc[:, 0:384]
    mc_ref[:, 384:512] = _rope_apply(mc[:, 384:512], c_m, sa_m, sb_m, 16)
    fq_ref[...] = proj(C_FQ, C_FK).astype(BF16)
    fk_ref[...] = proj(C_FK, C_FV).astype(BF16)
    ones_hi = jnp.where(lax.broadcasted_iota(I32, (1, FA_HEADS * HEAD_W), 1) % HEAD_W >= 64, 1.0, 0.0)
    fv_ref[...] = (proj(C_FV, C_END) + ones_hi).astype(BF16)


def _pack_w_in(w):
    d = w.shape[0]
    z = lambda n: jnp.zeros((d, n), F32)
    o = 0
    parts = {}
    for name, width in (("z", 256), ("xbc", 768), ("dt", 4), ("dq", 256), ("dk", 64), ("dv", 64), ("iq", 512),
                        ("ik", 64), ("iw", 8), ("cq", 256), ("ckv", 128), ("kr", 32), ("fq", 256), ("fk", 256),
                        ("fv", 256), ("ff", 4)):
        parts[name] = w[:, o:o + width]
        o += width
    scale = DSA_HEAD_DIM ** -0.5

    def per_head(a):
        return jnp.pad(a.reshape(d, FA_HEADS, 64), ((0, 0), (0, 0), (0, HEAD_W - 64))).reshape(d, FA_HEADS * HEAD_W)

    cat = jnp.concatenate([
        parts["z"], parts["xbc"],
        parts["ff"], parts["dt"], parts["iw"], z(112),
        parts["dq"] * (scale * LOG2E),
        parts["dk"], parts["dv"],
        parts["ik"], z(64),
        parts["iq"],
        parts["cq"], parts["ckv"], z(64), parts["kr"], z(32),
        per_head(parts["fq"] * (FOX_HEAD_DIM ** -0.5 * LOG2E)), per_head(parts["fk"]), per_head(parts["fv"])], axis=1)
    return cat.astype(BF16)


def _inproj(x, y_prev, g, w_cat, tab):
    t = x.shape[0]
    has_add = y_prev is not None
    tok = lambda n: pl.BlockSpec((TM, n), lambda i: (i, 0))
    full = lambda a: pl.BlockSpec(a.shape, lambda i: (0,) * a.ndim)
    ins = [x] + ([y_prev] if has_add else []) + [g, w_cat, tab]
    in_specs = [tok(D_MODEL)] + ([tok(D_MODEL)] if has_add else []) + [full(g), full(w_cat), tok(768)]
    outs = ([("x", D_MODEL, F32)] if has_add else []) + [
        ("zx", 1024, F32), ("gt", 128, F32), ("dq", 256, BF16), ("dk", 64, BF16), ("dv", 64, BF16),
        ("ik", 64, BF16), ("iq", 512, BF16), ("mc", 512, F32), ("fq", 512, BF16), ("fk", 512, BF16), ("fv", 512, BF16)]
    res = pl.pallas_call(
        functools.partial(_inproj_kernel, has_add=has_add),
        grid=(t // TM,),
        in_specs=in_specs,
        out_specs=[pl.BlockSpec((1, DSA_HEAD_DIM, TM), lambda i: (i, 0, 0)) if nm == "dv" else tok(n)
                   for nm, n, _ in outs],
        out_shape=[jax.ShapeDtypeStruct((t // TM, DSA_HEAD_DIM, TM) if nm == "dv" else (t, n), dt)
                   for nm, n, dt in outs],
        compiler_params=_cparams(("parallel",)),
        name="inproj",
    )(*ins)
    res = list(res)
    x_new = res.pop(0) if has_add else x
    return x_new, dict(zip([n for n, _, _ in outs if n != "x"], res))


def _mla_prep_kernel(mc_ref, qn_ref, kn_ref, wq_ref, wk_ref, wv_ref, tab_ref, q_ref, k_ref, v_ref):
    mc = mc_ref[...]
    cq = _rms(mc[:, 0:256], qn_ref[...]).astype(BF16)
    ckv = _rms(mc[:, 256:384], kn_ref[...]).astype(BF16)
    kr = mc[:, 384:512]
    c_m, sa_m, sb_m = tab_ref[:, 384:512], tab_ref[:, 512:640], tab_ref[:, 640:768]
    q = jnp.dot(cq, wq_ref[...], preferred_element_type=F32)
    k = jnp.dot(ckv, wk_ref[...], preferred_element_type=F32)
    for h in range(MLA_HEADS):
        sl = slice(h * 128, (h + 1) * 128)
        q_ref[:, sl] = _rope_apply(q[:, sl], c_m, sa_m, sb_m, 16).astype(BF16)
        k_ref[:, sl] = (k[:, sl] + kr).astype(BF16)
    ones_hi = jnp.where(lax.broadcasted_iota(I32, (1, FA_HEADS * HEAD_W), 1) % HEAD_W >= 64, 1.0, 0.0)
    v_ref[...] = (jnp.dot(ckv, wv_ref[...], preferred_element_type=F32) + ones_hi).astype(BF16)


def _mla_prep(mc, q_norm, kv_norm, w_uq, w_ukv, tab):
    t = mc.shape[0]
    dqk = MLA_NOPE + MLA_ROPE
    wq = jnp.pad(w_uq.reshape(MLA_Q_RANK, MLA_HEADS, dqk) * (dqk ** -0.5 * LOG2E), ((0, 0), (0, 0), (0, 128 - dqk)))
    wq = wq.reshape(MLA_Q_RANK, MLA_HEADS * 128).astype(BF16)
    wkv = w_ukv.reshape(MLA_KV_RANK, MLA_HEADS, MLA_NOPE + MLA_V)
    wk = jnp.pad(wkv[:, :, :MLA_NOPE], ((0, 0), (0, 0), (0, 128 - MLA_NOPE))).reshape(MLA_KV_RANK, MLA_HEADS * 128).astype(BF16)
    wv = jnp.pad(wkv[:, :, MLA_NOPE:], ((0, 0), (0, 0), (0, 128 - MLA_V))).reshape(MLA_KV_RANK, MLA_HEADS * 128).astype(BF16)
    tok = lambda n: pl.BlockSpec((TM, n), lambda i: (i, 0))
    full = lambda a: pl.BlockSpec(a.shape, lambda i: (0,) * a.ndim)
    qn, kn = q_norm.reshape(1, -1), kv_norm.reshape(1, -1)
    return pl.pallas_call(
        _mla_prep_kernel,
        grid=(t // TM,),
        in_specs=[tok(512), full(qn), full(kn), full(wq), full(wk), full(wv), tok(768)],
        out_specs=[tok(512), tok(512), tok(512)],
        out_shape=[jax.ShapeDtypeStruct((t, 512), BF16)] * 3,
        compiler_params=_cparams(("parallel",)),
        name="mla_prep",
    )(mc, qn, kn, wq, wk, wv, tab)


def _fox_cum_kernel(gt_ref, b_ref, q_ref, k_ref, qo_ref, ko_ref, carry_s, *, blk):
    @pl.when(pl.program_id(1) == 0)
    def _():
        carry_s[...] = jnp.zeros_like(carry_s)

    x = gt_ref[...] + b_ref[...]
    lf = -_softplus(-x)
    ri = lax.broadcasted_iota(I32, (blk, blk), 0)
    ci = lax.broadcasted_iota(I32, (blk, blk), 1)
    cs = jnp.dot((ri >= ci).astype(F32), lf, precision=HIGHEST, preferred_element_type=F32) + carry_s[0:1, :]
    carry_s[0:1, :] = cs[blk - 1:blk, :]
    f2 = cs * LOG2E
    lane = lax.broadcasted_iota(I32, (1, HEAD_W), 1)

    def pieces(f, base):
        hi = f.astype(BF16).astype(F32)
        mid = (f - hi).astype(BF16).astype(F32)
        lo = f - hi - mid
        return jnp.where(lane == base, hi, jnp.where(lane == base + 1, mid, jnp.where(lane == base + 2, lo, 0.0)))

    def ones(base):
        return jnp.where((lane >= base) & (lane < base + 3), 1.0, 0.0)

    for h in range(FOX_HEADS):
        fh = f2[:, G_FOX + h:G_FOX + h + 1]
        sl = slice(h * HEAD_W, (h + 1) * HEAD_W)
        qo_ref[:, sl] = (q_ref[:, sl].astype(F32) + pieces(fh, FOX_F_Q) + ones(FOX_ONE_Q)).astype(BF16)
        ko_ref[:, sl] = (k_ref[:, sl].astype(F32) - pieces(fh, FOX_F_K) + ones(FOX_ONE_K)).astype(BF16)


def _fox_cum(gt, f_bias, fq, fk, b, s):
    blk = 512
    bias = jnp.zeros((1, LANES), F32).at[0, G_FOX:G_FOX + FOX_HEADS].set(f_bias)
    nb = s // blk
    tok = lambda n: pl.BlockSpec((blk, n), lambda i, j: (i * nb + j, 0))
    w = FOX_HEADS * HEAD_W
    return pl.pallas_call(
        functools.partial(_fox_cum_kernel, blk=blk),
        grid=(b, nb),
        in_specs=[tok(LANES), pl.BlockSpec((1, LANES), lambda i, j: (0, 0)), tok(w), tok(w)],
        out_specs=[tok(w), tok(w)],
        out_shape=[jax.ShapeDtypeStruct((b * s, w), BF16)] * 2,
        scratch_shapes=[pltpu.VMEM((8, LANES), F32)],
        compiler_params=_cparams(("parallel", "arbitrary")),
        name="fox_cum",
    )(gt, bias, fq, fk)


def _ssd_kernel(zx_ref, gt_ref, cw_ref, cb_ref, pr_ref, pc_ref, drow_ref, nw_ref, o_ref, ext_s, st_s, y_s, *, q):
    @pl.when(pl.program_id(1) == 0)
    def _():
        ext_s[0:8, :] = jnp.zeros((8, SSD_CONV_DIM), F32)
        st_s[...] = jnp.zeros_like(st_s)

    raw = zx_ref[:, 256:1024]
    ext_s[8:8 + q, :] = raw
    acc = jnp.broadcast_to(cb_ref[...], (q, SSD_CONV_DIM))
    for j in range(SSD_CONV):
        acc = acc + cw_ref[j:j + 1, :] * ext_s[5 + j:5 + j + q, :]
    ext_s[0:8, :] = raw[q - 8:q, :]
    xbc = _silu(acc)
    xs = xbc[:, 0:SSD_D_INNER]

    g = gt_ref[...]
    lane = lax.broadcasted_iota(I32, (1, LANES), 1)
    dtc = _softplus(g + pr_ref[0:1, :])
    a_r = jnp.where((lane >= G_DT) & (lane < G_DT + SSD_HEADS), -jnp.exp(pr_ref[1:2, :]), 0.0)
    ri = lax.broadcasted_iota(I32, (q, q), 0)
    ci = lax.broadcasted_iota(I32, (q, q), 1)
    tri = ri >= ci
    acs_c = jnp.dot(tri.astype(F32), dtc * a_r, precision=HIGHEST, preferred_element_type=F32)
    sub = lax.broadcasted_iota(I32, (LANES, 1), 0)
    dtr = _softplus(g.T + pc_ref[:, 0:1])
    a_c = jnp.where((sub >= G_DT) & (sub < G_DT + SSD_HEADS), -jnp.exp(pc_ref[:, 1:2]), 0.0)
    acs_r = jnp.dot(dtr * a_c, (ri <= ci).astype(F32), precision=HIGHEST, preferred_element_type=F32)

    rep = SSD_HEADS // SSD_GROUPS
    for gi in range(SSD_GROUPS):
        bg = xbc[:, SSD_D_INNER + gi * SSD_STATE:SSD_D_INNER + (gi + 1) * SSD_STATE]
        cg = xbc[:, SSD_D_INNER + (SSD_GROUPS + gi) * SSD_STATE:SSD_D_INNER + (SSD_GROUPS + gi + 1) * SSD_STATE]
        bt = bg.T.astype(BF16)
        cb16 = cg.astype(BF16)
        cbm = jnp.dot(cb16, bt, preferred_element_type=F32)
        for hh in range(rep):
            h = gi * rep + hh
            ac = acs_c[:, G_DT + h:G_DT + h + 1]
            ar = acs_r[G_DT + h:G_DT + h + 1, :]
            seg = jnp.where(tri, jnp.exp(ac - ar), 0.0)
            xh = xs[:, h * SSD_HEAD_DIM:(h + 1) * SSD_HEAD_DIM]
            xdt = xh * dtc[:, G_DT + h:G_DT + h + 1]
            yd = jnp.dot((cbm * seg).astype(BF16), xdt.astype(BF16), preferred_element_type=F32)
            aend = ac[q - 1:q, :]
            st = st_s[h]
            yo = jnp.dot(cb16, st.astype(BF16), preferred_element_type=F32) * jnp.exp(ac)
            st_s[h] = st * jnp.exp(aend) + jnp.dot(bt, (xdt * jnp.exp(aend - ac)).astype(BF16),
                                                   preferred_element_type=F32)
            y_s[:, h * SSD_HEAD_DIM:(h + 1) * SSD_HEAD_DIM] = yd + yo
    y = y_s[...] + drow_ref[...] * xs
    y = y * _silu(zx_ref[:, 0:SSD_D_INNER])
    o_ref[...] = _rms(y, nw_ref[...]).astype(BF16)


def _ssd(zx, gt, conv_w, conv_b, dt_bias, a_log, d_skip, norm_w, b, s):
    q = SSD_Q
    nc = s // q
    pr = jnp.zeros((8, LANES), F32).at[0, G_DT:G_DT + SSD_HEADS].set(dt_bias).at[1, G_DT:G_DT + SSD_HEADS].set(a_log)
    pc = pr.T
    drow = jnp.repeat(d_skip, SSD_HEAD_DIM).reshape(1, SSD_D_INNER)
    cb = conv_b.reshape(1, -1)
    nw = norm_w.reshape(1, -1)
    full = lambda a: pl.BlockSpec(a.shape, lambda i, j: (0,) * a.ndim)
    return pl.pallas_call(
        functools.partial(_ssd_kernel, q=q),
        grid=(b, nc),
        in_specs=[pl.BlockSpec((q, 1024), lambda i, j: (i * nc + j, 0)), pl.BlockSpec((q, LANES), lambda i, j: (i * nc + j, 0)),
                  full(conv_w), full(cb), full(pr), full(pc), full(drow), full(nw)],
        out_specs=pl.BlockSpec((q, SSD_D_INNER), lambda i, j: (i * nc + j, 0)),
        out_shape=jax.ShapeDtypeStruct((b * s, SSD_D_INNER), BF16),
        scratch_shapes=[pltpu.VMEM((8 + q, SSD_CONV_DIM), F32), pltpu.VMEM((SSD_HEADS, SSD_STATE, SSD_HEAD_DIM), F32),
                        pltpu.VMEM((q, SSD_D_INNER), F32)],
        compiler_params=_cparams(("parallel", "arbitrary")),
        name="ssd",
    )(zx, gt, conv_w, cb, pr, pc, drow, nw)


def _key_to_f32(k):
    return pltpu.bitcast(jnp.where(k < 0, k ^ jnp.int32(0x7FFFFFFF), k), F32)


def _dsa_kernel(iq_ref, gt_ref, q_ref, ik_ref, k_ref, vt_ref, o_ref, sc_s, qi_s, qs_s, thr_s, *, bq, sc_w, n_sel):
    qb = pl.program_id(1)
    n_sc = ((qb + 1) * bq + sc_w - 1) // sc_w
    for j in range(IDX_HEADS):
        qi_s[j // 2, (j % 2) * bq:(j % 2 + 1) * bq, :] = iq_ref[:, j * IDX_DIM:(j + 1) * IDX_DIM]
    for h in range(DSA_HEADS):
        qs_s[h * bq:(h + 1) * bq, :] = q_ref[:, h * DSA_HEAD_DIM:(h + 1) * DSA_HEAD_DIM]
    w_t = (gt_ref[...] * ((IDX_HEADS * IDX_DIM) ** -0.5)).T
    qpos = qb * bq + lax.broadcasted_iota(I32, (1, bq), 1)

    def score_body(sc, carry):
        k0 = pl.multiple_of(sc * sc_w, sc_w)
        kidx = ik_ref[pl.ds(k0, sc_w), :]
        sco = jnp.zeros((sc_w, bq), F32)
        for jp in range(IDX_HEADS // 2):
            a = _dot_nt(kidx, qi_s[jp])
            for u in range(2):
                j = 2 * jp + u
                sco = sco + jnp.maximum(a[:, u * bq:(u + 1) * bq], 0.0) * w_t[G_IW + j:G_IW + j + 1, :]
        kpos = k0 + lax.broadcasted_iota(I32, (sc_w, 1), 0)
        sc_s[sc] = jnp.where(kpos <= qpos, sco, -jnp.inf)
        return carry

    lax.fori_loop(0, n_sc, score_body, 0)

    def count(pred):
        def body(sc, acc):
            hit = jnp.where(pred(sc_s[sc]), 1.0, 0.0)
            return acc + jnp.sum(hit.reshape(sc_w // 64, 64, bq), axis=0)
        acc = lax.fori_loop(0, n_sc, body, jnp.zeros((64, bq), F32))
        return jnp.sum(acc, axis=0, keepdims=True)

    thr_s[0:1, :] = jnp.full((1, bq), -jnp.finfo(jnp.float32).max, F32)

    @pl.when((qb + 1) * bq > n_sel)
    def _():
        active = qpos + 1 > n_sel

        def cond(st):
            it, lo, cnt_lo = st
            pending = active & (cnt_lo != n_sel)
            return (it < 32) & (jnp.max(jnp.where(pending, 1.0, 0.0)) > 0.0)

        def body(st):
            it, lo, cnt_lo = st
            for u in range(4):
                cand = lo ^ jnp.left_shift(jnp.int32(1), 31 - (it + u))
                thr_c = _key_to_f32(cand)
                cnt = count(lambda t: t >= thr_c)
                ok = cnt >= n_sel
                lo, cnt_lo = jnp.where(ok, cand, lo), jnp.where(ok, cnt, cnt_lo)
            return it + 4, lo, cnt_lo

        _, lo, cnt_lo = lax.while_loop(
            cond, body, (jnp.int32(0), jnp.full((1, bq), INT_MIN, I32), jnp.full((1, bq), 3.0e38, F32)))
        thr = _key_to_f32(lo)
        tie = active & (cnt_lo > n_sel)

        @pl.when(jnp.max(jnp.where(tie, 1.0, 0.0)) > 0.0)
        def _():
            need = n_sel - count(lambda t: t > thr)
            ri = lax.broadcasted_iota(I32, (sc_w, sc_w), 0)
            ci = lax.broadcasted_iota(I32, (sc_w, sc_w), 1)
            lower = jnp.where(ri >= ci, 1.0, 0.0).astype(BF16)

            def tie_body(sc, run):
                t = sc_s[sc]
                eq = (t == thr) & tie
                pre = jnp.dot(lower, jnp.where(eq, 1.0, 0.0).astype(BF16), preferred_element_type=F32)
                sc_s[sc] = jnp.where(eq & (run + pre > need), -jnp.inf, t)
                return run + pre[sc_w - 1:sc_w, :]

            lax.fori_loop(0, n_sc, tie_body, jnp.zeros((1, bq), F32))

        thr_s[0:1, :] = jnp.where(active, thr, thr_s[0:1, :])

    thr = thr_s[0:1, :]

    hq = DSA_HEADS * bq

    def att_body(sc, carry):
        m_prev, l_prev, acc = carry
        k0 = pl.multiple_of(sc * sc_w, sc_w)
        st = _dot_nt(k_ref[pl.ds(k0, sc_w), :], qs_s[...])
        drop = jnp.where(sc_s[sc] >= thr, 0.0, 2 * NEG_BIG)
        st = st + jnp.tile(drop, (1, DSA_HEADS))
        m_cur = jnp.max(jnp.max(st.reshape(sc_w // 64, 64, hq), axis=0), axis=0, keepdims=True)
        m_next = jnp.maximum(m_prev, m_cur)
        p = jnp.exp2(st - m_next)
        alpha = jnp.exp2(m_prev - m_next)
        l_cur = jnp.sum(jnp.sum(p.reshape(sc_w // 64, 64, hq), axis=0), axis=0, keepdims=True)
        pv = jnp.dot(vt_ref[sc], p.astype(BF16), preferred_element_type=F32)
        return m_next, alpha * l_prev + l_cur, alpha * acc + pv

    _, l_fin, acc = lax.fori_loop(0, n_sc, att_body, (jnp.full((1, hq), NEG_BIG, F32), jnp.zeros((1, hq), F32),
                                                      jnp.zeros((DSA_HEAD_DIM, hq), F32)))
    out_t = jnp.concatenate([acc / l_fin, jnp.zeros((LANES - DSA_HEAD_DIM, hq), F32)], axis=0).T
    for h in range(DSA_HEADS):
        o_ref[:, h * DSA_HEAD_DIM:(h + 1) * DSA_HEAD_DIM] = out_t[h * bq:(h + 1) * bq, 0:DSA_HEAD_DIM].astype(BF16)


def _dsa(iq, gt, dq, ik, dk, dvt, b, s, n_sel):
    bq, sc_w = DSA_BQ, DSA_SC
    nq = s // bq
    qblk = lambda n: pl.BlockSpec((bq, n), lambda i, j: (i * nq + j, 0))
    kblk = pl.BlockSpec((s, 64), lambda i, j: (i, 0))
    return pl.pallas_call(
        functools.partial(_dsa_kernel, bq=bq, sc_w=sc_w, n_sel=n_sel),
        grid=(b, nq),
        in_specs=[qblk(512), qblk(LANES), qblk(256), kblk, kblk,
                  pl.BlockSpec((s // sc_w, DSA_HEAD_DIM, sc_w), lambda i, j: (i, 0, 0))],
        out_specs=qblk(256),
        out_shape=jax.ShapeDtypeStruct((b * s, 256), BF16),
        scratch_shapes=[pltpu.VMEM((s // sc_w, sc_w, bq), F32), pltpu.VMEM((IDX_HEADS // 2, 2 * bq, IDX_DIM), BF16),
                        pltpu.VMEM((DSA_HEADS * bq, DSA_HEAD_DIM), BF16), pltpu.VMEM((8, bq), F32)],
        compiler_params=_cparams(("parallel", "arbitrary")),
        name="dsa",
    )(iq, gt, dq, ik, dk, dvt)


def _flash_kernel(qt_ref, kt_ref, q_ref, k_ref, v_ref, o_ref, m_s, acc_s, *, blk):
    t = pl.program_id(1)
    qi, ki = qt_ref[t], kt_ref[t]

    @pl.when(ki == 0)
    def _():
        m_s[...] = jnp.full(m_s.shape, NEG_BIG, F32)
        acc_s[...] = jnp.zeros_like(acc_s)

    def step(diagonal):
        if diagonal:
            keep = lax.broadcasted_iota(I32, (blk, blk), 0) >= lax.broadcasted_iota(I32, (blk, blk), 1)
        for h in range(FA_HEADS):
            sl = slice(h * HEAD_W, (h + 1) * HEAD_W)
            s = _dot_nt(q_ref[:, sl], k_ref[:, sl])
            if diagonal:
                s = jnp.where(keep, s, 2 * NEG_BIG)
            m_prev = m_s[h]
            m_next = jnp.maximum(m_prev, jnp.max(s, axis=1, keepdims=True))
            p = jnp.exp2(s - jnp.tile(m_next, (1, blk // LANES)))
            pv = jnp.dot(p.astype(BF16), v_ref[:, sl], preferred_element_type=F32)
            acc_s[h] = jnp.exp2(m_prev - m_next) * acc_s[h] + pv
            m_s[h] = m_next

    @pl.when(ki < qi)
    def _():
        step(False)

    @pl.when(ki == qi)
    def _():
        step(True)
        for h in range(FA_HEADS):
            acc = acc_s[h]
            o_ref[:, h * 64:(h + 1) * 64] = (acc / pltpu.roll(acc, 64, 1))[:, 0:64].astype(BF16)


def _flash(q, k, v, b, s, name):
    blk = FA_B
    nb = s // blk
    pairs = [(i, j) for i in range(nb) for j in range(i + 1)]
    qt = jnp.asarray([p[0] for p in pairs], I32)
    kt = jnp.asarray([p[1] for p in pairs], I32)
    w = FA_HEADS * HEAD_W
    grid_spec = pltpu.PrefetchScalarGridSpec(
        num_scalar_prefetch=2,
        grid=(b, len(pairs)),
        in_specs=[pl.BlockSpec((blk, w), lambda i, t, qt, kt: (i * nb + qt[t], 0)),
                  pl.BlockSpec((blk, w), lambda i, t, qt, kt: (i * nb + kt[t], 0)),
                  pl.BlockSpec((blk, w), lambda i, t, qt, kt: (i * nb + kt[t], 0))],
        out_specs=pl.BlockSpec((blk, FA_HEADS * 64), lambda i, t, qt, kt: (i * nb + qt[t], 0)),
        scratch_shapes=[pltpu.VMEM((FA_HEADS, blk, LANES), F32), pltpu.VMEM((FA_HEADS, blk, HEAD_W), F32)],
    )
    return pl.pallas_call(
        functools.partial(_flash_kernel, blk=blk),
        grid_spec=grid_spec,
        out_shape=jax.ShapeDtypeStruct((b * s, FA_HEADS * 64), BF16),
        compiler_params=_cparams(("parallel", "arbitrary")),
        name=name,
    )(qt, kt, q, k, v)


def _outproj_kernel(x_ref, y0_ref, y1_ref, y2_ref, y3_ref, wo_ref, g_ref, wr_ref, br_ref,
                    xo_ref, h_ref, cw_ref, rs_ref):
    acc = x_ref[...]
    for i, y_ref in enumerate((y0_ref, y1_ref, y2_ref, y3_ref)):
        acc = acc + jnp.dot(y_ref[...], wo_ref[i * 256:(i + 1) * 256, :], preferred_element_type=F32)
    xo_ref[...] = acc
    h2 = _rms(acc, g_ref[...])
    h_ref[...] = h2.astype(BF16)
    lg = jnp.dot(h2, wr_ref[...], precision=HIGHEST, preferred_element_type=F32) + br_ref[...]
    lane = lax.broadcasted_iota(I32, (1, LANES), 1)
    lanef = lane.astype(F32)
    is_g = (lane >= MOE_EXPERTS) & (lane < MOE_EXPERTS + MOE_GROUPS)
    gl = jnp.where(is_g, lg, -jnp.inf)
    gmax = jnp.max(gl, axis=1, keepdims=True)
    gidx = jnp.min(jnp.where(gl == gmax, lanef, 999.0), axis=1, keepdims=True) - MOE_EXPERTS
    g_p = 1.0 / jnp.sum(jnp.where(is_g, jnp.exp(gl - gmax), 0.0), axis=1, keepdims=True)
    in_g = (lane < MOE_EXPERTS) & (jnp.floor(lanef * (1.0 / MOE_EPG)) == gidx)
    el = jnp.where(in_g, lg, -jnp.inf)
    m1 = jnp.max(el, axis=1, keepdims=True)
    i1 = jnp.min(jnp.where(el == m1, lanef, 999.0), axis=1, keepdims=True)
    el2 = jnp.where(lanef == i1, -jnp.inf, el)
    m2 = jnp.max(el2, axis=1, keepdims=True)
    i2 = jnp.min(jnp.where(el2 == m2, lanef, 999.0), axis=1, keepdims=True)
    t = jnp.exp(m2 - m1)
    w1 = 1.0 / (1.0 + t)
    cw_ref[...] = jnp.where(lanef == i1, w1 * g_p, jnp.where(lanef == i2, t * w1 * g_p, 0.0))
    rs_ref[...] = jnp.where((lanef == i1) | (lanef == i2), 1.0, 0.0).astype(BF16)


def _outproj(x, ys, w_out, g, w_rg, b_rg, w_re, b_re):
    t = x.shape[0]
    wo = w_out.astype(BF16)
    wr = jnp.zeros((D_MODEL, LANES), F32).at[:, 0:MOE_EXPERTS].set(w_re).at[:, MOE_EXPERTS:MOE_EXPERTS + MOE_GROUPS].set(w_rg)
    br = jnp.zeros((1, LANES), F32).at[0, 0:MOE_EXPERTS].set(b_re).at[0, MOE_EXPERTS:MOE_EXPERTS + MOE_GROUPS].set(b_rg)
    tok = lambda n: pl.BlockSpec((TM, n), lambda i: (i, 0))
    full = lambda a: pl.BlockSpec(a.shape, lambda i: (0,) * a.ndim)
    return pl.pallas_call(
        _outproj_kernel,
        grid=(t // TM,),
        in_specs=[tok(D_MODEL)] + [tok(256)] * 4 + [full(wo), full(g), full(wr), full(br)],
        out_specs=[tok(D_MODEL), tok(D_MODEL), tok(LANES), tok(LANES)],
        out_shape=[jax.ShapeDtypeStruct((t, D_MODEL), F32), jax.ShapeDtypeStruct((t, D_MODEL), BF16),
                   jax.ShapeDtypeStruct((t, LANES), F32), jax.ShapeDtypeStruct((t, LANES), BF16)],
        compiler_params=_cparams(("parallel",)),
        name="outproj_router",
    )(x, *ys, wo, g, wr, br)


def _moe_plan_kernel(rs_ref, cw_ref, meta_ref, col_ref, row_ref, *, n):
    sel = rs_ref[...]
    self32 = sel.astype(F32)
    ti = lax.broadcasted_iota(I32, (n, n), 0)
    tj = lax.broadcasted_iota(I32, (n, n), 1)
    rank = jnp.dot(jnp.where(ti > tj, 1.0, 0.0).astype(BF16), sel, preferred_element_type=F32)
    cnt = jnp.sum(self32, axis=0, keepdims=True).astype(I32)
    cpad = ((cnt + (MOE_ALIGN - 1)) // MOE_ALIGN) * MOE_ALIGN
    li = lax.broadcasted_iota(I32, (LANES, LANES), 0)
    lj = lax.broadcasted_iota(I32, (LANES, LANES), 1)
    off = jnp.dot(jnp.broadcast_to(cpad.astype(F32), (8, LANES)), (li < lj).astype(F32),
                  precision=HIGHEST, preferred_element_type=F32)[0:1, :]
    dest = off + rank
    lane = lax.broadcasted_iota(I32, (1, LANES), 1)
    lanef = lane.astype(F32)
    on = self32 > 0.5
    e_lo = jnp.min(jnp.where(on, lanef, 999.0), axis=1, keepdims=True)
    e_hi = jnp.max(jnp.where(on, lanef, -1.0), axis=1, keepdims=True)
    is_lo, is_hi = lanef == e_lo, lanef == e_hi
    cw = cw_ref[...]
    pick = lambda m, a: jnp.sum(jnp.where(m, a, 0.0), axis=1, keepdims=True)
    d_lo, d_hi, w_lo, w_hi = pick(is_lo, dest), pick(is_hi, dest), pick(is_lo, cw), pick(is_hi, cw)
    colv = jnp.where(lane == 0, d_lo, jnp.where(lane == 1, d_hi, jnp.where(lane == 2, w_lo, jnp.where(lane == 3, w_hi, 0.0))))
    col_ref[...] = colv
    row_ref[0] = colv.T[0:8, :]
    sub = lax.broadcasted_iota(I32, (8, LANES), 0)
    meta_ref[0] = jnp.where(sub == 0, off.astype(I32), jnp.where(sub == 1, cnt, 0))


def _moe_kernel(meta_ref, h_ref, col_ref, row_ref, wg_ref, wu_ref, wd_ref, o_ref, s_s, y_s, *, n, ns):
    t, e = pl.program_id(0), pl.program_id(1)

    @pl.when(e == 0)
    def _():
        d_lo, d_hi = row_ref[0, 0:1, :], row_ref[0, 1:2, :]
        for r in range(ns // LANES):
            si = (r * LANES + lax.broadcasted_iota(I32, (LANES, 1), 0)).astype(F32)
            perm = jnp.where((si == d_lo) | (si == d_hi), 1.0, 0.0).astype(BF16)
            s_s[r * LANES:(r + 1) * LANES, :] = jnp.dot(perm, h_ref[...], preferred_element_type=F32).astype(BF16)
        y_s[...] = jnp.zeros_like(y_s)

    off = meta_ref[t * 2 * MOE_EXPERTS + e]
    cnt = meta_ref[t * 2 * MOE_EXPERTS + MOE_EXPERTS + e]

    def body(i, carry):
        r0 = pl.multiple_of(off + i * MOE_CH, MOE_ALIGN)
        xs = s_s[pl.ds(r0, MOE_CH), :]
        gate = jnp.dot(xs, wg_ref[0], preferred_element_type=F32)
        up = jnp.dot(xs, wu_ref[0], preferred_element_type=F32)
        hid = (_silu(gate) * up).astype(BF16)
        y_s[pl.ds(r0, MOE_CH), :] = jnp.dot(hid, wd_ref[0], preferred_element_type=F32).astype(BF16)
        return carry

    lax.fori_loop(0, (cnt + MOE_CH - 1) // MOE_CH, body, 0)

    @pl.when(e == pl.num_programs(1) - 1)
    def _():
        rows, kc = 256, ns // 3
        for c in range(n // rows):
            cv = col_ref[c * rows:(c + 1) * rows, :]
            d_lo, d_hi, w_lo, w_hi = cv[:, 0:1], cv[:, 1:2], cv[:, 2:3], cv[:, 3:4]
            acc = jnp.zeros((rows, D_MODEL), F32)
            for r in range(ns // kc):
                si = (r * kc + lax.broadcasted_iota(I32, (1, kc), 1)).astype(F32)
                pw = (jnp.where(si == d_lo, w_lo, 0.0) + jnp.where(si == d_hi, w_hi, 0.0)).astype(BF16)
                acc = acc + jnp.dot(pw, y_s[r * kc:(r + 1) * kc, :], preferred_element_type=F32)
            o_ref[c * rows:(c + 1) * rows, :] = acc.astype(BF16)


def _moe(h2, cw, rs, w_gate, w_up, w_down):
    t = h2.shape[0]
    n, ns = MOE_N, MOE_NS
    nt = t // n
    meta, col, row = pl.pallas_call(
        functools.partial(_moe_plan_kernel, n=n),
        grid=(nt,),
        in_specs=[pl.BlockSpec((n, LANES), lambda i: (i, 0)), pl.BlockSpec((n, LANES), lambda i: (i, 0))],
        out_specs=[pl.BlockSpec((1, 8, LANES), lambda i: (i, 0, 0)), pl.BlockSpec((n, LANES), lambda i: (i, 0)),
                   pl.BlockSpec((1, 8, n), lambda i: (i, 0, 0))],
        out_shape=[jax.ShapeDtypeStruct((nt, 8, LANES), I32), jax.ShapeDtypeStruct((t, LANES), F32),
                   jax.ShapeDtypeStruct((nt, 8, n), F32)],
        compiler_params=_cparams(("parallel",)),
        name="moe_plan",
    )(rs, cw)
    meta_flat = meta[:, 0:2, 0:MOE_EXPERTS].reshape(-1)
    wg, wu, wd = w_gate.astype(BF16), w_up.astype(BF16), w_down.astype(BF16)
    grid_spec = pltpu.PrefetchScalarGridSpec(
        num_scalar_prefetch=1,
        grid=(nt, MOE_EXPERTS),
        in_specs=[pl.BlockSpec((n, D_MODEL), lambda i, e, m: (i, 0)),
                  pl.BlockSpec((n, LANES), lambda i, e, m: (i, 0)),
                  pl.BlockSpec((1, 8, n), lambda i, e, m: (i, 0, 0)),
                  pl.BlockSpec((1, D_MODEL, MOE_FF), lambda i, e, m: (e, 0, 0)),
                  pl.BlockSpec((1, D_MODEL, MOE_FF), lambda i, e, m: (e, 0, 0)),
                  pl.BlockSpec((1, MOE_FF, D_MODEL), lambda i, e, m: (e, 0, 0))],
        out_specs=pl.BlockSpec((n, D_MODEL), lambda i, e, m: (i, 0)),
        scratch_shapes=[pltpu.VMEM((ns, D_MODEL), BF16), pltpu.VMEM((ns, D_MODEL), BF16)],
    )
    return pl.pallas_call(
        functools.partial(_moe_kernel, n=n, ns=ns),
        grid_spec=grid_spec,
        out_shape=jax.ShapeDtypeStruct((t, D_MODEL), BF16),
        compiler_params=_cparams(("parallel", "arbitrary")),
        name="moe_experts",
    )(meta_flat, h2, col, row, wg, wu, wd)


def _final_kernel(x_ref, y_ref, g_ref, o_ref):
    o_ref[...] = _rms(x_ref[...] + y_ref[...].astype(F32), g_ref[...])


def _final(x, y, g):
    t = x.shape[0]
    tok = pl.BlockSpec((TM, D_MODEL), lambda i: (i, 0))
    return pl.pallas_call(
        _final_kernel,
        grid=(t // TM,),
        in_specs=[tok, tok, pl.BlockSpec((1, D_MODEL), lambda i: (0, 0))],
        out_specs=tok,
        out_shape=jax.ShapeDtypeStruct((t, D_MODEL), F32),
        compiler_params=_cparams(("parallel",)),
        name="final_norm",
    )(x, y, g)


def kernel(x, positions, norm_mix, w_in, ssd_conv_w, ssd_conv_b, ssd_dt_bias, ssd_a_log, ssd_d, ssd_norm, mla_q_norm, mla_w_uq, mla_kv_norm, mla_w_ukv, fox_f_bias, w_out, norm_ffn, router_group_w, router_group_b, router_expert_w, router_expert_b, expert_w_gate, expert_w_up, expert_w_down, final_norm):
    b, s, d = x.shape
    assert d == D_MODEL and s % FA_B == 0 and (b * s) % MOE_N == 0 and TM == DSA_SC
    depth = w_in.shape[0]
    n_sel = min(IDX_TOPK_MAX, s // 4)
    tab = _rope_tables(positions)
    xf = x.reshape(b * s, d)
    y_ffn = None
    for l in range(depth):
        xf, p = _inproj(xf, y_ffn, norm_mix[l].reshape(1, d), _pack_w_in(w_in[l]), tab)
        mq, mk, mv = _mla_prep(p["mc"], mla_q_norm[l], mla_kv_norm[l], mla_w_uq[l], mla_w_ukv[l], tab)
        fq, fk = _fox_cum(p["gt"], fox_f_bias[l], p["fq"], p["fk"], b, s)
        y_ssd = _ssd(p["zx"], p["gt"], ssd_conv_w[l], ssd_conv_b[l], ssd_dt_bias[l], ssd_a_log[l], ssd_d[l],
                     ssd_norm[l], b, s)
        y_dsa = _dsa(p["iq"], p["gt"], p["dq"], p["ik"], p["dk"], p["dv"], b, s, n_sel)
        y_mla = _flash(mq, mk, mv, b, s, "flash_mla")
        y_fox = _flash(fq, fk, p["fv"], b, s, "flash_fox")
        xf, h2, cw, rs = _outproj(xf, (y_ssd, y_dsa, y_mla, y_fox), w_out[l], norm_ffn[l].reshape(1, d),
                                  router_group_w[l], router_group_b[l], router_expert_w[l], router_expert_b[l])
        y_ffn = _moe(h2, cw, rs, expert_w_gate[l], expert_w_up[l], expert_w_down[l])
    return _final(xf, y_ffn, final_norm.reshape(1, d)).reshape(b, s, d)
```

```python
import functools
import math

import jax
import jax.numpy as jnp
from jax import lax
from jax.experimental import pallas as pl
from jax.experimental.pallas import tpu as pltpu

F32 = jnp.float32
BF16 = jnp.bfloat16
I32 = jnp.int32
EPS = 1e-6
ROPE_THETA = 500000.0
HIGHEST = lax.Precision.HIGHEST

D_MODEL = 1024
SSD_HEADS, SSD_HEAD_DIM, SSD_GROUPS, SSD_STATE, SSD_CONV = 4, 64, 2, 128, 4
SSD_D_INNER = SSD_HEADS * SSD_HEAD_DIM
SSD_CONV_DIM = SSD_D_INNER + 2 * SSD_GROUPS * SSD_STATE
DSA_HEADS, DSA_HEAD_DIM, IDX_HEADS, IDX_DIM, IDX_TOPK_MAX = 4, 64, 8, 64, 256
MLA_HEADS, MLA_Q_RANK, MLA_KV_RANK, MLA_NOPE, MLA_ROPE, MLA_V = 4, 256, 128, 64, 32, 64
FOX_HEADS, FOX_HEAD_DIM = 4, 64
MOE_GROUPS, MOE_EPG, MOE_EXPERTS, MOE_FF = 4, 8, 32, 256

LANES = 128
VMEM_LIMIT = 56 * 1024 * 1024
INT_MIN = -(2 ** 31)
NEG_BIG = -1e30

TM = 512
SSD_Q = 256
DSA_BQ = 128
DSA_SC = 512
FA_B = 512
MOE_N = 1024
MOE_ALIGN = 16
MOE_CH = 128
MOE_EPS = 4
MOE_NS = 2 * MOE_N + MOE_EXPERTS * MOE_ALIGN + MOE_CH

C_ZX, C_GT, C_DQ, C_KV, C_IK, C_IQ, C_MC, C_FQ, C_FK, C_FV, C_END = (
    0, 1024, 1152, 1408, 1536, 1664, 2176, 2688, 3200, 3712, 4224)
G_FOX, G_DT, G_IW = 0, 4, 8
LOG2E = math.log2(math.e)
HEAD_W = 128
FA_HEADS = 4
FOX_ONE_Q, FOX_F_Q, FOX_F_K, FOX_ONE_K = 64, 67, 64, 67


def _cparams(sem):
    return pltpu.CompilerParams(dimension_semantics=sem, vmem_limit_bytes=VMEM_LIMIT)


def _rms(xf, g):
    return xf * lax.rsqrt(jnp.mean(xf * xf, axis=-1, keepdims=True) + EPS) * g


def _softplus(x):
    return jnp.maximum(x, 0.0) + jnp.log1p(jnp.exp(-jnp.abs(x)))


def _silu(x):
    return x * jax.nn.sigmoid(x)


def _rope_apply(a, c, sa, sb, half):
    return a * c + pltpu.roll(a, LANES - half, 1) * sa + pltpu.roll(a, half, 1) * sb


def _dot_nt(a, b):
    return lax.dot_general(a, b, (((1,), (1,)), ((), ())), preferred_element_type=F32)


def _split3(x):
    hi = x.astype(BF16)
    r = x - hi.astype(F32)
    mid = r.astype(BF16)
    return hi, mid, (r - mid.astype(F32)).astype(BF16)


def _dot_f32_by_mask(mask, x, mask_left):
    m = jnp.where(mask, 1.0, 0.0).astype(BF16)
    out = None
    for piece in _split3(x):
        d = jnp.dot(m, piece, preferred_element_type=F32) if mask_left else jnp.dot(piece, m, preferred_element_type=F32)
        out = d if out is None else out + d
    return out


def _rope_tab_kernel(pos_ref, inv_ref, o_ref):
    pos = pos_ref[...].astype(F32)
    lane = lax.broadcasted_iota(I32, (1, LANES), 1)
    ang = pos * inv_ref[0:1, :]
    c, s = jnp.cos(ang), jnp.sin(ang)
    jd = lane % 64
    o_ref[:, 0:128] = c
    o_ref[:, 128:256] = jnp.where(jd < 8, -s, 0.0)
    o_ref[:, 256:384] = jnp.where((jd >= 8) & (jd < 16), s, 0.0)
    ang = pos * inv_ref[1:2, :]
    c, s = jnp.cos(ang), jnp.sin(ang)
    jm = lane - 64
    o_ref[:, 384:512] = c
    o_ref[:, 512:640] = jnp.where((jm >= 0) & (jm < 16), -s, 0.0)
    o_ref[:, 640:768] = jnp.where((jm >= 16) & (jm < 32), s, 0.0)


def _rope_tables(positions):
    t = positions.size
    half_d, half_m = DSA_HEAD_DIM // 4 // 2, MLA_ROPE // 2
    inv_d = ROPE_THETA ** (-jnp.arange(half_d, dtype=F32) / half_d)
    inv_m = ROPE_THETA ** (-jnp.arange(half_m, dtype=F32) / half_m)
    lane = jnp.arange(LANES)
    jd = lane % 64
    row_d = jnp.where(jd < 2 * half_d, inv_d[jd % half_d], 0.0)
    jm = lane - 64
    row_m = jnp.where((jm >= 0) & (jm < 2 * half_m), inv_m[jm % half_m], 0.0)
    inv = jnp.zeros((8, LANES), F32).at[0].set(row_d).at[1].set(row_m)
    tm = 1024
    return pl.pallas_call(
        _rope_tab_kernel,
        grid=(t // tm,),
        in_specs=[pl.BlockSpec((tm, 1), lambda i: (i, 0)), pl.BlockSpec((8, LANES), lambda i: (0, 0))],
        out_specs=pl.BlockSpec((tm, 768), lambda i: (i, 0)),
        out_shape=jax.ShapeDtypeStruct((t, 768), F32),
        compiler_params=_cparams(("parallel",)),
        name="rope_tables",
    )(positions.reshape(t, 1), inv)


def _inproj_kernel(*refs, has_add):
    if has_add:
        (x_ref, yp_ref, g_ref, w_ref, tab_ref, xo_ref, zx_ref, gt_ref, dq_ref, dk_ref, dv_ref,
         ik_ref, iq_ref, mc_ref, fq_ref, fk_ref, fv_ref) = refs
        x = x_ref[...] + yp_ref[...].astype(F32)
        xo_ref[...] = x
    else:
        (x_ref, g_ref, w_ref, tab_ref, zx_ref, gt_ref, dq_ref, dk_ref, dv_ref,
         ik_ref, iq_ref, mc_ref, fq_ref, fk_ref, fv_ref) = refs
        x = x_ref[...]
    h = _rms(x, g_ref[...]).astype(BF16)

    def proj(a, b):
        return jnp.dot(h, w_ref[:, a:b], preferred_element_type=F32)

    zx_ref[...] = proj(C_ZX, C_GT)
    gt_ref[...] = proj(C_GT, C_DQ)
    c_d, sa_d, sb_d = tab_ref[:, 0:128], tab_ref[:, 128:256], tab_ref[:, 256:384]
    c_m, sa_m, sb_m = tab_ref[:, 384:512], tab_ref[:, 512:640], tab_ref[:, 640:768]
    q = proj(C_DQ, C_KV)
    for i in range(2):
        dq_ref[:, i * 128:(i + 1) * 128] = _rope_apply(q[:, i * 128:(i + 1) * 128], c_d, sa_d, sb_d, 8).astype(BF16)
    kv = proj(C_KV, C_IK)
    is_k = lax.broadcasted_iota(I32, (1, LANES), 1) < 64
    kvr = _rope_apply(kv, jnp.where(is_k, c_d, 1.0), jnp.where(is_k, sa_d, 0.0), jnp.where(is_k, sb_d, 0.0), 8)
    dk_ref[...] = kvr[:, 0:64].astype(BF16)
    dv_ref[0] = kvr.T[64:128, :].astype(BF16)
    ikb = proj(C_IK, C_IQ)
    ik_ref[...] = _rope_apply(ikb, c_d, sa_d, sb_d, 8)[:, 0:64].astype(BF16)
    iq = proj(C_IQ, C_MC)
    for i in range(4):
        iq_ref[:, i * 128:(i + 1) * 128] = _rope_apply(iq[:, i * 128:(i + 1) * 128], c_d, sa_d, sb_d, 8).astype(BF16)
    mc = proj(C_MC, C_FQ)
    mc_ref[:, 0:384] = mc[:, 0:384]
    mc_ref[:, 384:512] = _rope_apply(mc[:, 384:512], c_m, sa_m, sb_m, 16)
    fq_ref[...] = proj(C_FQ, C_FK).astype(BF16)
    fk_ref[...] = proj(C_FK, C_FV).astype(BF16)
    ones_hi = jnp.where(lax.broadcasted_iota(I32, (1, FA_HEADS * HEAD_W), 1) % HEAD_W >= 64, 1.0, 0.0)
    fv_ref[...] = (proj(C_FV, C_END) + ones_hi).astype(BF16)


def _pack_w_in(w):
    d = w.shape[0]
    z = lambda n: jnp.zeros((d, n), F32)
    o = 0
    parts = {}
    for name, width in (("z", 256), ("xbc", 768), ("dt", 4), ("dq", 256), ("dk", 64), ("dv", 64), ("iq", 512),
                        ("ik", 64), ("iw", 8), ("cq", 256), ("ckv", 128), ("kr", 32), ("fq", 256), ("fk", 256),
                        ("fv", 256), ("ff", 4)):
        parts[name] = w[:, o:o + width]
        o += width
    scale = DSA_HEAD_DIM ** -0.5

    def per_head(a):
        return jnp.pad(a.reshape(d, FA_HEADS, 64), ((0, 0), (0, 0), (0, HEAD_W - 64))).reshape(d, FA_HEADS * HEAD_W)

    cat = jnp.concatenate([
        parts["z"], parts["xbc"],
        parts["ff"], parts["dt"], parts["iw"], z(112),
        parts["dq"] * (scale * LOG2E),
        parts["dk"], parts["dv"],
        parts["ik"], z(64),
        parts["iq"],
        parts["cq"], parts["ckv"], z(64), parts["kr"], z(32),
        per_head(parts["fq"] * (FOX_HEAD_DIM ** -0.5 * LOG2E)), per_head(parts["fk"]), per_head(parts["fv"])], axis=1)
    return cat.astype(BF16)


def _inproj(x, y_prev, g, w_cat, tab):
    t = x.shape[0]
    has_add = y_prev is not None
    tok = lambda n: pl.BlockSpec((TM, n), lambda i: (i, 0))
    full = lambda a: pl.BlockSpec(a.shape, lambda i: (0,) * a.ndim)
    ins = [x] + ([y_prev] if has_add else []) + [g, w_cat, tab]
    in_specs = [tok(D_MODEL)] + ([tok(D_MODEL)] if has_add else []) + [full(g), full(w_cat), tok(768)]
    outs = ([("x", D_MODEL, F32)] if has_add else []) + [
        ("zx", 1024, F32), ("gt", 128, F32), ("dq", 256, BF16), ("dk", 64, BF16), ("dv", 64, BF16),
        ("ik", 64, BF16), ("iq", 512, BF16), ("mc", 512, F32), ("fq", 512, BF16), ("fk", 512, BF16), ("fv", 512, BF16)]
    res = pl.pallas_call(
        functools.partial(_inproj_kernel, has_add=has_add),
        grid=(t // TM,),
        in_specs=in_specs,
        out_specs=[pl.BlockSpec((1, DSA_HEAD_DIM, TM), lambda i: (i, 0, 0)) if nm == "dv" else tok(n)
                   for nm, n, _ in outs],
        out_shape=[jax.ShapeDtypeStruct((t // TM, DSA_HEAD_DIM, TM) if nm == "dv" else (t, n), dt)
                   for nm, n, dt in outs],
        compiler_params=_cparams(("parallel",)),
        name="inproj",
    )(*ins)
    res = list(res)
    x_new = res.pop(0) if has_add else x
    return x_new, dict(zip([n for n, _, _ in outs if n != "x"], res))


def _mla_prep_kernel(mc_ref, qn_ref, kn_ref, wq_ref, wk_ref, wv_ref, tab_ref, q_ref, k_ref, v_ref):
    mc = mc_ref[...]
    cq = _rms(mc[:, 0:256], qn_ref[...]).astype(BF16)
    ckv = _rms(mc[:, 256:384], kn_ref[...]).astype(BF16)
    kr = mc[:, 384:512]
    c_m, sa_m, sb_m = tab_ref[:, 384:512], tab_ref[:, 512:640], tab_ref[:, 640:768]
    q = jnp.dot(cq, wq_ref[...], preferred_element_type=F32)
    k = jnp.dot(ckv, wk_ref[...], preferred_element_type=F32)
    for h in range(MLA_HEADS):
        sl = slice(h * 128, (h + 1) * 128)
        q_ref[:, sl] = _rope_apply(q[:, sl], c_m, sa_m, sb_m, 16).astype(BF16)
        k_ref[:, sl] = (k[:, sl] + kr).astype(BF16)
    ones_hi = jnp.where(lax.broadcasted_iota(I32, (1, FA_HEADS * HEAD_W), 1) % HEAD_W >= 64, 1.0, 0.0)
    v_ref[...] = (jnp.dot(ckv, wv_ref[...], preferred_element_type=F32) + ones_hi).astype(BF16)


def _mla_prep(mc, q_norm, kv_norm, w_uq, w_ukv, tab):
    t = mc.shape[0]
    dqk = MLA_NOPE + MLA_ROPE
    wq = jnp.pad(w_uq.reshape(MLA_Q_RANK, MLA_HEADS, dqk) * (dqk ** -0.5 * LOG2E), ((0, 0), (0, 0), (0, 128 - dqk)))
    wq = wq.reshape(MLA_Q_RANK, MLA_HEADS * 128).astype(BF16)
    wkv = w_ukv.reshape(MLA_KV_RANK, MLA_HEADS, MLA_NOPE + MLA_V)
    wk = jnp.pad(wkv[:, :, :MLA_NOPE], ((0, 0), (0, 0), (0, 128 - MLA_NOPE))).reshape(MLA_KV_RANK, MLA_HEADS * 128).astype(BF16)
    wv = jnp.pad(wkv[:, :, MLA_NOPE:], ((0, 0), (0, 0), (0, 128 - MLA_V))).reshape(MLA_KV_RANK, MLA_HEADS * 128).astype(BF16)
    tok = lambda n: pl.BlockSpec((TM, n), lambda i: (i, 0))
    full = lambda a: pl.BlockSpec(a.shape, lambda i: (0,) * a.ndim)
    qn, kn = q_norm.reshape(1, -1), kv_norm.reshape(1, -1)
    return pl.pallas_call(
        _mla_prep_kernel,
        grid=(t // TM,),
        in_specs=[tok(512), full(qn), full(kn), full(wq), full(wk), full(wv), tok(768)],
        out_specs=[tok(512), tok(512), tok(512)],
        out_shape=[jax.ShapeDtypeStruct((t, 512), BF16)] * 3,
        compiler_params=_cparams(("parallel",)),
        name="mla_prep",
    )(mc, qn, kn, wq, wk, wv, tab)


def _fox_cum_kernel(gt_ref, b_ref, q_ref, k_ref, qo_ref, ko_ref, carry_s, *, blk):
    @pl.when(pl.program_id(1) == 0)
    def _():
        carry_s[...] = jnp.zeros_like(carry_s)

    x = gt_ref[...] + b_ref[...]
    lf = -_softplus(-x)
    ri = lax.broadcasted_iota(I32, (blk, blk), 0)
    ci = lax.broadcasted_iota(I32, (blk, blk), 1)
    cs = _dot_f32_by_mask(ri >= ci, lf, True) + carry_s[0:1, :]
    carry_s[0:1, :] = cs[blk - 1:blk, :]
    f2 = cs * LOG2E
    lane = lax.broadcasted_iota(I32, (1, HEAD_W), 1)

    def pieces(f, base):
        hi = f.astype(BF16).astype(F32)
        mid = (f - hi).astype(BF16).astype(F32)
        lo = f - hi - mid
        return jnp.where(lane == base, hi, jnp.where(lane == base + 1, mid, jnp.where(lane == base + 2, lo, 0.0)))

    def ones(base):
        return jnp.where((lane >= base) & (lane < base + 3), 1.0, 0.0)

    for h in range(FOX_HEADS):
        fh = f2[:, G_FOX + h:G_FOX + h + 1]
        sl = slice(h * HEAD_W, (h + 1) * HEAD_W)
        qo_ref[:, sl] = (q_ref[:, sl].astype(F32) + pieces(fh, FOX_F_Q) + ones(FOX_ONE_Q)).astype(BF16)
        ko_ref[:, sl] = (k_ref[:, sl].astype(F32) - pieces(fh, FOX_F_K) + ones(FOX_ONE_K)).astype(BF16)


def _fox_cum(gt, f_bias, fq, fk, b, s):
    blk = 512
    bias = jnp.zeros((1, LANES), F32).at[0, G_FOX:G_FOX + FOX_HEADS].set(f_bias)
    nb = s // blk
    tok = lambda n: pl.BlockSpec((blk, n), lambda i, j: (i * nb + j, 0))
    w = FOX_HEADS * HEAD_W
    return pl.pallas_call(
        functools.partial(_fox_cum_kernel, blk=blk),
        grid=(b, nb),
        in_specs=[tok(LANES), pl.BlockSpec((1, LANES), lambda i, j: (0, 0)), tok(w), tok(w)],
        out_specs=[tok(w), tok(w)],
        out_shape=[jax.ShapeDtypeStruct((b * s, w), BF16)] * 2,
        scratch_shapes=[pltpu.VMEM((8, LANES), F32)],
        compiler_params=_cparams(("parallel", "arbitrary")),
        name="fox_cum",
    )(gt, bias, fq, fk)


def _ssd_kernel(zx_ref, gt_ref, cw_ref, cb_ref, pr_ref, pc_ref, drow_ref, nw_ref, o_ref, ext_s, st_s, y_s, *, q):
    @pl.when(pl.program_id(1) == 0)
    def _():
        ext_s[0:8, :] = jnp.zeros((8, SSD_CONV_DIM), F32)
        st_s[...] = jnp.zeros_like(st_s)

    raw = zx_ref[:, 256:1024]
    ext_s[8:8 + q, :] = raw
    acc = jnp.broadcast_to(cb_ref[...], (q, SSD_CONV_DIM))
    for j in range(SSD_CONV):
        acc = acc + cw_ref[j:j + 1, :] * ext_s[5 + j:5 + j + q, :]
    ext_s[0:8, :] = raw[q - 8:q, :]
    xbc = _silu(acc)
    xs = xbc[:, 0:SSD_D_INNER]

    g = gt_ref[...]
    lane = lax.broadcasted_iota(I32, (1, LANES), 1)
    dtc = _softplus(g + pr_ref[0:1, :])
    a_r = jnp.where((lane >= G_DT) & (lane < G_DT + SSD_HEADS), -jnp.exp(pr_ref[1:2, :]), 0.0)
    ri = lax.broadcasted_iota(I32, (q, q), 0)
    ci = lax.broadcasted_iota(I32, (q, q), 1)
    tri = ri >= ci
    acs_c = _dot_f32_by_mask(tri, dtc * a_r, True)
    sub = lax.broadcasted_iota(I32, (LANES, 1), 0)
    dtr = _softplus(g.T + pc_ref[:, 0:1])
    a_c = jnp.where((sub >= G_DT) & (sub < G_DT + SSD_HEADS), -jnp.exp(pc_ref[:, 1:2]), 0.0)
    acs_r = _dot_f32_by_mask(ri <= ci, dtr * a_c, False)

    rep = SSD_HEADS // SSD_GROUPS
    for gi in range(SSD_GROUPS):
        bg = xbc[:, SSD_D_INNER + gi * SSD_STATE:SSD_D_INNER + (gi + 1) * SSD_STATE]
        cg = xbc[:, SSD_D_INNER + (SSD_GROUPS + gi) * SSD_STATE:SSD_D_INNER + (SSD_GROUPS + gi + 1) * SSD_STATE]
        bt = bg.T.astype(BF16)
        cb16 = cg.astype(BF16)
        cbm = jnp.dot(cb16, bt, preferred_element_type=F32)
        for hh in range(rep):
            h = gi * rep + hh
            ac = acs_c[:, G_DT + h:G_DT + h + 1]
            ar = acs_r[G_DT + h:G_DT + h + 1, :]
            seg = jnp.where(tri, jnp.exp(ac - ar), 0.0)
            xh = xs[:, h * SSD_HEAD_DIM:(h + 1) * SSD_HEAD_DIM]
            xdt = xh * dtc[:, G_DT + h:G_DT + h + 1]
            yd = jnp.dot((cbm * seg).astype(BF16), xdt.astype(BF16), preferred_element_type=F32)
            aend = ac[q - 1:q, :]
            st = st_s[h]
            yo = jnp.dot(cb16, st.astype(BF16), preferred_element_type=F32) * jnp.exp(ac)
            st_s[h] = st * jnp.exp(aend) + jnp.dot(bt, (xdt * jnp.exp(aend - ac)).astype(BF16),
                                                   preferred_element_type=F32)
            y_s[:, h * SSD_HEAD_DIM:(h + 1) * SSD_HEAD_DIM] = yd + yo
    y = y_s[...] + drow_ref[...] * xs
    y = y * _silu(zx_ref[:, 0:SSD_D_INNER])
    o_ref[...] = _rms(y, nw_ref[...]).astype(BF16)


def _ssd(zx, gt, conv_w, conv_b, dt_bias, a_log, d_skip, norm_w, b, s):
    q = SSD_Q
    nc = s // q
    pr = jnp.zeros((8, LANES), F32).at[0, G_DT:G_DT + SSD_HEADS].set(dt_bias).at[1, G_DT:G_DT + SSD_HEADS].set(a_log)
    pc = pr.T
    drow = jnp.repeat(d_skip, SSD_HEAD_DIM).reshape(1, SSD_D_INNER)
    cb = conv_b.reshape(1, -1)
    nw = norm_w.reshape(1, -1)
    full = lambda a: pl.BlockSpec(a.shape, lambda i, j: (0,) * a.ndim)
    return pl.pallas_call(
        functools.partial(_ssd_kernel, q=q),
        grid=(b, nc),
        in_specs=[pl.BlockSpec((q, 1024), lambda i, j: (i * nc + j, 0)), pl.BlockSpec((q, LANES), lambda i, j: (i * nc + j, 0)),
                  full(conv_w), full(cb), full(pr), full(pc), full(drow), full(nw)],
        out_specs=pl.BlockSpec((q, SSD_D_INNER), lambda i, j: (i * nc + j, 0)),
        out_shape=jax.ShapeDtypeStruct((b * s, SSD_D_INNER), BF16),
        scratch_shapes=[pltpu.VMEM((8 + q, SSD_CONV_DIM), F32), pltpu.VMEM((SSD_HEADS, SSD_STATE, SSD_HEAD_DIM), F32),
                        pltpu.VMEM((q, SSD_D_INNER), F32)],
        compiler_params=_cparams(("parallel", "arbitrary")),
        name="ssd",
    )(zx, gt, conv_w, cb, pr, pc, drow, nw)


def _key_to_f32(k):
    return pltpu.bitcast(jnp.where(k < 0, k ^ jnp.int32(0x7FFFFFFF), k), F32)


def _dsa_kernel(iq_ref, gt_ref, q_ref, ik_ref, k_ref, vt_ref, o_ref, sc_s, qi_s, qs_s, thr_s, *, bq, sc_w, n_sel):
    qb = pl.program_id(1)
    n_sc = ((qb + 1) * bq + sc_w - 1) // sc_w
    for j in range(IDX_HEADS):
        qi_s[j // 2, (j % 2) * bq:(j % 2 + 1) * bq, :] = iq_ref[:, j * IDX_DIM:(j + 1) * IDX_DIM]
    for h in range(DSA_HEADS):
        qs_s[h * bq:(h + 1) * bq, :] = q_ref[:, h * DSA_HEAD_DIM:(h + 1) * DSA_HEAD_DIM]
    w_t = (gt_ref[...] * ((IDX_HEADS * IDX_DIM) ** -0.5)).T
    qpos = qb * bq + lax.broadcasted_iota(I32, (1, bq), 1)

    def score_body(sc, carry):
        k0 = pl.multiple_of(sc * sc_w, sc_w)
        kidx = ik_ref[pl.ds(k0, sc_w), :]
        sco = jnp.zeros((sc_w, bq), F32)
        for jp in range(IDX_HEADS // 2):
            a = _dot_nt(kidx, qi_s[jp])
            for u in range(2):
                j = 2 * jp + u
                sco = sco + jnp.maximum(a[:, u * bq:(u + 1) * bq], 0.0) * w_t[G_IW + j:G_IW + j + 1, :]
        kpos = k0 + lax.broadcasted_iota(I32, (sc_w, 1), 0)
        sc_s[sc] = jnp.where(kpos <= qpos, sco, -jnp.inf)
        return carry

    lax.fori_loop(0, n_sc, score_body, 0)

    def count(pred):
        def body(sc, acc):
            hit = jnp.where(pred(sc_s[sc]), 1.0, 0.0)
            return acc + jnp.sum(hit.reshape(sc_w // 64, 64, bq), axis=0)
        acc = lax.fori_loop(0, n_sc, body, jnp.zeros((64, bq), F32))
        return jnp.sum(acc, axis=0, keepdims=True)

    thr_s[0:1, :] = jnp.full((1, bq), -jnp.finfo(jnp.float32).max, F32)

    @pl.when((qb + 1) * bq > n_sel)
    def _():
        active = qpos + 1 > n_sel

        def cond(st):
            it, lo, cnt_lo = st
            pending = active & (cnt_lo != n_sel)
            return (it < 32) & (jnp.max(jnp.where(pending, 1.0, 0.0)) > 0.0)

        def body(st):
            it, lo, cnt_lo = st
            for u in range(4):
                cand = lo ^ jnp.left_shift(jnp.int32(1), 31 - (it + u))
                thr_c = _key_to_f32(cand)
                cnt = count(lambda t: t >= thr_c)
                ok = cnt >= n_sel
                lo, cnt_lo = jnp.where(ok, cand, lo), jnp.where(ok, cnt, cnt_lo)
            return it + 4, lo, cnt_lo

        _, lo, cnt_lo = lax.while_loop(
            cond, body, (jnp.int32(0), jnp.full((1, bq), INT_MIN, I32), jnp.full((1, bq), 3.0e38, F32)))
        thr = _key_to_f32(lo)
        tie = active & (cnt_lo > n_sel)

        @pl.when(jnp.max(jnp.where(tie, 1.0, 0.0)) > 0.0)
        def _():
            need = n_sel - count(lambda t: t > thr)
            ri = lax.broadcasted_iota(I32, (sc_w, sc_w), 0)
            ci = lax.broadcasted_iota(I32, (sc_w, sc_w), 1)
            lower = jnp.where(ri >= ci, 1.0, 0.0).astype(BF16)

            def tie_body(sc, run):
                t = sc_s[sc]
                eq = (t == thr) & tie
                pre = jnp.dot(lower, jnp.where(eq, 1.0, 0.0).astype(BF16), preferred_element_type=F32)
                sc_s[sc] = jnp.where(eq & (run + pre > need), -jnp.inf, t)
                return run + pre[sc_w - 1:sc_w, :]

            lax.fori_loop(0, n_sc, tie_body, jnp.zeros((1, bq), F32))

        thr_s[0:1, :] = jnp.where(active, thr, thr_s[0:1, :])

    thr = thr_s[0:1, :]

    hq = DSA_HEADS * bq

    def att_body(sc, carry):
        m_prev, l_prev, acc = carry
        k0 = pl.multiple_of(sc * sc_w, sc_w)
        st = _dot_nt(k_ref[pl.ds(k0, sc_w), :], qs_s[...])
        drop = jnp.where(sc_s[sc] >= thr, 0.0, 2 * NEG_BIG)
        st = st + jnp.tile(drop, (1, DSA_HEADS))
        m_cur = jnp.max(jnp.max(st.reshape(sc_w // 64, 64, hq), axis=0), axis=0, keepdims=True)
        m_next = jnp.maximum(m_prev, m_cur)
        p = jnp.exp2(st - m_next)
        alpha = jnp.exp2(m_prev - m_next)
        l_cur = jnp.sum(jnp.sum(p.reshape(sc_w // 64, 64, hq), axis=0), axis=0, keepdims=True)
        pv = jnp.dot(vt_ref[sc], p.astype(BF16), preferred_element_type=F32)
        return m_next, alpha * l_prev + l_cur, alpha * acc + pv

    _, l_fin, acc = lax.fori_loop(0, n_sc, att_body, (jnp.full((1, hq), NEG_BIG, F32), jnp.zeros((1, hq), F32),
                                                      jnp.zeros((DSA_HEAD_DIM, hq), F32)))
    out_t = jnp.concatenate([acc / l_fin, jnp.zeros((LANES - DSA_HEAD_DIM, hq), F32)], axis=0).T
    for h in range(DSA_HEADS):
        o_ref[:, h * DSA_HEAD_DIM:(h + 1) * DSA_HEAD_DIM] = out_t[h * bq:(h + 1) * bq, 0:DSA_HEAD_DIM].astype(BF16)


def _dsa(iq, gt, dq, ik, dk, dvt, b, s, n_sel):
    bq, sc_w = DSA_BQ, DSA_SC
    nq = s // bq
    qblk = lambda n: pl.BlockSpec((bq, n), lambda i, j: (i * nq + j, 0))
    kblk = pl.BlockSpec((s, 64), lambda i, j: (i, 0))
    return pl.pallas_call(
        functools.partial(_dsa_kernel, bq=bq, sc_w=sc_w, n_sel=n_sel),
        grid=(b, nq),
        in_specs=[qblk(512), qblk(LANES), qblk(256), kblk, kblk,
                  pl.BlockSpec((s // sc_w, DSA_HEAD_DIM, sc_w), lambda i, j: (i, 0, 0))],
        out_specs=qblk(256),
        out_shape=jax.ShapeDtypeStruct((b * s, 256), BF16),
        scratch_shapes=[pltpu.VMEM((s // sc_w, sc_w, bq), F32), pltpu.VMEM((IDX_HEADS // 2, 2 * bq, IDX_DIM), BF16),
                        pltpu.VMEM((DSA_HEADS * bq, DSA_HEAD_DIM), BF16), pltpu.VMEM((8, bq), F32)],
        compiler_params=_cparams(("parallel", "arbitrary")),
        name="dsa",
    )(iq, gt, dq, ik, dk, dvt)


def _flash_kernel(qt_ref, kt_ref, q_ref, k_ref, v_ref, o_ref, m_s, acc_s, *, blk):
    t = pl.program_id(1)
    qi, ki = qt_ref[t], kt_ref[t]

    @pl.when(ki == 0)
    def _():
        m_s[...] = jnp.full(m_s.shape, NEG_BIG, F32)
        acc_s[...] = jnp.zeros_like(acc_s)

    def step(diagonal):
        if diagonal:
            keep = lax.broadcasted_iota(I32, (blk, blk), 0) >= lax.broadcasted_iota(I32, (blk, blk), 1)
        for h in range(FA_HEADS):
            sl = slice(h * HEAD_W, (h + 1) * HEAD_W)
            s = _dot_nt(q_ref[:, sl], k_ref[:, sl])
            if diagonal:
                s = jnp.where(keep, s, 2 * NEG_BIG)
            m_prev = m_s[h]
            m_next = jnp.maximum(m_prev, jnp.max(s, axis=1, keepdims=True))
            p = jnp.exp2(s - jnp.tile(m_next, (1, blk // LANES)))
            pv = jnp.dot(p.astype(BF16), v_ref[:, sl], preferred_element_type=F32)
            acc_s[h] = jnp.exp2(m_prev - m_next) * acc_s[h] + pv
            m_s[h] = m_next

    @pl.when(ki < qi)
    def _():
        step(False)

    @pl.when(ki == qi)
    def _():
        step(True)
        for h in range(FA_HEADS):
            acc = acc_s[h]
            o_ref[:, h * 64:(h + 1) * 64] = (acc / pltpu.roll(acc, 64, 1))[:, 0:64].astype(BF16)


def _flash(q, k, v, b, s, name):
    blk = FA_B
    nb = s // blk
    pairs = [(i, j) for i in range(nb) for j in range(i + 1)]
    qt = jnp.asarray([p[0] for p in pairs], I32)
    kt = jnp.asarray([p[1] for p in pairs], I32)
    w = FA_HEADS * HEAD_W
    grid_spec = pltpu.PrefetchScalarGridSpec(
        num_scalar_prefetch=2,
        grid=(b, len(pairs)),
        in_specs=[pl.BlockSpec((blk, w), lambda i, t, qt, kt: (i * nb + qt[t], 0)),
                  pl.BlockSpec((blk, w), lambda i, t, qt, kt: (i * nb + kt[t], 0)),
                  pl.BlockSpec((blk, w), lambda i, t, qt, kt: (i * nb + kt[t], 0))],
        out_specs=pl.BlockSpec((blk, FA_HEADS * 64), lambda i, t, qt, kt: (i * nb + qt[t], 0)),
        scratch_shapes=[pltpu.VMEM((FA_HEADS, blk, LANES), F32), pltpu.VMEM((FA_HEADS, blk, HEAD_W), F32)],
    )
    return pl.pallas_call(
        functools.partial(_flash_kernel, blk=blk),
        grid_spec=grid_spec,
        out_shape=jax.ShapeDtypeStruct((b * s, FA_HEADS * 64), BF16),
        compiler_params=_cparams(("parallel", "arbitrary")),
        name=name,
    )(qt, kt, q, k, v)


def _outproj_kernel(x_ref, y0_ref, y1_ref, y2_ref, y3_ref, wo_ref, g_ref, wr_ref, br_ref,
                    xo_ref, h_ref, cw_ref, rs_ref):
    acc = x_ref[...]
    for i, y_ref in enumerate((y0_ref, y1_ref, y2_ref, y3_ref)):
        acc = acc + jnp.dot(y_ref[...], wo_ref[i * 256:(i + 1) * 256, :], preferred_element_type=F32)
    xo_ref[...] = acc
    h2 = _rms(acc, g_ref[...])
    h_ref[...] = h2.astype(BF16)
    h_hi, h_lo, _ = _split3(h2)
    lg = (jnp.dot(h_hi, wr_ref[0], preferred_element_type=F32) + jnp.dot(h_lo, wr_ref[0], preferred_element_type=F32)
          + jnp.dot(h_hi, wr_ref[1], preferred_element_type=F32) + br_ref[...])
    lane = lax.broadcasted_iota(I32, (1, LANES), 1)
    lanef = lane.astype(F32)
    is_g = (lane >= MOE_EXPERTS) & (lane < MOE_EXPERTS + MOE_GROUPS)
    gl = jnp.where(is_g, lg, -jnp.inf)
    gmax = jnp.max(gl, axis=1, keepdims=True)
    gidx = jnp.min(jnp.where(gl == gmax, lanef, 999.0), axis=1, keepdims=True) - MOE_EXPERTS
    g_p = 1.0 / jnp.sum(jnp.where(is_g, jnp.exp(gl - gmax), 0.0), axis=1, keepdims=True)
    in_g = (lane < MOE_EXPERTS) & (jnp.floor(lanef * (1.0 / MOE_EPG)) == gidx)
    el = jnp.where(in_g, lg, -jnp.inf)
    m1 = jnp.max(el, axis=1, keepdims=True)
    i1 = jnp.min(jnp.where(el == m1, lanef, 999.0), axis=1, keepdims=True)
    el2 = jnp.where(lanef == i1, -jnp.inf, el)
    m2 = jnp.max(el2, axis=1, keepdims=True)
    i2 = jnp.min(jnp.where(el2 == m2, lanef, 999.0), axis=1, keepdims=True)
    t = jnp.exp(m2 - m1)
    w1 = 1.0 / (1.0 + t)
    cw_ref[...] = jnp.where(lanef == i1, w1 * g_p, jnp.where(lanef == i2, t * w1 * g_p, 0.0))
    rs_ref[...] = jnp.where((lanef == i1) | (lanef == i2), 1.0, 0.0).astype(BF16)


def _outproj(x, ys, w_out, g, w_rg, b_rg, w_re, b_re):
    t = x.shape[0]
    wo = w_out.astype(BF16)
    wr = jnp.zeros((D_MODEL, LANES), F32).at[:, 0:MOE_EXPERTS].set(w_re).at[:, MOE_EXPERTS:MOE_EXPERTS + MOE_GROUPS].set(w_rg)
    wr_hi = wr.astype(BF16)
    wr = jnp.stack([wr_hi, (wr - wr_hi.astype(F32)).astype(BF16)])
    br =jnp.zeros((1, LANES), F32).at[0, 0:MOE_EXPERTS].set(b_re).at[0, MOE_EXPERTS:MOE_EXPERTS + MOE_GROUPS].set(b_rg)
    tok = lambda n: pl.BlockSpec((TM, n), lambda i: (i, 0))
    full = lambda a: pl.BlockSpec(a.shape, lambda i: (0,) * a.ndim)
    return pl.pallas_call(
        _outproj_kernel,
        grid=(t // TM,),
        in_specs=[tok(D_MODEL)] + [tok(256)] * 4 + [full(wo), full(g), full(wr), full(br)],
        out_specs=[tok(D_MODEL), tok(D_MODEL), tok(LANES), tok(LANES)],
        out_shape=[jax.ShapeDtypeStruct((t, D_MODEL), F32), jax.ShapeDtypeStruct((t, D_MODEL), BF16),
                   jax.ShapeDtypeStruct((t, LANES), F32), jax.ShapeDtypeStruct((t, LANES), BF16)],
        compiler_params=_cparams(("parallel",)),
        name="outproj_router",
    )(x, *ys, wo, g, wr, br)


def _moe_plan_kernel(rs_ref, cw_ref, meta_ref, col_ref, row_ref, *, n):
    sel = rs_ref[...]
    self32 = sel.astype(F32)
    ti = lax.broadcasted_iota(I32, (n, n), 0)
    tj = lax.broadcasted_iota(I32, (n, n), 1)
    rank = jnp.dot(jnp.where(ti > tj, 1.0, 0.0).astype(BF16), sel, preferred_element_type=F32)
    cnt = jnp.sum(self32, axis=0, keepdims=True).astype(I32)
    cpad = ((cnt + (MOE_ALIGN - 1)) // MOE_ALIGN) * MOE_ALIGN
    li = lax.broadcasted_iota(I32, (LANES, LANES), 0)
    lj = lax.broadcasted_iota(I32, (LANES, LANES), 1)
    off = _dot_f32_by_mask(li < lj, jnp.broadcast_to(cpad.astype(F32), (8, LANES)), False)[0:1, :]
    dest = off + rank
    lane = lax.broadcasted_iota(I32, (1, LANES), 1)
    lanef = lane.astype(F32)
    on = self32 > 0.5
    e_lo = jnp.min(jnp.where(on, lanef, 999.0), axis=1, keepdims=True)
    e_hi = jnp.max(jnp.where(on, lanef, -1.0), axis=1, keepdims=True)
    is_lo, is_hi = lanef == e_lo, lanef == e_hi
    cw = cw_ref[...]
    pick = lambda m, a: jnp.sum(jnp.where(m, a, 0.0), axis=1, keepdims=True)
    d_lo, d_hi, w_lo, w_hi = pick(is_lo, dest), pick(is_hi, dest), pick(is_lo, cw), pick(is_hi, cw)
    colv = jnp.where(lane == 0, d_lo, jnp.where(lane == 1, d_hi, jnp.where(lane == 2, w_lo, jnp.where(lane == 3, w_hi, 0.0))))
    col_ref[...] = colv
    row_ref[0] = colv.T[0:8, :]
    sub = lax.broadcasted_iota(I32, (8, LANES), 0)
    meta_ref[0] = jnp.where(sub == 0, off.astype(I32), jnp.where(sub == 1, cnt, 0))


def _moe_kernel(meta_ref, h_ref, col_ref, row_ref, wg_ref, wu_ref, wd_ref, o_ref, s_s, y_s, *, n, ns):
    t, e = pl.program_id(0), pl.program_id(1)

    @pl.when(e == 0)
    def _():
        d_lo, d_hi = row_ref[0, 0:1, :], row_ref[0, 1:2, :]
        for r in range(ns // LANES):
            si = (r * LANES + lax.broadcasted_iota(I32, (LANES, 1), 0)).astype(F32)
            perm = jnp.where((si == d_lo) | (si == d_hi), 1.0, 0.0).astype(BF16)
            s_s[r * LANES:(r + 1) * LANES, :] = jnp.dot(perm, h_ref[...], preferred_element_type=F32).astype(BF16)
        y_s[...] = jnp.zeros_like(y_s)

    for j in range(MOE_EPS):
        off = meta_ref[t * 2 * MOE_EXPERTS + e * MOE_EPS + j]
        cnt = meta_ref[t * 2 * MOE_EXPERTS + MOE_EXPERTS + e * MOE_EPS + j]

        def body(i, carry, off=off, j=j):
            r0 = pl.multiple_of(off + i * MOE_CH, MOE_ALIGN)
            xs = s_s[pl.ds(r0, MOE_CH), :]
            gate = jnp.dot(xs, wg_ref[j], preferred_element_type=F32)
            up = jnp.dot(xs, wu_ref[j], preferred_element_type=F32)
            hid = (_silu(gate) * up).astype(BF16)
            y_s[pl.ds(r0, MOE_CH), :] = jnp.dot(hid, wd_ref[j], preferred_element_type=F32).astype(BF16)
            return carry

        lax.fori_loop(0, (cnt + MOE_CH - 1) // MOE_CH, body, 0)

    @pl.when(e == pl.num_programs(1) - 1)
    def _():
        rows, kc = 256, ns // 3
        for c in range(n // rows):
            cv = col_ref[c * rows:(c + 1) * rows, :]
            d_lo, d_hi, w_lo, w_hi = cv[:, 0:1], cv[:, 1:2], cv[:, 2:3], cv[:, 3:4]
            acc = jnp.zeros((rows, D_MODEL), F32)
            for r in range(ns // kc):
                si = (r * kc + lax.broadcasted_iota(I32, (1, kc), 1)).astype(F32)
                pw = (jnp.where(si == d_lo, w_lo, 0.0) + jnp.where(si == d_hi, w_hi, 0.0)).astype(BF16)
                acc = acc + jnp.dot(pw, y_s[r * kc:(r + 1) * kc, :], preferred_element_type=F32)
            o_ref[c * rows:(c + 1) * rows, :] = acc.astype(BF16)


def _moe(h2, cw, rs, w_gate, w_up, w_down):
    t = h2.shape[0]
    n, ns = MOE_N, MOE_NS
    nt = t // n
    meta, col, row = pl.pallas_call(
        functools.partial(_moe_plan_kernel, n=n),
        grid=(nt,),
        in_specs=[pl.BlockSpec((n, LANES), lambda i: (i, 0)), pl.BlockSpec((n, LANES), lambda i: (i, 0))],
        out_specs=[pl.BlockSpec((1, 8, LANES), lambda i: (i, 0, 0)), pl.BlockSpec((n, LANES), lambda i: (i, 0)),
                   pl.BlockSpec((1, 8, n), lambda i: (i, 0, 0))],
        out_shape=[jax.ShapeDtypeStruct((nt, 8, LANES), I32), jax.ShapeDtypeStruct((t, LANES), F32),
                   jax.ShapeDtypeStruct((nt, 8, n), F32)],
        compiler_params=_cparams(("parallel",)),
        name="moe_plan",
    )(rs, cw)
    meta_flat = meta[:, 0:2, 0:MOE_EXPERTS].reshape(-1)
    wg, wu, wd = w_gate.astype(BF16), w_up.astype(BF16), w_down.astype(BF16)
    grid_spec = pltpu.PrefetchScalarGridSpec(
        num_scalar_prefetch=1,
        grid=(nt, MOE_EXPERTS // MOE_EPS),
        in_specs=[pl.BlockSpec((n, D_MODEL), lambda i, e, m: (i, 0)),
                  pl.BlockSpec((n, LANES), lambda i, e, m: (i, 0)),
                  pl.BlockSpec((1, 8, n), lambda i, e, m: (i, 0, 0)),
                  pl.BlockSpec((MOE_EPS, D_MODEL, MOE_FF), lambda i, e, m: (e, 0, 0)),
                  pl.BlockSpec((MOE_EPS, D_MODEL, MOE_FF), lambda i, e, m: (e, 0, 0)),
                  pl.BlockSpec((MOE_EPS, MOE_FF, D_MODEL), lambda i, e, m: (e, 0, 0))],
        out_specs=pl.BlockSpec((n, D_MODEL), lambda i, e, m: (i, 0)),
        scratch_shapes=[pltpu.VMEM((ns, D_MODEL), BF16), pltpu.VMEM((ns, D_MODEL), BF16)],
    )
    return pl.pallas_call(
        functools.partial(_moe_kernel, n=n, ns=ns),
        grid_spec=grid_spec,
        out_shape=jax.ShapeDtypeStruct((t, D_MODEL), BF16),
        compiler_params=_cparams(("parallel", "arbitrary")),
        name="moe_experts",
    )(meta_flat, h2, col, row, wg, wu, wd)


def _final_kernel(x_ref, y_ref, g_ref, o_ref):
    o_ref[...] = _rms(x_ref[...] + y_ref[...].astype(F32), g_ref[...])


def _final(x, y, g):
    t = x.shape[0]
    tok = pl.BlockSpec((TM, D_MODEL), lambda i: (i, 0))
    return pl.pallas_call(
        _final_kernel,
        grid=(t // TM,),
        in_specs=[tok, tok, pl.BlockSpec((1, D_MODEL), lambda i: (0, 0))],
        out_specs=tok,
        out_shape=jax.ShapeDtypeStruct((t, D_MODEL), F32),
        compiler_params=_cparams(("parallel",)),
        name="final_norm",
    )(x, y, g)


def kernel(x, positions, norm_mix, w_in, ssd_conv_w, ssd_conv_b, ssd_dt_bias, ssd_a_log, ssd_d, ssd_norm, mla_q_norm, mla_w_uq, mla_kv_norm, mla_w_ukv, fox_f_bias, w_out, norm_ffn, router_group_w, router_group_b, router_expert_w, router_expert_b, expert_w_gate, expert_w_up, expert_w_down, final_norm):
    b, s, d = x.shape
    assert d == D_MODEL and s % FA_B == 0 and (b * s) % MOE_N == 0 and TM == DSA_SC
    depth = w_in.shape[0]
    n_sel = min(IDX_TOPK_MAX, s // 4)
    tab = _rope_tables(positions)
    xf = x.reshape(b * s, d)
    y_ffn = None
    for l in range(depth):
        xf, p = _inproj(xf, y_ffn, norm_mix[l].reshape(1, d), _pack_w_in(w_in[l]), tab)
        mq, mk, mv = _mla_prep(p["mc"], mla_q_norm[l], mla_kv_norm[l], mla_w_uq[l], mla_w_ukv[l], tab)
        fq, fk = _fox_cum(p["gt"], fox_f_bias[l], p["fq"], p["fk"], b, s)
        y_ssd = _ssd(p["zx"], p["gt"], ssd_conv_w[l], ssd_conv_b[l], ssd_dt_bias[l], ssd_a_log[l], ssd_d[l],
                     ssd_norm[l], b, s)
        y_dsa = _dsa(p["iq"], p["gt"], p["dq"], p["ik"], p["dk"], p["dv"], b, s, n_sel)
        y_mla = _flash(mq, mk, mv, b, s, "flash_mla")
        y_fox = _flash(fq, fk, p["fv"], b, s, "flash_fox")
        xf, h2, cw, rs = _outproj(xf, (y_ssd, y_dsa, y_mla, y_fox), w_out[l], norm_ffn[l].reshape(1, d),
                                  router_group_w[l], router_group_b[l], router_expert_w[l], router_expert_b[l])
        y_ffn = _moe(h2, cw, rs, expert_w_gate[l], expert_w_up[l], expert_w_down[l])
    return _final(xf, y_ffn, final_norm.reshape(1, d)).reshape(b, s, d)
```

```python
import functools
import math

import jax
import jax.numpy as jnp
from jax import lax
from jax.experimental import pallas as pl
from jax.experimental.pallas import tpu as pltpu

F32 = jnp.float32
BF16 = jnp.bfloat16
I32 = jnp.int32
EPS = 1e-6
ROPE_THETA = 500000.0
HIGHEST = lax.Precision.HIGHEST

D_MODEL = 1024
SSD_HEADS, SSD_HEAD_DIM, SSD_GROUPS, SSD_STATE, SSD_CONV = 4, 64, 2, 128, 4
SSD_D_INNER = SSD_HEADS * SSD_HEAD_DIM
SSD_CONV_DIM = SSD_D_INNER + 2 * SSD_GROUPS * SSD_STATE
DSA_HEADS, DSA_HEAD_DIM, IDX_HEADS, IDX_DIM, IDX_TOPK_MAX = 4, 64, 8, 64, 256
MLA_HEADS, MLA_Q_RANK, MLA_KV_RANK, MLA_NOPE, MLA_ROPE, MLA_V = 4, 256, 128, 64, 32, 64
FOX_HEADS, FOX_HEAD_DIM = 4, 64
MOE_GROUPS, MOE_EPG, MOE_EXPERTS, MOE_FF = 4, 8, 32, 256

LANES = 128
VMEM_LIMIT = 56 * 1024 * 1024
INT_MIN = -(2 ** 31)
NEG_BIG = -1e30

TM = 512
SSD_Q = 256
DSA_BQ = 256
DSA_SC = 512
FA_B = 512
MOE_N = 1024
MOE_ALIGN = 16
MOE_CH = 128
MOE_EPS = 4
MOE_NS = 2 * MOE_N + MOE_EXPERTS * MOE_ALIGN + MOE_CH

C_ZX, C_GT, C_DQ, C_KV, C_IK, C_IQ, C_MC, C_FQ, C_FK, C_FV, C_END = (
    0, 1024, 1152, 1408, 1536, 1664, 2176, 2688, 3200, 3712, 4224)
G_FOX, G_DT, G_IW = 0, 4, 8
LOG2E = math.log2(math.e)
HEAD_W = 128
FA_HEADS = 4
FOX_ONE_Q, FOX_F_Q, FOX_F_K, FOX_ONE_K = 64, 67, 64, 67


def _cparams(sem):
    return pltpu.CompilerParams(dimension_semantics=sem, vmem_limit_bytes=VMEM_LIMIT)


def _rms(xf, g):
    return xf * lax.rsqrt(jnp.mean(xf * xf, axis=-1, keepdims=True) + EPS) * g


def _softplus(x):
    return jnp.maximum(x, 0.0) + jnp.log1p(jnp.exp(-jnp.abs(x)))


def _silu(x):
    return x * jax.nn.sigmoid(x)


def _rope_apply(a, c, sa, sb, half):
    return a * c + pltpu.roll(a, LANES - half, 1) * sa + pltpu.roll(a, half, 1) * sb


def _dot_nt(a, b):
    return lax.dot_general(a, b, (((1,), (1,)), ((), ())), preferred_element_type=F32)


def _split3(x):
    hi = x.astype(BF16)
    r = x - hi.astype(F32)
    mid = r.astype(BF16)
    return hi, mid, (r - mid.astype(F32)).astype(BF16)


def _dot_f32_by_mask(mask, x, mask_left):
    m = jnp.where(mask, 1.0, 0.0).astype(BF16)
    out = None
    for piece in _split3(x):
        d = jnp.dot(m, piece, preferred_element_type=F32) if mask_left else jnp.dot(piece, m, preferred_element_type=F32)
        out = d if out is None else out + d
    return out


def _rope_tab_kernel(pos_ref, inv_ref, o_ref):
    pos = pos_ref[...].astype(F32)
    lane = lax.broadcasted_iota(I32, (1, LANES), 1)
    ang = pos * inv_ref[0:1, :]
    c, s = jnp.cos(ang), jnp.sin(ang)
    jd = lane % 64
    o_ref[:, 0:128] = c
    o_ref[:, 128:256] = jnp.where(jd < 8, -s, 0.0)
    o_ref[:, 256:384] = jnp.where((jd >= 8) & (jd < 16), s, 0.0)
    ang = pos * inv_ref[1:2, :]
    c, s = jnp.cos(ang), jnp.sin(ang)
    jm = lane - 64
    o_ref[:, 384:512] = c
    o_ref[:, 512:640] = jnp.where((jm >= 0) & (jm < 16), -s, 0.0)
    o_ref[:, 640:768] = jnp.where((jm >= 16) & (jm < 32), s, 0.0)


def _rope_tables(positions):
    t = positions.size
    half_d, half_m = DSA_HEAD_DIM // 4 // 2, MLA_ROPE // 2
    inv_d = ROPE_THETA ** (-jnp.arange(half_d, dtype=F32) / half_d)
    inv_m = ROPE_THETA ** (-jnp.arange(half_m, dtype=F32) / half_m)
    lane = jnp.arange(LANES)
    jd = lane % 64
    row_d = jnp.where(jd < 2 * half_d, inv_d[jd % half_d], 0.0)
    jm = lane - 64
    row_m = jnp.where((jm >= 0) & (jm < 2 * half_m), inv_m[jm % half_m], 0.0)
    inv = jnp.zeros((8, LANES), F32).at[0].set(row_d).at[1].set(row_m)
    tm = 1024
    return pl.pallas_call(
        _rope_tab_kernel,
        grid=(t // tm,),
        in_specs=[pl.BlockSpec((tm, 1), lambda i: (i, 0)), pl.BlockSpec((8, LANES), lambda i: (0, 0))],
        out_specs=pl.BlockSpec((tm, 768), lambda i: (i, 0)),
        out_shape=jax.ShapeDtypeStruct((t, 768), F32),
        compiler_params=_cparams(("parallel",)),
        name="rope_tables",
    )(positions.reshape(t, 1), inv)


def _inproj_kernel(*refs, has_add):
    if has_add:
        (x_ref, yp_ref, g_ref, w_ref, tab_ref, xo_ref, zx_ref, gt_ref, dq_ref, dk_ref, dv_ref,
         ik_ref, iq_ref, mc_ref, fq_ref, fk_ref, fv_ref) = refs
        x = x_ref[...] + yp_ref[...].astype(F32)
        xo_ref[...] = x
    else:
        (x_ref, g_ref, w_ref, tab_ref, zx_ref, gt_ref, dq_ref, dk_ref, dv_ref,
         ik_ref, iq_ref, mc_ref, fq_ref, fk_ref, fv_ref) = refs
        x = x_ref[...]
    h = _rms(x, g_ref[...]).astype(BF16)

    def proj(a, b):
        return jnp.dot(h, w_ref[:, a:b], preferred_element_type=F32)

    zx_ref[...] = proj(C_ZX, C_GT)
    gt_ref[...] = proj(C_GT, C_DQ)
    c_d, sa_d, sb_d = tab_ref[:, 0:128], tab_ref[:, 128:256], tab_ref[:, 256:384]
    c_m, sa_m, sb_m = tab_ref[:, 384:512], tab_ref[:, 512:640], tab_ref[:, 640:768]
    q = proj(C_DQ, C_KV)
    for i in range(2):
        dq_ref[:, i * 128:(i + 1) * 128] = _rope_apply(q[:, i * 128:(i + 1) * 128], c_d, sa_d, sb_d, 8).astype(BF16)
    kv = proj(C_KV, C_IK)
    is_k = lax.broadcasted_iota(I32, (1, LANES), 1) < 64
    kvr = _rope_apply(kv, jnp.where(is_k, c_d, 1.0), jnp.where(is_k, sa_d, 0.0), jnp.where(is_k, sb_d, 0.0), 8)
    dk_ref[...] = kvr[:, 0:64].astype(BF16)
    dv_ref[0] = kvr.T[64:128, :].astype(BF16)
    ikb = proj(C_IK, C_IQ)
    ik_ref[...] = _rope_apply(ikb, c_d, sa_d, sb_d, 8)[:, 0:64].astype(BF16)
    iq = proj(C_IQ, C_MC)
    for i in range(4):
        iq_ref[:, i * 128:(i + 1) * 128] = _rope_apply(iq[:, i * 128:(i + 1) * 128], c_d, sa_d, sb_d, 8).astype(BF16)
    mc = proj(C_MC, C_FQ)
    mc_ref[:, 0:384] = mc[:, 0:384]
    mc_ref[:, 384:512] = _rope_apply(mc[:, 384:512], c_m, sa_m, sb_m, 16)
    fq_ref[...] = proj(C_FQ, C_FK).astype(BF16)
    fk_ref[...] = proj(C_FK, C_FV).astype(BF16)
    ones_hi = jnp.where(lax.broadcasted_iota(I32, (1, FA_HEADS * HEAD_W), 1) % HEAD_W >= 64, 1.0, 0.0)
    fv_ref[...] = (proj(C_FV, C_END) + ones_hi).astype(BF16)


def _pack_w_in(w):
    d = w.shape[0]
    z = lambda n: jnp.zeros((d, n), F32)
    o = 0
    parts = {}
    for name, width in (("z", 256), ("xbc", 768), ("dt", 4), ("dq", 256), ("dk", 64), ("dv", 64), ("iq", 512),
                        ("ik", 64), ("iw", 8), ("cq", 256), ("ckv", 128), ("kr", 32), ("fq", 256), ("fk", 256),
                        ("fv", 256), ("ff", 4)):
        parts[name] = w[:, o:o + width]
        o += width
    scale = DSA_HEAD_DIM ** -0.5

    def per_head(a):
        return jnp.pad(a.reshape(d, FA_HEADS, 64), ((0, 0), (0, 0), (0, HEAD_W - 64))).reshape(d, FA_HEADS * HEAD_W)

    cat = jnp.concatenate([
        parts["z"], parts["xbc"],
        parts["ff"], parts["dt"], parts["iw"], z(112),
        parts["dq"] * (scale * LOG2E),
        parts["dk"], parts["dv"],
        parts["ik"], z(64),
        parts["iq"],
        parts["cq"], parts["ckv"], z(64), parts["kr"], z(32),
        per_head(parts["fq"] * (FOX_HEAD_DIM ** -0.5 * LOG2E)), per_head(parts["fk"]), per_head(parts["fv"])], axis=1)
    return cat.astype(BF16)


def _inproj(x, y_prev, g, w_cat, tab):
    t = x.shape[0]
    has_add = y_prev is not None
    tok = lambda n: pl.BlockSpec((TM, n), lambda i: (i, 0))
    full = lambda a: pl.BlockSpec(a.shape, lambda i: (0,) * a.ndim)
    ins = [x] + ([y_prev] if has_add else []) + [g, w_cat, tab]
    in_specs = [tok(D_MODEL)] + ([tok(D_MODEL)] if has_add else []) + [full(g), full(w_cat), tok(768)]
    outs = ([("x", D_MODEL, F32)] if has_add else []) + [
        ("zx", 1024, F32), ("gt", 128, F32), ("dq", 256, BF16), ("dk", 64, BF16), ("dv", 64, BF16),
        ("ik", 64, BF16), ("iq", 512, BF16), ("mc", 512, F32), ("fq", 512, BF16), ("fk", 512, BF16), ("fv", 512, BF16)]
    res = pl.pallas_call(
        functools.partial(_inproj_kernel, has_add=has_add),
        grid=(t // TM,),
        in_specs=in_specs,
        out_specs=[pl.BlockSpec((1, DSA_HEAD_DIM, TM), lambda i: (i, 0, 0)) if nm == "dv" else tok(n)
                   for nm, n, _ in outs],
        out_shape=[jax.ShapeDtypeStruct((t // TM, DSA_HEAD_DIM, TM) if nm == "dv" else (t, n), dt)
                   for nm, n, dt in outs],
        compiler_params=_cparams(("parallel",)),
        name="inproj",
    )(*ins)
    res = list(res)
    x_new = res.pop(0) if has_add else x
    return x_new, dict(zip([n for n, _, _ in outs if n != "x"], res))


def _mla_prep_kernel(mc_ref, qn_ref, kn_ref, wq_ref, wk_ref, wv_ref, tab_ref, q_ref, k_ref, v_ref):
    mc = mc_ref[...]
    cq = _rms(mc[:, 0:256], qn_ref[...]).astype(BF16)
    ckv = _rms(mc[:, 256:384], kn_ref[...]).astype(BF16)
    kr = mc[:, 384:512]
    c_m, sa_m, sb_m = tab_ref[:, 384:512], tab_ref[:, 512:640], tab_ref[:, 640:768]
    q = jnp.dot(cq, wq_ref[...], preferred_element_type=F32)
    k = jnp.dot(ckv, wk_ref[...], preferred_element_type=F32)
    for h in range(MLA_HEADS):
        sl = slice(h * 128, (h + 1) * 128)
        q_ref[:, sl] = _rope_apply(q[:, sl], c_m, sa_m, sb_m, 16).astype(BF16)
        k_ref[:, sl] = (k[:, sl] + kr).astype(BF16)
    ones_hi = jnp.where(lax.broadcasted_iota(I32, (1, FA_HEADS * HEAD_W), 1) % HEAD_W >= 64, 1.0, 0.0)
    v_ref[...] = (jnp.dot(ckv, wv_ref[...], preferred_element_type=F32) + ones_hi).astype(BF16)


def _mla_prep(mc, q_norm, kv_norm, w_uq, w_ukv, tab):
    t = mc.shape[0]
    dqk = MLA_NOPE + MLA_ROPE
    wq = jnp.pad(w_uq.reshape(MLA_Q_RANK, MLA_HEADS, dqk) * (dqk ** -0.5 * LOG2E), ((0, 0), (0, 0), (0, 128 - dqk)))
    wq = wq.reshape(MLA_Q_RANK, MLA_HEADS * 128).astype(BF16)
    wkv = w_ukv.reshape(MLA_KV_RANK, MLA_HEADS, MLA_NOPE + MLA_V)
    wk = jnp.pad(wkv[:, :, :MLA_NOPE], ((0, 0), (0, 0), (0, 128 - MLA_NOPE))).reshape(MLA_KV_RANK, MLA_HEADS * 128).astype(BF16)
    wv = jnp.pad(wkv[:, :, MLA_NOPE:], ((0, 0), (0, 0), (0, 128 - MLA_V))).reshape(MLA_KV_RANK, MLA_HEADS * 128).astype(BF16)
    tok = lambda n: pl.BlockSpec((TM, n), lambda i: (i, 0))
    full = lambda a: pl.BlockSpec(a.shape, lambda i: (0,) * a.ndim)
    qn, kn = q_norm.reshape(1, -1), kv_norm.reshape(1, -1)
    return pl.pallas_call(
        _mla_prep_kernel,
        grid=(t // TM,),
        in_specs=[tok(512), full(qn), full(kn), full(wq), full(wk), full(wv), tok(768)],
        out_specs=[tok(512), tok(512), tok(512)],
        out_shape=[jax.ShapeDtypeStruct((t, 512), BF16)] * 3,
        compiler_params=_cparams(("parallel",)),
        name="mla_prep",
    )(mc, qn, kn, wq, wk, wv, tab)


def _fox_cum_kernel(gt_ref, b_ref, q_ref, k_ref, qo_ref, ko_ref, carry_s, *, blk):
    @pl.when(pl.program_id(1) == 0)
    def _():
        carry_s[...] = jnp.zeros_like(carry_s)

    x = gt_ref[...] + b_ref[...]
    lf = -_softplus(-x)
    ri = lax.broadcasted_iota(I32, (blk, blk), 0)
    ci = lax.broadcasted_iota(I32, (blk, blk), 1)
    cs = _dot_f32_by_mask(ri >= ci, lf, True) + carry_s[0:1, :]
    carry_s[0:1, :] = cs[blk - 1:blk, :]
    f2 = cs * LOG2E
    lane = lax.broadcasted_iota(I32, (1, HEAD_W), 1)

    def pieces(f, base):
        hi = f.astype(BF16).astype(F32)
        mid = (f - hi).astype(BF16).astype(F32)
        lo = f - hi - mid
        return jnp.where(lane == base, hi, jnp.where(lane == base + 1, mid, jnp.where(lane == base + 2, lo, 0.0)))

    def ones(base):
        return jnp.where((lane >= base) & (lane < base + 3), 1.0, 0.0)

    for h in range(FOX_HEADS):
        fh = f2[:, G_FOX + h:G_FOX + h + 1]
        sl = slice(h * HEAD_W, (h + 1) * HEAD_W)
        qo_ref[:, sl] = (q_ref[:, sl].astype(F32) + pieces(fh, FOX_F_Q) + ones(FOX_ONE_Q)).astype(BF16)
        ko_ref[:, sl] = (k_ref[:, sl].astype(F32) - pieces(fh, FOX_F_K) + ones(FOX_ONE_K)).astype(BF16)


def _fox_cum(gt, f_bias, fq, fk, b, s):
    blk = 512
    bias = jnp.zeros((1, LANES), F32).at[0, G_FOX:G_FOX + FOX_HEADS].set(f_bias)
    nb = s // blk
    tok = lambda n: pl.BlockSpec((blk, n), lambda i, j: (i * nb + j, 0))
    w = FOX_HEADS * HEAD_W
    return pl.pallas_call(
        functools.partial(_fox_cum_kernel, blk=blk),
        grid=(b, nb),
        in_specs=[tok(LANES), pl.BlockSpec((1, LANES), lambda i, j: (0, 0)), tok(w), tok(w)],
        out_specs=[tok(w), tok(w)],
        out_shape=[jax.ShapeDtypeStruct((b * s, w), BF16)] * 2,
        scratch_shapes=[pltpu.VMEM((8, LANES), F32)],
        compiler_params=_cparams(("parallel", "arbitrary")),
        name="fox_cum",
    )(gt, bias, fq, fk)


def _ssd_kernel(zx_ref, gt_ref, cw_ref, cb_ref, pr_ref, pc_ref, drow_ref, nw_ref, o_ref, ext_s, st_s, y_s, *, q):
    @pl.when(pl.program_id(1) == 0)
    def _():
        ext_s[0:8, :] = jnp.zeros((8, SSD_CONV_DIM), F32)
        st_s[...] = jnp.zeros_like(st_s)

    raw = zx_ref[:, 256:1024]
    ext_s[8:8 + q, :] = raw
    acc = jnp.broadcast_to(cb_ref[...], (q, SSD_CONV_DIM))
    for j in range(SSD_CONV):
        acc = acc + cw_ref[j:j + 1, :] * ext_s[5 + j:5 + j + q, :]
    ext_s[0:8, :] = raw[q - 8:q, :]
    xbc = _silu(acc)
    xs = xbc[:, 0:SSD_D_INNER]

    g = gt_ref[...]
    lane = lax.broadcasted_iota(I32, (1, LANES), 1)
    dtc = _softplus(g + pr_ref[0:1, :])
    a_r = jnp.where((lane >= G_DT) & (lane < G_DT + SSD_HEADS), -jnp.exp(pr_ref[1:2, :]), 0.0)
    ri = lax.broadcasted_iota(I32, (q, q), 0)
    ci = lax.broadcasted_iota(I32, (q, q), 1)
    tri = ri >= ci
    acs_c = _dot_f32_by_mask(tri, dtc * a_r, True)
    sub = lax.broadcasted_iota(I32, (LANES, 1), 0)
    dtr = _softplus(g.T + pc_ref[:, 0:1])
    a_c = jnp.where((sub >= G_DT) & (sub < G_DT + SSD_HEADS), -jnp.exp(pc_ref[:, 1:2]), 0.0)
    acs_r = _dot_f32_by_mask(ri <= ci, dtr * a_c, False)

    rep = SSD_HEADS // SSD_GROUPS
    for gi in range(SSD_GROUPS):
        bg = xbc[:, SSD_D_INNER + gi * SSD_STATE:SSD_D_INNER + (gi + 1) * SSD_STATE]
        cg = xbc[:, SSD_D_INNER + (SSD_GROUPS + gi) * SSD_STATE:SSD_D_INNER + (SSD_GROUPS + gi + 1) * SSD_STATE]
        bt = bg.T.astype(BF16)
        cb16 = cg.astype(BF16)
        cbm = jnp.dot(cb16, bt, preferred_element_type=F32)
        for hh in range(rep):
            h = gi * rep + hh
            ac = acs_c[:, G_DT + h:G_DT + h + 1]
            ar = acs_r[G_DT + h:G_DT + h + 1, :]
            seg = jnp.where(tri, jnp.exp(ac - ar), 0.0)
            xh = xs[:, h * SSD_HEAD_DIM:(h + 1) * SSD_HEAD_DIM]
            xdt = xh * dtc[:, G_DT + h:G_DT + h + 1]
            yd = jnp.dot((cbm * seg).astype(BF16), xdt.astype(BF16), preferred_element_type=F32)
            aend = ac[q - 1:q, :]
            st = st_s[h]
            yo = jnp.dot(cb16, st.astype(BF16), preferred_element_type=F32) * jnp.exp(ac)
            st_s[h] = st * jnp.exp(aend) + jnp.dot(bt, (xdt * jnp.exp(aend - ac)).astype(BF16),
                                                   preferred_element_type=F32)
            y_s[:, h * SSD_HEAD_DIM:(h + 1) * SSD_HEAD_DIM] = yd + yo
    y = y_s[...] + drow_ref[...] * xs
    y = y * _silu(zx_ref[:, 0:SSD_D_INNER])
    o_ref[...] = _rms(y, nw_ref[...]).astype(BF16)


def _ssd(zx, gt, conv_w, conv_b, dt_bias, a_log, d_skip, norm_w, b, s):
    q = SSD_Q
    nc = s // q
    pr = jnp.zeros((8, LANES), F32).at[0, G_DT:G_DT + SSD_HEADS].set(dt_bias).at[1, G_DT:G_DT + SSD_HEADS].set(a_log)
    pc = pr.T
    drow = jnp.repeat(d_skip, SSD_HEAD_DIM).reshape(1, SSD_D_INNER)
    cb = conv_b.reshape(1, -1)
    nw = norm_w.reshape(1, -1)
    full = lambda a: pl.BlockSpec(a.shape, lambda i, j: (0,) * a.ndim)
    return pl.pallas_call(
        functools.partial(_ssd_kernel, q=q),
        grid=(b, nc),
        in_specs=[pl.BlockSpec((q, 1024), lambda i, j: (i * nc + j, 0)), pl.BlockSpec((q, LANES), lambda i, j: (i * nc + j, 0)),
                  full(conv_w), full(cb), full(pr), full(pc), full(drow), full(nw)],
        out_specs=pl.BlockSpec((q, SSD_D_INNER), lambda i, j: (i * nc + j, 0)),
        out_shape=jax.ShapeDtypeStruct((b * s, SSD_D_INNER), BF16),
        scratch_shapes=[pltpu.VMEM((8 + q, SSD_CONV_DIM), F32), pltpu.VMEM((SSD_HEADS, SSD_STATE, SSD_HEAD_DIM), F32),
                        pltpu.VMEM((q, SSD_D_INNER), F32)],
        compiler_params=_cparams(("parallel", "arbitrary")),
        name="ssd",
    )(zx, gt, conv_w, cb, pr, pc, drow, nw)


def _key_to_f32(k):
    return pltpu.bitcast(jnp.where(k < 0, k ^ jnp.int32(0x7FFFFFFF), k), F32)


def _dsa_kernel(iq_ref, gt_ref, q_ref, ik_ref, k_ref, vt_ref, o_ref, sc_s, qi_s, qs_s, thr_s, *, bq, sc_w, n_sel):
    qb = pl.program_id(1)
    n_sc = ((qb + 1) * bq + sc_w - 1) // sc_w
    for j in range(IDX_HEADS):
        qi_s[j // 2, (j % 2) * bq:(j % 2 + 1) * bq, :] = iq_ref[:, j * IDX_DIM:(j + 1) * IDX_DIM]
    for h in range(DSA_HEADS):
        qs_s[h * bq:(h + 1) * bq, :] = q_ref[:, h * DSA_HEAD_DIM:(h + 1) * DSA_HEAD_DIM]
    w_t = (gt_ref[...] * ((IDX_HEADS * IDX_DIM) ** -0.5)).T
    qpos = qb * bq + lax.broadcasted_iota(I32, (1, bq), 1)

    def score_body(sc, carry):
        k0 = pl.multiple_of(sc * sc_w, sc_w)
        kidx = ik_ref[pl.ds(k0, sc_w), :]
        sco = jnp.zeros((sc_w, bq), F32)
        for jp in range(IDX_HEADS // 2):
            a = _dot_nt(kidx, qi_s[jp])
            for u in range(2):
                j = 2 * jp + u
                sco = sco + jnp.maximum(a[:, u * bq:(u + 1) * bq], 0.0) * w_t[G_IW + j:G_IW + j + 1, :]
        kpos = k0 + lax.broadcasted_iota(I32, (sc_w, 1), 0)
        sc_s[sc] = jnp.where(kpos <= qpos, sco, -jnp.inf)
        return carry

    lax.fori_loop(0, n_sc, score_body, 0)

    def count(pred):
        def body(sc, acc):
            hit = jnp.where(pred(sc_s[sc]), 1.0, 0.0)
            return acc + jnp.sum(hit.reshape(sc_w // 64, 64, bq), axis=0)
        acc = lax.fori_loop(0, n_sc, body, jnp.zeros((64, bq), F32))
        return jnp.sum(acc, axis=0, keepdims=True)

    thr_s[0:1, :] = jnp.full((1, bq), -jnp.finfo(jnp.float32).max, F32)

    @pl.when((qb + 1) * bq > n_sel)
    def _():
        active = qpos + 1 > n_sel

        def cond(st):
            it, lo, cnt_lo = st
            pending = active & (cnt_lo != n_sel)
            return (it < 32) & (jnp.max(jnp.where(pending, 1.0, 0.0)) > 0.0)

        def body(st):
            it, lo, cnt_lo = st
            for u in range(4):
                cand = lo ^ jnp.left_shift(jnp.int32(1), 31 - (it + u))
                thr_c = _key_to_f32(cand)
                cnt = count(lambda t: t >= thr_c)
                ok = cnt >= n_sel
                lo, cnt_lo = jnp.where(ok, cand, lo), jnp.where(ok, cnt, cnt_lo)
            return it + 4, lo, cnt_lo

        _, lo, cnt_lo = lax.while_loop(
            cond, body, (jnp.int32(0), jnp.full((1, bq), INT_MIN, I32), jnp.full((1, bq), 3.0e38, F32)))
        thr = _key_to_f32(lo)
        tie = active & (cnt_lo > n_sel)

        @pl.when(jnp.max(jnp.where(tie, 1.0, 0.0)) > 0.0)
        def _():
            need = n_sel - count(lambda t: t > thr)
            ri = lax.broadcasted_iota(I32, (sc_w, sc_w), 0)
            ci = lax.broadcasted_iota(I32, (sc_w, sc_w), 1)
            lower = jnp.where(ri >= ci, 1.0, 0.0).astype(BF16)

            def tie_body(sc, run):
                t = sc_s[sc]
                eq = (t == thr) & tie
                pre = jnp.dot(lower, jnp.where(eq, 1.0, 0.0).astype(BF16), preferred_element_type=F32)
                sc_s[sc] = jnp.where(eq & (run + pre > need), -jnp.inf, t)
                return run + pre[sc_w - 1:sc_w, :]

            lax.fori_loop(0, n_sc, tie_body, jnp.zeros((1, bq), F32))

        thr_s[0:1, :] = jnp.where(active, thr, thr_s[0:1, :])

    thr = thr_s[0:1, :]

    hq = DSA_HEADS * bq

    def att_body(sc, carry):
        m_prev, l_prev, acc = carry
        k0 = pl.multiple_of(sc * sc_w, sc_w)
        st = _dot_nt(k_ref[pl.ds(k0, sc_w), :], qs_s[...])
        drop = jnp.where(sc_s[sc] >= thr, 0.0, 2 * NEG_BIG)
        st = st + jnp.tile(drop, (1, DSA_HEADS))
        m_cur = jnp.max(jnp.max(st.reshape(sc_w // 64, 64, hq), axis=0), axis=0, keepdims=True)
        m_next = jnp.maximum(m_prev, m_cur)
        p = jnp.exp2(st - m_next)
        alpha = jnp.exp2(m_prev - m_next)
        l_cur = jnp.sum(jnp.sum(p.reshape(sc_w // 64, 64, hq), axis=0), axis=0, keepdims=True)
        pv = jnp.dot(vt_ref[sc], p.astype(BF16), preferred_element_type=F32)
        return m_next, alpha * l_prev + l_cur, alpha * acc + pv

    def att_body2(i, carry):
        return att_body(2 * i + 1, att_body(2 * i, carry))

    carry = lax.fori_loop(0, n_sc // 2, att_body2, (jnp.full((1, hq), NEG_BIG, F32), jnp.zeros((1, hq), F32),
                                                    jnp.zeros((DSA_HEAD_DIM, hq), F32)))
    _, l_fin, acc = lax.cond(n_sc % 2 == 1, lambda c: att_body(n_sc - 1, c), lambda c: c, carry)
    out_t = jnp.concatenate([acc / l_fin, jnp.zeros((LANES - DSA_HEAD_DIM, hq), F32)], axis=0).T
    for h in range(DSA_HEADS):
        o_ref[:, h * DSA_HEAD_DIM:(h + 1) * DSA_HEAD_DIM] = out_t[h * bq:(h + 1) * bq, 0:DSA_HEAD_DIM].astype(BF16)


def _dsa(iq, gt, dq, ik, dk, dvt, b, s, n_sel):
    bq, sc_w = DSA_BQ, DSA_SC
    nq = s // bq
    qblk = lambda n: pl.BlockSpec((bq, n), lambda i, j: (i * nq + j, 0))
    kblk = pl.BlockSpec((s, 64), lambda i, j: (i, 0))
    return pl.pallas_call(
        functools.partial(_dsa_kernel, bq=bq, sc_w=sc_w, n_sel=n_sel),
        grid=(b, nq),
        in_specs=[qblk(512), qblk(LANES), qblk(256), kblk, kblk,
                  pl.BlockSpec((s // sc_w, DSA_HEAD_DIM, sc_w), lambda i, j: (i, 0, 0))],
        out_specs=qblk(256),
        out_shape=jax.ShapeDtypeStruct((b * s, 256), BF16),
        scratch_shapes=[pltpu.VMEM((s // sc_w, sc_w, bq), F32), pltpu.VMEM((IDX_HEADS // 2, 2 * bq, IDX_DIM), BF16),
                        pltpu.VMEM((DSA_HEADS * bq, DSA_HEAD_DIM), BF16), pltpu.VMEM((8, bq), F32)],
        compiler_params=_cparams(("parallel", "arbitrary")),
        name="dsa",
    )(iq, gt, dq, ik, dk, dvt)


def _flash_kernel(qt_ref, kt_ref, q_ref, k_ref, v_ref, o_ref, m_s, acc_s, *, blk):
    t = pl.program_id(1)
    qi, ki = qt_ref[t], kt_ref[t]

    @pl.when(ki == 0)
    def _():
        m_s[...] = jnp.full(m_s.shape, NEG_BIG, F32)
        acc_s[...] = jnp.zeros_like(acc_s)

    def step(diagonal):
        if diagonal:
            keep = lax.broadcasted_iota(I32, (blk, blk), 0) >= lax.broadcasted_iota(I32, (blk, blk), 1)
        for h in range(FA_HEADS):
            sl = slice(h * HEAD_W, (h + 1) * HEAD_W)
            s = _dot_nt(q_ref[:, sl], k_ref[:, sl])
            if diagonal:
                s = jnp.where(keep, s, 2 * NEG_BIG)
            m_prev = m_s[h]
            m_next = jnp.maximum(m_prev, jnp.max(s, axis=1, keepdims=True))
            p = jnp.exp2(s - jnp.tile(m_next, (1, blk // LANES)))
            pv = jnp.dot(p.astype(BF16), v_ref[:, sl], preferred_element_type=F32)
            acc_s[h] = jnp.exp2(m_prev - m_next) * acc_s[h] + pv
            m_s[h] = m_next

    @pl.when(ki < qi)
    def _():
        step(False)

    @pl.when(ki == qi)
    def _():
        step(True)
        for h in range(FA_HEADS):
            acc = acc_s[h]
            o_ref[:, h * 64:(h + 1) * 64] = (acc / pltpu.roll(acc, 64, 1))[:, 0:64].astype(BF16)


def _flash(q, k, v, b, s, name):
    blk = FA_B
    nb = s // blk
    pairs = [(i, j) for i in range(nb) for j in range(i + 1)]
    qt = jnp.asarray([p[0] for p in pairs], I32)
    kt = jnp.asarray([p[1] for p in pairs], I32)
    w = FA_HEADS * HEAD_W
    grid_spec = pltpu.PrefetchScalarGridSpec(
        num_scalar_prefetch=2,
        grid=(b, len(pairs)),
        in_specs=[pl.BlockSpec((blk, w), lambda i, t, qt, kt: (i * nb + qt[t], 0)),
                  pl.BlockSpec((blk, w), lambda i, t, qt, kt: (i * nb + kt[t], 0)),
                  pl.BlockSpec((blk, w), lambda i, t, qt, kt: (i * nb + kt[t], 0))],
        out_specs=pl.BlockSpec((blk, FA_HEADS * 64), lambda i, t, qt, kt: (i * nb + qt[t], 0)),
        scratch_shapes=[pltpu.VMEM((FA_HEADS, blk, LANES), F32), pltpu.VMEM((FA_HEADS, blk, HEAD_W), F32)],
    )
    return pl.pallas_call(
        functools.partial(_flash_kernel, blk=blk),
        grid_spec=grid_spec,
        out_shape=jax.ShapeDtypeStruct((b * s, FA_HEADS * 64), BF16),
        compiler_params=_cparams(("parallel", "arbitrary")),
        name=name,
    )(qt, kt, q, k, v)


def _outproj_kernel(x_ref, y0_ref, y1_ref, y2_ref, y3_ref, wo_ref, g_ref, wr_ref, br_ref,
                    xo_ref, h_ref, cw_ref, rs_ref):
    acc = x_ref[...]
    for i, y_ref in enumerate((y0_ref, y1_ref, y2_ref, y3_ref)):
        acc = acc + jnp.dot(y_ref[...], wo_ref[i * 256:(i + 1) * 256, :], preferred_element_type=F32)
    xo_ref[...] = acc
    h2 = _rms(acc, g_ref[...])
    h_ref[...] = h2.astype(BF16)
    h_hi, h_lo, _ = _split3(h2)
    lg = (jnp.dot(h_hi, wr_ref[0], preferred_element_type=F32) + jnp.dot(h_lo, wr_ref[0], preferred_element_type=F32)
          + jnp.dot(h_hi, wr_ref[1], preferred_element_type=F32) + br_ref[...])
    lane = lax.broadcasted_iota(I32, (1, LANES), 1)
    lanef = lane.astype(F32)
    is_g = (lane >= MOE_EXPERTS) & (lane < MOE_EXPERTS + MOE_GROUPS)
    gl = jnp.where(is_g, lg, -jnp.inf)
    gmax = jnp.max(gl, axis=1, keepdims=True)
    gidx = jnp.min(jnp.where(gl == gmax, lanef, 999.0), axis=1, keepdims=True) - MOE_EXPERTS
    g_p = 1.0 / jnp.sum(jnp.where(is_g, jnp.exp(gl - gmax), 0.0), axis=1, keepdims=True)
    in_g = (lane < MOE_EXPERTS) & (jnp.floor(lanef * (1.0 / MOE_EPG)) == gidx)
    el = jnp.where(in_g, lg, -jnp.inf)
    m1 = jnp.max(el, axis=1, keepdims=True)
    i1 = jnp.min(jnp.where(el == m1, lanef, 999.0), axis=1, keepdims=True)
    el2 = jnp.where(lanef == i1, -jnp.inf, el)
    m2 = jnp.max(el2, axis=1, keepdims=True)
    i2 = jnp.min(jnp.where(el2 == m2, lanef, 999.0), axis=1, keepdims=True)
    t = jnp.exp(m2 - m1)
    w1 = 1.0 / (1.0 + t)
    cw_ref[...] = jnp.where(lanef == i1, w1 * g_p, jnp.where(lanef == i2, t * w1 * g_p, 0.0))
    rs_ref[...] = jnp.where((lanef == i1) | (lanef == i2), 1.0, 0.0).astype(BF16)


def _outproj(x, ys, w_out, g, w_rg, b_rg, w_re, b_re):
    t = x.shape[0]
    wo = w_out.astype(BF16)
    wr = jnp.zeros((D_MODEL, LANES), F32).at[:, 0:MOE_EXPERTS].set(w_re).at[:, MOE_EXPERTS:MOE_EXPERTS + MOE_GROUPS].set(w_rg)
    wr_hi = wr.astype(BF16)
    wr = jnp.stack([wr_hi, (wr - wr_hi.astype(F32)).astype(BF16)])
    br =jnp.zeros((1, LANES), F32).at[0, 0:MOE_EXPERTS].set(b_re).at[0, MOE_EXPERTS:MOE_EXPERTS + MOE_GROUPS].set(b_rg)
    tok = lambda n: pl.BlockSpec((TM, n), lambda i: (i, 0))
    full = lambda a: pl.BlockSpec(a.shape, lambda i: (0,) * a.ndim)
    return pl.pallas_call(
        _outproj_kernel,
        grid=(t // TM,),
        in_specs=[tok(D_MODEL)] + [tok(256)] * 4 + [full(wo), full(g), full(wr), full(br)],
        out_specs=[tok(D_MODEL), tok(D_MODEL), tok(LANES), tok(LANES)],
        out_shape=[jax.ShapeDtypeStruct((t, D_MODEL), F32), jax.ShapeDtypeStruct((t, D_MODEL), BF16),
                   jax.ShapeDtypeStruct((t, LANES), F32), jax.ShapeDtypeStruct((t, LANES), BF16)],
        compiler_params=_cparams(("parallel",)),
        name="outproj_router",
    )(x, *ys, wo, g, wr, br)


def _moe_plan_kernel(rs_ref, cw_ref, meta_ref, col_ref, row_ref, *, n):
    sel = rs_ref[...]
    self32 = sel.astype(F32)
    ti = lax.broadcasted_iota(I32, (n, n), 0)
    tj = lax.broadcasted_iota(I32, (n, n), 1)
    rank = jnp.dot(jnp.where(ti > tj, 1.0, 0.0).astype(BF16), sel, preferred_element_type=F32)
    cnt = jnp.sum(self32, axis=0, keepdims=True).astype(I32)
    cpad = ((cnt + (MOE_ALIGN - 1)) // MOE_ALIGN) * MOE_ALIGN
    li = lax.broadcasted_iota(I32, (LANES, LANES), 0)
    lj = lax.broadcasted_iota(I32, (LANES, LANES), 1)
    off = _dot_f32_by_mask(li < lj, jnp.broadcast_to(cpad.astype(F32), (8, LANES)), False)[0:1, :]
    dest = off + rank
    lane = lax.broadcasted_iota(I32, (1, LANES), 1)
    lanef = lane.astype(F32)
    on = self32 > 0.5
    e_lo = jnp.min(jnp.where(on, lanef, 999.0), axis=1, keepdims=True)
    e_hi = jnp.max(jnp.where(on, lanef, -1.0), axis=1, keepdims=True)
    is_lo, is_hi = lanef == e_lo, lanef == e_hi
    cw = cw_ref[...]
    pick = lambda m, a: jnp.sum(jnp.where(m, a, 0.0), axis=1, keepdims=True)
    d_lo, d_hi, w_lo, w_hi = pick(is_lo, dest), pick(is_hi, dest), pick(is_lo, cw), pick(is_hi, cw)
    colv = jnp.where(lane == 0, d_lo, jnp.where(lane == 1, d_hi, jnp.where(lane == 2, w_lo, jnp.where(lane == 3, w_hi, 0.0))))
    col_ref[...] = colv
    row_ref[0] = colv.T[0:8, :]
    sub = lax.broadcasted_iota(I32, (8, LANES), 0)
    meta_ref[0] = jnp.where(sub == 0, off.astype(I32), jnp.where(sub == 1, cnt, 0))


def _moe_kernel(meta_ref, h_ref, col_ref, row_ref, wg_ref, wu_ref, wd_ref, o_ref, s_s, y_s, *, n, ns):
    t, e = pl.program_id(0), pl.program_id(1)

    @pl.when(e == 0)
    def _():
        d_lo, d_hi = row_ref[0, 0:1, :], row_ref[0, 1:2, :]
        for r in range(ns // LANES):
            si = (r * LANES + lax.broadcasted_iota(I32, (LANES, 1), 0)).astype(F32)
            perm = jnp.where((si == d_lo) | (si == d_hi), 1.0, 0.0).astype(BF16)
            s_s[r * LANES:(r + 1) * LANES, :] = jnp.dot(perm, h_ref[...], preferred_element_type=F32).astype(BF16)
        y_s[...] = jnp.zeros_like(y_s)

    def chunk(j, r0):
        r0 = pl.multiple_of(r0, MOE_ALIGN)
        xs = s_s[pl.ds(r0, MOE_CH), :]
        gate = jnp.dot(xs, wg_ref[j], preferred_element_type=F32)
        up = jnp.dot(xs, wu_ref[j], preferred_element_type=F32)
        hid = (_silu(gate) * up).astype(BF16)
        y_s[pl.ds(r0, MOE_CH), :] = jnp.dot(hid, wd_ref[j], preferred_element_type=F32).astype(BF16)

    offs = [meta_ref[t * 2 * MOE_EXPERTS + e * MOE_EPS + j] for j in range(MOE_EPS)]
    cnts = [meta_ref[t * 2 * MOE_EXPERTS + MOE_EXPERTS + e * MOE_EPS + j] for j in range(MOE_EPS)]
    for j in range(MOE_EPS):
        chunk(j, offs[j])
    for j in range(MOE_EPS):
        seg_end = offs[j] + ((cnts[j] + MOE_ALIGN - 1) // MOE_ALIGN) * MOE_ALIGN

        def body(i, carry, j=j, seg_end=seg_end):
            chunk(j, jnp.minimum(offs[j] + i * MOE_CH, seg_end - MOE_CH))
            return carry

        lax.fori_loop(1, (cnts[j] + MOE_CH - 1) // MOE_CH, body, 0)

    @pl.when(e == pl.num_programs(1) - 1)
    def _():
        rows, kc = 256, ns // 3
        for c in range(n // rows):
            cv = col_ref[c * rows:(c + 1) * rows, :]
            d_lo, d_hi, w_lo, w_hi = cv[:, 0:1], cv[:, 1:2], cv[:, 2:3], cv[:, 3:4]
            acc = jnp.zeros((rows, D_MODEL), F32)
            for r in range(ns // kc):
                si = (r * kc + lax.broadcasted_iota(I32, (1, kc), 1)).astype(F32)
                pw = (jnp.where(si == d_lo, w_lo, 0.0) + jnp.where(si == d_hi, w_hi, 0.0)).astype(BF16)
                acc = acc + jnp.dot(pw, y_s[r * kc:(r + 1) * kc, :], preferred_element_type=F32)
            o_ref[c * rows:(c + 1) * rows, :] = acc.astype(BF16)


def _moe(h2, cw, rs, w_gate, w_up, w_down):
    t = h2.shape[0]
    n, ns = MOE_N, MOE_NS
    nt = t // n
    meta, col, row = pl.pallas_call(
        functools.partial(_moe_plan_kernel, n=n),
        grid=(nt,),
        in_specs=[pl.BlockSpec((n, LANES), lambda i: (i, 0)), pl.BlockSpec((n, LANES), lambda i: (i, 0))],
        out_specs=[pl.BlockSpec((1, 8, LANES), lambda i: (i, 0, 0)), pl.BlockSpec((n, LANES), lambda i: (i, 0)),
                   pl.BlockSpec((1, 8, n), lambda i: (i, 0, 0))],
        out_shape=[jax.ShapeDtypeStruct((nt, 8, LANES), I32), jax.ShapeDtypeStruct((t, LANES), F32),
                   jax.ShapeDtypeStruct((nt, 8, n), F32)],
        compiler_params=_cparams(("parallel",)),
        name="moe_plan",
    )(rs, cw)
    meta_flat = meta[:, 0:2, 0:MOE_EXPERTS].reshape(-1)
    wg, wu, wd = w_gate.astype(BF16), w_up.astype(BF16), w_down.astype(BF16)
    grid_spec = pltpu.PrefetchScalarGridSpec(
        num_scalar_prefetch=1,
        grid=(nt, MOE_EXPERTS // MOE_EPS),
        in_specs=[pl.BlockSpec((n, D_MODEL), lambda i, e, m: (i, 0)),
                  pl.BlockSpec((n, LANES), lambda i, e, m: (i, 0)),
                  pl.BlockSpec((1, 8, n), lambda i, e, m: (i, 0, 0)),
                  pl.BlockSpec((MOE_EPS, D_MODEL, MOE_FF), lambda i, e, m: (e, 0, 0)),
                  pl.BlockSpec((MOE_EPS, D_MODEL, MOE_FF), lambda i, e, m: (e, 0, 0)),
                  pl.BlockSpec((MOE_EPS, MOE_FF, D_MODEL), lambda i, e, m: (e, 0, 0))],
        out_specs=pl.BlockSpec((n, D_MODEL), lambda i, e, m: (i, 0)),
        scratch_shapes=[pltpu.VMEM((ns, D_MODEL), BF16), pltpu.VMEM((ns, D_MODEL), BF16)],
    )
    return pl.pallas_call(
        functools.partial(_moe_kernel, n=n, ns=ns),
        grid_spec=grid_spec,
        out_shape=jax.ShapeDtypeStruct((t, D_MODEL), BF16),
        compiler_params=_cparams(("parallel", "arbitrary")),
        name="moe_experts",
    )(meta_flat, h2, col, row, wg, wu, wd)


def _final_kernel(x_ref, y_ref, g_ref, o_ref):
    o_ref[...] = _rms(x_ref[...] + y_ref[...].astype(F32), g_ref[...])


def _final(x, y, g):
    t = x.shape[0]
    tok = pl.BlockSpec((TM, D_MODEL), lambda i: (i, 0))
    return pl.pallas_call(
        _final_kernel,
        grid=(t // TM,),
        in_specs=[tok, tok, pl.BlockSpec((1, D_MODEL), lambda i: (0, 0))],
        out_specs=tok,
        out_shape=jax.ShapeDtypeStruct((t, D_MODEL), F32),
        compiler_params=_cparams(("parallel",)),
        name="final_norm",
    )(x, y, g)


def kernel(x, positions, norm_mix, w_in, ssd_conv_w, ssd_conv_b, ssd_dt_bias, ssd_a_log, ssd_d, ssd_norm, mla_q_norm, mla_w_uq, mla_kv_norm, mla_w_ukv, fox_f_bias, w_out, norm_ffn, router_group_w, router_group_b, router_expert_w, router_expert_b, expert_w_gate, expert_w_up, expert_w_down, final_norm):
    b, s, d = x.shape
    assert d == D_MODEL and s % FA_B == 0 and (b * s) % MOE_N == 0 and TM == DSA_SC
    depth = w_in.shape[0]
    n_sel = min(IDX_TOPK_MAX, s // 4)
    tab = _rope_tables(positions)
    xf = x.reshape(b * s, d)
    y_ffn = None
    for l in range(depth):
        xf, p = _inproj(xf, y_ffn, norm_mix[l].reshape(1, d), _pack_w_in(w_in[l]), tab)
        mq, mk, mv = _mla_prep(p["mc"], mla_q_norm[l], mla_kv_norm[l], mla_w_uq[l], mla_w_ukv[l], tab)
        fq, fk = _fox_cum(p["gt"], fox_f_bias[l], p["fq"], p["fk"], b, s)
        y_ssd = _ssd(p["zx"], p["gt"], ssd_conv_w[l], ssd_conv_b[l], ssd_dt_bias[l], ssd_a_log[l], ssd_d[l],
                     ssd_norm[l], b, s)
        y_dsa = _dsa(p["iq"], p["gt"], p["dq"], p["ik"], p["dk"], p["dv"], b, s, n_sel)
        y_mla = _flash(mq, mk, mv, b, s, "flash_mla")
        y_fox = _flash(fq, fk, p["fv"], b, s, "flash_fox")
        xf, h2, cw, rs = _outproj(xf, (y_ssd, y_dsa, y_mla, y_fox), w_out[l], norm_ffn[l].reshape(1, d),
                                  router_group_w[l], router_group_b[l], router_expert_w[l], router_expert_b[l])
        y_ffn = _moe(h2, cw, rs, expert_w_gate[l], expert_w_up[l], expert_w_down[l])
    return _final(xf, y_ffn, final_norm.reshape(1, d)).reshape(b, s, d)
```

```python
import functools
import math

import jax
import jax.numpy as jnp
from jax import lax
from jax.experimental import pallas as pl
from jax.experimental.pallas import tpu as pltpu

F32 = jnp.float32
BF16 = jnp.bfloat16
I32 = jnp.int32
EPS = 1e-6
ROPE_THETA = 500000.0
HIGHEST = lax.Precision.HIGHEST

D_MODEL = 1024
SSD_HEADS, SSD_HEAD_DIM, SSD_GROUPS, SSD_STATE, SSD_CONV = 4, 64, 2, 128, 4
SSD_D_INNER = SSD_HEADS * SSD_HEAD_DIM
SSD_CONV_DIM = SSD_D_INNER + 2 * SSD_GROUPS * SSD_STATE
DSA_HEADS, DSA_HEAD_DIM, IDX_HEADS, IDX_DIM, IDX_TOPK_MAX = 4, 64, 8, 64, 256
MLA_HEADS, MLA_Q_RANK, MLA_KV_RANK, MLA_NOPE, MLA_ROPE, MLA_V = 4, 256, 128, 64, 32, 64
FOX_HEADS, FOX_HEAD_DIM = 4, 64
MOE_GROUPS, MOE_EPG, MOE_EXPERTS, MOE_FF = 4, 8, 32, 256

LANES = 128
VMEM_LIMIT = 56 * 1024 * 1024
INT_MIN = -(2 ** 31)
NEG_BIG = -1e30

TM = 512
SSD_Q = 256
DSA_BQ = 256
DSA_SC = 512
FA_B = 512
MOE_N = 1024
MOE_ALIGN = 16
MOE_CH = 128
MOE_EPS = 2
MOE_SUB = 2
MOE_NS = 2 * MOE_N + MOE_EXPERTS * MOE_ALIGN + MOE_CH

C_ZX, C_GT, C_DQ, C_KV, C_IK, C_IQ, C_MC, C_FQ, C_FK, C_FV, C_END = (
    0, 1024, 1152, 1408, 1536, 1664, 2176, 2688, 3200, 3712, 4224)
G_FOX, G_DT, G_IW = 0, 4, 8
LOG2E = math.log2(math.e)
HEAD_W = 128
FA_HEADS = 4
FOX_ONE_Q, FOX_F_Q, FOX_F_K, FOX_ONE_K = 64, 67, 64, 67


def _cparams(sem):
    return pltpu.CompilerParams(dimension_semantics=sem, vmem_limit_bytes=VMEM_LIMIT)


def _rms(xf, g):
    return xf * lax.rsqrt(jnp.mean(xf * xf, axis=-1, keepdims=True) + EPS) * g


def _softplus(x):
    return jnp.maximum(x, 0.0) + jnp.log1p(jnp.exp(-jnp.abs(x)))


def _silu(x):
    return x * jax.nn.sigmoid(x)


def _rope_apply(a, c, sa, sb, half):
    return a * c + pltpu.roll(a, LANES - half, 1) * sa + pltpu.roll(a, half, 1) * sb


def _dot_nt(a, b):
    return lax.dot_general(a, b, (((1,), (1,)), ((), ())), preferred_element_type=F32)


def _split3(x):
    hi = x.astype(BF16)
    r = x - hi.astype(F32)
    mid = r.astype(BF16)
    return hi, mid, (r - mid.astype(F32)).astype(BF16)


def _dot_f32_by_mask(mask, x, mask_left):
    m = jnp.where(mask, 1.0, 0.0).astype(BF16)
    out = None
    for piece in _split3(x):
        d = jnp.dot(m, piece, preferred_element_type=F32) if mask_left else jnp.dot(piece, m, preferred_element_type=F32)
        out = d if out is None else out + d
    return out


def _rope_tab_kernel(pos_ref, inv_ref, o_ref):
    pos = pos_ref[...].astype(F32)
    lane = lax.broadcasted_iota(I32, (1, LANES), 1)
    ang = pos * inv_ref[0:1, :]
    c, s = jnp.cos(ang), jnp.sin(ang)
    jd = lane % 64
    o_ref[:, 0:128] = c
    o_ref[:, 128:256] = jnp.where(jd < 8, -s, 0.0)
    o_ref[:, 256:384] = jnp.where((jd >= 8) & (jd < 16), s, 0.0)
    ang = pos * inv_ref[1:2, :]
    c, s = jnp.cos(ang), jnp.sin(ang)
    jm = lane - 64
    o_ref[:, 384:512] = c
    o_ref[:, 512:640] = jnp.where((jm >= 0) & (jm < 16), -s, 0.0)
    o_ref[:, 640:768] = jnp.where((jm >= 16) & (jm < 32), s, 0.0)


def _rope_tables(positions):
    t = positions.size
    half_d, half_m = DSA_HEAD_DIM // 4 // 2, MLA_ROPE // 2
    inv_d = ROPE_THETA ** (-jnp.arange(half_d, dtype=F32) / half_d)
    inv_m = ROPE_THETA ** (-jnp.arange(half_m, dtype=F32) / half_m)
    lane = jnp.arange(LANES)
    jd = lane % 64
    row_d = jnp.where(jd < 2 * half_d, inv_d[jd % half_d], 0.0)
    jm = lane - 64
    row_m = jnp.where((jm >= 0) & (jm < 2 * half_m), inv_m[jm % half_m], 0.0)
    inv = jnp.zeros((8, LANES), F32).at[0].set(row_d).at[1].set(row_m)
    tm = 1024
    return pl.pallas_call(
        _rope_tab_kernel,
        grid=(t // tm,),
        in_specs=[pl.BlockSpec((tm, 1), lambda i: (i, 0)), pl.BlockSpec((8, LANES), lambda i: (0, 0))],
        out_specs=pl.BlockSpec((tm, 768), lambda i: (i, 0)),
        out_shape=jax.ShapeDtypeStruct((t, 768), F32),
        compiler_params=_cparams(("parallel",)),
        name="rope_tables",
    )(positions.reshape(t, 1), inv)


def _inproj_kernel(*refs, has_add):
    if has_add:
        (x_ref, yp_ref, g_ref, w_ref, tab_ref, xo_ref, zx_ref, gt_ref, dq_ref, dk_ref, dv_ref,
         ik_ref, iq_ref, mc_ref, fq_ref, fk_ref, fv_ref) = refs
        x = x_ref[...] + yp_ref[...].astype(F32)
        xo_ref[...] = x
    else:
        (x_ref, g_ref, w_ref, tab_ref, zx_ref, gt_ref, dq_ref, dk_ref, dv_ref,
         ik_ref, iq_ref, mc_ref, fq_ref, fk_ref, fv_ref) = refs
        x = x_ref[...]
    h = _rms(x, g_ref[...]).astype(BF16)

    def proj(a, b):
        return jnp.dot(h, w_ref[:, a:b], preferred_element_type=F32)

    zx_ref[...] = proj(C_ZX, C_GT)
    gt_ref[...] = proj(C_GT, C_DQ)
    c_d, sa_d, sb_d = tab_ref[:, 0:128], tab_ref[:, 128:256], tab_ref[:, 256:384]
    c_m, sa_m, sb_m = tab_ref[:, 384:512], tab_ref[:, 512:640], tab_ref[:, 640:768]
    q = proj(C_DQ, C_KV)
    for i in range(2):
        dq_ref[:, i * 128:(i + 1) * 128] = _rope_apply(q[:, i * 128:(i + 1) * 128], c_d, sa_d, sb_d, 8).astype(BF16)
    kv = proj(C_KV, C_IK)
    is_k = lax.broadcasted_iota(I32, (1, LANES), 1) < 64
    kvr = _rope_apply(kv, jnp.where(is_k, c_d, 1.0), jnp.where(is_k, sa_d, 0.0), jnp.where(is_k, sb_d, 0.0), 8)
    dk_ref[...] = kvr[:, 0:64].astype(BF16)
    dv_ref[0] = kvr.T[64:128, :].astype(BF16)
    ikb = proj(C_IK, C_IQ)
    ik_ref[...] = _rope_apply(ikb, c_d, sa_d, sb_d, 8)[:, 0:64].astype(BF16)
    iq = proj(C_IQ, C_MC)
    for i in range(4):
        iq_ref[:, i * 128:(i + 1) * 128] = _rope_apply(iq[:, i * 128:(i + 1) * 128], c_d, sa_d, sb_d, 8).astype(BF16)
    mc = proj(C_MC, C_FQ)
    mc_ref[:, 0:384] = mc[:, 0:384]
    mc_ref[:, 384:512] = _rope_apply(mc[:, 384:512], c_m, sa_m, sb_m, 16)
    fq_ref[...] = proj(C_FQ, C_FK).astype(BF16)
    fk_ref[...] = proj(C_FK, C_FV).astype(BF16)
    ones_hi = jnp.where(lax.broadcasted_iota(I32, (1, FA_HEADS * HEAD_W), 1) % HEAD_W >= 64, 1.0, 0.0)
    fv_ref[...] = (proj(C_FV, C_END) + ones_hi).astype(BF16)


def _pack_w_in(w):
    d = w.shape[0]
    z = lambda n: jnp.zeros((d, n), F32)
    o = 0
    parts = {}
    for name, width in (("z", 256), ("xbc", 768), ("dt", 4), ("dq", 256), ("dk", 64), ("dv", 64), ("iq", 512),
                        ("ik", 64), ("iw", 8), ("cq", 256), ("ckv", 128), ("kr", 32), ("fq", 256), ("fk", 256),
                        ("fv", 256), ("ff", 4)):
        parts[name] = w[:, o:o + width]
        o += width
    scale = DSA_HEAD_DIM ** -0.5

    def per_head(a):
        return jnp.pad(a.reshape(d, FA_HEADS, 64), ((0, 0), (0, 0), (0, HEAD_W - 64))).reshape(d, FA_HEADS * HEAD_W)

    cat = jnp.concatenate([
        parts["z"], parts["xbc"],
        parts["ff"], parts["dt"], parts["iw"], z(112),
        parts["dq"] * (scale * LOG2E),
        parts["dk"], parts["dv"],
        parts["ik"], z(64),
        parts["iq"],
        parts["cq"], parts["ckv"], z(64), parts["kr"], z(32),
        per_head(parts["fq"] * (FOX_HEAD_DIM ** -0.5 * LOG2E)), per_head(parts["fk"]), per_head(parts["fv"])], axis=1)
    return cat.astype(BF16)


def _inproj(x, y_prev, g, w_cat, tab):
    t = x.shape[0]
    has_add = y_prev is not None
    tok = lambda n: pl.BlockSpec((TM, n), lambda i: (i, 0))
    full = lambda a: pl.BlockSpec(a.shape, lambda i: (0,) * a.ndim)
    ins = [x] + ([y_prev] if has_add else []) + [g, w_cat, tab]
    in_specs = [tok(D_MODEL)] + ([tok(D_MODEL)] if has_add else []) + [full(g), full(w_cat), tok(768)]
    outs = ([("x", D_MODEL, F32)] if has_add else []) + [
        ("zx", 1024, F32), ("gt", 128, F32), ("dq", 256, BF16), ("dk", 64, BF16), ("dv", 64, BF16),
        ("ik", 64, BF16), ("iq", 512, BF16), ("mc", 512, F32), ("fq", 512, BF16), ("fk", 512, BF16), ("fv", 512, BF16)]
    res = pl.pallas_call(
        functools.partial(_inproj_kernel, has_add=has_add),
        grid=(t // TM,),
        in_specs=in_specs,
        out_specs=[pl.BlockSpec((1, DSA_HEAD_DIM, TM), lambda i: (i, 0, 0)) if nm == "dv" else tok(n)
                   for nm, n, _ in outs],
        out_shape=[jax.ShapeDtypeStruct((t // TM, DSA_HEAD_DIM, TM) if nm == "dv" else (t, n), dt)
                   for nm, n, dt in outs],
        compiler_params=_cparams(("parallel",)),
        name="inproj",
    )(*ins)
    res = list(res)
    x_new = res.pop(0) if has_add else x
    return x_new, dict(zip([n for n, _, _ in outs if n != "x"], res))


def _mla_prep_kernel(mc_ref, qn_ref, kn_ref, wq_ref, wk_ref, wv_ref, tab_ref, q_ref, k_ref, v_ref):
    mc = mc_ref[...]
    cq = _rms(mc[:, 0:256], qn_ref[...]).astype(BF16)
    ckv = _rms(mc[:, 256:384], kn_ref[...]).astype(BF16)
    kr = mc[:, 384:512]
    c_m, sa_m, sb_m = tab_ref[:, 384:512], tab_ref[:, 512:640], tab_ref[:, 640:768]
    q = jnp.dot(cq, wq_ref[...], preferred_element_type=F32)
    k = jnp.dot(ckv, wk_ref[...], preferred_element_type=F32)
    for h in range(MLA_HEADS):
        sl = slice(h * 128, (h + 1) * 128)
        q_ref[:, sl] = _rope_apply(q[:, sl], c_m, sa_m, sb_m, 16).astype(BF16)
        k_ref[:, sl] = (k[:, sl] + kr).astype(BF16)
    ones_hi = jnp.where(lax.broadcasted_iota(I32, (1, FA_HEADS * HEAD_W), 1) % HEAD_W >= 64, 1.0, 0.0)
    v_ref[...] = (jnp.dot(ckv, wv_ref[...], preferred_element_type=F32) + ones_hi).astype(BF16)


def _mla_prep(mc, q_norm, kv_norm, w_uq, w_ukv, tab):
    t = mc.shape[0]
    dqk = MLA_NOPE + MLA_ROPE
    wq = jnp.pad(w_uq.reshape(MLA_Q_RANK, MLA_HEADS, dqk) * (dqk ** -0.5 * LOG2E), ((0, 0), (0, 0), (0, 128 - dqk)))
    wq = wq.reshape(MLA_Q_RANK, MLA_HEADS * 128).astype(BF16)
    wkv = w_ukv.reshape(MLA_KV_RANK, MLA_HEADS, MLA_NOPE + MLA_V)
    wk = jnp.pad(wkv[:, :, :MLA_NOPE], ((0, 0), (0, 0), (0, 128 - MLA_NOPE))).reshape(MLA_KV_RANK, MLA_HEADS * 128).astype(BF16)
    wv = jnp.pad(wkv[:, :, MLA_NOPE:], ((0, 0), (0, 0), (0, 128 - MLA_V))).reshape(MLA_KV_RANK, MLA_HEADS * 128).astype(BF16)
    tok = lambda n: pl.BlockSpec((TM, n), lambda i: (i, 0))
    full = lambda a: pl.BlockSpec(a.shape, lambda i: (0,) * a.ndim)
    qn, kn = q_norm.reshape(1, -1), kv_norm.reshape(1, -1)
    return pl.pallas_call(
        _mla_prep_kernel,
        grid=(t // TM,),
        in_specs=[tok(512), full(qn), full(kn), full(wq), full(wk), full(wv), tok(768)],
        out_specs=[tok(512), tok(512), tok(512)],
        out_shape=[jax.ShapeDtypeStruct((t, 512), BF16)] * 3,
        compiler_params=_cparams(("parallel",)),
        name="mla_prep",
    )(mc, qn, kn, wq, wk, wv, tab)


def _fox_cum_kernel(gt_ref, b_ref, q_ref, k_ref, qo_ref, ko_ref, carry_s, *, blk):
    @pl.when(pl.program_id(1) == 0)
    def _():
        carry_s[...] = jnp.zeros_like(carry_s)

    x = gt_ref[...] + b_ref[...]
    lf = -_softplus(-x)
    ri = lax.broadcasted_iota(I32, (blk, blk), 0)
    ci = lax.broadcasted_iota(I32, (blk, blk), 1)
    cs = _dot_f32_by_mask(ri >= ci, lf, True) + carry_s[0:1, :]
    carry_s[0:1, :] = cs[blk - 1:blk, :]
    f2 = cs * LOG2E
    lane = lax.broadcasted_iota(I32, (1, HEAD_W), 1)

    def pieces(f, base):
        hi = f.astype(BF16).astype(F32)
        mid = (f - hi).astype(BF16).astype(F32)
        lo = f - hi - mid
        return jnp.where(lane == base, hi, jnp.where(lane == base + 1, mid, jnp.where(lane == base + 2, lo, 0.0)))

    def ones(base):
        return jnp.where((lane >= base) & (lane < base + 3), 1.0, 0.0)

    for h in range(FOX_HEADS):
        fh = f2[:, G_FOX + h:G_FOX + h + 1]
        sl = slice(h * HEAD_W, (h + 1) * HEAD_W)
        qo_ref[:, sl] = (q_ref[:, sl].astype(F32) + pieces(fh, FOX_F_Q) + ones(FOX_ONE_Q)).astype(BF16)
        ko_ref[:, sl] = (k_ref[:, sl].astype(F32) - pieces(fh, FOX_F_K) + ones(FOX_ONE_K)).astype(BF16)


def _fox_cum(gt, f_bias, fq, fk, b, s):
    blk = 512
    bias = jnp.zeros((1, LANES), F32).at[0, G_FOX:G_FOX + FOX_HEADS].set(f_bias)
    nb = s // blk
    tok = lambda n: pl.BlockSpec((blk, n), lambda i, j: (i * nb + j, 0))
    w = FOX_HEADS * HEAD_W
    return pl.pallas_call(
        functools.partial(_fox_cum_kernel, blk=blk),
        grid=(b, nb),
        in_specs=[tok(LANES), pl.BlockSpec((1, LANES), lambda i, j: (0, 0)), tok(w), tok(w)],
        out_specs=[tok(w), tok(w)],
        out_shape=[jax.ShapeDtypeStruct((b * s, w), BF16)] * 2,
        scratch_shapes=[pltpu.VMEM((8, LANES), F32)],
        compiler_params=_cparams(("parallel", "arbitrary")),
        name="fox_cum",
    )(gt, bias, fq, fk)


def _ssd_kernel(zx_ref, gt_ref, cw_ref, cb_ref, pr_ref, pc_ref, drow_ref, nw_ref, o_ref, ext_s, st_s, y_s, *, q):
    @pl.when(pl.program_id(1) == 0)
    def _():
        ext_s[0:8, :] = jnp.zeros((8, SSD_CONV_DIM), F32)
        st_s[...] = jnp.zeros_like(st_s)

    raw = zx_ref[:, 256:1024]
    ext_s[8:8 + q, :] = raw
    acc = jnp.broadcast_to(cb_ref[...], (q, SSD_CONV_DIM))
    for j in range(SSD_CONV):
        acc = acc + cw_ref[j:j + 1, :] * ext_s[5 + j:5 + j + q, :]
    ext_s[0:8, :] = raw[q - 8:q, :]
    xbc = _silu(acc)
    xs = xbc[:, 0:SSD_D_INNER]

    g = gt_ref[...]
    lane = lax.broadcasted_iota(I32, (1, LANES), 1)
    dtc = _softplus(g + pr_ref[0:1, :])
    a_r = jnp.where((lane >= G_DT) & (lane < G_DT + SSD_HEADS), -jnp.exp(pr_ref[1:2, :]), 0.0)
    ri = lax.broadcasted_iota(I32, (q, q), 0)
    ci = lax.broadcasted_iota(I32, (q, q), 1)
    tri = ri >= ci
    acs_c = _dot_f32_by_mask(tri, dtc * a_r, True)
    sub = lax.broadcasted_iota(I32, (LANES, 1), 0)
    dtr = _softplus(g.T + pc_ref[:, 0:1])
    a_c = jnp.where((sub >= G_DT) & (sub < G_DT + SSD_HEADS), -jnp.exp(pc_ref[:, 1:2]), 0.0)
    acs_r = _dot_f32_by_mask(ri <= ci, dtr * a_c, False)

    rep = SSD_HEADS // SSD_GROUPS
    for gi in range(SSD_GROUPS):
        bg = xbc[:, SSD_D_INNER + gi * SSD_STATE:SSD_D_INNER + (gi + 1) * SSD_STATE]
        cg = xbc[:, SSD_D_INNER + (SSD_GROUPS + gi) * SSD_STATE:SSD_D_INNER + (SSD_GROUPS + gi + 1) * SSD_STATE]
        bt = bg.T.astype(BF16)
        cb16 = cg.astype(BF16)
        cbm = jnp.dot(cb16, bt, preferred_element_type=F32)
        for hh in range(rep):
            h = gi * rep + hh
            ac = acs_c[:, G_DT + h:G_DT + h + 1]
            ar = acs_r[G_DT + h:G_DT + h + 1, :]
            seg = jnp.where(tri, jnp.exp(ac - ar), 0.0)
            xh = xs[:, h * SSD_HEAD_DIM:(h + 1) * SSD_HEAD_DIM]
            xdt = xh * dtc[:, G_DT + h:G_DT + h + 1]
            yd = jnp.dot((cbm * seg).astype(BF16), xdt.astype(BF16), preferred_element_type=F32)
            aend = ac[q - 1:q, :]
            st = st_s[h]
            yo = jnp.dot(cb16, st.astype(BF16), preferred_element_type=F32) * jnp.exp(ac)
            st_s[h] = st * jnp.exp(aend) + jnp.dot(bt, (xdt * jnp.exp(aend - ac)).astype(BF16),
                                                   preferred_element_type=F32)
            y_s[:, h * SSD_HEAD_DIM:(h + 1) * SSD_HEAD_DIM] = yd + yo
    y = y_s[...] + drow_ref[...] * xs
    y = y * _silu(zx_ref[:, 0:SSD_D_INNER])
    o_ref[...] = _rms(y, nw_ref[...]).astype(BF16)


def _ssd(zx, gt, conv_w, conv_b, dt_bias, a_log, d_skip, norm_w, b, s):
    q = SSD_Q
    nc = s // q
    pr = jnp.zeros((8, LANES), F32).at[0, G_DT:G_DT + SSD_HEADS].set(dt_bias).at[1, G_DT:G_DT + SSD_HEADS].set(a_log)
    pc = pr.T
    drow = jnp.repeat(d_skip, SSD_HEAD_DIM).reshape(1, SSD_D_INNER)
    cb = conv_b.reshape(1, -1)
    nw = norm_w.reshape(1, -1)
    full = lambda a: pl.BlockSpec(a.shape, lambda i, j: (0,) * a.ndim)
    return pl.pallas_call(
        functools.partial(_ssd_kernel, q=q),
        grid=(b, nc),
        in_specs=[pl.BlockSpec((q, 1024), lambda i, j: (i * nc + j, 0)), pl.BlockSpec((q, LANES), lambda i, j: (i * nc + j, 0)),
                  full(conv_w), full(cb), full(pr), full(pc), full(drow), full(nw)],
        out_specs=pl.BlockSpec((q, SSD_D_INNER), lambda i, j: (i * nc + j, 0)),
        out_shape=jax.ShapeDtypeStruct((b * s, SSD_D_INNER), BF16),
        scratch_shapes=[pltpu.VMEM((8 + q, SSD_CONV_DIM), F32), pltpu.VMEM((SSD_HEADS, SSD_STATE, SSD_HEAD_DIM), F32),
                        pltpu.VMEM((q, SSD_D_INNER), F32)],
        compiler_params=_cparams(("parallel", "arbitrary")),
        name="ssd",
    )(zx, gt, conv_w, cb, pr, pc, drow, nw)


def _key_to_f32(k):
    return pltpu.bitcast(jnp.where(k < 0, k ^ jnp.int32(0x7FFFFFFF), k), F32)


def _dsa_kernel(iq_ref, gt_ref, q_ref, ik_ref, k_ref, vt_ref, o_ref, sc_s, qi_s, qs_s, thr_s, *, bq, sc_w, n_sel):
    qb = pl.program_id(1)
    n_sc = ((qb + 1) * bq + sc_w - 1) // sc_w
    for j in range(IDX_HEADS):
        qi_s[j // 2, (j % 2) * bq:(j % 2 + 1) * bq, :] = iq_ref[:, j * IDX_DIM:(j + 1) * IDX_DIM]
    for h in range(DSA_HEADS):
        qs_s[h * bq:(h + 1) * bq, :] = q_ref[:, h * DSA_HEAD_DIM:(h + 1) * DSA_HEAD_DIM]
    w_t = (gt_ref[...] * ((IDX_HEADS * IDX_DIM) ** -0.5)).T
    qpos = qb * bq + lax.broadcasted_iota(I32, (1, bq), 1)

    def score_body(sc, carry):
        k0 = pl.multiple_of(sc * sc_w, sc_w)
        kidx = ik_ref[pl.ds(k0, sc_w), :]
        sco = jnp.zeros((sc_w, bq), F32)
        for jp in range(IDX_HEADS // 2):
            a = _dot_nt(kidx, qi_s[jp])
            for u in range(2):
                j = 2 * jp + u
                sco = sco + jnp.maximum(a[:, u * bq:(u + 1) * bq], 0.0) * w_t[G_IW + j:G_IW + j + 1, :]
        kpos = k0 + lax.broadcasted_iota(I32, (sc_w, 1), 0)
        sc_s[sc] = jnp.where(kpos <= qpos, sco, -jnp.inf)
        return carry

    lax.fori_loop(0, n_sc, score_body, 0)

    def count(pred):
        def body(sc, acc):
            hit = jnp.where(pred(sc_s[sc]), 1.0, 0.0)
            return acc + jnp.sum(hit.reshape(sc_w // 64, 64, bq), axis=0)
        acc = lax.fori_loop(0, n_sc, body, jnp.zeros((64, bq), F32))
        return jnp.sum(acc, axis=0, keepdims=True)

    thr_s[0:1, :] = jnp.full((1, bq), -jnp.finfo(jnp.float32).max, F32)

    @pl.when((qb + 1) * bq > n_sel)
    def _():
        active = qpos + 1 > n_sel

        def cond(st):
            it, lo, cnt_lo = st
            pending = active & (cnt_lo != n_sel)
            return (it < 32) & (jnp.max(jnp.where(pending, 1.0, 0.0)) > 0.0)

        def body(st):
            it, lo, cnt_lo = st
            for u in range(4):
                cand = lo ^ jnp.left_shift(jnp.int32(1), 31 - (it + u))
                thr_c = _key_to_f32(cand)
                cnt = count(lambda t: t >= thr_c)
                ok = cnt >= n_sel
                lo, cnt_lo = jnp.where(ok, cand, lo), jnp.where(ok, cnt, cnt_lo)
            return it + 4, lo, cnt_lo

        _, lo, cnt_lo = lax.while_loop(
            cond, body, (jnp.int32(0), jnp.full((1, bq), INT_MIN, I32), jnp.full((1, bq), 3.0e38, F32)))
        thr = _key_to_f32(lo)
        tie = active & (cnt_lo > n_sel)

        @pl.when(jnp.max(jnp.where(tie, 1.0, 0.0)) > 0.0)
        def _():
            need = n_sel - count(lambda t: t > thr)
            ri = lax.broadcasted_iota(I32, (sc_w, sc_w), 0)
            ci = lax.broadcasted_iota(I32, (sc_w, sc_w), 1)
            lower = jnp.where(ri >= ci, 1.0, 0.0).astype(BF16)

            def tie_body(sc, run):
                t = sc_s[sc]
                eq = (t == thr) & tie
                pre = jnp.dot(lower, jnp.where(eq, 1.0, 0.0).astype(BF16), preferred_element_type=F32)
                sc_s[sc] = jnp.where(eq & (run + pre > need), -jnp.inf, t)
                return run + pre[sc_w - 1:sc_w, :]

            lax.fori_loop(0, n_sc, tie_body, jnp.zeros((1, bq), F32))

        thr_s[0:1, :] = jnp.where(active, thr, thr_s[0:1, :])

    thr = thr_s[0:1, :]

    hq = DSA_HEADS * bq

    def att_body(sc, carry):
        m_prev, l_prev, acc = carry
        k0 = pl.multiple_of(sc * sc_w, sc_w)
        st = _dot_nt(k_ref[pl.ds(k0, sc_w), :], qs_s[...])
        drop = jnp.where(sc_s[sc] >= thr, 0.0, 2 * NEG_BIG)
        st = st + jnp.tile(drop, (1, DSA_HEADS))
        m_cur = jnp.max(jnp.max(st.reshape(sc_w // 64, 64, hq), axis=0), axis=0, keepdims=True)
        m_next = jnp.maximum(m_prev, m_cur)
        p = jnp.exp2(st - m_next)
        alpha = jnp.exp2(m_prev - m_next)
        l_cur = jnp.sum(jnp.sum(p.reshape(sc_w // 64, 64, hq), axis=0), axis=0, keepdims=True)
        pv = jnp.dot(vt_ref[sc], p.astype(BF16), preferred_element_type=F32)
        return m_next, alpha * l_prev + l_cur, alpha * acc + pv

    def att_body2(i, carry):
        return att_body(2 * i + 1, att_body(2 * i, carry))

    carry = lax.fori_loop(0, n_sc // 2, att_body2, (jnp.full((1, hq), NEG_BIG, F32), jnp.zeros((1, hq), F32),
                                                    jnp.zeros((DSA_HEAD_DIM, hq), F32)))
    _, l_fin, acc = lax.cond(n_sc % 2 == 1, lambda c: att_body(n_sc - 1, c), lambda c: c, carry)
    out_t = jnp.concatenate([acc / l_fin, jnp.zeros((LANES - DSA_HEAD_DIM, hq), F32)], axis=0).T
    for h in range(DSA_HEADS):
        o_ref[:, h * DSA_HEAD_DIM:(h + 1) * DSA_HEAD_DIM] = out_t[h * bq:(h + 1) * bq, 0:DSA_HEAD_DIM].astype(BF16)


def _dsa(iq, gt, dq, ik, dk, dvt, b, s, n_sel):
    bq, sc_w = DSA_BQ, DSA_SC
    nq = s // bq
    qblk = lambda n: pl.BlockSpec((bq, n), lambda i, j: (i * nq + j, 0))
    kblk = pl.BlockSpec((s, 64), lambda i, j: (i, 0))
    return pl.pallas_call(
        functools.partial(_dsa_kernel, bq=bq, sc_w=sc_w, n_sel=n_sel),
        grid=(b, nq),
        in_specs=[qblk(512), qblk(LANES), qblk(256), kblk, kblk,
                  pl.BlockSpec((s // sc_w, DSA_HEAD_DIM, sc_w), lambda i, j: (i, 0, 0))],
        out_specs=qblk(256),
        out_shape=jax.ShapeDtypeStruct((b * s, 256), BF16),
        scratch_shapes=[pltpu.VMEM((s // sc_w, sc_w, bq), F32), pltpu.VMEM((IDX_HEADS // 2, 2 * bq, IDX_DIM), BF16),
                        pltpu.VMEM((DSA_HEADS * bq, DSA_HEAD_DIM), BF16), pltpu.VMEM((8, bq), F32)],
        compiler_params=_cparams(("parallel", "arbitrary")),
        name="dsa",
    )(iq, gt, dq, ik, dk, dvt)


def _flash_kernel(qt_ref, kt_ref, q_ref, k_ref, v_ref, o_ref, m_s, acc_s, *, blk):
    t = pl.program_id(1)
    qi, ki = qt_ref[t], kt_ref[t]

    @pl.when(ki == 0)
    def _():
        m_s[...] = jnp.full(m_s.shape, NEG_BIG, F32)
        acc_s[...] = jnp.zeros_like(acc_s)

    def step(diagonal):
        if diagonal:
            keep = lax.broadcasted_iota(I32, (blk, blk), 0) >= lax.broadcasted_iota(I32, (blk, blk), 1)
        for h in range(FA_HEADS):
            sl = slice(h * HEAD_W, (h + 1) * HEAD_W)
            s = _dot_nt(q_ref[:, sl], k_ref[:, sl])
            if diagonal:
                s = jnp.where(keep, s, 2 * NEG_BIG)
            m_prev = m_s[h]
            m_next = jnp.maximum(m_prev, jnp.max(s, axis=1, keepdims=True))
            p = jnp.exp2(s - jnp.tile(m_next, (1, blk // LANES)))
            pv = jnp.dot(p.astype(BF16), v_ref[:, sl], preferred_element_type=F32)
            acc_s[h] = jnp.exp2(m_prev - m_next) * acc_s[h] + pv
            m_s[h] = m_next

    @pl.when(ki < qi)
    def _():
        step(False)

    @pl.when(ki == qi)
    def _():
        step(True)
        for h in range(FA_HEADS):
            acc = acc_s[h]
            o_ref[:, h * 64:(h + 1) * 64] = (acc / pltpu.roll(acc, 64, 1))[:, 0:64].astype(BF16)


def _flash(q, k, v, b, s, name):
    blk = FA_B
    nb = s // blk
    pairs = [(i, j) for i in range(nb) for j in range(i + 1)]
    qt = jnp.asarray([p[0] for p in pairs], I32)
    kt = jnp.asarray([p[1] for p in pairs], I32)
    w = FA_HEADS * HEAD_W
    grid_spec = pltpu.PrefetchScalarGridSpec(
        num_scalar_prefetch=2,
        grid=(b, len(pairs)),
        in_specs=[pl.BlockSpec((blk, w), lambda i, t, qt, kt: (i * nb + qt[t], 0)),
                  pl.BlockSpec((blk, w), lambda i, t, qt, kt: (i * nb + kt[t], 0)),
                  pl.BlockSpec((blk, w), lambda i, t, qt, kt: (i * nb + kt[t], 0))],
        out_specs=pl.BlockSpec((blk, FA_HEADS * 64), lambda i, t, qt, kt: (i * nb + qt[t], 0)),
        scratch_shapes=[pltpu.VMEM((FA_HEADS, blk, LANES), F32), pltpu.VMEM((FA_HEADS, blk, HEAD_W), F32)],
    )
    return pl.pallas_call(
        functools.partial(_flash_kernel, blk=blk),
        grid_spec=grid_spec,
        out_shape=jax.ShapeDtypeStruct((b * s, FA_HEADS * 64), BF16),
        compiler_params=_cparams(("parallel", "arbitrary")),
        name=name,
    )(qt, kt, q, k, v)


def _outproj_kernel(x_ref, y0_ref, y1_ref, y2_ref, y3_ref, wo_ref, g_ref, wr_ref, br_ref,
                    xo_ref, h_ref, cw_ref, rs_ref):
    acc = x_ref[...]
    for i, y_ref in enumerate((y0_ref, y1_ref, y2_ref, y3_ref)):
        acc = acc + jnp.dot(y_ref[...], wo_ref[i * 256:(i + 1) * 256, :], preferred_element_type=F32)
    xo_ref[...] = acc
    h2 = _rms(acc, g_ref[...])
    h_ref[...] = h2.astype(BF16)
    h_hi, h_lo, _ = _split3(h2)
    lg = (jnp.dot(h_hi, wr_ref[0], preferred_element_type=F32) + jnp.dot(h_lo, wr_ref[0], preferred_element_type=F32)
          + jnp.dot(h_hi, wr_ref[1], preferred_element_type=F32) + br_ref[...])
    lane = lax.broadcasted_iota(I32, (1, LANES), 1)
    lanef = lane.astype(F32)
    is_g = (lane >= MOE_EXPERTS) & (lane < MOE_EXPERTS + MOE_GROUPS)
    gl = jnp.where(is_g, lg, -jnp.inf)
    gmax = jnp.max(gl, axis=1, keepdims=True)
    gidx = jnp.min(jnp.where(gl == gmax, lanef, 999.0), axis=1, keepdims=True) - MOE_EXPERTS
    g_p = 1.0 / jnp.sum(jnp.where(is_g, jnp.exp(gl - gmax), 0.0), axis=1, keepdims=True)
    in_g = (lane < MOE_EXPERTS) & (jnp.floor(lanef * (1.0 / MOE_EPG)) == gidx)
    el = jnp.where(in_g, lg, -jnp.inf)
    m1 = jnp.max(el, axis=1, keepdims=True)
    i1 = jnp.min(jnp.where(el == m1, lanef, 999.0), axis=1, keepdims=True)
    el2 = jnp.where(lanef == i1, -jnp.inf, el)
    m2 = jnp.max(el2, axis=1, keepdims=True)
    i2 = jnp.min(jnp.where(el2 == m2, lanef, 999.0), axis=1, keepdims=True)
    t = jnp.exp(m2 - m1)
    w1 = 1.0 / (1.0 + t)
    cw_ref[...] = jnp.where(lanef == i1, w1 * g_p, jnp.where(lanef == i2, t * w1 * g_p, 0.0))
    rs_ref[...] = jnp.where((lanef == i1) | (lanef == i2), 1.0, 0.0).astype(BF16)


def _outproj(x, ys, w_out, g, w_rg, b_rg, w_re, b_re):
    t = x.shape[0]
    wo = w_out.astype(BF16)
    wr = jnp.zeros((D_MODEL, LANES), F32).at[:, 0:MOE_EXPERTS].set(w_re).at[:, MOE_EXPERTS:MOE_EXPERTS + MOE_GROUPS].set(w_rg)
    wr_hi = wr.astype(BF16)
    wr = jnp.stack([wr_hi, (wr - wr_hi.astype(F32)).astype(BF16)])
    br =jnp.zeros((1, LANES), F32).at[0, 0:MOE_EXPERTS].set(b_re).at[0, MOE_EXPERTS:MOE_EXPERTS + MOE_GROUPS].set(b_rg)
    tok = lambda n: pl.BlockSpec((TM, n), lambda i: (i, 0))
    full = lambda a: pl.BlockSpec(a.shape, lambda i: (0,) * a.ndim)
    return pl.pallas_call(
        _outproj_kernel,
        grid=(t // TM,),
        in_specs=[tok(D_MODEL)] + [tok(256)] * 4 + [full(wo), full(g), full(wr), full(br)],
        out_specs=[tok(D_MODEL), tok(D_MODEL), tok(LANES), tok(LANES)],
        out_shape=[jax.ShapeDtypeStruct((t, D_MODEL), F32), jax.ShapeDtypeStruct((t, D_MODEL), BF16),
                   jax.ShapeDtypeStruct((t, LANES), F32), jax.ShapeDtypeStruct((t, LANES), BF16)],
        compiler_params=_cparams(("parallel",)),
        name="outproj_router",
    )(x, *ys, wo, g, wr, br)


def _moe_plan_kernel(rs_ref, cw_ref, meta_ref, col_ref, row_ref, *, n):
    sel = rs_ref[...]
    self32 = sel.astype(F32)
    ti = lax.broadcasted_iota(I32, (n, n), 0)
    tj = lax.broadcasted_iota(I32, (n, n), 1)
    rank = jnp.dot(jnp.where(ti > tj, 1.0, 0.0).astype(BF16), sel, preferred_element_type=F32)
    cnt = jnp.sum(self32, axis=0, keepdims=True).astype(I32)
    cpad = ((cnt + (MOE_ALIGN - 1)) // MOE_ALIGN) * MOE_ALIGN
    li = lax.broadcasted_iota(I32, (LANES, LANES), 0)
    lj = lax.broadcasted_iota(I32, (LANES, LANES), 1)
    off = _dot_f32_by_mask(li < lj, jnp.broadcast_to(cpad.astype(F32), (8, LANES)), False)[0:1, :]
    dest = off + rank
    lane = lax.broadcasted_iota(I32, (1, LANES), 1)
    lanef = lane.astype(F32)
    on = self32 > 0.5
    e_lo = jnp.min(jnp.where(on, lanef, 999.0), axis=1, keepdims=True)
    e_hi = jnp.max(jnp.where(on, lanef, -1.0), axis=1, keepdims=True)
    is_lo, is_hi = lanef == e_lo, lanef == e_hi
    cw = cw_ref[...]
    pick = lambda m, a: jnp.sum(jnp.where(m, a, 0.0), axis=1, keepdims=True)
    d_lo, d_hi, w_lo, w_hi = pick(is_lo, dest), pick(is_hi, dest), pick(is_lo, cw), pick(is_hi, cw)
    colv = jnp.where(lane == 0, d_lo, jnp.where(lane == 1, d_hi, jnp.where(lane == 2, w_lo, jnp.where(lane == 3, w_hi, 0.0))))
    col_ref[...] = colv
    row_ref[0] = colv.T[0:8, :]
    sub = lax.broadcasted_iota(I32, (8, LANES), 0)
    meta_ref[0] = jnp.where(sub == 0, off.astype(I32), jnp.where(sub == 1, cnt, 0))


def _moe_kernel(meta_ref, h_ref, col_ref, row_ref, wg_ref, wu_ref, wd_ref, o_ref, s_s, y_s, *, n, ns):
    t, e = pl.program_id(0), pl.program_id(1)

    @pl.when(e == 0)
    def _():
        for g in range(MOE_SUB):
            d_lo, d_hi = row_ref[g, 0:1, :], row_ref[g, 1:2, :]
            hg = h_ref[g * n:(g + 1) * n, :]
            for r in range(ns // LANES):
                si = (r * LANES + lax.broadcasted_iota(I32, (LANES, 1), 0)).astype(F32)
                perm = jnp.where((si == d_lo) | (si == d_hi), 1.0, 0.0).astype(BF16)
                s_s[g, r * LANES:(r + 1) * LANES, :] = jnp.dot(perm, hg, preferred_element_type=F32).astype(BF16)
        y_s[...] = jnp.zeros_like(y_s)

    def mlp(j, starts, store_ok):
        starts = [pl.multiple_of(r0, MOE_ALIGN) for r0 in starts]
        xs = jnp.concatenate([s_s[g, pl.ds(starts[g], MOE_CH), :] for g in range(MOE_SUB)], axis=0)
        gate = jnp.dot(xs, wg_ref[j], preferred_element_type=F32)
        up = jnp.dot(xs, wu_ref[j], preferred_element_type=F32)
        hid = (_silu(gate) * up).astype(BF16)
        y = jnp.dot(hid, wd_ref[j], preferred_element_type=F32).astype(BF16)
        for g in range(MOE_SUB):
            def store(g=g):
                y_s[g, pl.ds(starts[g], MOE_CH), :] = y[g * MOE_CH:(g + 1) * MOE_CH, :]
            if store_ok is None:
                store()
            else:
                pl.when(store_ok[g])(store)

    def meta(g, j, what):
        return meta_ref[(t * MOE_SUB + g) * 2 * MOE_EXPERTS + what * MOE_EXPERTS + e * MOE_EPS + j]

    offs = [[meta(g, j, 0) for g in range(MOE_SUB)] for j in range(MOE_EPS)]
    cnts = [[meta(g, j, 1) for g in range(MOE_SUB)] for j in range(MOE_EPS)]
    for j in range(MOE_EPS):
        mlp(j, offs[j], None)
    for j in range(MOE_EPS):
        ends = [offs[j][g] + ((cnts[j][g] + MOE_ALIGN - 1) // MOE_ALIGN) * MOE_ALIGN for g in range(MOE_SUB)]
        nch = [(cnts[j][g] + MOE_CH - 1) // MOE_CH for g in range(MOE_SUB)]

        def body(i, carry, j=j, ends=ends, nch=nch):
            more = [i < nch[g] for g in range(MOE_SUB)]
            starts = [jnp.where(more[g], jnp.minimum(offs[j][g] + i * MOE_CH, ends[g] - MOE_CH), offs[j][g])
                      for g in range(MOE_SUB)]
            mlp(j, starts, more)
            return carry

        lax.fori_loop(1, functools.reduce(jnp.maximum, nch), body, 0)

    @pl.when(e == pl.num_programs(1) - 1)
    def _():
        rows, kc = 256, ns // 3
        for g in range(MOE_SUB):
            for c in range(n // rows):
                r0 = g * n + c * rows
                cv = col_ref[r0:r0 + rows, :]
                d_lo, d_hi, w_lo, w_hi = cv[:, 0:1], cv[:, 1:2], cv[:, 2:3], cv[:, 3:4]
                acc = jnp.zeros((rows, D_MODEL), F32)
                for r in range(ns // kc):
                    si = (r * kc + lax.broadcasted_iota(I32, (1, kc), 1)).astype(F32)
                    pw = (jnp.where(si == d_lo, w_lo, 0.0) + jnp.where(si == d_hi, w_hi, 0.0)).astype(BF16)
                    acc = acc + jnp.dot(pw, y_s[g, r * kc:(r + 1) * kc, :], preferred_element_type=F32)
                o_ref[r0:r0 + rows, :] = acc.astype(BF16)


def _moe(h2, cw, rs, w_gate, w_up, w_down):
    t = h2.shape[0]
    n, ns = MOE_N, MOE_NS
    nt = t // n
    meta, col, row = pl.pallas_call(
        functools.partial(_moe_plan_kernel, n=n),
        grid=(nt,),
        in_specs=[pl.BlockSpec((n, LANES), lambda i: (i, 0)), pl.BlockSpec((n, LANES), lambda i: (i, 0))],
        out_specs=[pl.BlockSpec((1, 8, LANES), lambda i: (i, 0, 0)), pl.BlockSpec((n, LANES), lambda i: (i, 0)),
                   pl.BlockSpec((1, 8, n), lambda i: (i, 0, 0))],
        out_shape=[jax.ShapeDtypeStruct((nt, 8, LANES), I32), jax.ShapeDtypeStruct((t, LANES), F32),
                   jax.ShapeDtypeStruct((nt, 8, n), F32)],
        compiler_params=_cparams(("parallel",)),
        name="moe_plan",
    )(rs, cw)
    meta_flat = meta[:, 0:2, 0:MOE_EXPERTS].reshape(-1)
    wg, wu, wd = w_gate.astype(BF16), w_up.astype(BF16), w_down.astype(BF16)
    grid_spec = pltpu.PrefetchScalarGridSpec(
        num_scalar_prefetch=1,
        grid=(nt // MOE_SUB, MOE_EXPERTS // MOE_EPS),
        in_specs=[pl.BlockSpec((MOE_SUB * n, D_MODEL), lambda i, e, m: (i, 0)),
                  pl.BlockSpec((MOE_SUB * n, LANES), lambda i, e, m: (i, 0)),
                  pl.BlockSpec((MOE_SUB, 8, n), lambda i, e, m: (i, 0, 0)),
                  pl.BlockSpec((MOE_EPS, D_MODEL, MOE_FF), lambda i, e, m: (e, 0, 0)),
                  pl.BlockSpec((MOE_EPS, D_MODEL, MOE_FF), lambda i, e, m: (e, 0, 0)),
                  pl.BlockSpec((MOE_EPS, MOE_FF, D_MODEL), lambda i, e, m: (e, 0, 0))],
        out_specs=pl.BlockSpec((MOE_SUB * n, D_MODEL), lambda i, e, m: (i, 0)),
        scratch_shapes=[pltpu.VMEM((MOE_SUB, ns, D_MODEL), BF16), pltpu.VMEM((MOE_SUB, ns, D_MODEL), BF16)],
    )
    return pl.pallas_call(
        functools.partial(_moe_kernel, n=n, ns=ns),
        grid_spec=grid_spec,
        out_shape=jax.ShapeDtypeStruct((t, D_MODEL), BF16),
        compiler_params=_cparams(("parallel", "arbitrary")),
        name="moe_experts",
    )(meta_flat, h2, col, row, wg, wu, wd)


def _final_kernel(x_ref, y_ref, g_ref, o_ref):
    o_ref[...] = _rms(x_ref[...] + y_ref[...].astype(F32), g_ref[...])


def _final(x, y, g):
    t = x.shape[0]
    tok = pl.BlockSpec((TM, D_MODEL), lambda i: (i, 0))
    return pl.pallas_call(
        _final_kernel,
        grid=(t // TM,),
        in_specs=[tok, tok, pl.BlockSpec((1, D_MODEL), lambda i: (0, 0))],
        out_specs=tok,
        out_shape=jax.ShapeDtypeStruct((t, D_MODEL), F32),
        compiler_params=_cparams(("parallel",)),
        name="final_norm",
    )(x, y, g)


def kernel(x, positions, norm_mix, w_in, ssd_conv_w, ssd_conv_b, ssd_dt_bias, ssd_a_log, ssd_d, ssd_norm, mla_q_norm, mla_w_uq, mla_kv_norm, mla_w_ukv, fox_f_bias, w_out, norm_ffn, router_group_w, router_group_b, router_expert_w, router_expert_b, expert_w_gate, expert_w_up, expert_w_down, final_norm):
    b, s, d = x.shape
    assert d == D_MODEL and s % FA_B == 0 and (b * s) % (MOE_N * MOE_SUB) == 0 and TM == DSA_SC
    depth = w_in.shape[0]
    n_sel = min(IDX_TOPK_MAX, s // 4)
    tab = _rope_tables(positions)
    xf = x.reshape(b * s, d)
    y_ffn = None
    for l in range(depth):
        xf, p = _inproj(xf, y_ffn, norm_mix[l].reshape(1, d), _pack_w_in(w_in[l]), tab)
        mq, mk, mv = _mla_prep(p["mc"], mla_q_norm[l], mla_kv_norm[l], mla_w_uq[l], mla_w_ukv[l], tab)
        fq, fk = _fox_cum(p["gt"], fox_f_bias[l], p["fq"], p["fk"], b, s)
        y_ssd = _ssd(p["zx"], p["gt"], ssd_conv_w[l], ssd_conv_b[l], ssd_dt_bias[l], ssd_a_log[l], ssd_d[l],
                     ssd_norm[l], b, s)
        y_dsa = _dsa(p["iq"], p["gt"], p["dq"], p["ik"], p["dk"], p["dv"], b, s, n_sel)
        y_mla = _flash(mq, mk, mv, b, s, "flash_mla")
        y_fox = _flash(fq, fk, p["fv"], b, s, "flash_fox")
        xf, h2, cw, rs = _outproj(xf, (y_ssd, y_dsa, y_mla, y_fox), w_out[l], norm_ffn[l].reshape(1, d),
                                  router_group_w[l], router_group_b[l], router_expert_w[l], router_expert_b[l])
        y_ffn = _moe(h2, cw, rs, expert_w_gate[l], expert_w_up[l], expert_w_down[l])
    return _final(xf, y_ffn, final_norm.reshape(1, d)).reshape(b, s, d)
```

```python
import functools
import math

import jax
import jax.numpy as jnp
from jax import lax
from jax.experimental import pallas as pl
from jax.experimental.pallas import tpu as pltpu

F32 = jnp.float32
BF16 = jnp.bfloat16
I32 = jnp.int32
EPS = 1e-6
ROPE_THETA = 500000.0
HIGHEST = lax.Precision.HIGHEST

D_MODEL = 1024
SSD_HEADS, SSD_HEAD_DIM, SSD_GROUPS, SSD_STATE, SSD_CONV = 4, 64, 2, 128, 4
SSD_D_INNER = SSD_HEADS * SSD_HEAD_DIM
SSD_CONV_DIM = SSD_D_INNER + 2 * SSD_GROUPS * SSD_STATE
DSA_HEADS, DSA_HEAD_DIM, IDX_HEADS, IDX_DIM, IDX_TOPK_MAX = 4, 64, 8, 64, 256
MLA_HEADS, MLA_Q_RANK, MLA_KV_RANK, MLA_NOPE, MLA_ROPE, MLA_V = 4, 256, 128, 64, 32, 64
FOX_HEADS, FOX_HEAD_DIM = 4, 64
MOE_GROUPS, MOE_EPG, MOE_EXPERTS, MOE_FF = 4, 8, 32, 256

LANES = 128
VMEM_LIMIT = 56 * 1024 * 1024
INT_MIN = -(2 ** 31)
NEG_BIG = -1e30

TM = 512
SSD_Q = 256
DSA_BQ = 256
DSA_SC = 512
FA_B = 512
MOE_N = 1024
MOE_ALIGN = 16
MOE_CH = 128
MOE_EPS = 2
MOE_SUB = 2
MOE_NS = 2 * MOE_N + MOE_EXPERTS * MOE_ALIGN + MOE_CH

C_ZX, C_GT, C_DQ, C_KV, C_IK, C_IQ, C_MC, C_FQ, C_FK, C_FV, C_END = (
    0, 1024, 1152, 1408, 1536, 1664, 2176, 2688, 3200, 3712, 4224)
G_FOX, G_DT, G_IW = 0, 4, 8
LOG2E = math.log2(math.e)
HEAD_W = 128
FA_HEADS = 4
VT_ROWS = 80
FOX_ONE_Q, FOX_F_Q, FOX_F_K, FOX_ONE_K = 64, 67, 64, 67


def _cparams(sem):
    return pltpu.CompilerParams(dimension_semantics=sem, vmem_limit_bytes=VMEM_LIMIT)


def _rms(xf, g):
    return xf * lax.rsqrt(jnp.mean(xf * xf, axis=-1, keepdims=True) + EPS) * g


def _softplus(x):
    return jnp.maximum(x, 0.0) + jnp.log1p(jnp.exp(-jnp.abs(x)))


def _silu(x):
    return x * jax.nn.sigmoid(x)


def _rope_apply(a, c, sa, sb, half):
    return a * c + pltpu.roll(a, LANES - half, 1) * sa + pltpu.roll(a, half, 1) * sb


def _store_values_t(vt_ref, v, heads, first_head=0):
    v_t = v.astype(BF16).astype(F32).T
    for h in range(heads):
        vt_ref[0, h * VT_ROWS:h * VT_ROWS + 64, :] = v_t[(first_head + h) * 64:(first_head + h + 1) * 64, :].astype(BF16)
        vt_ref[0, h * VT_ROWS + 64:(h + 1) * VT_ROWS, :] = jnp.ones((VT_ROWS - 64, v_t.shape[1]), BF16)


def _dot_nt(a, b):
    return lax.dot_general(a, b, (((1,), (1,)), ((), ())), preferred_element_type=F32)


def _split3(x):
    hi = x.astype(BF16)
    r = x - hi.astype(F32)
    mid = r.astype(BF16)
    return hi, mid, (r - mid.astype(F32)).astype(BF16)


def _dot_f32_by_mask(mask, x, mask_left):
    m = jnp.where(mask, 1.0, 0.0).astype(BF16)
    out = None
    for piece in _split3(x):
        d = jnp.dot(m, piece, preferred_element_type=F32) if mask_left else jnp.dot(piece, m, preferred_element_type=F32)
        out = d if out is None else out + d
    return out


def _rope_tab_kernel(pos_ref, inv_ref, o_ref):
    pos = pos_ref[...].astype(F32)
    lane = lax.broadcasted_iota(I32, (1, LANES), 1)
    ang = pos * inv_ref[0:1, :]
    c, s = jnp.cos(ang), jnp.sin(ang)
    jd = lane % 64
    o_ref[:, 0:128] = c
    o_ref[:, 128:256] = jnp.where(jd < 8, -s, 0.0)
    o_ref[:, 256:384] = jnp.where((jd >= 8) & (jd < 16), s, 0.0)
    ang = pos * inv_ref[1:2, :]
    c, s = jnp.cos(ang), jnp.sin(ang)
    jm = lane - 64
    o_ref[:, 384:512] = c
    o_ref[:, 512:640] = jnp.where((jm >= 0) & (jm < 16), -s, 0.0)
    o_ref[:, 640:768] = jnp.where((jm >= 16) & (jm < 32), s, 0.0)


def _rope_tables(positions):
    t = positions.size
    half_d, half_m = DSA_HEAD_DIM // 4 // 2, MLA_ROPE // 2
    inv_d = ROPE_THETA ** (-jnp.arange(half_d, dtype=F32) / half_d)
    inv_m = ROPE_THETA ** (-jnp.arange(half_m, dtype=F32) / half_m)
    lane = jnp.arange(LANES)
    jd = lane % 64
    row_d = jnp.where(jd < 2 * half_d, inv_d[jd % half_d], 0.0)
    jm = lane - 64
    row_m = jnp.where((jm >= 0) & (jm < 2 * half_m), inv_m[jm % half_m], 0.0)
    inv = jnp.zeros((8, LANES), F32).at[0].set(row_d).at[1].set(row_m)
    tm = 1024
    return pl.pallas_call(
        _rope_tab_kernel,
        grid=(t // tm,),
        in_specs=[pl.BlockSpec((tm, 1), lambda i: (i, 0)), pl.BlockSpec((8, LANES), lambda i: (0, 0))],
        out_specs=pl.BlockSpec((tm, 768), lambda i: (i, 0)),
        out_shape=jax.ShapeDtypeStruct((t, 768), F32),
        compiler_params=_cparams(("parallel",)),
        name="rope_tables",
    )(positions.reshape(t, 1), inv)


def _inproj_kernel(*refs, has_add):
    if has_add:
        (x_ref, yp_ref, g_ref, w_ref, tab_ref, xo_ref, zx_ref, gt_ref, dq_ref, dk_ref, dv_ref,
         ik_ref, iq_ref, mc_ref, fq_ref, fk_ref, fv_ref) = refs
        x = x_ref[...] + yp_ref[...].astype(F32)
        xo_ref[...] = x
    else:
        (x_ref, g_ref, w_ref, tab_ref, zx_ref, gt_ref, dq_ref, dk_ref, dv_ref,
         ik_ref, iq_ref, mc_ref, fq_ref, fk_ref, fv_ref) = refs
        x = x_ref[...]
    h = _rms(x, g_ref[...]).astype(BF16)

    def proj(a, b):
        return jnp.dot(h, w_ref[:, a:b], preferred_element_type=F32)

    zx_ref[...] = proj(C_ZX, C_GT)
    gt_ref[...] = proj(C_GT, C_DQ)
    c_d, sa_d, sb_d = tab_ref[:, 0:128], tab_ref[:, 128:256], tab_ref[:, 256:384]
    c_m, sa_m, sb_m = tab_ref[:, 384:512], tab_ref[:, 512:640], tab_ref[:, 640:768]
    q = proj(C_DQ, C_KV)
    for i in range(2):
        dq_ref[:, i * 128:(i + 1) * 128] = _rope_apply(q[:, i * 128:(i + 1) * 128], c_d, sa_d, sb_d, 8).astype(BF16)
    kv = proj(C_KV, C_IK)
    is_k = lax.broadcasted_iota(I32, (1, LANES), 1) < 64
    kvr = _rope_apply(kv, jnp.where(is_k, c_d, 1.0), jnp.where(is_k, sa_d, 0.0), jnp.where(is_k, sb_d, 0.0), 8)
    dk_ref[...] = kvr[:, 0:64].astype(BF16)
    _store_values_t(dv_ref, kvr, 1, first_head=1)
    ikb = proj(C_IK, C_IQ)
    ik_ref[...] = _rope_apply(ikb, c_d, sa_d, sb_d, 8)[:, 0:64].astype(BF16)
    iq = proj(C_IQ, C_MC)
    for i in range(4):
        iq_ref[:, i * 128:(i + 1) * 128] = _rope_apply(iq[:, i * 128:(i + 1) * 128], c_d, sa_d, sb_d, 8).astype(BF16)
    mc = proj(C_MC, C_FQ)
    mc_ref[:, 0:384] = mc[:, 0:384]
    mc_ref[:, 384:512] = _rope_apply(mc[:, 384:512], c_m, sa_m, sb_m, 16)
    fq_ref[...] = proj(C_FQ, C_FK).astype(BF16)
    fk_ref[...] = proj(C_FK, C_FV).astype(BF16)
    ones_hi = jnp.where(lax.broadcasted_iota(I32, (1, FA_HEADS * HEAD_W), 1) % HEAD_W >= 64, 1.0, 0.0)
    fv_ref[...] = (proj(C_FV, C_END) + ones_hi).astype(BF16)


def _pack_w_in(w):
    d = w.shape[0]
    z = lambda n: jnp.zeros((d, n), F32)
    o = 0
    parts = {}
    for name, width in (("z", 256), ("xbc", 768), ("dt", 4), ("dq", 256), ("dk", 64), ("dv", 64), ("iq", 512),
                        ("ik", 64), ("iw", 8), ("cq", 256), ("ckv", 128), ("kr", 32), ("fq", 256), ("fk", 256),
                        ("fv", 256), ("ff", 4)):
        parts[name] = w[:, o:o + width]
        o += width
    scale = DSA_HEAD_DIM ** -0.5

    def per_head(a):
        return jnp.pad(a.reshape(d, FA_HEADS, 64), ((0, 0), (0, 0), (0, HEAD_W - 64))).reshape(d, FA_HEADS * HEAD_W)

    cat = jnp.concatenate([
        parts["z"], parts["xbc"],
        parts["ff"], parts["dt"], parts["iw"], z(112),
        parts["dq"] * (scale * LOG2E),
        parts["dk"], parts["dv"],
        parts["ik"], z(64),
        parts["iq"],
        parts["cq"], parts["ckv"], z(64), parts["kr"], z(32),
        per_head(parts["fq"] * (FOX_HEAD_DIM ** -0.5 * LOG2E)), per_head(parts["fk"]), per_head(parts["fv"])], axis=1)
    return cat.astype(BF16)


def _inproj(x, y_prev, g, w_cat, tab):
    t = x.shape[0]
    has_add = y_prev is not None
    tok = lambda n: pl.BlockSpec((TM, n), lambda i: (i, 0))
    full = lambda a: pl.BlockSpec(a.shape, lambda i: (0,) * a.ndim)
    ins = [x] + ([y_prev] if has_add else []) + [g, w_cat, tab]
    in_specs = [tok(D_MODEL)] + ([tok(D_MODEL)] if has_add else []) + [full(g), full(w_cat), tok(768)]
    outs = ([("x", D_MODEL, F32)] if has_add else []) + [
        ("zx", 1024, F32), ("gt", 128, F32), ("dq", 256, BF16), ("dk", 64, BF16), ("dv", 64, BF16),
        ("ik", 64, BF16), ("iq", 512, BF16), ("mc", 512, F32), ("fq", 512, BF16), ("fk", 512, BF16), ("fv", 512, BF16)]
    vt_rows = {"dv": VT_ROWS}
    res = pl.pallas_call(
        functools.partial(_inproj_kernel, has_add=has_add),
        grid=(t // TM,),
        in_specs=in_specs,
        out_specs=[pl.BlockSpec((1, vt_rows[nm], TM), lambda i: (i, 0, 0)) if nm in vt_rows else tok(n)
                   for nm, n, _ in outs],
        out_shape=[jax.ShapeDtypeStruct((t // TM, vt_rows[nm], TM) if nm in vt_rows else (t, n), dt)
                   for nm, n, dt in outs],
        compiler_params=_cparams(("parallel",)),
        name="inproj",
    )(*ins)
    res = list(res)
    x_new = res.pop(0) if has_add else x
    return x_new, dict(zip([n for n, _, _ in outs if n != "x"], res))


def _mla_prep_kernel(mc_ref, qn_ref, kn_ref, wq_ref, wk_ref, wv_ref, tab_ref, q_ref, k_ref, v_ref):
    mc = mc_ref[...]
    cq = _rms(mc[:, 0:256], qn_ref[...]).astype(BF16)
    ckv = _rms(mc[:, 256:384], kn_ref[...]).astype(BF16)
    kr = mc[:, 384:512]
    c_m, sa_m, sb_m = tab_ref[:, 384:512], tab_ref[:, 512:640], tab_ref[:, 640:768]
    q = jnp.dot(cq, wq_ref[...], preferred_element_type=F32)
    k = jnp.dot(ckv, wk_ref[...], preferred_element_type=F32)
    for h in range(MLA_HEADS):
        sl = slice(h * 128, (h + 1) * 128)
        q_ref[:, sl] = _rope_apply(q[:, sl], c_m, sa_m, sb_m, 16).astype(BF16)
        k_ref[:, sl] = (k[:, sl] + kr).astype(BF16)
    ones_hi = jnp.where(lax.broadcasted_iota(I32, (1, FA_HEADS * HEAD_W), 1) % HEAD_W >= 64, 1.0, 0.0)
    v_ref[...] = (jnp.dot(ckv, wv_ref[...], preferred_element_type=F32) + ones_hi).astype(BF16)


def _mla_prep(mc, q_norm, kv_norm, w_uq, w_ukv, tab):
    t = mc.shape[0]
    dqk = MLA_NOPE + MLA_ROPE
    wq = jnp.pad(w_uq.reshape(MLA_Q_RANK, MLA_HEADS, dqk) * (dqk ** -0.5 * LOG2E), ((0, 0), (0, 0), (0, 128 - dqk)))
    wq = wq.reshape(MLA_Q_RANK, MLA_HEADS * 128).astype(BF16)
    wkv = w_ukv.reshape(MLA_KV_RANK, MLA_HEADS, MLA_NOPE + MLA_V)
    wk = jnp.pad(wkv[:, :, :MLA_NOPE], ((0, 0), (0, 0), (0, 128 - MLA_NOPE))).reshape(MLA_KV_RANK, MLA_HEADS * 128).astype(BF16)
    wv = jnp.pad(wkv[:, :, MLA_NOPE:], ((0, 0), (0, 0), (0, 128 - MLA_V))).reshape(MLA_KV_RANK, MLA_HEADS * 128).astype(BF16)
    tok = lambda n: pl.BlockSpec((TM, n), lambda i: (i, 0))
    full = lambda a: pl.BlockSpec(a.shape, lambda i: (0,) * a.ndim)
    qn, kn = q_norm.reshape(1, -1), kv_norm.reshape(1, -1)
    return pl.pallas_call(
        _mla_prep_kernel,
        grid=(t // TM,),
        in_specs=[tok(512), full(qn), full(kn), full(wq), full(wk), full(wv), tok(768)],
        out_specs=[tok(512), tok(512), tok(512)],
        out_shape=[jax.ShapeDtypeStruct((t, 512), BF16)] * 3,
        compiler_params=_cparams(("parallel",)),
        name="mla_prep",
    )(mc, qn, kn, wq, wk, wv, tab)


def _fox_cum_kernel(gt_ref, b_ref, q_ref, k_ref, qo_ref, ko_ref, carry_s, *, blk):
    @pl.when(pl.program_id(1) == 0)
    def _():
        carry_s[...] = jnp.zeros_like(carry_s)

    x = gt_ref[...] + b_ref[...]
    lf = -_softplus(-x)
    ri = lax.broadcasted_iota(I32, (blk, blk), 0)
    ci = lax.broadcasted_iota(I32, (blk, blk), 1)
    cs = _dot_f32_by_mask(ri >= ci, lf, True) + carry_s[0:1, :]
    carry_s[0:1, :] = cs[blk - 1:blk, :]
    f2 = cs * LOG2E
    lane = lax.broadcasted_iota(I32, (1, HEAD_W), 1)

    def pieces(f, base):
        hi = f.astype(BF16).astype(F32)
        mid = (f - hi).astype(BF16).astype(F32)
        lo = f - hi - mid
        return jnp.where(lane == base, hi, jnp.where(lane == base + 1, mid, jnp.where(lane == base + 2, lo, 0.0)))

    def ones(base):
        return jnp.where((lane >= base) & (lane < base + 3), 1.0, 0.0)

    for h in range(FOX_HEADS):
        fh = f2[:, G_FOX + h:G_FOX + h + 1]
        sl = slice(h * HEAD_W, (h + 1) * HEAD_W)
        qo_ref[:, sl] = (q_ref[:, sl].astype(F32) + pieces(fh, FOX_F_Q) + ones(FOX_ONE_Q)).astype(BF16)
        ko_ref[:, sl] = (k_ref[:, sl].astype(F32) - pieces(fh, FOX_F_K) + ones(FOX_ONE_K)).astype(BF16)


def _fox_cum(gt, f_bias, fq, fk, b, s):
    blk = 512
    bias = jnp.zeros((1, LANES), F32).at[0, G_FOX:G_FOX + FOX_HEADS].set(f_bias)
    nb = s // blk
    tok = lambda n: pl.BlockSpec((blk, n), lambda i, j: (i * nb + j, 0))
    w = FOX_HEADS * HEAD_W
    return pl.pallas_call(
        functools.partial(_fox_cum_kernel, blk=blk),
        grid=(b, nb),
        in_specs=[tok(LANES), pl.BlockSpec((1, LANES), lambda i, j: (0, 0)), tok(w), tok(w)],
        out_specs=[tok(w), tok(w)],
        out_shape=[jax.ShapeDtypeStruct((b * s, w), BF16)] * 2,
        scratch_shapes=[pltpu.VMEM((8, LANES), F32)],
        compiler_params=_cparams(("parallel", "arbitrary")),
        name="fox_cum",
    )(gt, bias, fq, fk)


def _ssd_kernel(zx_ref, gt_ref, cw_ref, cb_ref, pr_ref, pc_ref, drow_ref, nw_ref, o_ref, ext_s, st_s, y_s, *, q):
    @pl.when(pl.program_id(1) == 0)
    def _():
        ext_s[0:8, :] = jnp.zeros((8, SSD_CONV_DIM), F32)
        st_s[...] = jnp.zeros_like(st_s)

    raw = zx_ref[:, 256:1024]
    ext_s[8:8 + q, :] = raw
    acc = jnp.broadcast_to(cb_ref[...], (q, SSD_CONV_DIM))
    for j in range(SSD_CONV):
        acc = acc + cw_ref[j:j + 1, :] * ext_s[5 + j:5 + j + q, :]
    ext_s[0:8, :] = raw[q - 8:q, :]
    xbc = _silu(acc)
    xs = xbc[:, 0:SSD_D_INNER]

    g = gt_ref[...]
    lane = lax.broadcasted_iota(I32, (1, LANES), 1)
    dtc = _softplus(g + pr_ref[0:1, :])
    a_r = jnp.where((lane >= G_DT) & (lane < G_DT + SSD_HEADS), -jnp.exp(pr_ref[1:2, :]), 0.0)
    ri = lax.broadcasted_iota(I32, (q, q), 0)
    ci = lax.broadcasted_iota(I32, (q, q), 1)
    tri = ri >= ci
    acs_c = _dot_f32_by_mask(tri, dtc * a_r, True)
    sub = lax.broadcasted_iota(I32, (LANES, 1), 0)
    dtr = _softplus(g.T + pc_ref[:, 0:1])
    a_c = jnp.where((sub >= G_DT) & (sub < G_DT + SSD_HEADS), -jnp.exp(pc_ref[:, 1:2]), 0.0)
    acs_r = _dot_f32_by_mask(ri <= ci, dtr * a_c, False)

    rep = SSD_HEADS // SSD_GROUPS
    for gi in range(SSD_GROUPS):
        bg = xbc[:, SSD_D_INNER + gi * SSD_STATE:SSD_D_INNER + (gi + 1) * SSD_STATE]
        cg = xbc[:, SSD_D_INNER + (SSD_GROUPS + gi) * SSD_STATE:SSD_D_INNER + (SSD_GROUPS + gi + 1) * SSD_STATE]
        bt = bg.T.astype(BF16)
        cb16 = cg.astype(BF16)
        cbm = jnp.dot(cb16, bt, preferred_element_type=F32)
        for hh in range(rep):
            h = gi * rep + hh
            ac = acs_c[:, G_DT + h:G_DT + h + 1]
            ar = acs_r[G_DT + h:G_DT + h + 1, :]
            seg = jnp.where(tri, jnp.exp(ac - ar), 0.0)
            xh = xs[:, h * SSD_HEAD_DIM:(h + 1) * SSD_HEAD_DIM]
            xdt = xh * dtc[:, G_DT + h:G_DT + h + 1]
            yd = jnp.dot((cbm * seg).astype(BF16), xdt.astype(BF16), preferred_element_type=F32)
            aend = ac[q - 1:q, :]
            st = st_s[h]
            yo = jnp.dot(cb16, st.astype(BF16), preferred_element_type=F32) * jnp.exp(ac)
            st_s[h] = st * jnp.exp(aend) + jnp.dot(bt, (xdt * jnp.exp(aend - ac)).astype(BF16),
                                                   preferred_element_type=F32)
            y_s[:, h * SSD_HEAD_DIM:(h + 1) * SSD_HEAD_DIM] = yd + yo
    y = y_s[...] + drow_ref[...] * xs
    y = y * _silu(zx_ref[:, 0:SSD_D_INNER])
    o_ref[...] = _rms(y, nw_ref[...]).astype(BF16)


def _ssd(zx, gt, conv_w, conv_b, dt_bias, a_log, d_skip, norm_w, b, s):
    q = SSD_Q
    nc = s // q
    pr = jnp.zeros((8, LANES), F32).at[0, G_DT:G_DT + SSD_HEADS].set(dt_bias).at[1, G_DT:G_DT + SSD_HEADS].set(a_log)
    pc = pr.T
    drow = jnp.repeat(d_skip, SSD_HEAD_DIM).reshape(1, SSD_D_INNER)
    cb = conv_b.reshape(1, -1)
    nw = norm_w.reshape(1, -1)
    full = lambda a: pl.BlockSpec(a.shape, lambda i, j: (0,) * a.ndim)
    return pl.pallas_call(
        functools.partial(_ssd_kernel, q=q),
        grid=(b, nc),
        in_specs=[pl.BlockSpec((q, 1024), lambda i, j: (i * nc + j, 0)), pl.BlockSpec((q, LANES), lambda i, j: (i * nc + j, 0)),
                  full(conv_w), full(cb), full(pr), full(pc), full(drow), full(nw)],
        out_specs=pl.BlockSpec((q, SSD_D_INNER), lambda i, j: (i * nc + j, 0)),
        out_shape=jax.ShapeDtypeStruct((b * s, SSD_D_INNER), BF16),
        scratch_shapes=[pltpu.VMEM((8 + q, SSD_CONV_DIM), F32), pltpu.VMEM((SSD_HEADS, SSD_STATE, SSD_HEAD_DIM), F32),
                        pltpu.VMEM((q, SSD_D_INNER), F32)],
        compiler_params=_cparams(("parallel", "arbitrary")),
        name="ssd",
    )(zx, gt, conv_w, cb, pr, pc, drow, nw)


def _key_to_f32(k):
    return pltpu.bitcast(jnp.where(k < 0, k ^ jnp.int32(0x7FFFFFFF), k), F32)


def _dsa_kernel(iq_ref, gt_ref, q_ref, ik_ref, k_ref, vt_ref, o_ref, sc_s, qi_s, qs_s, thr_s, *, bq, sc_w, n_sel):
    qb = pl.program_id(1)
    n_sc = ((qb + 1) * bq + sc_w - 1) // sc_w
    for j in range(IDX_HEADS):
        qi_s[j // 2, (j % 2) * bq:(j % 2 + 1) * bq, :] = iq_ref[:, j * IDX_DIM:(j + 1) * IDX_DIM]
    for h in range(DSA_HEADS):
        qs_s[h * bq:(h + 1) * bq, :] = q_ref[:, h * DSA_HEAD_DIM:(h + 1) * DSA_HEAD_DIM]
    w_t = (gt_ref[...] * ((IDX_HEADS * IDX_DIM) ** -0.5)).T
    qpos = qb * bq + lax.broadcasted_iota(I32, (1, bq), 1)

    def score_body(sc, carry):
        k0 = pl.multiple_of(sc * sc_w, sc_w)
        kidx = ik_ref[pl.ds(k0, sc_w), :]
        sco = jnp.zeros((sc_w, bq), F32)
        for jp in range(IDX_HEADS // 2):
            a = _dot_nt(kidx, qi_s[jp])
            for u in range(2):
                j = 2 * jp + u
                sco = sco + jnp.maximum(a[:, u * bq:(u + 1) * bq], 0.0) * w_t[G_IW + j:G_IW + j + 1, :]
        kpos = k0 + lax.broadcasted_iota(I32, (sc_w, 1), 0)
        sc_s[sc] = jnp.where(kpos <= qpos, sco, -jnp.inf)
        return carry

    lax.fori_loop(0, n_sc, score_body, 0)

    def count(pred):
        def body(sc, acc):
            for r in range(sc_w // 64):
                acc = acc + jnp.where(pred(sc_s[sc, r * 64:(r + 1) * 64, :]), 1.0, 0.0)
            return acc
        acc = lax.fori_loop(0, n_sc, body, jnp.zeros((64, bq), F32))
        return jnp.sum(acc, axis=0, keepdims=True)

    thr_s[0:1, :] = jnp.full((1, bq), -jnp.finfo(jnp.float32).max, F32)

    @pl.when((qb + 1) * bq > n_sel)
    def _():
        active = qpos + 1 > n_sel

        def cond(st):
            it, lo, cnt_lo = st
            pending = active & (cnt_lo != n_sel)
            return (it < 32) & (jnp.max(jnp.where(pending, 1.0, 0.0)) > 0.0)

        def body(st):
            it, lo, cnt_lo = st
            for u in range(4):
                cand = lo ^ jnp.left_shift(jnp.int32(1), 31 - (it + u))
                thr_c = _key_to_f32(cand)
                cnt = count(lambda t: t >= thr_c)
                ok = cnt >= n_sel
                lo, cnt_lo = jnp.where(ok, cand, lo), jnp.where(ok, cnt, cnt_lo)
            return it + 4, lo, cnt_lo

        _, lo, cnt_lo = lax.while_loop(
            cond, body, (jnp.int32(0), jnp.full((1, bq), INT_MIN, I32), jnp.full((1, bq), 3.0e38, F32)))
        thr = _key_to_f32(lo)
        tie = active & (cnt_lo > n_sel)

        @pl.when(jnp.max(jnp.where(tie, 1.0, 0.0)) > 0.0)
        def _():
            need = n_sel - count(lambda t: t > thr)
            ri = lax.broadcasted_iota(I32, (sc_w, sc_w), 0)
            ci = lax.broadcasted_iota(I32, (sc_w, sc_w), 1)
            lower = jnp.where(ri >= ci, 1.0, 0.0).astype(BF16)

            def tie_body(sc, run):
                t = sc_s[sc]
                eq = (t == thr) & tie
                pre = jnp.dot(lower, jnp.where(eq, 1.0, 0.0).astype(BF16), preferred_element_type=F32)
                sc_s[sc] = jnp.where(eq & (run + pre > need), -jnp.inf, t)
                return run + pre[sc_w - 1:sc_w, :]

            lax.fori_loop(0, n_sc, tie_body, jnp.zeros((1, bq), F32))

        thr_s[0:1, :] = jnp.where(active, thr, thr_s[0:1, :])

    thr = thr_s[0:1, :]

    hq = DSA_HEADS * bq

    def att_body(sc, carry):
        m_prev, acc = carry
        k0 = pl.multiple_of(sc * sc_w, sc_w)
        st = _dot_nt(k_ref[pl.ds(k0, sc_w), :], qs_s[...])
        drop = jnp.where(sc_s[sc] >= thr, 0.0, 2 * NEG_BIG)
        st = st + jnp.tile(drop, (1, DSA_HEADS))
        m_cur = jnp.max(jnp.max(st.reshape(sc_w // 64, 64, hq), axis=0), axis=0, keepdims=True)
        m_next = jnp.maximum(m_prev, m_cur)
        p = jnp.exp2(st - m_next)
        pv = jnp.dot(vt_ref[sc], p.astype(BF16), preferred_element_type=F32)
        return m_next, jnp.exp2(m_prev - m_next) * acc + pv

    def att_body2(i, carry):
        return att_body(2 * i + 1, att_body(2 * i, carry))

    carry = lax.fori_loop(0, n_sc // 2, att_body2, (jnp.full((1, hq), NEG_BIG, F32), jnp.zeros((VT_ROWS, hq), F32)))
    _, acc = lax.cond(n_sc % 2 == 1, lambda c: att_body(n_sc - 1, c), lambda c: c, carry)
    out_t = jnp.concatenate([acc[0:DSA_HEAD_DIM, :] / acc[DSA_HEAD_DIM:DSA_HEAD_DIM + 1, :],
                             jnp.zeros((LANES - DSA_HEAD_DIM, hq), F32)], axis=0).T
    for h in range(DSA_HEADS):
        o_ref[:, h * DSA_HEAD_DIM:(h + 1) * DSA_HEAD_DIM] = out_t[h * bq:(h + 1) * bq, 0:DSA_HEAD_DIM].astype(BF16)


def _dsa(iq, gt, dq, ik, dk, dvt, b, s, n_sel):
    bq, sc_w = DSA_BQ, DSA_SC
    nq = s // bq
    qblk = lambda n: pl.BlockSpec((bq, n), lambda i, j: (i * nq + j, 0))
    kblk = pl.BlockSpec((s, 64), lambda i, j: (i, 0))
    return pl.pallas_call(
        functools.partial(_dsa_kernel, bq=bq, sc_w=sc_w, n_sel=n_sel),
        grid=(b, nq),
        in_specs=[qblk(512), qblk(LANES), qblk(256), kblk, kblk,
                  pl.BlockSpec((s // sc_w, VT_ROWS, sc_w), lambda i, j: (i, 0, 0))],
        out_specs=qblk(256),
        out_shape=jax.ShapeDtypeStruct((b * s, 256), BF16),
        scratch_shapes=[pltpu.VMEM((s // sc_w, sc_w, bq), F32), pltpu.VMEM((IDX_HEADS // 2, 2 * bq, IDX_DIM), BF16),
                        pltpu.VMEM((DSA_HEADS * bq, DSA_HEAD_DIM), BF16), pltpu.VMEM((8, bq), F32)],
        compiler_params=_cparams(("parallel", "arbitrary")),
        name="dsa",
    )(iq, gt, dq, ik, dk, dvt)


def _flash_kernel(qt_ref, kt_ref, q_ref, k_ref, v_ref, o_ref, m_s, acc_s, *, blk):
    t = pl.program_id(1)
    qi, ki = qt_ref[t], kt_ref[t]

    @pl.when(ki == 0)
    def _():
        m_s[...] = jnp.full(m_s.shape, NEG_BIG, F32)
        acc_s[...] = jnp.zeros_like(acc_s)

    def step(diagonal):
        if diagonal:
            keep = lax.broadcasted_iota(I32, (blk, blk), 0) >= lax.broadcasted_iota(I32, (blk, blk), 1)
        for h in range(FA_HEADS):
            sl = slice(h * HEAD_W, (h + 1) * HEAD_W)
            s = _dot_nt(q_ref[:, sl], k_ref[:, sl])
            if diagonal:
                s = jnp.where(keep, s, 2 * NEG_BIG)
            m_prev = m_s[h]
            m_next = jnp.maximum(m_prev, jnp.max(s, axis=1, keepdims=True))
            p = jnp.exp2(s - jnp.tile(m_next, (1, blk // LANES)))
            pv = jnp.dot(p.astype(BF16), v_ref[:, sl], preferred_element_type=F32)
            acc_s[h] = jnp.exp2(m_prev - m_next) * acc_s[h] + pv
            m_s[h] = m_next

    @pl.when(ki < qi)
    def _():
        step(False)

    @pl.when(ki == qi)
    def _():
        step(True)
        for h in range(FA_HEADS):
            acc = acc_s[h]
            o_ref[:, h * 64:(h + 1) * 64] = (acc / pltpu.roll(acc, 64, 1))[:, 0:64].astype(BF16)


def _flash(q, k, v, b, s, name):
    blk = FA_B
    nb = s // blk
    pairs = [(i, j) for i in range(nb) for j in range(i + 1)]
    qt = jnp.asarray([p[0] for p in pairs], I32)
    kt = jnp.asarray([p[1] for p in pairs], I32)
    w = FA_HEADS * HEAD_W
    grid_spec = pltpu.PrefetchScalarGridSpec(
        num_scalar_prefetch=2,
        grid=(b, len(pairs)),
        in_specs=[pl.BlockSpec((blk, w), lambda i, t, qt, kt: (i * nb + qt[t], 0)),
                  pl.BlockSpec((blk, w), lambda i, t, qt, kt: (i * nb + kt[t], 0)),
                  pl.BlockSpec((blk, w), lambda i, t, qt, kt: (i * nb + kt[t], 0))],
        out_specs=pl.BlockSpec((blk, FA_HEADS * 64), lambda i, t, qt, kt: (i * nb + qt[t], 0)),
        scratch_shapes=[pltpu.VMEM((FA_HEADS, blk, LANES), F32), pltpu.VMEM((FA_HEADS, blk, HEAD_W), F32)],
    )
    return pl.pallas_call(
        functools.partial(_flash_kernel, blk=blk),
        grid_spec=grid_spec,
        out_shape=jax.ShapeDtypeStruct((b * s, FA_HEADS * 64), BF16),
        compiler_params=_cparams(("parallel", "arbitrary")),
        name=name,
    )(qt, kt, q, k, v)


def _outproj_kernel(x_ref, y0_ref, y1_ref, y2_ref, y3_ref, wo_ref, g_ref, wr_ref, br_ref,
                    xo_ref, h_ref, cw_ref, rs_ref):
    acc = x_ref[...]
    for i, y_ref in enumerate((y0_ref, y1_ref, y2_ref, y3_ref)):
        acc = acc + jnp.dot(y_ref[...], wo_ref[i * 256:(i + 1) * 256, :], preferred_element_type=F32)
    xo_ref[...] = acc
    h2 = _rms(acc, g_ref[...])
    h_ref[...] = h2.astype(BF16)
    h_hi, h_lo, _ = _split3(h2)
    lg = (jnp.dot(h_hi, wr_ref[0], preferred_element_type=F32) + jnp.dot(h_lo, wr_ref[0], preferred_element_type=F32)
          + jnp.dot(h_hi, wr_ref[1], preferred_element_type=F32) + br_ref[...])
    lane = lax.broadcasted_iota(I32, (1, LANES), 1)
    lanef = lane.astype(F32)
    is_g = (lane >= MOE_EXPERTS) & (lane < MOE_EXPERTS + MOE_GROUPS)
    gl = jnp.where(is_g, lg, -jnp.inf)
    gmax = jnp.max(gl, axis=1, keepdims=True)
    gidx = jnp.min(jnp.where(gl == gmax, lanef, 999.0), axis=1, keepdims=True) - MOE_EXPERTS
    g_p = 1.0 / jnp.sum(jnp.where(is_g, jnp.exp(gl - gmax), 0.0), axis=1, keepdims=True)
    in_g = (lane < MOE_EXPERTS) & (jnp.floor(lanef * (1.0 / MOE_EPG)) == gidx)
    el = jnp.where(in_g, lg, -jnp.inf)
    m1 = jnp.max(el, axis=1, keepdims=True)
    i1 = jnp.min(jnp.where(el == m1, lanef, 999.0), axis=1, keepdims=True)
    el2 = jnp.where(lanef == i1, -jnp.inf, el)
    m2 = jnp.max(el2, axis=1, keepdims=True)
    i2 = jnp.min(jnp.where(el2 == m2, lanef, 999.0), axis=1, keepdims=True)
    t = jnp.exp(m2 - m1)
    w1 = 1.0 / (1.0 + t)
    cw_ref[...] = jnp.where(lanef == i1, w1 * g_p, jnp.where(lanef == i2, t * w1 * g_p, 0.0))
    rs_ref[...] = jnp.where((lanef == i1) | (lanef == i2), 1.0, 0.0).astype(BF16)


def _outproj(x, ys, w_out, g, w_rg, b_rg, w_re, b_re):
    t = x.shape[0]
    wo = w_out.astype(BF16)
    wr = jnp.zeros((D_MODEL, LANES), F32).at[:, 0:MOE_EXPERTS].set(w_re).at[:, MOE_EXPERTS:MOE_EXPERTS + MOE_GROUPS].set(w_rg)
    wr_hi = wr.astype(BF16)
    wr = jnp.stack([wr_hi, (wr - wr_hi.astype(F32)).astype(BF16)])
    br =jnp.zeros((1, LANES), F32).at[0, 0:MOE_EXPERTS].set(b_re).at[0, MOE_EXPERTS:MOE_EXPERTS + MOE_GROUPS].set(b_rg)
    tok = lambda n: pl.BlockSpec((TM, n), lambda i: (i, 0))
    full = lambda a: pl.BlockSpec(a.shape, lambda i: (0,) * a.ndim)
    return pl.pallas_call(
        _outproj_kernel,
        grid=(t // TM,),
        in_specs=[tok(D_MODEL)] + [tok(256)] * 4 + [full(wo), full(g), full(wr), full(br)],
        out_specs=[tok(D_MODEL), tok(D_MODEL), tok(LANES), tok(LANES)],
        out_shape=[jax.ShapeDtypeStruct((t, D_MODEL), F32), jax.ShapeDtypeStruct((t, D_MODEL), BF16),
                   jax.ShapeDtypeStruct((t, LANES), F32), jax.ShapeDtypeStruct((t, LANES), BF16)],
        compiler_params=_cparams(("parallel",)),
        name="outproj_router",
    )(x, *ys, wo, g, wr, br)


def _moe_plan_kernel(rs_ref, cw_ref, meta_ref, col_ref, row_ref, *, n):
    sel = rs_ref[...]
    self32 = sel.astype(F32)
    ti = lax.broadcasted_iota(I32, (n, n), 0)
    tj = lax.broadcasted_iota(I32, (n, n), 1)
    rank = jnp.dot(jnp.where(ti > tj, 1.0, 0.0).astype(BF16), sel, preferred_element_type=F32)
    cnt = jnp.sum(self32, axis=0, keepdims=True).astype(I32)
    cpad = ((cnt + (MOE_ALIGN - 1)) // MOE_ALIGN) * MOE_ALIGN
    li = lax.broadcasted_iota(I32, (LANES, LANES), 0)
    lj = lax.broadcasted_iota(I32, (LANES, LANES), 1)
    off = _dot_f32_by_mask(li < lj, jnp.broadcast_to(cpad.astype(F32), (8, LANES)), False)[0:1, :]
    dest = off + rank
    lane = lax.broadcasted_iota(I32, (1, LANES), 1)
    lanef = lane.astype(F32)
    on = self32 > 0.5
    e_lo = jnp.min(jnp.where(on, lanef, 999.0), axis=1, keepdims=True)
    e_hi = jnp.max(jnp.where(on, lanef, -1.0), axis=1, keepdims=True)
    is_lo, is_hi = lanef == e_lo, lanef == e_hi
    cw = cw_ref[...]
    pick = lambda m, a: jnp.sum(jnp.where(m, a, 0.0), axis=1, keepdims=True)
    d_lo, d_hi, w_lo, w_hi = pick(is_lo, dest), pick(is_hi, dest), pick(is_lo, cw), pick(is_hi, cw)
    colv = jnp.where(lane == 0, d_lo, jnp.where(lane == 1, d_hi, jnp.where(lane == 2, w_lo, jnp.where(lane == 3, w_hi, 0.0))))
    col_ref[...] = colv
    row_ref[0] = colv.T[0:8, :]
    sub = lax.broadcasted_iota(I32, (8, LANES), 0)
    meta_ref[0] = jnp.where(sub == 0, off.astype(I32), jnp.where(sub == 1, cnt, 0))


def _moe_kernel(meta_ref, h_ref, col_ref, row_ref, wg_ref, wu_ref, wd_ref, o_ref, s_s, y_s, *, n, ns):
    t, e = pl.program_id(0), pl.program_id(1)

    @pl.when(e == 0)
    def _():
        for g in range(MOE_SUB):
            d_lo, d_hi = row_ref[g, 0:1, :], row_ref[g, 1:2, :]
            hg = h_ref[g * n:(g + 1) * n, :]
            for r in range(ns // LANES):
                si = (r * LANES + lax.broadcasted_iota(I32, (LANES, 1), 0)).astype(F32)
                perm = jnp.where((si == d_lo) | (si == d_hi), 1.0, 0.0).astype(BF16)
                s_s[g, r * LANES:(r + 1) * LANES, :] = jnp.dot(perm, hg, preferred_element_type=F32).astype(BF16)
        y_s[...] = jnp.zeros_like(y_s)

    def mlp(j, starts, store_ok):
        starts = [pl.multiple_of(r0, MOE_ALIGN) for r0 in starts]
        xs = jnp.concatenate([s_s[g, pl.ds(starts[g], MOE_CH), :] for g in range(MOE_SUB)], axis=0)
        gate = jnp.dot(xs, wg_ref[j], preferred_element_type=F32)
        up = jnp.dot(xs, wu_ref[j], preferred_element_type=F32)
        hid = (_silu(gate) * up).astype(BF16)
        y = jnp.dot(hid, wd_ref[j], preferred_element_type=F32).astype(BF16)
        for g in range(MOE_SUB):
            def store(g=g):
                y_s[g, pl.ds(starts[g], MOE_CH), :] = y[g * MOE_CH:(g + 1) * MOE_CH, :]
            if store_ok is None:
                store()
            else:
                pl.when(store_ok[g])(store)

    def meta(g, j, what):
        return meta_ref[(t * MOE_SUB + g) * 2 * MOE_EXPERTS + what * MOE_EXPERTS + e * MOE_EPS + j]

    offs = [[meta(g, j, 0) for g in range(MOE_SUB)] for j in range(MOE_EPS)]
    cnts = [[meta(g, j, 1) for g in range(MOE_SUB)] for j in range(MOE_EPS)]
    for j in range(MOE_EPS):
        mlp(j, offs[j], None)
    for j in range(MOE_EPS):
        ends = [offs[j][g] + ((cnts[j][g] + MOE_ALIGN - 1) // MOE_ALIGN) * MOE_ALIGN for g in range(MOE_SUB)]
        nch = [(cnts[j][g] + MOE_CH - 1) // MOE_CH for g in range(MOE_SUB)]

        def body(i, carry, j=j, ends=ends, nch=nch):
            more = [i < nch[g] for g in range(MOE_SUB)]
            starts = [jnp.where(more[g], jnp.minimum(offs[j][g] + i * MOE_CH, ends[g] - MOE_CH), offs[j][g])
                      for g in range(MOE_SUB)]
            mlp(j, starts, more)
            return carry

        lax.fori_loop(1, functools.reduce(jnp.maximum, nch), body, 0)

    @pl.when(e == pl.num_programs(1) - 1)
    def _():
        rows, kc = 256, ns // 3
        for g in range(MOE_SUB):
            for c in range(n // rows):
                r0 = g * n + c * rows
                cv = col_ref[r0:r0 + rows, :]
                d_lo, d_hi, w_lo, w_hi = cv[:, 0:1], cv[:, 1:2], cv[:, 2:3], cv[:, 3:4]
                acc = jnp.zeros((rows, D_MODEL), F32)
                for r in range(ns // kc):
                    si = (r * kc + lax.broadcasted_iota(I32, (1, kc), 1)).astype(F32)
                    pw = (jnp.where(si == d_lo, w_lo, 0.0) + jnp.where(si == d_hi, w_hi, 0.0)).astype(BF16)
                    acc = acc + jnp.dot(pw, y_s[g, r * kc:(r + 1) * kc, :], preferred_element_type=F32)
                o_ref[r0:r0 + rows, :] = acc.astype(BF16)


def _moe(h2, cw, rs, w_gate, w_up, w_down):
    t = h2.shape[0]
    n, ns = MOE_N, MOE_NS
    nt = t // n
    meta, col, row = pl.pallas_call(
        functools.partial(_moe_plan_kernel, n=n),
        grid=(nt,),
        in_specs=[pl.BlockSpec((n, LANES), lambda i: (i, 0)), pl.BlockSpec((n, LANES), lambda i: (i, 0))],
        out_specs=[pl.BlockSpec((1, 8, LANES), lambda i: (i, 0, 0)), pl.BlockSpec((n, LANES), lambda i: (i, 0)),
                   pl.BlockSpec((1, 8, n), lambda i: (i, 0, 0))],
        out_shape=[jax.ShapeDtypeStruct((nt, 8, LANES), I32), jax.ShapeDtypeStruct((t, LANES), F32),
                   jax.ShapeDtypeStruct((nt, 8, n), F32)],
        compiler_params=_cparams(("parallel",)),
        name="moe_plan",
    )(rs, cw)
    meta_flat = meta[:, 0:2, 0:MOE_EXPERTS].reshape(-1)
    wg, wu, wd = w_gate.astype(BF16), w_up.astype(BF16), w_down.astype(BF16)
    grid_spec = pltpu.PrefetchScalarGridSpec(
        num_scalar_prefetch=1,
        grid=(nt // MOE_SUB, MOE_EXPERTS // MOE_EPS),
        in_specs=[pl.BlockSpec((MOE_SUB * n, D_MODEL), lambda i, e, m: (i, 0)),
                  pl.BlockSpec((MOE_SUB * n, LANES), lambda i, e, m: (i, 0)),
                  pl.BlockSpec((MOE_SUB, 8, n), lambda i, e, m: (i, 0, 0)),
                  pl.BlockSpec((MOE_EPS, D_MODEL, MOE_FF), lambda i, e, m: (e, 0, 0)),
                  pl.BlockSpec((MOE_EPS, D_MODEL, MOE_FF), lambda i, e, m: (e, 0, 0)),
                  pl.BlockSpec((MOE_EPS, MOE_FF, D_MODEL), lambda i, e, m: (e, 0, 0))],
        out_specs=pl.BlockSpec((MOE_SUB * n, D_MODEL), lambda i, e, m: (i, 0)),
        scratch_shapes=[pltpu.VMEM((MOE_SUB, ns, D_MODEL), BF16), pltpu.VMEM((MOE_SUB, ns, D_MODEL), BF16)],
    )
    return pl.pallas_call(
        functools.partial(_moe_kernel, n=n, ns=ns),
        grid_spec=grid_spec,
        out_shape=jax.ShapeDtypeStruct((t, D_MODEL), BF16),
        compiler_params=_cparams(("parallel", "arbitrary")),
        name="moe_experts",
    )(meta_flat, h2, col, row, wg, wu, wd)


def _final_kernel(x_ref, y_ref, g_ref, o_ref):
    o_ref[...] = _rms(x_ref[...] + y_ref[...].astype(F32), g_ref[...])


def _final(x, y, g):
    t = x.shape[0]
    tok = pl.BlockSpec((TM, D_MODEL), lambda i: (i, 0))
    return pl.pallas_call(
        _final_kernel,
        grid=(t // TM,),
        in_specs=[tok, tok, pl.BlockSpec((1, D_MODEL), lambda i: (0, 0))],
        out_specs=tok,
        out_shape=jax.ShapeDtypeStruct((t, D_MODEL), F32),
        compiler_params=_cparams(("parallel",)),
        name="final_norm",
    )(x, y, g)


def kernel(x, positions, norm_mix, w_in, ssd_conv_w, ssd_conv_b, ssd_dt_bias, ssd_a_log, ssd_d, ssd_norm, mla_q_norm, mla_w_uq, mla_kv_norm, mla_w_ukv, fox_f_bias, w_out, norm_ffn, router_group_w, router_group_b, router_expert_w, router_expert_b, expert_w_gate, expert_w_up, expert_w_down, final_norm):
    b, s, d = x.shape
    assert d == D_MODEL and s % FA_B == 0 and (b * s) % (MOE_N * MOE_SUB) == 0 and TM == DSA_SC
    depth = w_in.shape[0]
    n_sel = min(IDX_TOPK_MAX, s // 4)
    tab = _rope_tables(positions)
    xf = x.reshape(b * s, d)
    y_ffn = None
    for l in range(depth):
        xf, p = _inproj(xf, y_ffn, norm_mix[l].reshape(1, d), _pack_w_in(w_in[l]), tab)
        mq, mk, mv = _mla_prep(p["mc"], mla_q_norm[l], mla_kv_norm[l], mla_w_uq[l], mla_w_ukv[l], tab)
        fq, fk = _fox_cum(p["gt"], fox_f_bias[l], p["fq"], p["fk"], b, s)
        y_ssd = _ssd(p["zx"], p["gt"], ssd_conv_w[l], ssd_conv_b[l], ssd_dt_bias[l], ssd_a_log[l], ssd_d[l],
                     ssd_norm[l], b, s)
        y_dsa = _dsa(p["iq"], p["gt"], p["dq"], p["ik"], p["dk"], p["dv"], b, s, n_sel)
        y_mla = _flash(mq, mk, mv, b, s, "flash_mla")
        y_fox = _flash(fq, fk, p["fv"], b, s, "flash_fox")
        xf, h2, cw, rs = _outproj(xf, (y_ssd, y_dsa, y_mla, y_fox), w_out[l], norm_ffn[l].reshape(1, d),
                                  router_group_w[l], router_group_b[l], router_expert_w[l], router_expert_b[l])
        y_ffn = _moe(h2, cw, rs, expert_w_gate[l], expert_w_up[l], expert_w_down[l])
    return _final(xf, y_ffn, final_norm.reshape(1, d)).reshape(b, s, d)
```

```python
import functools
import math

import jax
import jax.numpy as jnp
from jax import lax
from jax.experimental import pallas as pl
from jax.experimental.pallas import tpu as pltpu

F32 = jnp.float32
BF16 = jnp.bfloat16
I32 = jnp.int32
EPS = 1e-6
ROPE_THETA = 500000.0
HIGHEST = lax.Precision.HIGHEST

D_MODEL = 1024
SSD_HEADS, SSD_HEAD_DIM, SSD_GROUPS, SSD_STATE, SSD_CONV = 4, 64, 2, 128, 4
SSD_D_INNER = SSD_HEADS * SSD_HEAD_DIM
SSD_CONV_DIM = SSD_D_INNER + 2 * SSD_GROUPS * SSD_STATE
DSA_HEADS, DSA_HEAD_DIM, IDX_HEADS, IDX_DIM, IDX_TOPK_MAX = 4, 64, 8, 64, 256
MLA_HEADS, MLA_Q_RANK, MLA_KV_RANK, MLA_NOPE, MLA_ROPE, MLA_V = 4, 256, 128, 64, 32, 64
FOX_HEADS, FOX_HEAD_DIM = 4, 64
MOE_GROUPS, MOE_EPG, MOE_EXPERTS, MOE_FF = 4, 8, 32, 256

LANES = 128
VMEM_LIMIT = 56 * 1024 * 1024
INT_MIN = -(2 ** 31)
NEG_BIG = -1e30

TM = 512
SSD_Q = 256
DSA_BQ = 256
DSA_SC = 512
FA_B = 512
MOE_N = 1024
MOE_ALIGN = 16
MOE_CH = 128
MOE_EPS = 2
MOE_SUB = 2
MOE_NS = 2 * MOE_N + MOE_EXPERTS * MOE_ALIGN + MOE_CH

C_ZX, C_GT, C_DQ, C_KV, C_IK, C_IQ, C_MC, C_FQ, C_FK, C_FV, C_END = (
    0, 1024, 1152, 1408, 1536, 1664, 2176, 2688, 3200, 3712, 4224)
G_FOX, G_DT, G_IW = 0, 4, 8
LOG2E = math.log2(math.e)
HEAD_W = 128
FA_HEADS = 4
VT_ROWS = 80
FOX_ONE_Q, FOX_F_Q, FOX_F_K, FOX_ONE_K = 64, 67, 64, 67


def _cparams(sem):
    return pltpu.CompilerParams(dimension_semantics=sem, vmem_limit_bytes=VMEM_LIMIT)


def _rms(xf, g):
    return xf * lax.rsqrt(jnp.mean(xf * xf, axis=-1, keepdims=True) + EPS) * g


def _softplus(x):
    return jnp.maximum(x, 0.0) + jnp.log1p(jnp.exp(-jnp.abs(x)))


def _silu(x):
    return x * jax.nn.sigmoid(x)


def _rope_apply(a, c, sa, sb, half):
    return a * c + pltpu.roll(a, LANES - half, 1) * sa + pltpu.roll(a, half, 1) * sb


def _store_values_t(vt_ref, v, heads, first_head=0):
    v_t = v.astype(BF16).astype(F32).T
    for h in range(heads):
        vt_ref[0, h * VT_ROWS:h * VT_ROWS + 64, :] = v_t[(first_head + h) * 64:(first_head + h + 1) * 64, :].astype(BF16)
        vt_ref[0, h * VT_ROWS + 64:(h + 1) * VT_ROWS, :] = jnp.ones((VT_ROWS - 64, v_t.shape[1]), BF16)


def _dot_nt(a, b):
    return lax.dot_general(a, b, (((1,), (1,)), ((), ())), preferred_element_type=F32)


def _split3(x):
    hi = x.astype(BF16)
    r = x - hi.astype(F32)
    mid = r.astype(BF16)
    return hi, mid, (r - mid.astype(F32)).astype(BF16)


def _dot_f32_by_mask(mask, x, mask_left):
    m = jnp.where(mask, 1.0, 0.0).astype(BF16)
    out = None
    for piece in _split3(x):
        d = jnp.dot(m, piece, preferred_element_type=F32) if mask_left else jnp.dot(piece, m, preferred_element_type=F32)
        out = d if out is None else out + d
    return out


def _rope_tab_kernel(pos_ref, inv_ref, o_ref):
    pos = pos_ref[...].astype(F32)
    lane = lax.broadcasted_iota(I32, (1, LANES), 1)
    ang = pos * inv_ref[0:1, :]
    c, s = jnp.cos(ang), jnp.sin(ang)
    jd = lane % 64
    o_ref[:, 0:128] = c
    o_ref[:, 128:256] = jnp.where(jd < 8, -s, 0.0)
    o_ref[:, 256:384] = jnp.where((jd >= 8) & (jd < 16), s, 0.0)
    ang = pos * inv_ref[1:2, :]
    c, s = jnp.cos(ang), jnp.sin(ang)
    jm = lane - 64
    o_ref[:, 384:512] = c
    o_ref[:, 512:640] = jnp.where((jm >= 0) & (jm < 16), -s, 0.0)
    o_ref[:, 640:768] = jnp.where((jm >= 16) & (jm < 32), s, 0.0)


def _rope_tables(positions):
    t = positions.size
    half_d, half_m = DSA_HEAD_DIM // 4 // 2, MLA_ROPE // 2
    inv_d = ROPE_THETA ** (-jnp.arange(half_d, dtype=F32) / half_d)
    inv_m = ROPE_THETA ** (-jnp.arange(half_m, dtype=F32) / half_m)
    lane = jnp.arange(LANES)
    jd = lane % 64
    row_d = jnp.where(jd < 2 * half_d, inv_d[jd % half_d], 0.0)
    jm = lane - 64
    row_m = jnp.where((jm >= 0) & (jm < 2 * half_m), inv_m[jm % half_m], 0.0)
    inv = jnp.zeros((8, LANES), F32).at[0].set(row_d).at[1].set(row_m)
    tm = 1024
    return pl.pallas_call(
        _rope_tab_kernel,
        grid=(t // tm,),
        in_specs=[pl.BlockSpec((tm, 1), lambda i: (i, 0)), pl.BlockSpec((8, LANES), lambda i: (0, 0))],
        out_specs=pl.BlockSpec((tm, 768), lambda i: (i, 0)),
        out_shape=jax.ShapeDtypeStruct((t, 768), F32),
        compiler_params=_cparams(("parallel",)),
        name="rope_tables",
    )(positions.reshape(t, 1), inv)


def _inproj_kernel(*refs, has_add, tiles_per_seq):
    refs = list(refs)
    x_ref = refs.pop(0)
    yp_ref = refs.pop(0) if has_add else None
    g_ref, w_ref, tab_ref, fb_ref, qn_ref, kn_ref, wq_ref, wk_ref, wv_ref = refs[:9]
    refs = refs[9:]
    xo_ref = refs.pop(0) if has_add else None
    (zx_ref, gt_ref, dq_ref, dk_ref, dv_ref, ik_ref, iq_ref, mq_ref, mk_ref, mv_ref, fq_ref, fk_ref, fv_ref,
     carry_s) = refs
    x = x_ref[...]
    if has_add:
        x = x + yp_ref[...].astype(F32)
        xo_ref[...] = x
    h = _rms(x, g_ref[...]).astype(BF16)

    def proj(a, b):
        return jnp.dot(h, w_ref[:, a:b], preferred_element_type=F32)

    zx_ref[...] = proj(C_ZX, C_GT)
    gates = proj(C_GT, C_DQ)
    gt_ref[...] = gates
    c_d, sa_d, sb_d = tab_ref[:, 0:128], tab_ref[:, 128:256], tab_ref[:, 256:384]
    c_m, sa_m, sb_m = tab_ref[:, 384:512], tab_ref[:, 512:640], tab_ref[:, 640:768]
    q = proj(C_DQ, C_KV)
    for i in range(2):
        dq_ref[:, i * 128:(i + 1) * 128] = _rope_apply(q[:, i * 128:(i + 1) * 128], c_d, sa_d, sb_d, 8).astype(BF16)
    kv = proj(C_KV, C_IK)
    is_k = lax.broadcasted_iota(I32, (1, LANES), 1) < 64
    kvr = _rope_apply(kv, jnp.where(is_k, c_d, 1.0), jnp.where(is_k, sa_d, 0.0), jnp.where(is_k, sb_d, 0.0), 8)
    dk_ref[...] = kvr[:, 0:64].astype(BF16)
    _store_values_t(dv_ref, kvr, 1, first_head=1)
    ikb = proj(C_IK, C_IQ)
    ik_ref[...] = _rope_apply(ikb, c_d, sa_d, sb_d, 8)[:, 0:64].astype(BF16)
    iq = proj(C_IQ, C_MC)
    for i in range(4):
        iq_ref[:, i * 128:(i + 1) * 128] = _rope_apply(iq[:, i * 128:(i + 1) * 128], c_d, sa_d, sb_d, 8).astype(BF16)
    ones_hi = jnp.where(lax.broadcasted_iota(I32, (1, FA_HEADS * HEAD_W), 1) % HEAD_W >= 64, 1.0, 0.0)

    mc = proj(C_MC, C_FQ)
    cq = _rms(mc[:, 0:256], qn_ref[...]).astype(BF16)
    ckv = _rms(mc[:, 256:384], kn_ref[...]).astype(BF16)
    kr = _rope_apply(mc[:, 384:512], c_m, sa_m, sb_m, 16)
    mq = jnp.dot(cq, wq_ref[...], preferred_element_type=F32)
    mk = jnp.dot(ckv, wk_ref[...], preferred_element_type=F32)
    for hd in range(MLA_HEADS):
        sl = slice(hd * HEAD_W, (hd + 1) * HEAD_W)
        mq_ref[:, sl] = _rope_apply(mq[:, sl], c_m, sa_m, sb_m, 16).astype(BF16)
        mk_ref[:, sl] = (mk[:, sl] + kr).astype(BF16)
    mv_ref[...] = (jnp.dot(ckv, wv_ref[...], preferred_element_type=F32) + ones_hi).astype(BF16)

    @pl.when(pl.program_id(0) % tiles_per_seq == 0)
    def _():
        carry_s[...] = jnp.zeros_like(carry_s)

    tm = x.shape[0]
    log_f = -_softplus(-(gates + fb_ref[...]))
    ri = lax.broadcasted_iota(I32, (tm, tm), 0)
    ci = lax.broadcasted_iota(I32, (tm, tm), 1)
    cs = _dot_f32_by_mask(ri >= ci, log_f, True) + carry_s[0:1, :]
    carry_s[0:1, :] = cs[tm - 1:tm, :]
    f2 = cs * LOG2E
    lane = lax.broadcasted_iota(I32, (1, HEAD_W), 1)

    def pieces(f, base):
        hi, mid, lo = _split3(f)
        return jnp.where(lane == base, hi.astype(F32),
                         jnp.where(lane == base + 1, mid.astype(F32), jnp.where(lane == base + 2, lo.astype(F32), 0.0)))

    def ones(base):
        return jnp.where((lane >= base) & (lane < base + 3), 1.0, 0.0)

    fq = proj(C_FQ, C_FK)
    fk = proj(C_FK, C_FV)
    for hd in range(FOX_HEADS):
        fh = f2[:, G_FOX + hd:G_FOX + hd + 1]
        sl = slice(hd * HEAD_W, (hd + 1) * HEAD_W)
        fq_ref[:, sl] = (fq[:, sl] + pieces(fh, FOX_F_Q) + ones(FOX_ONE_Q)).astype(BF16)
        fk_ref[:, sl] = (fk[:, sl] - pieces(fh, FOX_F_K) + ones(FOX_ONE_K)).astype(BF16)
    fv_ref[...] = (proj(C_FV, C_END) + ones_hi).astype(BF16)


def _pack_w_in(w):
    d = w.shape[0]
    z = lambda n: jnp.zeros((d, n), F32)
    o = 0
    parts = {}
    for name, width in (("z", 256), ("xbc", 768), ("dt", 4), ("dq", 256), ("dk", 64), ("dv", 64), ("iq", 512),
                        ("ik", 64), ("iw", 8), ("cq", 256), ("ckv", 128), ("kr", 32), ("fq", 256), ("fk", 256),
                        ("fv", 256), ("ff", 4)):
        parts[name] = w[:, o:o + width]
        o += width
    scale = DSA_HEAD_DIM ** -0.5

    def per_head(a):
        return jnp.pad(a.reshape(d, FA_HEADS, 64), ((0, 0), (0, 0), (0, HEAD_W - 64))).reshape(d, FA_HEADS * HEAD_W)

    cat = jnp.concatenate([
        parts["z"], parts["xbc"],
        parts["ff"], parts["dt"], parts["iw"], z(112),
        parts["dq"] * (scale * LOG2E),
        parts["dk"], parts["dv"],
        parts["ik"], z(64),
        parts["iq"],
        parts["cq"], parts["ckv"], z(64), parts["kr"], z(32),
        per_head(parts["fq"] * (FOX_HEAD_DIM ** -0.5 * LOG2E)), per_head(parts["fk"]), per_head(parts["fv"])], axis=1)
    return cat.astype(BF16)


def _mla_weights(w_uq, w_ukv):
    dqk = MLA_NOPE + MLA_ROPE
    wq = jnp.pad(w_uq.reshape(MLA_Q_RANK, MLA_HEADS, dqk) * (dqk ** -0.5 * LOG2E), ((0, 0), (0, 0), (0, HEAD_W - dqk)))
    wkv = w_ukv.reshape(MLA_KV_RANK, MLA_HEADS, MLA_NOPE + MLA_V)
    wk = jnp.pad(wkv[:, :, :MLA_NOPE], ((0, 0), (0, 0), (0, HEAD_W - MLA_NOPE)))
    wv = jnp.pad(wkv[:, :, MLA_NOPE:], ((0, 0), (0, 0), (0, HEAD_W - MLA_V)))
    flat = lambda a: a.reshape(a.shape[0], MLA_HEADS * HEAD_W).astype(BF16)
    return flat(wq), flat(wk), flat(wv)


def _inproj(x, y_prev, g, w_cat, tab, f_bias, q_norm, kv_norm, w_uq, w_ukv, s):
    t = x.shape[0]
    has_add = y_prev is not None
    tok = lambda n: pl.BlockSpec((TM, n), lambda i: (i, 0))
    full = lambda a: pl.BlockSpec(a.shape, lambda i: (0,) * a.ndim)
    fb = jnp.zeros((1, LANES), F32).at[0, G_FOX:G_FOX + FOX_HEADS].set(f_bias)
    params = [g, w_cat, tab, fb, q_norm.reshape(1, -1), kv_norm.reshape(1, -1), *_mla_weights(w_uq, w_ukv)]
    ins = [x] + ([y_prev] if has_add else []) + params
    in_specs = ([tok(D_MODEL)] + ([tok(D_MODEL)] if has_add else [])
                + [tok(768) if a is tab else full(a) for a in params])
    outs = ([("x", D_MODEL, F32)] if has_add else []) + [
        ("zx", 1024, F32), ("gt", 128, F32), ("dq", 256, BF16), ("dk", 64, BF16), ("dv", 64, BF16),
        ("ik", 64, BF16), ("iq", 512, BF16), ("mq", 512, BF16), ("mk", 512, BF16), ("mv", 512, BF16),
        ("fq", 512, BF16), ("fk", 512, BF16), ("fv", 512, BF16)]
    vt_rows = {"dv": VT_ROWS}
    res = pl.pallas_call(
        functools.partial(_inproj_kernel, has_add=has_add, tiles_per_seq=s // TM),
        grid=(t // TM,),
        in_specs=in_specs,
        out_specs=[pl.BlockSpec((1, vt_rows[nm], TM), lambda i: (i, 0, 0)) if nm in vt_rows else tok(n)
                   for nm, n, _ in outs],
        out_shape=[jax.ShapeDtypeStruct((t // TM, vt_rows[nm], TM) if nm in vt_rows else (t, n), dt)
                   for nm, n, dt in outs],
        scratch_shapes=[pltpu.VMEM((8, LANES), F32)],
        compiler_params=_cparams(("arbitrary",)),
        name="inproj",
    )(*ins)
    res = list(res)
    x_new = res.pop(0) if has_add else x
    return x_new, dict(zip([n for n, _, _ in outs if n != "x"], res))


def _ssd_kernel(zx_ref, gt_ref, cw_ref, cb_ref, pr_ref, pc_ref, drow_ref, nw_ref, o_ref, ext_s, st_s, y_s, *, q):
    @pl.when(pl.program_id(1) == 0)
    def _():
        ext_s[0:8, :] = jnp.zeros((8, SSD_CONV_DIM), F32)
        st_s[...] = jnp.zeros_like(st_s)

    raw = zx_ref[:, 256:1024]
    ext_s[8:8 + q, :] = raw
    acc = jnp.broadcast_to(cb_ref[...], (q, SSD_CONV_DIM))
    for j in range(SSD_CONV):
        acc = acc + cw_ref[j:j + 1, :] * ext_s[5 + j:5 + j + q, :]
    ext_s[0:8, :] = raw[q - 8:q, :]
    xbc = _silu(acc)
    xs = xbc[:, 0:SSD_D_INNER]

    g = gt_ref[...]
    lane = lax.broadcasted_iota(I32, (1, LANES), 1)
    dtc = _softplus(g + pr_ref[0:1, :])
    a_r = jnp.where((lane >= G_DT) & (lane < G_DT + SSD_HEADS), -jnp.exp(pr_ref[1:2, :]), 0.0)
    ri = lax.broadcasted_iota(I32, (q, q), 0)
    ci = lax.broadcasted_iota(I32, (q, q), 1)
    tri = ri >= ci
    acs_c = _dot_f32_by_mask(tri, dtc * a_r, True)
    sub = lax.broadcasted_iota(I32, (LANES, 1), 0)
    dtr = _softplus(g.T + pc_ref[:, 0:1])
    a_c = jnp.where((sub >= G_DT) & (sub < G_DT + SSD_HEADS), -jnp.exp(pc_ref[:, 1:2]), 0.0)
    acs_r = _dot_f32_by_mask(ri <= ci, dtr * a_c, False)

    rep = SSD_HEADS // SSD_GROUPS
    for gi in range(SSD_GROUPS):
        bg = xbc[:, SSD_D_INNER + gi * SSD_STATE:SSD_D_INNER + (gi + 1) * SSD_STATE]
        cg = xbc[:, SSD_D_INNER + (SSD_GROUPS + gi) * SSD_STATE:SSD_D_INNER + (SSD_GROUPS + gi + 1) * SSD_STATE]
        bt = bg.T.astype(BF16)
        cb16 = cg.astype(BF16)
        cbm = jnp.dot(cb16, bt, preferred_element_type=F32)
        for hh in range(rep):
            h = gi * rep + hh
            ac = acs_c[:, G_DT + h:G_DT + h + 1]
            ar = acs_r[G_DT + h:G_DT + h + 1, :]
            seg = jnp.where(tri, jnp.exp(ac - ar), 0.0)
            xh = xs[:, h * SSD_HEAD_DIM:(h + 1) * SSD_HEAD_DIM]
            xdt = xh * dtc[:, G_DT + h:G_DT + h + 1]
            yd = jnp.dot((cbm * seg).astype(BF16), xdt.astype(BF16), preferred_element_type=F32)
            aend = ac[q - 1:q, :]
            st = st_s[h]
            yo = jnp.dot(cb16, st.astype(BF16), preferred_element_type=F32) * jnp.exp(ac)
            st_s[h] = st * jnp.exp(aend) + jnp.dot(bt, (xdt * jnp.exp(aend - ac)).astype(BF16),
                                                   preferred_element_type=F32)
            y_s[:, h * SSD_HEAD_DIM:(h + 1) * SSD_HEAD_DIM] = yd + yo
    y = y_s[...] + drow_ref[...] * xs
    y = y * _silu(zx_ref[:, 0:SSD_D_INNER])
    o_ref[...] = _rms(y, nw_ref[...]).astype(BF16)


def _ssd(zx, gt, conv_w, conv_b, dt_bias, a_log, d_skip, norm_w, b, s):
    q = SSD_Q
    nc = s // q
    pr = jnp.zeros((8, LANES), F32).at[0, G_DT:G_DT + SSD_HEADS].set(dt_bias).at[1, G_DT:G_DT + SSD_HEADS].set(a_log)
    pc = pr.T
    drow = jnp.repeat(d_skip, SSD_HEAD_DIM).reshape(1, SSD_D_INNER)
    cb = conv_b.reshape(1, -1)
    nw = norm_w.reshape(1, -1)
    full = lambda a: pl.BlockSpec(a.shape, lambda i, j: (0,) * a.ndim)
    return pl.pallas_call(
        functools.partial(_ssd_kernel, q=q),
        grid=(b, nc),
        in_specs=[pl.BlockSpec((q, 1024), lambda i, j: (i * nc + j, 0)), pl.BlockSpec((q, LANES), lambda i, j: (i * nc + j, 0)),
                  full(conv_w), full(cb), full(pr), full(pc), full(drow), full(nw)],
        out_specs=pl.BlockSpec((q, SSD_D_INNER), lambda i, j: (i * nc + j, 0)),
        out_shape=jax.ShapeDtypeStruct((b * s, SSD_D_INNER), BF16),
        scratch_shapes=[pltpu.VMEM((8 + q, SSD_CONV_DIM), F32), pltpu.VMEM((SSD_HEADS, SSD_STATE, SSD_HEAD_DIM), F32),
                        pltpu.VMEM((q, SSD_D_INNER), F32)],
        compiler_params=_cparams(("parallel", "arbitrary")),
        name="ssd",
    )(zx, gt, conv_w, cb, pr, pc, drow, nw)


def _key_to_f32(k):
    return pltpu.bitcast(jnp.where(k < 0, k ^ jnp.int32(0x7FFFFFFF), k), F32)


def _dsa_kernel(iq_ref, gt_ref, q_ref, ik_ref, k_ref, vt_ref, o_ref, sc_s, qi_s, qs_s, thr_s, *, bq, sc_w, n_sel):
    qb = pl.program_id(1)
    n_sc = ((qb + 1) * bq + sc_w - 1) // sc_w
    for j in range(IDX_HEADS):
        qi_s[j // 2, (j % 2) * bq:(j % 2 + 1) * bq, :] = iq_ref[:, j * IDX_DIM:(j + 1) * IDX_DIM]
    for h in range(DSA_HEADS):
        qs_s[h * bq:(h + 1) * bq, :] = q_ref[:, h * DSA_HEAD_DIM:(h + 1) * DSA_HEAD_DIM]
    w_t = (gt_ref[...] * ((IDX_HEADS * IDX_DIM) ** -0.5)).T
    qpos = qb * bq + lax.broadcasted_iota(I32, (1, bq), 1)

    def score_body(sc, carry):
        k0 = pl.multiple_of(sc * sc_w, sc_w)
        kidx = ik_ref[pl.ds(k0, sc_w), :]
        sco = jnp.zeros((sc_w, bq), F32)
        for jp in range(IDX_HEADS // 2):
            a = _dot_nt(kidx, qi_s[jp])
            for u in range(2):
                j = 2 * jp + u
                sco = sco + jnp.maximum(a[:, u * bq:(u + 1) * bq], 0.0) * w_t[G_IW + j:G_IW + j + 1, :]
        kpos = k0 + lax.broadcasted_iota(I32, (sc_w, 1), 0)
        sc_s[sc] = jnp.where(kpos <= qpos, sco, -jnp.inf)
        return carry

    lax.fori_loop(0, n_sc, score_body, 0)

    def count(pred):
        def body(sc, acc):
            for r in range(sc_w // 64):
                acc = acc + jnp.where(pred(sc_s[sc, r * 64:(r + 1) * 64, :]), 1.0, 0.0)
            return acc
        acc = lax.fori_loop(0, n_sc, body, jnp.zeros((64, bq), F32))
        return jnp.sum(acc, axis=0, keepdims=True)

    thr_s[0:1, :] = jnp.full((1, bq), -jnp.finfo(jnp.float32).max, F32)

    @pl.when((qb + 1) * bq > n_sel)
    def _():
        active = qpos + 1 > n_sel

        def cond(st):
            it, lo, cnt_lo = st
            pending = active & (cnt_lo != n_sel)
            return (it < 32) & (jnp.max(jnp.where(pending, 1.0, 0.0)) > 0.0)

        def body(st):
            it, lo, cnt_lo = st
            for u in range(4):
                cand = lo ^ jnp.left_shift(jnp.int32(1), 31 - (it + u))
                thr_c = _key_to_f32(cand)
                cnt = count(lambda t: t >= thr_c)
                ok = cnt >= n_sel
                lo, cnt_lo = jnp.where(ok, cand, lo), jnp.where(ok, cnt, cnt_lo)
            return it + 4, lo, cnt_lo

        _, lo, cnt_lo = lax.while_loop(
            cond, body, (jnp.int32(0), jnp.full((1, bq), INT_MIN, I32), jnp.full((1, bq), 3.0e38, F32)))
        thr = _key_to_f32(lo)
        tie = active & (cnt_lo > n_sel)

        @pl.when(jnp.max(jnp.where(tie, 1.0, 0.0)) > 0.0)
        def _():
            need = n_sel - count(lambda t: t > thr)
            ri = lax.broadcasted_iota(I32, (sc_w, sc_w), 0)
            ci = lax.broadcasted_iota(I32, (sc_w, sc_w), 1)
            lower = jnp.where(ri >= ci, 1.0, 0.0).astype(BF16)

            def tie_body(sc, run):
                t = sc_s[sc]
                eq = (t == thr) & tie
                pre = jnp.dot(lower, jnp.where(eq, 1.0, 0.0).astype(BF16), preferred_element_type=F32)
                sc_s[sc] = jnp.where(eq & (run + pre > need), -jnp.inf, t)
                return run + pre[sc_w - 1:sc_w, :]

            lax.fori_loop(0, n_sc, tie_body, jnp.zeros((1, bq), F32))

        thr_s[0:1, :] = jnp.where(active, thr, thr_s[0:1, :])

    thr = thr_s[0:1, :]

    hq = DSA_HEADS * bq

    def att_body(sc, carry):
        m_prev, acc = carry
        k0 = pl.multiple_of(sc * sc_w, sc_w)
        st = _dot_nt(k_ref[pl.ds(k0, sc_w), :], qs_s[...])
        drop = jnp.where(sc_s[sc] >= thr, 0.0, 2 * NEG_BIG)
        st = st + jnp.tile(drop, (1, DSA_HEADS))
        m_cur = jnp.max(jnp.max(st.reshape(sc_w // 64, 64, hq), axis=0), axis=0, keepdims=True)
        m_next = jnp.maximum(m_prev, m_cur)
        p = jnp.exp2(st - m_next)
        pv = jnp.dot(vt_ref[sc], p.astype(BF16), preferred_element_type=F32)
        return m_next, jnp.exp2(m_prev - m_next) * acc + pv

    def att_body2(i, carry):
        return att_body(2 * i + 1, att_body(2 * i, carry))

    carry = lax.fori_loop(0, n_sc // 2, att_body2, (jnp.full((1, hq), NEG_BIG, F32), jnp.zeros((VT_ROWS, hq), F32)))
    _, acc = lax.cond(n_sc % 2 == 1, lambda c: att_body(n_sc - 1, c), lambda c: c, carry)
    out_t = jnp.concatenate([acc[0:DSA_HEAD_DIM, :] / acc[DSA_HEAD_DIM:DSA_HEAD_DIM + 1, :],
                             jnp.zeros((LANES - DSA_HEAD_DIM, hq), F32)], axis=0).T
    for h in range(DSA_HEADS):
        o_ref[:, h * DSA_HEAD_DIM:(h + 1) * DSA_HEAD_DIM] = out_t[h * bq:(h + 1) * bq, 0:DSA_HEAD_DIM].astype(BF16)


def _dsa(iq, gt, dq, ik, dk, dvt, b, s, n_sel):
    bq, sc_w = DSA_BQ, DSA_SC
    nq = s // bq
    qblk = lambda n: pl.BlockSpec((bq, n), lambda i, j: (i * nq + j, 0))
    kblk = pl.BlockSpec((s, 64), lambda i, j: (i, 0))
    return pl.pallas_call(
        functools.partial(_dsa_kernel, bq=bq, sc_w=sc_w, n_sel=n_sel),
        grid=(b, nq),
        in_specs=[qblk(512), qblk(LANES), qblk(256), kblk, kblk,
                  pl.BlockSpec((s // sc_w, VT_ROWS, sc_w), lambda i, j: (i, 0, 0))],
        out_specs=qblk(256),
        out_shape=jax.ShapeDtypeStruct((b * s, 256), BF16),
        scratch_shapes=[pltpu.VMEM((s // sc_w, sc_w, bq), F32), pltpu.VMEM((IDX_HEADS // 2, 2 * bq, IDX_DIM), BF16),
                        pltpu.VMEM((DSA_HEADS * bq, DSA_HEAD_DIM), BF16), pltpu.VMEM((8, bq), F32)],
        compiler_params=_cparams(("parallel", "arbitrary")),
        name="dsa",
    )(iq, gt, dq, ik, dk, dvt)


def _flash_kernel(qt_ref, kt_ref, q_ref, k_ref, v_ref, o_ref, m_s, acc_s, *, blk):
    t = pl.program_id(1)
    qi, ki = qt_ref[t], kt_ref[t]

    @pl.when(ki == 0)
    def _():
        m_s[...] = jnp.full(m_s.shape, NEG_BIG, F32)
        acc_s[...] = jnp.zeros_like(acc_s)

    def step(diagonal):
        if diagonal:
            keep = lax.broadcasted_iota(I32, (blk, blk), 0) >= lax.broadcasted_iota(I32, (blk, blk), 1)
        for h in range(FA_HEADS):
            sl = slice(h * HEAD_W, (h + 1) * HEAD_W)
            s = _dot_nt(q_ref[:, sl], k_ref[:, sl])
            if diagonal:
                s = jnp.where(keep, s, 2 * NEG_BIG)
            m_prev = m_s[h]
            m_next = jnp.maximum(m_prev, jnp.max(s, axis=1, keepdims=True))
            p = jnp.exp2(s - jnp.tile(m_next, (1, blk // LANES)))
            pv = jnp.dot(p.astype(BF16), v_ref[:, sl], preferred_element_type=F32)
            acc_s[h] = jnp.exp2(m_prev - m_next) * acc_s[h] + pv
            m_s[h] = m_next

    @pl.when(ki < qi)
    def _():
        step(False)

    @pl.when(ki == qi)
    def _():
        step(True)
        for h in range(FA_HEADS):
            acc = acc_s[h]
            o_ref[:, h * 64:(h + 1) * 64] = (acc / pltpu.roll(acc, 64, 1))[:, 0:64].astype(BF16)


def _flash(q, k, v, b, s, name):
    blk = FA_B
    nb = s // blk
    pairs = [(i, j) for i in range(nb) for j in range(i + 1)]
    qt = jnp.asarray([p[0] for p in pairs], I32)
    kt = jnp.asarray([p[1] for p in pairs], I32)
    w = FA_HEADS * HEAD_W
    grid_spec = pltpu.PrefetchScalarGridSpec(
        num_scalar_prefetch=2,
        grid=(b, len(pairs)),
        in_specs=[pl.BlockSpec((blk, w), lambda i, t, qt, kt: (i * nb + qt[t], 0)),
                  pl.BlockSpec((blk, w), lambda i, t, qt, kt: (i * nb + kt[t], 0)),
                  pl.BlockSpec((blk, w), lambda i, t, qt, kt: (i * nb + kt[t], 0))],
        out_specs=pl.BlockSpec((blk, FA_HEADS * 64), lambda i, t, qt, kt: (i * nb + qt[t], 0)),
        scratch_shapes=[pltpu.VMEM((FA_HEADS, blk, LANES), F32), pltpu.VMEM((FA_HEADS, blk, HEAD_W), F32)],
    )
    return pl.pallas_call(
        functools.partial(_flash_kernel, blk=blk),
        grid_spec=grid_spec,
        out_shape=jax.ShapeDtypeStruct((b * s, FA_HEADS * 64), BF16),
        compiler_params=_cparams(("parallel", "arbitrary")),
        name=name,
    )(qt, kt, q, k, v)


def _outproj_kernel(x_ref, y0_ref, y1_ref, y2_ref, y3_ref, wo_ref, g_ref, wr_ref, br_ref,
                    xo_ref, h_ref, cw_ref, rs_ref):
    acc = x_ref[...]
    for i, y_ref in enumerate((y0_ref, y1_ref, y2_ref, y3_ref)):
        acc = acc + jnp.dot(y_ref[...], wo_ref[i * 256:(i + 1) * 256, :], preferred_element_type=F32)
    xo_ref[...] = acc
    h2 = _rms(acc, g_ref[...])
    h_ref[...] = h2.astype(BF16)
    h_hi, h_lo, _ = _split3(h2)
    lg = (jnp.dot(h_hi, wr_ref[0], preferred_element_type=F32) + jnp.dot(h_lo, wr_ref[0], preferred_element_type=F32)
          + jnp.dot(h_hi, wr_ref[1], preferred_element_type=F32) + br_ref[...])
    lane = lax.broadcasted_iota(I32, (1, LANES), 1)
    lanef = lane.astype(F32)
    is_g = (lane >= MOE_EXPERTS) & (lane < MOE_EXPERTS + MOE_GROUPS)
    gl = jnp.where(is_g, lg, -jnp.inf)
    gmax = jnp.max(gl, axis=1, keepdims=True)
    gidx = jnp.min(jnp.where(gl == gmax, lanef, 999.0), axis=1, keepdims=True) - MOE_EXPERTS
    g_p = 1.0 / jnp.sum(jnp.where(is_g, jnp.exp(gl - gmax), 0.0), axis=1, keepdims=True)
    in_g = (lane < MOE_EXPERTS) & (jnp.floor(lanef * (1.0 / MOE_EPG)) == gidx)
    el = jnp.where(in_g, lg, -jnp.inf)
    m1 = jnp.max(el, axis=1, keepdims=True)
    i1 = jnp.min(jnp.where(el == m1, lanef, 999.0), axis=1, keepdims=True)
    el2 = jnp.where(lanef == i1, -jnp.inf, el)
    m2 = jnp.max(el2, axis=1, keepdims=True)
    i2 = jnp.min(jnp.where(el2 == m2, lanef, 999.0), axis=1, keepdims=True)
    t = jnp.exp(m2 - m1)
    w1 = 1.0 / (1.0 + t)
    cw_ref[...] = jnp.where(lanef == i1, w1 * g_p, jnp.where(lanef == i2, t * w1 * g_p, 0.0))
    rs_ref[...] = jnp.where((lanef == i1) | (lanef == i2), 1.0, 0.0).astype(BF16)


def _outproj(x, ys, w_out, g, w_rg, b_rg, w_re, b_re):
    t = x.shape[0]
    wo = w_out.astype(BF16)
    wr = jnp.zeros((D_MODEL, LANES), F32).at[:, 0:MOE_EXPERTS].set(w_re).at[:, MOE_EXPERTS:MOE_EXPERTS + MOE_GROUPS].set(w_rg)
    wr_hi = wr.astype(BF16)
    wr = jnp.stack([wr_hi, (wr - wr_hi.astype(F32)).astype(BF16)])
    br =jnp.zeros((1, LANES), F32).at[0, 0:MOE_EXPERTS].set(b_re).at[0, MOE_EXPERTS:MOE_EXPERTS + MOE_GROUPS].set(b_rg)
    tok = lambda n: pl.BlockSpec((TM, n), lambda i: (i, 0))
    full = lambda a: pl.BlockSpec(a.shape, lambda i: (0,) * a.ndim)
    return pl.pallas_call(
        _outproj_kernel,
        grid=(t // TM,),
        in_specs=[tok(D_MODEL)] + [tok(256)] * 4 + [full(wo), full(g), full(wr), full(br)],
        out_specs=[tok(D_MODEL), tok(D_MODEL), tok(LANES), tok(LANES)],
        out_shape=[jax.ShapeDtypeStruct((t, D_MODEL), F32), jax.ShapeDtypeStruct((t, D_MODEL), BF16),
                   jax.ShapeDtypeStruct((t, LANES), F32), jax.ShapeDtypeStruct((t, LANES), BF16)],
        compiler_params=_cparams(("parallel",)),
        name="outproj_router",
    )(x, *ys, wo, g, wr, br)


def _moe_plan_kernel(rs_ref, cw_ref, meta_ref, col_ref, row_ref, *, n):
    sel = rs_ref[...]
    self32 = sel.astype(F32)
    ti = lax.broadcasted_iota(I32, (n, n), 0)
    tj = lax.broadcasted_iota(I32, (n, n), 1)
    rank = jnp.dot(jnp.where(ti > tj, 1.0, 0.0).astype(BF16), sel, preferred_element_type=F32)
    cnt = jnp.sum(self32, axis=0, keepdims=True).astype(I32)
    cpad = ((cnt + (MOE_ALIGN - 1)) // MOE_ALIGN) * MOE_ALIGN
    li = lax.broadcasted_iota(I32, (LANES, LANES), 0)
    lj = lax.broadcasted_iota(I32, (LANES, LANES), 1)
    off = _dot_f32_by_mask(li < lj, jnp.broadcast_to(cpad.astype(F32), (8, LANES)), False)[0:1, :]
    dest = off + rank
    lane = lax.broadcasted_iota(I32, (1, LANES), 1)
    lanef = lane.astype(F32)
    on = self32 > 0.5
    e_lo = jnp.min(jnp.where(on, lanef, 999.0), axis=1, keepdims=True)
    e_hi = jnp.max(jnp.where(on, lanef, -1.0), axis=1, keepdims=True)
    is_lo, is_hi = lanef == e_lo, lanef == e_hi
    cw = cw_ref[...]
    pick = lambda m, a: jnp.sum(jnp.where(m, a, 0.0), axis=1, keepdims=True)
    d_lo, d_hi, w_lo, w_hi = pick(is_lo, dest), pick(is_hi, dest), pick(is_lo, cw), pick(is_hi, cw)
    colv = jnp.where(lane == 0, d_lo, jnp.where(lane == 1, d_hi, jnp.where(lane == 2, w_lo, jnp.where(lane == 3, w_hi, 0.0))))
    col_ref[...] = colv
    row_ref[0] = colv.T[0:8, :]
    sub = lax.broadcasted_iota(I32, (8, LANES), 0)
    meta_ref[0] = jnp.where(sub == 0, off.astype(I32), jnp.where(sub == 1, cnt, 0))


def _moe_kernel(meta_ref, h_ref, col_ref, row_ref, wg_ref, wu_ref, wd_ref, o_ref, s_s, y_s, *, n, ns):
    t, e = pl.program_id(0), pl.program_id(1)

    @pl.when(e == 0)
    def _():
        for g in range(MOE_SUB):
            d_lo, d_hi = row_ref[g, 0:1, :], row_ref[g, 1:2, :]
            hg = h_ref[g * n:(g + 1) * n, :]
            for r in range(ns // LANES):
                si = (r * LANES + lax.broadcasted_iota(I32, (LANES, 1), 0)).astype(F32)
                perm = jnp.where((si == d_lo) | (si == d_hi), 1.0, 0.0).astype(BF16)
                s_s[g, r * LANES:(r + 1) * LANES, :] = jnp.dot(perm, hg, preferred_element_type=F32).astype(BF16)
        y_s[...] = jnp.zeros_like(y_s)

    def mlp(j, starts, store_ok):
        starts = [pl.multiple_of(r0, MOE_ALIGN) for r0 in starts]
        xs = jnp.concatenate([s_s[g, pl.ds(starts[g], MOE_CH), :] for g in range(MOE_SUB)], axis=0)
        gate = jnp.dot(xs, wg_ref[j], preferred_element_type=F32)
        up = jnp.dot(xs, wu_ref[j], preferred_element_type=F32)
        hid = (_silu(gate) * up).astype(BF16)
        y = jnp.dot(hid, wd_ref[j], preferred_element_type=F32).astype(BF16)
        for g in range(MOE_SUB):
            def store(g=g):
                y_s[g, pl.ds(starts[g], MOE_CH), :] = y[g * MOE_CH:(g + 1) * MOE_CH, :]
            if store_ok is None:
                store()
            else:
                pl.when(store_ok[g])(store)

    def meta(g, j, what):
        return meta_ref[(t * MOE_SUB + g) * 2 * MOE_EXPERTS + what * MOE_EXPERTS + e * MOE_EPS + j]

    offs = [[meta(g, j, 0) for g in range(MOE_SUB)] for j in range(MOE_EPS)]
    cnts = [[meta(g, j, 1) for g in range(MOE_SUB)] for j in range(MOE_EPS)]
    for j in range(MOE_EPS):
        mlp(j, offs[j], None)
    for j in range(MOE_EPS):
        ends = [offs[j][g] + ((cnts[j][g] + MOE_ALIGN - 1) // MOE_ALIGN) * MOE_ALIGN for g in range(MOE_SUB)]
        nch = [(cnts[j][g] + MOE_CH - 1) // MOE_CH for g in range(MOE_SUB)]

        def body(i, carry, j=j, ends=ends, nch=nch):
            more = [i < nch[g] for g in range(MOE_SUB)]
            starts = [jnp.where(more[g], jnp.minimum(offs[j][g] + i * MOE_CH, ends[g] - MOE_CH), offs[j][g])
                      for g in range(MOE_SUB)]
            mlp(j, starts, more)
            return carry

        lax.fori_loop(1, functools.reduce(jnp.maximum, nch), body, 0)

    @pl.when(e == pl.num_programs(1) - 1)
    def _():
        rows, kc = 256, ns // 3
        for g in range(MOE_SUB):
            for c in range(n // rows):
                r0 = g * n + c * rows
                cv = col_ref[r0:r0 + rows, :]
                d_lo, d_hi, w_lo, w_hi = cv[:, 0:1], cv[:, 1:2], cv[:, 2:3], cv[:, 3:4]
                acc = jnp.zeros((rows, D_MODEL), F32)
                for r in range(ns // kc):
                    si = (r * kc + lax.broadcasted_iota(I32, (1, kc), 1)).astype(F32)
                    pw = (jnp.where(si == d_lo, w_lo, 0.0) + jnp.where(si == d_hi, w_hi, 0.0)).astype(BF16)
                    acc = acc + jnp.dot(pw, y_s[g, r * kc:(r + 1) * kc, :], preferred_element_type=F32)
                o_ref[r0:r0 + rows, :] = acc.astype(BF16)


def _moe(h2, cw, rs, w_gate, w_up, w_down):
    t = h2.shape[0]
    n, ns = MOE_N, MOE_NS
    nt = t // n
    meta, col, row = pl.pallas_call(
        functools.partial(_moe_plan_kernel, n=n),
        grid=(nt,),
        in_specs=[pl.BlockSpec((n, LANES), lambda i: (i, 0)), pl.BlockSpec((n, LANES), lambda i: (i, 0))],
        out_specs=[pl.BlockSpec((1, 8, LANES), lambda i: (i, 0, 0)), pl.BlockSpec((n, LANES), lambda i: (i, 0)),
                   pl.BlockSpec((1, 8, n), lambda i: (i, 0, 0))],
        out_shape=[jax.ShapeDtypeStruct((nt, 8, LANES), I32), jax.ShapeDtypeStruct((t, LANES), F32),
                   jax.ShapeDtypeStruct((nt, 8, n), F32)],
        compiler_params=_cparams(("parallel",)),
        name="moe_plan",
    )(rs, cw)
    meta_flat = meta[:, 0:2, 0:MOE_EXPERTS].reshape(-1)
    wg, wu, wd = w_gate.astype(BF16), w_up.astype(BF16), w_down.astype(BF16)
    grid_spec = pltpu.PrefetchScalarGridSpec(
        num_scalar_prefetch=1,
        grid=(nt // MOE_SUB, MOE_EXPERTS // MOE_EPS),
        in_specs=[pl.BlockSpec((MOE_SUB * n, D_MODEL), lambda i, e, m: (i, 0)),
                  pl.BlockSpec((MOE_SUB * n, LANES), lambda i, e, m: (i, 0)),
                  pl.BlockSpec((MOE_SUB, 8, n), lambda i, e, m: (i, 0, 0)),
                  pl.BlockSpec((MOE_EPS, D_MODEL, MOE_FF), lambda i, e, m: (e, 0, 0)),
                  pl.BlockSpec((MOE_EPS, D_MODEL, MOE_FF), lambda i, e, m: (e, 0, 0)),
                  pl.BlockSpec((MOE_EPS, MOE_FF, D_MODEL), lambda i, e, m: (e, 0, 0))],
        out_specs=pl.BlockSpec((MOE_SUB * n, D_MODEL), lambda i, e, m: (i, 0)),
        scratch_shapes=[pltpu.VMEM((MOE_SUB, ns, D_MODEL), BF16), pltpu.VMEM((MOE_SUB, ns, D_MODEL), BF16)],
    )
    return pl.pallas_call(
        functools.partial(_moe_kernel, n=n, ns=ns),
        grid_spec=grid_spec,
        out_shape=jax.ShapeDtypeStruct((t, D_MODEL), BF16),
        compiler_params=_cparams(("parallel", "arbitrary")),
        name="moe_experts",
    )(meta_flat, h2, col, row, wg, wu, wd)


def _final_kernel(x_ref, y_ref, g_ref, o_ref):
    o_ref[...] = _rms(x_ref[...] + y_ref[...].astype(F32), g_ref[...])


def _final(x, y, g):
    t = x.shape[0]
    tok = pl.BlockSpec((TM, D_MODEL), lambda i: (i, 0))
    return pl.pallas_call(
        _final_kernel,
        grid=(t // TM,),
        in_specs=[tok, tok, pl.BlockSpec((1, D_MODEL), lambda i: (0, 0))],
        out_specs=tok,
        out_shape=jax.ShapeDtypeStruct((t, D_MODEL), F32),
        compiler_params=_cparams(("parallel",)),
        name="final_norm",
    )(x, y, g)


def kernel(x, positions, norm_mix, w_in, ssd_conv_w, ssd_conv_b, ssd_dt_bias, ssd_a_log, ssd_d, ssd_norm, mla_q_norm, mla_w_uq, mla_kv_norm, mla_w_ukv, fox_f_bias, w_out, norm_ffn, router_group_w, router_group_b, router_expert_w, router_expert_b, expert_w_gate, expert_w_up, expert_w_down, final_norm):
    b, s, d = x.shape
    assert d == D_MODEL and s % FA_B == 0 and (b * s) % (MOE_N * MOE_SUB) == 0 and TM == DSA_SC
    depth = w_in.shape[0]
    n_sel = min(IDX_TOPK_MAX, s // 4)
    tab = _rope_tables(positions)
    xf = x.reshape(b * s, d)
    y_ffn = None
    for l in range(depth):
        xf, p = _inproj(xf, y_ffn, norm_mix[l].reshape(1, d), _pack_w_in(w_in[l]), tab, fox_f_bias[l],
                        mla_q_norm[l], mla_kv_norm[l], mla_w_uq[l], mla_w_ukv[l], s)
        y_ssd = _ssd(p["zx"], p["gt"], ssd_conv_w[l], ssd_conv_b[l], ssd_dt_bias[l], ssd_a_log[l], ssd_d[l],
                     ssd_norm[l], b, s)
        y_dsa = _dsa(p["iq"], p["gt"], p["dq"], p["ik"], p["dk"], p["dv"], b, s, n_sel)
        y_mla = _flash(p["mq"], p["mk"], p["mv"], b, s, "flash_mla")
        y_fox = _flash(p["fq"], p["fk"], p["fv"], b, s, "flash_fox")
        xf, h2, cw, rs = _outproj(xf, (y_ssd, y_dsa, y_mla, y_fox), w_out[l], norm_ffn[l].reshape(1, d),
                                  router_group_w[l], router_group_b[l], router_expert_w[l], router_expert_b[l])
        y_ffn = _moe(h2, cw, rs, expert_w_gate[l], expert_w_up[l], expert_w_down[l])
    return _final(xf, y_ffn, final_norm.reshape(1, d)).reshape(b, s, d)
```

```python
import functools
import math

import jax
import jax.numpy as jnp
from jax import lax
from jax.experimental import pallas as pl
from jax.experimental.pallas import tpu as pltpu

F32 = jnp.float32
BF16 = jnp.bfloat16
I32 = jnp.int32
EPS = 1e-6
ROPE_THETA = 500000.0

D_MODEL = 1024
SSD_HEADS, SSD_HEAD_DIM, SSD_GROUPS, SSD_STATE, SSD_CONV = 4, 64, 2, 128, 4
SSD_D_INNER = SSD_HEADS * SSD_HEAD_DIM
SSD_CONV_DIM = SSD_D_INNER + 2 * SSD_GROUPS * SSD_STATE
DSA_HEADS, DSA_HEAD_DIM, IDX_HEADS, IDX_DIM, IDX_TOPK_MAX = 4, 64, 8, 64, 256
MLA_HEADS, MLA_Q_RANK, MLA_KV_RANK, MLA_NOPE, MLA_ROPE, MLA_V = 4, 256, 128, 64, 32, 64
FOX_HEADS, FOX_HEAD_DIM = 4, 64
MOE_GROUPS, MOE_EPG, MOE_EXPERTS, MOE_FF = 4, 8, 32, 256

LANES = 128
VMEM_LIMIT = 56 * 1024 * 1024
INT_MIN = -(2 ** 31)
NEG_BIG = -1e30

TM = 512
SSD_Q = 256
DSA_BQ = 256
DSA_SC = 512
FA_B = 512
MOE_N = 512
MOE_ALIGN = 16
MOE_CH = 64
MOE_EPS = 2
MOE_SUB = 4
MOE_NS = -(-(2 * MOE_N + MOE_EXPERTS * MOE_ALIGN + MOE_CH) // LANES) * LANES

C_ZX, C_GT, C_DQ, C_KV, C_IK, C_IQ, C_MC, C_FQ, C_FK, C_FV, C_END = (
    0, 1024, 1152, 1408, 1536, 1664, 2176, 2688, 3200, 3712, 4224)
G_FOX, G_DT, G_IW = 0, 4, 8
LOG2E = math.log2(math.e)
HEAD_W = 128
FA_HEADS = 4
VT_ROWS = 80
FOX_ONE_Q, FOX_F_Q, FOX_F_K, FOX_ONE_K = 64, 67, 64, 67


def _cparams(sem):
    return pltpu.CompilerParams(dimension_semantics=sem, vmem_limit_bytes=VMEM_LIMIT)


def _rms(xf, g):
    return xf * lax.rsqrt(jnp.mean(xf * xf, axis=-1, keepdims=True) + EPS) * g


def _softplus(x):
    return jnp.maximum(x, 0.0) + jnp.log1p(jnp.exp(-jnp.abs(x)))


def _silu(x):
    return x * jax.nn.sigmoid(x)


def _rope_apply(a, c, sa, sb, half):
    return a * c + pltpu.roll(a, LANES - half, 1) * sa + pltpu.roll(a, half, 1) * sb


def _dot_nt(a, b):
    return lax.dot_general(a, b, (((1,), (1,)), ((), ())), preferred_element_type=F32)


def _split3(x):
    hi = x.astype(BF16)
    r = x - hi.astype(F32)
    mid = r.astype(BF16)
    return hi, mid, (r - mid.astype(F32)).astype(BF16)


def _dot_f32_by_mask(mask, x, mask_left):
    m = jnp.where(mask, 1.0, 0.0).astype(BF16)
    out = None
    for piece in _split3(x):
        d = jnp.dot(m, piece, preferred_element_type=F32) if mask_left else jnp.dot(piece, m, preferred_element_type=F32)
        out = d if out is None else out + d
    return out


def _rope_tab_kernel(pos_ref, inv_ref, o_ref):
    pos = pos_ref[...].astype(F32)
    lane = lax.broadcasted_iota(I32, (1, LANES), 1)
    ang = pos * inv_ref[0:1, :]
    c, s = jnp.cos(ang), jnp.sin(ang)
    jd = lane % 64
    o_ref[:, 0:128] = c
    o_ref[:, 128:256] = jnp.where(jd < 8, -s, 0.0)
    o_ref[:, 256:384] = jnp.where((jd >= 8) & (jd < 16), s, 0.0)
    ang = pos * inv_ref[1:2, :]
    c, s = jnp.cos(ang), jnp.sin(ang)
    jm = lane - 64
    o_ref[:, 384:512] = c
    o_ref[:, 512:640] = jnp.where((jm >= 0) & (jm < 16), -s, 0.0)
    o_ref[:, 640:768] = jnp.where((jm >= 16) & (jm < 32), s, 0.0)


def _rope_tables(positions):
    t = positions.size
    half_d, half_m = DSA_HEAD_DIM // 4 // 2, MLA_ROPE // 2
    inv_d = ROPE_THETA ** (-jnp.arange(half_d, dtype=F32) / half_d)
    inv_m = ROPE_THETA ** (-jnp.arange(half_m, dtype=F32) / half_m)
    lane = jnp.arange(LANES)
    jd = lane % 64
    row_d = jnp.where(jd < 2 * half_d, inv_d[jd % half_d], 0.0)
    jm = lane - 64
    row_m = jnp.where((jm >= 0) & (jm < 2 * half_m), inv_m[jm % half_m], 0.0)
    inv = jnp.zeros((8, LANES), F32).at[0].set(row_d).at[1].set(row_m)
    tm = 1024
    return pl.pallas_call(
        _rope_tab_kernel,
        grid=(t // tm,),
        in_specs=[pl.BlockSpec((tm, 1), lambda i: (i, 0)), pl.BlockSpec((8, LANES), lambda i: (0, 0))],
        out_specs=pl.BlockSpec((tm, 768), lambda i: (i, 0)),
        out_shape=jax.ShapeDtypeStruct((t, 768), F32),
        compiler_params=_cparams(("parallel",)),
        name="rope_tables",
    )(positions.reshape(t, 1), inv)


def _inproj_kernel(*refs, has_add, tiles_per_seq):
    refs = list(refs)
    x_ref = refs.pop(0)
    yp_ref = refs.pop(0) if has_add else None
    g_ref, w_ref, tab_ref, fb_ref, qn_ref, kn_ref, wq_ref, wk_ref, wv_ref = refs[:9]
    refs = refs[9:]
    xo_ref = refs.pop(0) if has_add else None
    (zx_ref, gt_ref, dq_ref, dk_ref, dv_ref, ik_ref, iq_ref, mq_ref, mk_ref, mv_ref, fq_ref, fk_ref, fv_ref,
     carry_s) = refs
    x = x_ref[...]
    if has_add:
        x = x + yp_ref[...].astype(F32)
        xo_ref[...] = x
    h = _rms(x, g_ref[...]).astype(BF16)

    def proj(a, b):
        return jnp.dot(h, w_ref[:, a:b], preferred_element_type=F32)

    zx_ref[...] = proj(C_ZX, C_GT)
    gates = proj(C_GT, C_DQ)
    gt_ref[...] = gates
    c_d, sa_d, sb_d = tab_ref[:, 0:128], tab_ref[:, 128:256], tab_ref[:, 256:384]
    c_m, sa_m, sb_m = tab_ref[:, 384:512], tab_ref[:, 512:640], tab_ref[:, 640:768]
    q = proj(C_DQ, C_KV)
    for i in range(2):
        dq_ref[:, i * 128:(i + 1) * 128] = _rope_apply(q[:, i * 128:(i + 1) * 128], c_d, sa_d, sb_d, 8).astype(BF16)
    kv = proj(C_KV, C_IK)
    is_k = lax.broadcasted_iota(I32, (1, LANES), 1) < 64
    kvr = _rope_apply(kv, jnp.where(is_k, c_d, 1.0), jnp.where(is_k, sa_d, 0.0), jnp.where(is_k, sb_d, 0.0), 8)
    dk_ref[...] = kvr[:, 0:64].astype(BF16)
    dv_ref[0, 0:64, :] = kvr.astype(BF16).astype(F32).T[64:128, :].astype(BF16)
    dv_ref[0, 64:VT_ROWS, :] = jnp.ones((VT_ROWS - 64, kvr.shape[0]), BF16)
    ikb = proj(C_IK, C_IQ)
    ik_ref[...] = _rope_apply(ikb, c_d, sa_d, sb_d, 8)[:, 0:64].astype(BF16)
    iq = proj(C_IQ, C_MC)
    for i in range(4):
        iq_ref[:, i * 128:(i + 1) * 128] = _rope_apply(iq[:, i * 128:(i + 1) * 128], c_d, sa_d, sb_d, 8).astype(BF16)
    ones_hi = jnp.where(lax.broadcasted_iota(I32, (1, FA_HEADS * HEAD_W), 1) % HEAD_W >= 64, 1.0, 0.0)

    mc = proj(C_MC, C_FQ)
    cq = _rms(mc[:, 0:256], qn_ref[...]).astype(BF16)
    ckv = _rms(mc[:, 256:384], kn_ref[...]).astype(BF16)
    kr = _rope_apply(mc[:, 384:512], c_m, sa_m, sb_m, 16)
    mq = jnp.dot(cq, wq_ref[...], preferred_element_type=F32)
    mk = jnp.dot(ckv, wk_ref[...], preferred_element_type=F32)
    for hd in range(MLA_HEADS):
        sl = slice(hd * HEAD_W, (hd + 1) * HEAD_W)
        mq_ref[:, sl] = _rope_apply(mq[:, sl], c_m, sa_m, sb_m, 16).astype(BF16)
        mk_ref[:, sl] = (mk[:, sl] + kr).astype(BF16)
    mv_ref[...] = (jnp.dot(ckv, wv_ref[...], preferred_element_type=F32) + ones_hi).astype(BF16)

    @pl.when(pl.program_id(0) % tiles_per_seq == 0)
    def _():
        carry_s[...] = jnp.zeros_like(carry_s)

    tm = x.shape[0]
    log_f = -_softplus(-(gates + fb_ref[...]))
    ri = lax.broadcasted_iota(I32, (tm, tm), 0)
    ci = lax.broadcasted_iota(I32, (tm, tm), 1)
    cs = _dot_f32_by_mask(ri >= ci, log_f, True) + carry_s[0:1, :]
    carry_s[0:1, :] = cs[tm - 1:tm, :]
    f2 = cs * LOG2E
    lane = lax.broadcasted_iota(I32, (1, HEAD_W), 1)

    def pieces(f, base):
        hi, mid, lo = _split3(f)
        return jnp.where(lane == base, hi.astype(F32),
                         jnp.where(lane == base + 1, mid.astype(F32), jnp.where(lane == base + 2, lo.astype(F32), 0.0)))

    def ones(base):
        return jnp.where((lane >= base) & (lane < base + 3), 1.0, 0.0)

    fq = proj(C_FQ, C_FK)
    fk = proj(C_FK, C_FV)
    for hd in range(FOX_HEADS):
        fh = f2[:, G_FOX + hd:G_FOX + hd + 1]
        sl = slice(hd * HEAD_W, (hd + 1) * HEAD_W)
        fq_ref[:, sl] = (fq[:, sl] + pieces(fh, FOX_F_Q) + ones(FOX_ONE_Q)).astype(BF16)
        fk_ref[:, sl] = (fk[:, sl] - pieces(fh, FOX_F_K) + ones(FOX_ONE_K)).astype(BF16)
    fv_ref[...] = (proj(C_FV, C_END) + ones_hi).astype(BF16)


def _pack_w_in(w):
    d = w.shape[0]
    z = lambda n: jnp.zeros((d, n), F32)
    o = 0
    parts = {}
    for name, width in (("z", 256), ("xbc", 768), ("dt", 4), ("dq", 256), ("dk", 64), ("dv", 64), ("iq", 512),
                        ("ik", 64), ("iw", 8), ("cq", 256), ("ckv", 128), ("kr", 32), ("fq", 256), ("fk", 256),
                        ("fv", 256), ("ff", 4)):
        parts[name] = w[:, o:o + width]
        o += width
    scale = DSA_HEAD_DIM ** -0.5

    def per_head(a):
        return jnp.pad(a.reshape(d, FA_HEADS, 64), ((0, 0), (0, 0), (0, HEAD_W - 64))).reshape(d, FA_HEADS * HEAD_W)

    cat = jnp.concatenate([
        parts["z"], parts["xbc"],
        parts["ff"], parts["dt"], parts["iw"], z(112),
        parts["dq"] * (scale * LOG2E),
        parts["dk"], parts["dv"],
        parts["ik"], z(64),
        parts["iq"],
        parts["cq"], parts["ckv"], z(64), parts["kr"], z(32),
        per_head(parts["fq"] * (FOX_HEAD_DIM ** -0.5 * LOG2E)), per_head(parts["fk"]), per_head(parts["fv"])], axis=1)
    return cat.astype(BF16)


def _mla_weights(w_uq, w_ukv):
    dqk = MLA_NOPE + MLA_ROPE
    wq = jnp.pad(w_uq.reshape(MLA_Q_RANK, MLA_HEADS, dqk) * (dqk ** -0.5 * LOG2E), ((0, 0), (0, 0), (0, HEAD_W - dqk)))
    wkv = w_ukv.reshape(MLA_KV_RANK, MLA_HEADS, MLA_NOPE + MLA_V)
    wk = jnp.pad(wkv[:, :, :MLA_NOPE], ((0, 0), (0, 0), (0, HEAD_W - MLA_NOPE)))
    wv = jnp.pad(wkv[:, :, MLA_NOPE:], ((0, 0), (0, 0), (0, HEAD_W - MLA_V)))
    flat = lambda a: a.reshape(a.shape[0], MLA_HEADS * HEAD_W).astype(BF16)
    return flat(wq), flat(wk), flat(wv)


def _inproj(x, y_prev, g, w_cat, tab, f_bias, q_norm, kv_norm, w_uq, w_ukv, s):
    t = x.shape[0]
    has_add = y_prev is not None
    tok = lambda n: pl.BlockSpec((TM, n), lambda i: (i, 0))
    full = lambda a: pl.BlockSpec(a.shape, lambda i: (0,) * a.ndim)
    fb = jnp.zeros((1, LANES), F32).at[0, G_FOX:G_FOX + FOX_HEADS].set(f_bias)
    params = [g, w_cat, tab, fb, q_norm.reshape(1, -1), kv_norm.reshape(1, -1), *_mla_weights(w_uq, w_ukv)]
    ins = [x] + ([y_prev] if has_add else []) + params
    in_specs = ([tok(D_MODEL)] + ([tok(D_MODEL)] if has_add else [])
                + [tok(768) if a is tab else full(a) for a in params])
    outs = ([("x", D_MODEL, F32)] if has_add else []) + [
        ("zx", 1024, F32), ("gt", 128, F32), ("dq", 256, BF16), ("dk", 64, BF16), ("dv", 64, BF16),
        ("ik", 64, BF16), ("iq", 512, BF16), ("mq", 512, BF16), ("mk", 512, BF16), ("mv", 512, BF16),
        ("fq", 512, BF16), ("fk", 512, BF16), ("fv", 512, BF16)]
    vt_rows = {"dv": VT_ROWS}
    res = pl.pallas_call(
        functools.partial(_inproj_kernel, has_add=has_add, tiles_per_seq=s // TM),
        grid=(t // TM,),
        in_specs=in_specs,
        out_specs=[pl.BlockSpec((1, vt_rows[nm], TM), lambda i: (i, 0, 0)) if nm in vt_rows else tok(n)
                   for nm, n, _ in outs],
        out_shape=[jax.ShapeDtypeStruct((t // TM, vt_rows[nm], TM) if nm in vt_rows else (t, n), dt)
                   for nm, n, dt in outs],
        scratch_shapes=[pltpu.VMEM((8, LANES), F32)],
        compiler_params=_cparams(("arbitrary",)),
        name="inproj",
    )(*ins)
    res = list(res)
    x_new = res.pop(0) if has_add else x
    return x_new, dict(zip([n for n, _, _ in outs if n != "x"], res))


def _ssd_kernel(zx_ref, gt_ref, cw_ref, cb_ref, pr_ref, pc_ref, drow_ref, nw_ref, o_ref, ext_s, st_s, y_s, *, q):
    @pl.when(pl.program_id(1) == 0)
    def _():
        ext_s[0:8, :] = jnp.zeros((8, SSD_CONV_DIM), F32)
        st_s[...] = jnp.zeros_like(st_s)

    raw = zx_ref[:, 256:1024]
    ext_s[8:8 + q, :] = raw
    acc = jnp.broadcast_to(cb_ref[...], (q, SSD_CONV_DIM))
    for j in range(SSD_CONV):
        acc = acc + cw_ref[j:j + 1, :] * ext_s[5 + j:5 + j + q, :]
    ext_s[0:8, :] = raw[q - 8:q, :]
    xbc = _silu(acc)
    xs = xbc[:, 0:SSD_D_INNER]

    g = gt_ref[...]
    lane = lax.broadcasted_iota(I32, (1, LANES), 1)
    dtc = _softplus(g + pr_ref[0:1, :])
    a_r = jnp.where((lane >= G_DT) & (lane < G_DT + SSD_HEADS), -jnp.exp(pr_ref[1:2, :]), 0.0)
    ri = lax.broadcasted_iota(I32, (q, q), 0)
    ci = lax.broadcasted_iota(I32, (q, q), 1)
    tri = ri >= ci
    acs_c = _dot_f32_by_mask(tri, dtc * a_r, True)
    sub = lax.broadcasted_iota(I32, (LANES, 1), 0)
    dtr = _softplus(g.T + pc_ref[:, 0:1])
    a_c = jnp.where((sub >= G_DT) & (sub < G_DT + SSD_HEADS), -jnp.exp(pc_ref[:, 1:2]), 0.0)
    acs_r = _dot_f32_by_mask(ri <= ci, dtr * a_c, False)

    rep = SSD_HEADS // SSD_GROUPS
    for gi in range(SSD_GROUPS):
        bg = xbc[:, SSD_D_INNER + gi * SSD_STATE:SSD_D_INNER + (gi + 1) * SSD_STATE]
        cg = xbc[:, SSD_D_INNER + (SSD_GROUPS + gi) * SSD_STATE:SSD_D_INNER + (SSD_GROUPS + gi + 1) * SSD_STATE]
        bt = bg.T.astype(BF16)
        cb16 = cg.astype(BF16)
        cbm = jnp.dot(cb16, bt, preferred_element_type=F32)
        for hh in range(rep):
            h = gi * rep + hh
            ac = acs_c[:, G_DT + h:G_DT + h + 1]
            ar = acs_r[G_DT + h:G_DT + h + 1, :]
            seg = jnp.where(tri, jnp.exp(ac - ar), 0.0)
            xh = xs[:, h * SSD_HEAD_DIM:(h + 1) * SSD_HEAD_DIM]
            xdt = xh * dtc[:, G_DT + h:G_DT + h + 1]
            yd = jnp.dot((cbm * seg).astype(BF16), xdt.astype(BF16), preferred_element_type=F32)
            aend = ac[q - 1:q, :]
            st = st_s[h]
            yo = jnp.dot(cb16, st.astype(BF16), preferred_element_type=F32) * jnp.exp(ac)
            st_s[h] = st * jnp.exp(aend) + jnp.dot(bt, (xdt * jnp.exp(aend - ac)).astype(BF16),
                                                   preferred_element_type=F32)
            y_s[:, h * SSD_HEAD_DIM:(h + 1) * SSD_HEAD_DIM] = yd + yo
    y = y_s[...] + drow_ref[...] * xs
    y = y * _silu(zx_ref[:, 0:SSD_D_INNER])
    o_ref[...] = _rms(y, nw_ref[...]).astype(BF16)


def _ssd(zx, gt, conv_w, conv_b, dt_bias, a_log, d_skip, norm_w, b, s):
    q = SSD_Q
    nc = s // q
    pr = jnp.zeros((8, LANES), F32).at[0, G_DT:G_DT + SSD_HEADS].set(dt_bias).at[1, G_DT:G_DT + SSD_HEADS].set(a_log)
    pc = pr.T
    drow = jnp.repeat(d_skip, SSD_HEAD_DIM).reshape(1, SSD_D_INNER)
    cb = conv_b.reshape(1, -1)
    nw = norm_w.reshape(1, -1)
    full = lambda a: pl.BlockSpec(a.shape, lambda i, j: (0,) * a.ndim)
    return pl.pallas_call(
        functools.partial(_ssd_kernel, q=q),
        grid=(b, nc),
        in_specs=[pl.BlockSpec((q, 1024), lambda i, j: (i * nc + j, 0)), pl.BlockSpec((q, LANES), lambda i, j: (i * nc + j, 0)),
                  full(conv_w), full(cb), full(pr), full(pc), full(drow), full(nw)],
        out_specs=pl.BlockSpec((q, SSD_D_INNER), lambda i, j: (i * nc + j, 0)),
        out_shape=jax.ShapeDtypeStruct((b * s, SSD_D_INNER), BF16),
        scratch_shapes=[pltpu.VMEM((8 + q, SSD_CONV_DIM), F32), pltpu.VMEM((SSD_HEADS, SSD_STATE, SSD_HEAD_DIM), F32),
                        pltpu.VMEM((q, SSD_D_INNER), F32)],
        compiler_params=_cparams(("parallel", "arbitrary")),
        name="ssd",
    )(zx, gt, conv_w, cb, pr, pc, drow, nw)


def _key_to_f32(k):
    return pltpu.bitcast(jnp.where(k < 0, k ^ jnp.int32(0x7FFFFFFF), k), F32)


def _dsa_kernel(iq_ref, gt_ref, q_ref, ik_ref, k_ref, vt_ref, o_ref, sc_s, qi_s, qs_s, thr_s, *, bq, sc_w, n_sel):
    qb = pl.program_id(1)
    n_sc = ((qb + 1) * bq + sc_w - 1) // sc_w
    for j in range(IDX_HEADS):
        qi_s[j // 2, (j % 2) * bq:(j % 2 + 1) * bq, :] = iq_ref[:, j * IDX_DIM:(j + 1) * IDX_DIM]
    for h in range(DSA_HEADS):
        qs_s[h * bq:(h + 1) * bq, :] = q_ref[:, h * DSA_HEAD_DIM:(h + 1) * DSA_HEAD_DIM]
    w_t = (gt_ref[...] * ((IDX_HEADS * IDX_DIM) ** -0.5)).T
    qpos = qb * bq + lax.broadcasted_iota(I32, (1, bq), 1)

    def score_body(sc, carry):
        k0 = pl.multiple_of(sc * sc_w, sc_w)
        kidx = ik_ref[pl.ds(k0, sc_w), :]
        sco = jnp.zeros((sc_w, bq), F32)
        for jp in range(IDX_HEADS // 2):
            a = _dot_nt(kidx, qi_s[jp])
            for u in range(2):
                j = 2 * jp + u
                sco = sco + jnp.maximum(a[:, u * bq:(u + 1) * bq], 0.0) * w_t[G_IW + j:G_IW + j + 1, :]
        kpos = k0 + lax.broadcasted_iota(I32, (sc_w, 1), 0)
        sc_s[sc] = jnp.where(kpos <= qpos, sco, -jnp.inf)
        return carry

    lax.fori_loop(0, n_sc, score_body, 0)

    def count(pred):
        def body(sc, acc):
            for r in range(sc_w // 64):
                acc = acc + jnp.where(pred(sc_s[sc, r * 64:(r + 1) * 64, :]), 1.0, 0.0)
            return acc
        acc = lax.fori_loop(0, n_sc, body, jnp.zeros((64, bq), F32))
        return jnp.sum(acc, axis=0, keepdims=True)

    thr_s[0:1, :] = jnp.full((1, bq), -jnp.finfo(jnp.float32).max, F32)

    @pl.when((qb + 1) * bq > n_sel)
    def _():
        active = qpos + 1 > n_sel

        def cond(st):
            it, lo, cnt_lo = st
            pending = active & (cnt_lo != n_sel)
            return (it < 32) & (jnp.max(jnp.where(pending, 1.0, 0.0)) > 0.0)

        def body(st):
            it, lo, cnt_lo = st
            for u in range(4):
                cand = lo ^ jnp.left_shift(jnp.int32(1), 31 - (it + u))
                thr_c = _key_to_f32(cand)
                cnt = count(lambda t: t >= thr_c)
                ok = cnt >= n_sel
                lo, cnt_lo = jnp.where(ok, cand, lo), jnp.where(ok, cnt, cnt_lo)
            return it + 4, lo, cnt_lo

        _, lo, cnt_lo = lax.while_loop(
            cond, body, (jnp.int32(0), jnp.full((1, bq), INT_MIN, I32), jnp.full((1, bq), 3.0e38, F32)))
        thr = _key_to_f32(lo)
        tie = active & (cnt_lo > n_sel)

        @pl.when(jnp.max(jnp.where(tie, 1.0, 0.0)) > 0.0)
        def _():
            need = n_sel - count(lambda t: t > thr)
            ri = lax.broadcasted_iota(I32, (sc_w, sc_w), 0)
            ci = lax.broadcasted_iota(I32, (sc_w, sc_w), 1)
            lower = jnp.where(ri >= ci, 1.0, 0.0).astype(BF16)

            def tie_body(sc, run):
                t = sc_s[sc]
                eq = (t == thr) & tie
                pre = jnp.dot(lower, jnp.where(eq, 1.0, 0.0).astype(BF16), preferred_element_type=F32)
                sc_s[sc] = jnp.where(eq & (run + pre > need), -jnp.inf, t)
                return run + pre[sc_w - 1:sc_w, :]

            lax.fori_loop(0, n_sc, tie_body, jnp.zeros((1, bq), F32))

        thr_s[0:1, :] = jnp.where(active, thr, thr_s[0:1, :])

    thr = thr_s[0:1, :]

    hq = DSA_HEADS * bq

    def att_body(sc, carry):
        m_prev, acc = carry
        k0 = pl.multiple_of(sc * sc_w, sc_w)
        st = _dot_nt(k_ref[pl.ds(k0, sc_w), :], qs_s[...])
        drop = jnp.where(sc_s[sc] >= thr, 0.0, 2 * NEG_BIG)
        st = st + jnp.tile(drop, (1, DSA_HEADS))
        m_cur = jnp.max(jnp.max(st.reshape(sc_w // 64, 64, hq), axis=0), axis=0, keepdims=True)
        m_next = jnp.maximum(m_prev, m_cur)
        p = jnp.exp2(st - m_next)
        pv = jnp.dot(vt_ref[sc], p.astype(BF16), preferred_element_type=F32)
        return m_next, jnp.exp2(m_prev - m_next) * acc + pv

    def att_body2(i, carry):
        return att_body(2 * i + 1, att_body(2 * i, carry))

    carry = lax.fori_loop(0, n_sc // 2, att_body2, (jnp.full((1, hq), NEG_BIG, F32), jnp.zeros((VT_ROWS, hq), F32)))
    _, acc = lax.cond(n_sc % 2 == 1, lambda c: att_body(n_sc - 1, c), lambda c: c, carry)
    out_t = jnp.concatenate([acc[0:DSA_HEAD_DIM, :] / acc[DSA_HEAD_DIM:DSA_HEAD_DIM + 1, :],
                             jnp.zeros((LANES - DSA_HEAD_DIM, hq), F32)], axis=0).T
    for h in range(DSA_HEADS):
        o_ref[:, h * DSA_HEAD_DIM:(h + 1) * DSA_HEAD_DIM] = out_t[h * bq:(h + 1) * bq, 0:DSA_HEAD_DIM].astype(BF16)


def _dsa(iq, gt, dq, ik, dk, dvt, b, s, n_sel):
    bq, sc_w = DSA_BQ, DSA_SC
    nq = s // bq
    qblk = lambda n: pl.BlockSpec((bq, n), lambda i, j: (i * nq + j, 0))
    kblk = pl.BlockSpec((s, 64), lambda i, j: (i, 0))
    return pl.pallas_call(
        functools.partial(_dsa_kernel, bq=bq, sc_w=sc_w, n_sel=n_sel),
        grid=(b, nq),
        in_specs=[qblk(512), qblk(LANES), qblk(256), kblk, kblk,
                  pl.BlockSpec((s // sc_w, VT_ROWS, sc_w), lambda i, j: (i, 0, 0))],
        out_specs=qblk(256),
        out_shape=jax.ShapeDtypeStruct((b * s, 256), BF16),
        scratch_shapes=[pltpu.VMEM((s // sc_w, sc_w, bq), F32), pltpu.VMEM((IDX_HEADS // 2, 2 * bq, IDX_DIM), BF16),
                        pltpu.VMEM((DSA_HEADS * bq, DSA_HEAD_DIM), BF16), pltpu.VMEM((8, bq), F32)],
        compiler_params=_cparams(("parallel", "arbitrary")),
        name="dsa",
    )(iq, gt, dq, ik, dk, dvt)


def _flash_kernel(qt_ref, kt_ref, q_ref, k_ref, v_ref, o_ref, m_s, acc_s, *, blk):
    t = pl.program_id(1)
    qi, ki = qt_ref[t], kt_ref[t]

    @pl.when(ki == 0)
    def _():
        m_s[...] = jnp.full(m_s.shape, NEG_BIG, F32)
        acc_s[...] = jnp.zeros_like(acc_s)

    def step(diagonal):
        if diagonal:
            keep = lax.broadcasted_iota(I32, (blk, blk), 0) >= lax.broadcasted_iota(I32, (blk, blk), 1)
        for h in range(FA_HEADS):
            sl = slice(h * HEAD_W, (h + 1) * HEAD_W)
            s = _dot_nt(q_ref[:, sl], k_ref[:, sl])
            if diagonal:
                s = jnp.where(keep, s, 2 * NEG_BIG)
            m_prev = m_s[h]
            m_next = jnp.maximum(m_prev, jnp.max(s, axis=1, keepdims=True))
            p = jnp.exp2(s - jnp.tile(m_next, (1, blk // LANES)))
            pv = jnp.dot(p.astype(BF16), v_ref[:, sl], preferred_element_type=F32)
            acc_s[h] = jnp.exp2(m_prev - m_next) * acc_s[h] + pv
            m_s[h] = m_next

    @pl.when(ki < qi)
    def _():
        step(False)

    @pl.when(ki == qi)
    def _():
        step(True)
        for h in range(FA_HEADS):
            acc = acc_s[h]
            o_ref[:, h * 64:(h + 1) * 64] = (acc / pltpu.roll(acc, 64, 1))[:, 0:64].astype(BF16)


def _flash(q, k, v, b, s, name):
    blk = FA_B
    nb = s // blk
    pairs = [(i, j) for i in range(nb) for j in range(i + 1)]
    qt = jnp.asarray([p[0] for p in pairs], I32)
    kt = jnp.asarray([p[1] for p in pairs], I32)
    w = FA_HEADS * HEAD_W
    grid_spec = pltpu.PrefetchScalarGridSpec(
        num_scalar_prefetch=2,
        grid=(b, len(pairs)),
        in_specs=[pl.BlockSpec((blk, w), lambda i, t, qt, kt: (i * nb + qt[t], 0)),
                  pl.BlockSpec((blk, w), lambda i, t, qt, kt: (i * nb + kt[t], 0)),
                  pl.BlockSpec((blk, w), lambda i, t, qt, kt: (i * nb + kt[t], 0))],
        out_specs=pl.BlockSpec((blk, FA_HEADS * 64), lambda i, t, qt, kt: (i * nb + qt[t], 0)),
        scratch_shapes=[pltpu.VMEM((FA_HEADS, blk, LANES), F32), pltpu.VMEM((FA_HEADS, blk, HEAD_W), F32)],
    )
    return pl.pallas_call(
        functools.partial(_flash_kernel, blk=blk),
        grid_spec=grid_spec,
        out_shape=jax.ShapeDtypeStruct((b * s, FA_HEADS * 64), BF16),
        compiler_params=_cparams(("parallel", "arbitrary")),
        name=name,
    )(qt, kt, q, k, v)


def _outproj_kernel(x_ref, y0_ref, y1_ref, y2_ref, y3_ref, wo_ref, g_ref, wr_ref, br_ref,
                    xo_ref, h_ref, cw_ref, rs_ref):
    acc = x_ref[...]
    for i, y_ref in enumerate((y0_ref, y1_ref, y2_ref, y3_ref)):
        acc = acc + jnp.dot(y_ref[...], wo_ref[i * 256:(i + 1) * 256, :], preferred_element_type=F32)
    xo_ref[...] = acc
    h2 = _rms(acc, g_ref[...])
    h_ref[...] = h2.astype(BF16)
    h_hi, h_lo, _ = _split3(h2)
    lg = (jnp.dot(h_hi, wr_ref[0], preferred_element_type=F32) + jnp.dot(h_lo, wr_ref[0], preferred_element_type=F32)
          + jnp.dot(h_hi, wr_ref[1], preferred_element_type=F32) + br_ref[...])
    lane = lax.broadcasted_iota(I32, (1, LANES), 1)
    lanef = lane.astype(F32)
    is_g = (lane >= MOE_EXPERTS) & (lane < MOE_EXPERTS + MOE_GROUPS)
    gl = jnp.where(is_g, lg, -jnp.inf)
    gmax = jnp.max(gl, axis=1, keepdims=True)
    gidx = jnp.min(jnp.where(gl == gmax, lanef, 999.0), axis=1, keepdims=True) - MOE_EXPERTS
    g_p = 1.0 / jnp.sum(jnp.where(is_g, jnp.exp(gl - gmax), 0.0), axis=1, keepdims=True)
    in_g = (lane < MOE_EXPERTS) & (jnp.floor(lanef * (1.0 / MOE_EPG)) == gidx)
    el = jnp.where(in_g, lg, -jnp.inf)
    m1 = jnp.max(el, axis=1, keepdims=True)
    i1 = jnp.min(jnp.where(el == m1, lanef, 999.0), axis=1, keepdims=True)
    el2 = jnp.where(lanef == i1, -jnp.inf, el)
    m2 = jnp.max(el2, axis=1, keepdims=True)
    i2 = jnp.min(jnp.where(el2 == m2, lanef, 999.0), axis=1, keepdims=True)
    t = jnp.exp(m2 - m1)
    w1 = 1.0 / (1.0 + t)
    cw_ref[...] = jnp.where(lanef == i1, w1 * g_p, jnp.where(lanef == i2, t * w1 * g_p, 0.0))
    rs_ref[...] = jnp.where((lanef == i1) | (lanef == i2), 1.0, 0.0).astype(BF16)


def _outproj(x, ys, w_out, g, w_rg, b_rg, w_re, b_re):
    t = x.shape[0]
    wo = w_out.astype(BF16)
    wr = jnp.zeros((D_MODEL, LANES), F32).at[:, 0:MOE_EXPERTS].set(w_re).at[:, MOE_EXPERTS:MOE_EXPERTS + MOE_GROUPS].set(w_rg)
    wr_hi = wr.astype(BF16)
    wr = jnp.stack([wr_hi, (wr - wr_hi.astype(F32)).astype(BF16)])
    br =jnp.zeros((1, LANES), F32).at[0, 0:MOE_EXPERTS].set(b_re).at[0, MOE_EXPERTS:MOE_EXPERTS + MOE_GROUPS].set(b_rg)
    tok = lambda n: pl.BlockSpec((TM, n), lambda i: (i, 0))
    full = lambda a: pl.BlockSpec(a.shape, lambda i: (0,) * a.ndim)
    return pl.pallas_call(
        _outproj_kernel,
        grid=(t // TM,),
        in_specs=[tok(D_MODEL)] + [tok(256)] * 4 + [full(wo), full(g), full(wr), full(br)],
        out_specs=[tok(D_MODEL), tok(D_MODEL), tok(LANES), tok(LANES)],
        out_shape=[jax.ShapeDtypeStruct((t, D_MODEL), F32), jax.ShapeDtypeStruct((t, D_MODEL), BF16),
                   jax.ShapeDtypeStruct((t, LANES), F32), jax.ShapeDtypeStruct((t, LANES), BF16)],
        compiler_params=_cparams(("parallel",)),
        name="outproj_router",
    )(x, *ys, wo, g, wr, br)


def _moe_plan_kernel(rs_ref, cw_ref, meta_ref, col_ref, row_ref, *, n):
    sel = rs_ref[...]
    self32 = sel.astype(F32)
    ti = lax.broadcasted_iota(I32, (n, n), 0)
    tj = lax.broadcasted_iota(I32, (n, n), 1)
    rank = jnp.dot(jnp.where(ti > tj, 1.0, 0.0).astype(BF16), sel, preferred_element_type=F32)
    cnt = jnp.sum(self32, axis=0, keepdims=True).astype(I32)
    cpad = ((cnt + (MOE_ALIGN - 1)) // MOE_ALIGN) * MOE_ALIGN
    li = lax.broadcasted_iota(I32, (LANES, LANES), 0)
    lj = lax.broadcasted_iota(I32, (LANES, LANES), 1)
    off = _dot_f32_by_mask(li < lj, jnp.broadcast_to(cpad.astype(F32), (8, LANES)), False)[0:1, :]
    dest = off + rank
    lane = lax.broadcasted_iota(I32, (1, LANES), 1)
    lanef = lane.astype(F32)
    on = self32 > 0.5
    e_lo = jnp.min(jnp.where(on, lanef, 999.0), axis=1, keepdims=True)
    e_hi = jnp.max(jnp.where(on, lanef, -1.0), axis=1, keepdims=True)
    is_lo, is_hi = lanef == e_lo, lanef == e_hi
    cw = cw_ref[...]
    pick = lambda m, a: jnp.sum(jnp.where(m, a, 0.0), axis=1, keepdims=True)
    d_lo, d_hi, w_lo, w_hi = pick(is_lo, dest), pick(is_hi, dest), pick(is_lo, cw), pick(is_hi, cw)
    colv = jnp.where(lane == 0, d_lo, jnp.where(lane == 1, d_hi, jnp.where(lane == 2, w_lo, jnp.where(lane == 3, w_hi, 0.0))))
    col_ref[...] = colv
    row_ref[0] = colv.T[0:8, :]
    sub = lax.broadcasted_iota(I32, (8, LANES), 0)
    meta_ref[0] = jnp.where(sub == 0, off.astype(I32), jnp.where(sub == 1, cnt, 0))


def _moe_kernel(meta_ref, h_ref, col_ref, row_ref, wg_ref, wu_ref, wd_ref, o_ref, s_s, y_s, *, n, ns):
    t, e = pl.program_id(0), pl.program_id(1)

    @pl.when(e == 0)
    def _():
        for g in range(MOE_SUB):
            d_lo, d_hi = row_ref[g, 0:1, :], row_ref[g, 1:2, :]
            hg = h_ref[g * n:(g + 1) * n, :]
            for r in range(ns // LANES):
                si = (r * LANES + lax.broadcasted_iota(I32, (LANES, 1), 0)).astype(F32)
                perm = jnp.where((si == d_lo) | (si == d_hi), 1.0, 0.0).astype(BF16)
                s_s[g, r * LANES:(r + 1) * LANES, :] = jnp.dot(perm, hg, preferred_element_type=F32).astype(BF16)
        y_s[...] = jnp.zeros_like(y_s)

    def mlp(j, starts, store_ok):
        starts = [pl.multiple_of(r0, MOE_ALIGN) for r0 in starts]
        xs = jnp.concatenate([s_s[g, pl.ds(starts[g], MOE_CH), :] for g in range(MOE_SUB)], axis=0)
        gate = jnp.dot(xs, wg_ref[j], preferred_element_type=F32)
        up = jnp.dot(xs, wu_ref[j], preferred_element_type=F32)
        hid = (_silu(gate) * up).astype(BF16)
        y = jnp.dot(hid, wd_ref[j], preferred_element_type=F32).astype(BF16)
        for g in range(MOE_SUB):
            def store(g=g):
                y_s[g, pl.ds(starts[g], MOE_CH), :] = y[g * MOE_CH:(g + 1) * MOE_CH, :]
            if store_ok is None:
                store()
            else:
                pl.when(store_ok[g])(store)

    def meta(g, j, what):
        return meta_ref[(t * MOE_SUB + g) * 2 * MOE_EXPERTS + what * MOE_EXPERTS + e * MOE_EPS + j]

    offs = [[meta(g, j, 0) for g in range(MOE_SUB)] for j in range(MOE_EPS)]
    cnts = [[meta(g, j, 1) for g in range(MOE_SUB)] for j in range(MOE_EPS)]
    for j in range(MOE_EPS):
        mlp(j, offs[j], None)
    for j in range(MOE_EPS):
        ends = [offs[j][g] + ((cnts[j][g] + MOE_ALIGN - 1) // MOE_ALIGN) * MOE_ALIGN for g in range(MOE_SUB)]
        nch = [(cnts[j][g] + MOE_CH - 1) // MOE_CH for g in range(MOE_SUB)]

        def body(i, carry, j=j, ends=ends, nch=nch):
            more = [i < nch[g] for g in range(MOE_SUB)]
            starts = [jnp.where(more[g], jnp.minimum(offs[j][g] + i * MOE_CH, ends[g] - MOE_CH), offs[j][g])
                      for g in range(MOE_SUB)]
            mlp(j, starts, more)
            return carry

        lax.fori_loop(1, functools.reduce(jnp.maximum, nch), body, 0)

    @pl.when(e == pl.num_programs(1) - 1)
    def _():
        rows = 256
        kc = ns // next(k for k in (3, 2, 1) if ns % (k * LANES) == 0)
        for g in range(MOE_SUB):
            for c in range(n // rows):
                r0 = g * n + c * rows
                cv = col_ref[r0:r0 + rows, :]
                d_lo, d_hi, w_lo, w_hi = cv[:, 0:1], cv[:, 1:2], cv[:, 2:3], cv[:, 3:4]
                acc = jnp.zeros((rows, D_MODEL), F32)
                for r in range(ns // kc):
                    si = (r * kc + lax.broadcasted_iota(I32, (1, kc), 1)).astype(F32)
                    pw = (jnp.where(si == d_lo, w_lo, 0.0) + jnp.where(si == d_hi, w_hi, 0.0)).astype(BF16)
                    acc = acc + jnp.dot(pw, y_s[g, r * kc:(r + 1) * kc, :], preferred_element_type=F32)
                o_ref[r0:r0 + rows, :] = acc.astype(BF16)


def _moe(h2, cw, rs, w_gate, w_up, w_down):
    t = h2.shape[0]
    n, ns = MOE_N, MOE_NS
    nt = t // n
    meta, col, row = pl.pallas_call(
        functools.partial(_moe_plan_kernel, n=n),
        grid=(nt,),
        in_specs=[pl.BlockSpec((n, LANES), lambda i: (i, 0)), pl.BlockSpec((n, LANES), lambda i: (i, 0))],
        out_specs=[pl.BlockSpec((1, 8, LANES), lambda i: (i, 0, 0)), pl.BlockSpec((n, LANES), lambda i: (i, 0)),
                   pl.BlockSpec((1, 8, n), lambda i: (i, 0, 0))],
        out_shape=[jax.ShapeDtypeStruct((nt, 8, LANES), I32), jax.ShapeDtypeStruct((t, LANES), F32),
                   jax.ShapeDtypeStruct((nt, 8, n), F32)],
        compiler_params=_cparams(("parallel",)),
        name="moe_plan",
    )(rs, cw)
    meta_flat = meta[:, 0:2, 0:MOE_EXPERTS].reshape(-1)
    wg, wu, wd = w_gate.astype(BF16), w_up.astype(BF16), w_down.astype(BF16)
    grid_spec = pltpu.PrefetchScalarGridSpec(
        num_scalar_prefetch=1,
        grid=(nt // MOE_SUB, MOE_EXPERTS // MOE_EPS),
        in_specs=[pl.BlockSpec((MOE_SUB * n, D_MODEL), lambda i, e, m: (i, 0)),
                  pl.BlockSpec((MOE_SUB * n, LANES), lambda i, e, m: (i, 0)),
                  pl.BlockSpec((MOE_SUB, 8, n), lambda i, e, m: (i, 0, 0)),
                  pl.BlockSpec((MOE_EPS, D_MODEL, MOE_FF), lambda i, e, m: (e, 0, 0)),
                  pl.BlockSpec((MOE_EPS, D_MODEL, MOE_FF), lambda i, e, m: (e, 0, 0)),
                  pl.BlockSpec((MOE_EPS, MOE_FF, D_MODEL), lambda i, e, m: (e, 0, 0))],
        out_specs=pl.BlockSpec((MOE_SUB * n, D_MODEL), lambda i, e, m: (i, 0)),
        scratch_shapes=[pltpu.VMEM((MOE_SUB, ns, D_MODEL), BF16), pltpu.VMEM((MOE_SUB, ns, D_MODEL), BF16)],
    )
    return pl.pallas_call(
        functools.partial(_moe_kernel, n=n, ns=ns),
        grid_spec=grid_spec,
        out_shape=jax.ShapeDtypeStruct((t, D_MODEL), BF16),
        compiler_params=_cparams(("parallel", "arbitrary")),
        name="moe_experts",
    )(meta_flat, h2, col, row, wg, wu, wd)


def _final_kernel(x_ref, y_ref, g_ref, o_ref):
    o_ref[...] = _rms(x_ref[...] + y_ref[...].astype(F32), g_ref[...])


def _final(x, y, g):
    t = x.shape[0]
    tok = pl.BlockSpec((TM, D_MODEL), lambda i: (i, 0))
    return pl.pallas_call(
        _final_kernel,
        grid=(t // TM,),
        in_specs=[tok, tok, pl.BlockSpec((1, D_MODEL), lambda i: (0, 0))],
        out_specs=tok,
        out_shape=jax.ShapeDtypeStruct((t, D_MODEL), F32),
        compiler_params=_cparams(("parallel",)),
        name="final_norm",
    )(x, y, g)


def kernel(x, positions, norm_mix, w_in, ssd_conv_w, ssd_conv_b, ssd_dt_bias, ssd_a_log, ssd_d, ssd_norm, mla_q_norm, mla_w_uq, mla_kv_norm, mla_w_ukv, fox_f_bias, w_out, norm_ffn, router_group_w, router_group_b, router_expert_w, router_expert_b, expert_w_gate, expert_w_up, expert_w_down, final_norm):
    b, s, d = x.shape
    assert d == D_MODEL and s % FA_B == 0 and (b * s) % (MOE_N * MOE_SUB) == 0 and TM == DSA_SC
    depth = w_in.shape[0]
    n_sel = min(IDX_TOPK_MAX, s // 4)
    tab = _rope_tables(positions)
    xf = x.reshape(b * s, d)
    y_ffn = None
    for l in range(depth):
        xf, p = _inproj(xf, y_ffn, norm_mix[l].reshape(1, d), _pack_w_in(w_in[l]), tab, fox_f_bias[l],
                        mla_q_norm[l], mla_kv_norm[l], mla_w_uq[l], mla_w_ukv[l], s)
        y_ssd = _ssd(p["zx"], p["gt"], ssd_conv_w[l], ssd_conv_b[l], ssd_dt_bias[l], ssd_a_log[l], ssd_d[l],
                     ssd_norm[l], b, s)
        y_dsa = _dsa(p["iq"], p["gt"], p["dq"], p["ik"], p["dk"], p["dv"], b, s, n_sel)
        y_mla = _flash(p["mq"], p["mk"], p["mv"], b, s, "flash_mla")
        y_fox = _flash(p["fq"], p["fk"], p["fv"], b, s, "flash_fox")
        xf, h2, cw, rs = _outproj(xf, (y_ssd, y_dsa, y_mla, y_fox), w_out[l], norm_ffn[l].reshape(1, d),
                                  router_group_w[l], router_group_b[l], router_expert_w[l], router_expert_b[l])
        y_ffn = _moe(h2, cw, rs, expert_w_gate[l], expert_w_up[l], expert_w_down[l])
    return _final(xf, y_ffn, final_norm.reshape(1, d)).reshape(b, s, d)
```

```python
import functools
import math

import jax
import jax.numpy as jnp
from jax import lax
from jax.experimental import pallas as pl
from jax.experimental.pallas import tpu as pltpu

F32 = jnp.float32
BF16 = jnp.bfloat16
I32 = jnp.int32
EPS = 1e-6
ROPE_THETA = 500000.0

D_MODEL = 1024
SSD_HEADS, SSD_HEAD_DIM, SSD_GROUPS, SSD_STATE, SSD_CONV = 4, 64, 2, 128, 4
SSD_D_INNER = SSD_HEADS * SSD_HEAD_DIM
SSD_CONV_DIM = SSD_D_INNER + 2 * SSD_GROUPS * SSD_STATE
DSA_HEADS, DSA_HEAD_DIM, IDX_HEADS, IDX_DIM, IDX_TOPK_MAX = 4, 64, 8, 64, 256
MLA_HEADS, MLA_Q_RANK, MLA_KV_RANK, MLA_NOPE, MLA_ROPE, MLA_V = 4, 256, 128, 64, 32, 64
FOX_HEADS, FOX_HEAD_DIM = 4, 64
MOE_GROUPS, MOE_EPG, MOE_EXPERTS, MOE_FF = 4, 8, 32, 256

LANES = 128
VMEM_LIMIT = 56 * 1024 * 1024
INT_MIN = -(2 ** 31)
NEG_BIG = -1e30

TM = 512
SSD_Q = 256
DSA_BQ = 256
DSA_SC = 512
FA_B = 512
MOE_N = 512
MOE_ALIGN = 16
MOE_CH = 64
MOE_EPS = 2
MOE_SUB = 4
MOE_NS = -(-(2 * MOE_N + MOE_EXPERTS * MOE_ALIGN + MOE_CH) // LANES) * LANES

C_ZX, C_GT, C_DQ, C_KV, C_IK, C_IQ, C_MC, C_FQ, C_FK, C_FV, C_END = (
    0, 1024, 1152, 1408, 1536, 1664, 2176, 2688, 3200, 3712, 4224)
G_FOX, G_DT, G_IW = 0, 4, 8
LOG2E = math.log2(math.e)
HEAD_W = 128
FA_HEADS = 4
VT_ROWS = 80
FOX_ONE_Q, FOX_F_Q, FOX_F_K, FOX_ONE_K = 64, 67, 64, 67


def _cparams(sem):
    return pltpu.CompilerParams(dimension_semantics=sem, vmem_limit_bytes=VMEM_LIMIT)


def _rms(xf, g):
    return xf * lax.rsqrt(jnp.mean(xf * xf, axis=-1, keepdims=True) + EPS) * g


def _softplus(x):
    return jnp.maximum(x, 0.0) + jnp.log1p(jnp.exp(-jnp.abs(x)))


def _silu(x):
    return x * jax.nn.sigmoid(x)


def _rope_apply(a, c, sa, sb, half):
    return a * c + pltpu.roll(a, LANES - half, 1) * sa + pltpu.roll(a, half, 1) * sb


def _dot_nt(a, b):
    return lax.dot_general(a, b, (((1,), (1,)), ((), ())), preferred_element_type=F32)


def _split3(x):
    hi = x.astype(BF16)
    r = x - hi.astype(F32)
    mid = r.astype(BF16)
    return hi, mid, (r - mid.astype(F32)).astype(BF16)


def _dot_f32_by_mask(mask, x, mask_left):
    m = jnp.where(mask, 1.0, 0.0).astype(BF16)
    out = None
    for piece in _split3(x):
        d = jnp.dot(m, piece, preferred_element_type=F32) if mask_left else jnp.dot(piece, m, preferred_element_type=F32)
        out = d if out is None else out + d
    return out


def _rope_tab_kernel(pos_ref, inv_ref, o_ref):
    pos = pos_ref[...].astype(F32)
    lane = lax.broadcasted_iota(I32, (1, LANES), 1)
    ang = pos * inv_ref[0:1, :]
    c, s = jnp.cos(ang), jnp.sin(ang)
    jd = lane % 64
    o_ref[:, 0:128] = c
    o_ref[:, 128:256] = jnp.where(jd < 8, -s, 0.0)
    o_ref[:, 256:384] = jnp.where((jd >= 8) & (jd < 16), s, 0.0)
    ang = pos * inv_ref[1:2, :]
    c, s = jnp.cos(ang), jnp.sin(ang)
    jm = lane - 64
    o_ref[:, 384:512] = c
    o_ref[:, 512:640] = jnp.where((jm >= 0) & (jm < 16), -s, 0.0)
    o_ref[:, 640:768] = jnp.where((jm >= 16) & (jm < 32), s, 0.0)


def _rope_tables(positions):
    t = positions.size
    half_d, half_m = DSA_HEAD_DIM // 4 // 2, MLA_ROPE // 2
    inv_d = ROPE_THETA ** (-jnp.arange(half_d, dtype=F32) / half_d)
    inv_m = ROPE_THETA ** (-jnp.arange(half_m, dtype=F32) / half_m)
    lane = jnp.arange(LANES)
    jd = lane % 64
    row_d = jnp.where(jd < 2 * half_d, inv_d[jd % half_d], 0.0)
    jm = lane - 64
    row_m = jnp.where((jm >= 0) & (jm < 2 * half_m), inv_m[jm % half_m], 0.0)
    inv = jnp.zeros((8, LANES), F32).at[0].set(row_d).at[1].set(row_m)
    tm = 1024
    return pl.pallas_call(
        _rope_tab_kernel,
        grid=(t // tm,),
        in_specs=[pl.BlockSpec((tm, 1), lambda i: (i, 0)), pl.BlockSpec((8, LANES), lambda i: (0, 0))],
        out_specs=pl.BlockSpec((tm, 768), lambda i: (i, 0)),
        out_shape=jax.ShapeDtypeStruct((t, 768), F32),
        compiler_params=_cparams(("parallel",)),
        name="rope_tables",
    )(positions.reshape(t, 1), inv)


def _inproj_kernel(*refs, has_add, tiles_per_seq):
    refs = list(refs)
    x_ref = refs.pop(0)
    yp_ref = refs.pop(0) if has_add else None
    g_ref, w_ref, tab_ref, fb_ref, qn_ref, kn_ref, wq_ref, wk_ref, wv_ref = refs[:9]
    refs = refs[9:]
    xo_ref = refs.pop(0) if has_add else None
    (zx_ref, gt_ref, dq_ref, dk_ref, dv_ref, ik_ref, iq_ref, mq_ref, mk_ref, mv_ref, fq_ref, fk_ref, fv_ref,
     carry_s) = refs
    x = x_ref[...]
    if has_add:
        x = x + yp_ref[...].astype(F32)
        xo_ref[...] = x
    h = _rms(x, g_ref[...]).astype(BF16)

    def proj(a, b):
        return jnp.dot(h, w_ref[:, a:b], preferred_element_type=F32)

    zx_ref[...] = proj(C_ZX, C_GT)
    gates = proj(C_GT, C_DQ)
    gt_ref[...] = gates
    c_d, sa_d, sb_d = tab_ref[:, 0:128], tab_ref[:, 128:256], tab_ref[:, 256:384]
    c_m, sa_m, sb_m = tab_ref[:, 384:512], tab_ref[:, 512:640], tab_ref[:, 640:768]
    q = proj(C_DQ, C_KV)
    for i in range(2):
        dq_ref[:, i * 128:(i + 1) * 128] = _rope_apply(q[:, i * 128:(i + 1) * 128], c_d, sa_d, sb_d, 8).astype(BF16)
    kv = proj(C_KV, C_IK)
    is_k = lax.broadcasted_iota(I32, (1, LANES), 1) < 64
    kvr = _rope_apply(kv, jnp.where(is_k, c_d, 1.0), jnp.where(is_k, sa_d, 0.0), jnp.where(is_k, sb_d, 0.0), 8)
    dk_ref[...] = kvr[:, 0:64].astype(BF16)
    dv_ref[0, 0:64, :] = kvr.astype(BF16).astype(F32).T[64:128, :].astype(BF16)
    dv_ref[0, 64:VT_ROWS, :] = jnp.ones((VT_ROWS - 64, kvr.shape[0]), BF16)
    ikb = proj(C_IK, C_IQ)
    ik_ref[...] = _rope_apply(ikb, c_d, sa_d, sb_d, 8)[:, 0:64].astype(BF16)
    iq = proj(C_IQ, C_MC)
    for i in range(4):
        iq_ref[:, i * 128:(i + 1) * 128] = _rope_apply(iq[:, i * 128:(i + 1) * 128], c_d, sa_d, sb_d, 8).astype(BF16)
    ones_hi = jnp.where(lax.broadcasted_iota(I32, (1, FA_HEADS * HEAD_W), 1) % HEAD_W >= 64, 1.0, 0.0)

    mc = proj(C_MC, C_FQ)
    cq = _rms(mc[:, 0:256], qn_ref[...]).astype(BF16)
    ckv = _rms(mc[:, 256:384], kn_ref[...]).astype(BF16)
    kr = _rope_apply(mc[:, 384:512], c_m, sa_m, sb_m, 16)
    mq = jnp.dot(cq, wq_ref[...], preferred_element_type=F32)
    mk = jnp.dot(ckv, wk_ref[...], preferred_element_type=F32)
    for hd in range(MLA_HEADS):
        sl = slice(hd * HEAD_W, (hd + 1) * HEAD_W)
        mq_ref[:, sl] = _rope_apply(mq[:, sl], c_m, sa_m, sb_m, 16).astype(BF16)
        mk_ref[:, sl] = (mk[:, sl] + kr).astype(BF16)
    mv_ref[...] = (jnp.dot(ckv, wv_ref[...], preferred_element_type=F32) + ones_hi).astype(BF16)

    @pl.when(pl.program_id(0) % tiles_per_seq == 0)
    def _():
        carry_s[...] = jnp.zeros_like(carry_s)

    tm = x.shape[0]
    log_f = -_softplus(-(gates + fb_ref[...]))
    ri = lax.broadcasted_iota(I32, (tm, tm), 0)
    ci = lax.broadcasted_iota(I32, (tm, tm), 1)
    cs = _dot_f32_by_mask(ri >= ci, log_f, True) + carry_s[0:1, :]
    carry_s[0:1, :] = cs[tm - 1:tm, :]
    f2 = cs * LOG2E
    lane = lax.broadcasted_iota(I32, (1, HEAD_W), 1)

    def pieces(f, base):
        hi, mid, lo = _split3(f)
        return jnp.where(lane == base, hi.astype(F32),
                         jnp.where(lane == base + 1, mid.astype(F32), jnp.where(lane == base + 2, lo.astype(F32), 0.0)))

    def ones(base):
        return jnp.where((lane >= base) & (lane < base + 3), 1.0, 0.0)

    fq = proj(C_FQ, C_FK)
    fk = proj(C_FK, C_FV)
    for hd in range(FOX_HEADS):
        fh = f2[:, G_FOX + hd:G_FOX + hd + 1]
        sl = slice(hd * HEAD_W, (hd + 1) * HEAD_W)
        fq_ref[:, sl] = (fq[:, sl] + pieces(fh, FOX_F_Q) + ones(FOX_ONE_Q)).astype(BF16)
        fk_ref[:, sl] = (fk[:, sl] - pieces(fh, FOX_F_K) + ones(FOX_ONE_K)).astype(BF16)
    fv_ref[...] = (proj(C_FV, C_END) + ones_hi).astype(BF16)


def _pack_w_in(w):
    d = w.shape[0]
    z = lambda n: jnp.zeros((d, n), F32)
    o = 0
    parts = {}
    for name, width in (("z", 256), ("xbc", 768), ("dt", 4), ("dq", 256), ("dk", 64), ("dv", 64), ("iq", 512),
                        ("ik", 64), ("iw", 8), ("cq", 256), ("ckv", 128), ("kr", 32), ("fq", 256), ("fk", 256),
                        ("fv", 256), ("ff", 4)):
        parts[name] = w[:, o:o + width]
        o += width
    scale = DSA_HEAD_DIM ** -0.5

    def per_head(a):
        return jnp.pad(a.reshape(d, FA_HEADS, 64), ((0, 0), (0, 0), (0, HEAD_W - 64))).reshape(d, FA_HEADS * HEAD_W)

    cat = jnp.concatenate([
        parts["z"], parts["xbc"],
        parts["ff"], parts["dt"], parts["iw"], z(112),
        parts["dq"] * (scale * LOG2E),
        parts["dk"], parts["dv"],
        parts["ik"], z(64),
        parts["iq"],
        parts["cq"], parts["ckv"], z(64), parts["kr"], z(32),
        per_head(parts["fq"] * (FOX_HEAD_DIM ** -0.5 * LOG2E)), per_head(parts["fk"]), per_head(parts["fv"])], axis=1)
    return cat.astype(BF16)


def _mla_weights(w_uq, w_ukv):
    dqk = MLA_NOPE + MLA_ROPE
    wq = jnp.pad(w_uq.reshape(MLA_Q_RANK, MLA_HEADS, dqk) * (dqk ** -0.5 * LOG2E), ((0, 0), (0, 0), (0, HEAD_W - dqk)))
    wkv = w_ukv.reshape(MLA_KV_RANK, MLA_HEADS, MLA_NOPE + MLA_V)
    wk = jnp.pad(wkv[:, :, :MLA_NOPE], ((0, 0), (0, 0), (0, HEAD_W - MLA_NOPE)))
    wv = jnp.pad(wkv[:, :, MLA_NOPE:], ((0, 0), (0, 0), (0, HEAD_W - MLA_V)))
    flat = lambda a: a.reshape(a.shape[0], MLA_HEADS * HEAD_W).astype(BF16)
    return flat(wq), flat(wk), flat(wv)


def _inproj(x, y_prev, g, w_cat, tab, f_bias, q_norm, kv_norm, w_uq, w_ukv, s):
    t = x.shape[0]
    has_add = y_prev is not None
    tok = lambda n: pl.BlockSpec((TM, n), lambda i: (i, 0))
    full = lambda a: pl.BlockSpec(a.shape, lambda i: (0,) * a.ndim)
    fb = jnp.zeros((1, LANES), F32).at[0, G_FOX:G_FOX + FOX_HEADS].set(f_bias)
    params = [g, w_cat, tab, fb, q_norm.reshape(1, -1), kv_norm.reshape(1, -1), *_mla_weights(w_uq, w_ukv)]
    ins = [x] + ([y_prev] if has_add else []) + params
    in_specs = ([tok(D_MODEL)] + ([tok(D_MODEL)] if has_add else [])
                + [tok(768) if a is tab else full(a) for a in params])
    outs = ([("x", D_MODEL, F32)] if has_add else []) + [
        ("zx", 1024, F32), ("gt", 128, F32), ("dq", 256, BF16), ("dk", 64, BF16), ("dv", 64, BF16),
        ("ik", 64, BF16), ("iq", 512, BF16), ("mq", 512, BF16), ("mk", 512, BF16), ("mv", 512, BF16),
        ("fq", 512, BF16), ("fk", 512, BF16), ("fv", 512, BF16)]
    vt_rows = {"dv": VT_ROWS}
    res = pl.pallas_call(
        functools.partial(_inproj_kernel, has_add=has_add, tiles_per_seq=s // TM),
        grid=(t // TM,),
        in_specs=in_specs,
        out_specs=[pl.BlockSpec((1, vt_rows[nm], TM), lambda i: (i, 0, 0)) if nm in vt_rows else tok(n)
                   for nm, n, _ in outs],
        out_shape=[jax.ShapeDtypeStruct((t // TM, vt_rows[nm], TM) if nm in vt_rows else (t, n), dt)
                   for nm, n, dt in outs],
        scratch_shapes=[pltpu.VMEM((8, LANES), F32)],
        compiler_params=_cparams(("arbitrary",)),
        name="inproj",
    )(*ins)
    res = list(res)
    x_new = res.pop(0) if has_add else x
    return x_new, dict(zip([n for n, _, _ in outs if n != "x"], res))


def _ssd_kernel(zx_ref, gt_ref, cw_ref, cb_ref, pr_ref, pc_ref, drow_ref, nw_ref, o_ref, ext_s, st_s, y_s, *, q):
    @pl.when(pl.program_id(1) == 0)
    def _():
        ext_s[0:8, :] = jnp.zeros((8, SSD_CONV_DIM), F32)
        st_s[...] = jnp.zeros_like(st_s)

    raw = zx_ref[:, 256:1024]
    ext_s[8:8 + q, :] = raw
    acc = jnp.broadcast_to(cb_ref[...], (q, SSD_CONV_DIM))
    for j in range(SSD_CONV):
        acc = acc + cw_ref[j:j + 1, :] * ext_s[5 + j:5 + j + q, :]
    ext_s[0:8, :] = raw[q - 8:q, :]
    xbc = _silu(acc)
    xs = xbc[:, 0:SSD_D_INNER]

    g = gt_ref[...]
    lane = lax.broadcasted_iota(I32, (1, LANES), 1)
    dtc = _softplus(g + pr_ref[0:1, :])
    a_r = jnp.where((lane >= G_DT) & (lane < G_DT + SSD_HEADS), -jnp.exp(pr_ref[1:2, :]), 0.0)
    ri = lax.broadcasted_iota(I32, (q, q), 0)
    ci = lax.broadcasted_iota(I32, (q, q), 1)
    tri = ri >= ci
    acs_c = _dot_f32_by_mask(tri, dtc * a_r, True)
    sub = lax.broadcasted_iota(I32, (LANES, 1), 0)
    dtr = _softplus(g.T + pc_ref[:, 0:1])
    a_c = jnp.where((sub >= G_DT) & (sub < G_DT + SSD_HEADS), -jnp.exp(pc_ref[:, 1:2]), 0.0)
    acs_r = _dot_f32_by_mask(ri <= ci, dtr * a_c, False)

    rep = SSD_HEADS // SSD_GROUPS
    for gi in range(SSD_GROUPS):
        bg = xbc[:, SSD_D_INNER + gi * SSD_STATE:SSD_D_INNER + (gi + 1) * SSD_STATE]
        cg = xbc[:, SSD_D_INNER + (SSD_GROUPS + gi) * SSD_STATE:SSD_D_INNER + (SSD_GROUPS + gi + 1) * SSD_STATE]
        bt = bg.T.astype(BF16)
        cb16 = cg.astype(BF16)
        cbm = jnp.dot(cb16, bt, preferred_element_type=F32)
        for hh in range(rep):
            h = gi * rep + hh
            ac = acs_c[:, G_DT + h:G_DT + h + 1]
            ar = acs_r[G_DT + h:G_DT + h + 1, :]
            seg = jnp.where(tri, jnp.exp(ac - ar), 0.0)
            xh = xs[:, h * SSD_HEAD_DIM:(h + 1) * SSD_HEAD_DIM]
            xdt = xh * dtc[:, G_DT + h:G_DT + h + 1]
            yd = jnp.dot((cbm * seg).astype(BF16), xdt.astype(BF16), preferred_element_type=F32)
            aend = ac[q - 1:q, :]
            st = st_s[h]
            yo = jnp.dot(cb16, st.astype(BF16), preferred_element_type=F32) * jnp.exp(ac)
            st_s[h] = st * jnp.exp(aend) + jnp.dot(bt, (xdt * jnp.exp(aend - ac)).astype(BF16),
                                                   preferred_element_type=F32)
            y_s[:, h * SSD_HEAD_DIM:(h + 1) * SSD_HEAD_DIM] = yd + yo
    y = y_s[...] + drow_ref[...] * xs
    y = y * _silu(zx_ref[:, 0:SSD_D_INNER])
    o_ref[...] = _rms(y, nw_ref[...]).astype(BF16)


def _ssd(zx, gt, conv_w, conv_b, dt_bias, a_log, d_skip, norm_w, b, s):
    q = SSD_Q
    nc = s // q
    pr = jnp.zeros((8, LANES), F32).at[0, G_DT:G_DT + SSD_HEADS].set(dt_bias).at[1, G_DT:G_DT + SSD_HEADS].set(a_log)
    pc = pr.T
    drow = jnp.repeat(d_skip, SSD_HEAD_DIM).reshape(1, SSD_D_INNER)
    cb = conv_b.reshape(1, -1)
    nw = norm_w.reshape(1, -1)
    full = lambda a: pl.BlockSpec(a.shape, lambda i, j: (0,) * a.ndim)
    return pl.pallas_call(
        functools.partial(_ssd_kernel, q=q),
        grid=(b, nc),
        in_specs=[pl.BlockSpec((q, 1024), lambda i, j: (i * nc + j, 0)), pl.BlockSpec((q, LANES), lambda i, j: (i * nc + j, 0)),
                  full(conv_w), full(cb), full(pr), full(pc), full(drow), full(nw)],
        out_specs=pl.BlockSpec((q, SSD_D_INNER), lambda i, j: (i * nc + j, 0)),
        out_shape=jax.ShapeDtypeStruct((b * s, SSD_D_INNER), BF16),
        scratch_shapes=[pltpu.VMEM((8 + q, SSD_CONV_DIM), F32), pltpu.VMEM((SSD_HEADS, SSD_STATE, SSD_HEAD_DIM), F32),
                        pltpu.VMEM((q, SSD_D_INNER), F32)],
        compiler_params=_cparams(("parallel", "arbitrary")),
        name="ssd",
    )(zx, gt, conv_w, cb, pr, pc, drow, nw)


def _key_to_f32(k):
    return pltpu.bitcast(jnp.where(k < 0, k ^ jnp.int32(0x7FFFFFFF), k), F32)


def _dsa_kernel(iq_ref, gt_ref, q_ref, ik_ref, k_ref, vt_ref, o_ref, sc_s, qi_s, qs_s, thr_s, s16_s,
                *, bq, sc_w, n_sel):
    qb = pl.program_id(1)
    n_sc = ((qb + 1) * bq + sc_w - 1) // sc_w
    for j in range(IDX_HEADS):
        qi_s[j // 2, (j % 2) * bq:(j % 2 + 1) * bq, :] = iq_ref[:, j * IDX_DIM:(j + 1) * IDX_DIM]
    for h in range(DSA_HEADS):
        qs_s[h * bq:(h + 1) * bq, :] = q_ref[:, h * DSA_HEAD_DIM:(h + 1) * DSA_HEAD_DIM]
    w_t = (gt_ref[...] * ((IDX_HEADS * IDX_DIM) ** -0.5)).T
    qpos = qb * bq + lax.broadcasted_iota(I32, (1, bq), 1)

    def score_body(sc, carry):
        k0 = pl.multiple_of(sc * sc_w, sc_w)
        kidx = ik_ref[pl.ds(k0, sc_w), :]
        sco = jnp.zeros((sc_w, bq), F32)
        for jp in range(IDX_HEADS // 2):
            a = _dot_nt(kidx, qi_s[jp])
            for u in range(2):
                j = 2 * jp + u
                sco = sco + jnp.maximum(a[:, u * bq:(u + 1) * bq], 0.0) * w_t[G_IW + j:G_IW + j + 1, :]
        kpos = k0 + lax.broadcasted_iota(I32, (sc_w, 1), 0)
        sco = jnp.where(kpos <= qpos, sco, -jnp.inf)
        sc_s[sc] = sco
        s16_s[sc] = sco.astype(BF16)
        return carry

    lax.fori_loop(0, n_sc, score_body, 0)

    def count(pred):
        def body(sc, acc):
            for r in range(sc_w // 64):
                acc = acc + jnp.where(pred(sc_s[sc, r * 64:(r + 1) * 64, :]), 1.0, 0.0)
            return acc
        acc = lax.fori_loop(0, n_sc, body, jnp.zeros((64, bq), F32))
        return jnp.sum(acc, axis=0, keepdims=True)

    thr_s[0:1, :] = jnp.full((1, bq), -jnp.finfo(jnp.float32).max, F32)

    @pl.when((qb + 1) * bq > n_sel)
    def _():
        active = qpos + 1 > n_sel

        one16, zero16 = jnp.ones((), BF16), jnp.zeros((), BF16)

        def count16(c16):
            def body(sc, acc):
                a16 = jnp.zeros((64, bq), BF16)
                for r in range(sc_w // 64):
                    a16 = a16 + jnp.where(s16_s[sc, r * 64:(r + 1) * 64, :] >= c16, one16, zero16)
                return acc + a16.astype(F32)
            acc = lax.fori_loop(0, n_sc, body, jnp.zeros((64, bq), F32))
            return jnp.sum(acc, axis=0, keepdims=True)

        def coarse(it, lo):
            cand = lo ^ jnp.left_shift(jnp.int32(1), 31 - it)
            return jnp.where(count16(_key_to_f32(cand).astype(BF16)) >= n_sel, cand, lo)

        lo16 = lax.fori_loop(0, 16, coarse, jnp.full((1, bq), INT_MIN, I32))
        base = jnp.maximum(lo16, jnp.int32(INT_MIN + (1 << 16))) - jnp.int32(1 << 16)

        def cond(st):
            it, off, cnt_lo = st
            pending = active & (cnt_lo != n_sel)
            return (it < 32) & (jnp.max(jnp.where(pending, 1.0, 0.0)) > 0.0)

        def fine(it, off, cnt_lo):
            cand = off | jnp.left_shift(jnp.int32(1), 31 - it)
            thr_c = _key_to_f32(base + cand)
            cnt = count(lambda t: t >= thr_c)
            ok = cnt >= n_sel
            return jnp.where(ok, cand, off), jnp.where(ok, cnt, cnt_lo)

        def body(st):
            it, off, cnt_lo = st
            for u in range(4):
                off, cnt_lo = fine(it + u, off, cnt_lo)
            return it + 4, off, cnt_lo

        off, cnt_lo = fine(15, jnp.zeros((1, bq), I32), jnp.full((1, bq), 3.0e38, F32))
        _, off, cnt_lo = lax.while_loop(cond, body, (jnp.int32(16), off, cnt_lo))
        lo = base + off
        thr = _key_to_f32(lo)
        tie = active & (cnt_lo > n_sel)

        @pl.when(jnp.max(jnp.where(tie, 1.0, 0.0)) > 0.0)
        def _():
            need = n_sel - count(lambda t: t > thr)
            ri = lax.broadcasted_iota(I32, (sc_w, sc_w), 0)
            ci = lax.broadcasted_iota(I32, (sc_w, sc_w), 1)
            lower = jnp.where(ri >= ci, 1.0, 0.0).astype(BF16)

            def tie_body(sc, run):
                t = sc_s[sc]
                eq = (t == thr) & tie
                pre = jnp.dot(lower, jnp.where(eq, 1.0, 0.0).astype(BF16), preferred_element_type=F32)
                sc_s[sc] = jnp.where(eq & (run + pre > need), -jnp.inf, t)
                return run + pre[sc_w - 1:sc_w, :]

            lax.fori_loop(0, n_sc, tie_body, jnp.zeros((1, bq), F32))

        thr_s[0:1, :] = jnp.where(active, thr, thr_s[0:1, :])

    thr = thr_s[0:1, :]

    hq = DSA_HEADS * bq

    def att_body(sc, carry):
        m_prev, acc = carry
        k0 = pl.multiple_of(sc * sc_w, sc_w)
        st = _dot_nt(k_ref[pl.ds(k0, sc_w), :], qs_s[...])
        drop = jnp.where(sc_s[sc] >= thr, 0.0, 2 * NEG_BIG)
        st = st + jnp.tile(drop, (1, DSA_HEADS))
        m_cur = jnp.max(jnp.max(st.reshape(sc_w // 64, 64, hq), axis=0), axis=0, keepdims=True)
        m_next = jnp.maximum(m_prev, m_cur)
        p = jnp.exp2(st - m_next)
        pv = jnp.dot(vt_ref[sc], p.astype(BF16), preferred_element_type=F32)
        return m_next, jnp.exp2(m_prev - m_next) * acc + pv

    def att_body2(i, carry):
        return att_body(2 * i + 1, att_body(2 * i, carry))

    carry = lax.fori_loop(0, n_sc // 2, att_body2, (jnp.full((1, hq), NEG_BIG, F32), jnp.zeros((VT_ROWS, hq), F32)))
    _, acc = lax.cond(n_sc % 2 == 1, lambda c: att_body(n_sc - 1, c), lambda c: c, carry)
    out_t = jnp.concatenate([acc[0:DSA_HEAD_DIM, :] / acc[DSA_HEAD_DIM:DSA_HEAD_DIM + 1, :],
                             jnp.zeros((LANES - DSA_HEAD_DIM, hq), F32)], axis=0).T
    for h in range(DSA_HEADS):
        o_ref[:, h * DSA_HEAD_DIM:(h + 1) * DSA_HEAD_DIM] = out_t[h * bq:(h + 1) * bq, 0:DSA_HEAD_DIM].astype(BF16)


def _dsa(iq, gt, dq, ik, dk, dvt, b, s, n_sel):
    bq, sc_w = DSA_BQ, DSA_SC
    nq = s // bq
    qblk = lambda n: pl.BlockSpec((bq, n), lambda i, j: (i * nq + j, 0))
    kblk = pl.BlockSpec((s, 64), lambda i, j: (i, 0))
    return pl.pallas_call(
        functools.partial(_dsa_kernel, bq=bq, sc_w=sc_w, n_sel=n_sel),
        grid=(b, nq),
        in_specs=[qblk(512), qblk(LANES), qblk(256), kblk, kblk,
                  pl.BlockSpec((s // sc_w, VT_ROWS, sc_w), lambda i, j: (i, 0, 0))],
        out_specs=qblk(256),
        out_shape=jax.ShapeDtypeStruct((b * s, 256), BF16),
        scratch_shapes=[pltpu.VMEM((s // sc_w, sc_w, bq), F32), pltpu.VMEM((IDX_HEADS // 2, 2 * bq, IDX_DIM), BF16),
                        pltpu.VMEM((DSA_HEADS * bq, DSA_HEAD_DIM), BF16), pltpu.VMEM((8, bq), F32),
                        pltpu.VMEM((s // sc_w, sc_w, bq), BF16)],
        compiler_params=_cparams(("parallel", "arbitrary")),
        name="dsa",
    )(iq, gt, dq, ik, dk, dvt)


def _flash_kernel(qt_ref, kt_ref, q_ref, k_ref, v_ref, o_ref, m_s, acc_s, *, blk):
    t = pl.program_id(1)
    qi, ki = qt_ref[t], kt_ref[t]

    @pl.when(ki == 0)
    def _():
        m_s[...] = jnp.full(m_s.shape, NEG_BIG, F32)
        acc_s[...] = jnp.zeros_like(acc_s)

    def step(diagonal):
        if diagonal:
            keep = lax.broadcasted_iota(I32, (blk, blk), 0) >= lax.broadcasted_iota(I32, (blk, blk), 1)
        for h in range(FA_HEADS):
            sl = slice(h * HEAD_W, (h + 1) * HEAD_W)
            s = _dot_nt(q_ref[:, sl], k_ref[:, sl])
            if diagonal:
                s = jnp.where(keep, s, 2 * NEG_BIG)
            m_prev = m_s[h]
            m_next = jnp.maximum(m_prev, jnp.max(s, axis=1, keepdims=True))
            p = jnp.exp2(s - jnp.tile(m_next, (1, blk // LANES)))
            pv = jnp.dot(p.astype(BF16), v_ref[:, sl], preferred_element_type=F32)
            acc_s[h] = jnp.exp2(m_prev - m_next) * acc_s[h] + pv
            m_s[h] = m_next

    @pl.when(ki < qi)
    def _():
        step(False)

    @pl.when(ki == qi)
    def _():
        step(True)
        for h in range(FA_HEADS):
            acc = acc_s[h]
            o_ref[:, h * 64:(h + 1) * 64] = (acc / pltpu.roll(acc, 64, 1))[:, 0:64].astype(BF16)


def _flash(q, k, v, b, s, name):
    blk = FA_B
    nb = s // blk
    pairs = [(i, j) for i in range(nb) for j in range(i + 1)]
    qt = jnp.asarray([p[0] for p in pairs], I32)
    kt = jnp.asarray([p[1] for p in pairs], I32)
    w = FA_HEADS * HEAD_W
    grid_spec = pltpu.PrefetchScalarGridSpec(
        num_scalar_prefetch=2,
        grid=(b, len(pairs)),
        in_specs=[pl.BlockSpec((blk, w), lambda i, t, qt, kt: (i * nb + qt[t], 0)),
                  pl.BlockSpec((blk, w), lambda i, t, qt, kt: (i * nb + kt[t], 0)),
                  pl.BlockSpec((blk, w), lambda i, t, qt, kt: (i * nb + kt[t], 0))],
        out_specs=pl.BlockSpec((blk, FA_HEADS * 64), lambda i, t, qt, kt: (i * nb + qt[t], 0)),
        scratch_shapes=[pltpu.VMEM((FA_HEADS, blk, LANES), F32), pltpu.VMEM((FA_HEADS, blk, HEAD_W), F32)],
    )
    return pl.pallas_call(
        functools.partial(_flash_kernel, blk=blk),
        grid_spec=grid_spec,
        out_shape=jax.ShapeDtypeStruct((b * s, FA_HEADS * 64), BF16),
        compiler_params=_cparams(("parallel", "arbitrary")),
        name=name,
    )(qt, kt, q, k, v)


def _outproj_kernel(x_ref, y0_ref, y1_ref, y2_ref, y3_ref, wo_ref, g_ref, wr_ref, br_ref,
                    xo_ref, h_ref, cw_ref, rs_ref):
    acc = x_ref[...]
    for i, y_ref in enumerate((y0_ref, y1_ref, y2_ref, y3_ref)):
        acc = acc + jnp.dot(y_ref[...], wo_ref[i * 256:(i + 1) * 256, :], preferred_element_type=F32)
    xo_ref[...] = acc
    h2 = _rms(acc, g_ref[...])
    h_ref[...] = h2.astype(BF16)
    h_hi, h_lo, _ = _split3(h2)
    lg = (jnp.dot(h_hi, wr_ref[0], preferred_element_type=F32) + jnp.dot(h_lo, wr_ref[0], preferred_element_type=F32)
          + jnp.dot(h_hi, wr_ref[1], preferred_element_type=F32) + br_ref[...])
    lane = lax.broadcasted_iota(I32, (1, LANES), 1)
    lanef = lane.astype(F32)
    is_g = (lane >= MOE_EXPERTS) & (lane < MOE_EXPERTS + MOE_GROUPS)
    gl = jnp.where(is_g, lg, -jnp.inf)
    gmax = jnp.max(gl, axis=1, keepdims=True)
    gidx = jnp.min(jnp.where(gl == gmax, lanef, 999.0), axis=1, keepdims=True) - MOE_EXPERTS
    g_p = 1.0 / jnp.sum(jnp.where(is_g, jnp.exp(gl - gmax), 0.0), axis=1, keepdims=True)
    in_g = (lane < MOE_EXPERTS) & (jnp.floor(lanef * (1.0 / MOE_EPG)) == gidx)
    el = jnp.where(in_g, lg, -jnp.inf)
    m1 = jnp.max(el, axis=1, keepdims=True)
    i1 = jnp.min(jnp.where(el == m1, lanef, 999.0), axis=1, keepdims=True)
    el2 = jnp.where(lanef == i1, -jnp.inf, el)
    m2 = jnp.max(el2, axis=1, keepdims=True)
    i2 = jnp.min(jnp.where(el2 == m2, lanef, 999.0), axis=1, keepdims=True)
    t = jnp.exp(m2 - m1)
    w1 = 1.0 / (1.0 + t)
    cw_ref[...] = jnp.where(lanef == i1, w1 * g_p, jnp.where(lanef == i2, t * w1 * g_p, 0.0))
    rs_ref[...] = jnp.where((lanef == i1) | (lanef == i2), 1.0, 0.0).astype(BF16)


def _outproj(x, ys, w_out, g, w_rg, b_rg, w_re, b_re):
    t = x.shape[0]
    wo = w_out.astype(BF16)
    wr = jnp.zeros((D_MODEL, LANES), F32).at[:, 0:MOE_EXPERTS].set(w_re).at[:, MOE_EXPERTS:MOE_EXPERTS + MOE_GROUPS].set(w_rg)
    wr_hi = wr.astype(BF16)
    wr = jnp.stack([wr_hi, (wr - wr_hi.astype(F32)).astype(BF16)])
    br =jnp.zeros((1, LANES), F32).at[0, 0:MOE_EXPERTS].set(b_re).at[0, MOE_EXPERTS:MOE_EXPERTS + MOE_GROUPS].set(b_rg)
    tok = lambda n: pl.BlockSpec((TM, n), lambda i: (i, 0))
    full = lambda a: pl.BlockSpec(a.shape, lambda i: (0,) * a.ndim)
    return pl.pallas_call(
        _outproj_kernel,
        grid=(t // TM,),
        in_specs=[tok(D_MODEL)] + [tok(256)] * 4 + [full(wo), full(g), full(wr), full(br)],
        out_specs=[tok(D_MODEL), tok(D_MODEL), tok(LANES), tok(LANES)],
        out_shape=[jax.ShapeDtypeStruct((t, D_MODEL), F32), jax.ShapeDtypeStruct((t, D_MODEL), BF16),
                   jax.ShapeDtypeStruct((t, LANES), F32), jax.ShapeDtypeStruct((t, LANES), BF16)],
        compiler_params=_cparams(("parallel",)),
        name="outproj_router",
    )(x, *ys, wo, g, wr, br)


def _moe_plan_kernel(rs_ref, cw_ref, meta_ref, col_ref, row_ref, *, n):
    sel = rs_ref[...]
    self32 = sel.astype(F32)
    ti = lax.broadcasted_iota(I32, (n, n), 0)
    tj = lax.broadcasted_iota(I32, (n, n), 1)
    rank = jnp.dot(jnp.where(ti > tj, 1.0, 0.0).astype(BF16), sel, preferred_element_type=F32)
    cnt = jnp.sum(self32, axis=0, keepdims=True).astype(I32)
    cpad = ((cnt + (MOE_ALIGN - 1)) // MOE_ALIGN) * MOE_ALIGN
    li = lax.broadcasted_iota(I32, (LANES, LANES), 0)
    lj = lax.broadcasted_iota(I32, (LANES, LANES), 1)
    off = _dot_f32_by_mask(li < lj, jnp.broadcast_to(cpad.astype(F32), (8, LANES)), False)[0:1, :]
    dest = off + rank
    lane = lax.broadcasted_iota(I32, (1, LANES), 1)
    lanef = lane.astype(F32)
    on = self32 > 0.5
    e_lo = jnp.min(jnp.where(on, lanef, 999.0), axis=1, keepdims=True)
    e_hi = jnp.max(jnp.where(on, lanef, -1.0), axis=1, keepdims=True)
    is_lo, is_hi = lanef == e_lo, lanef == e_hi
    cw = cw_ref[...]
    pick = lambda m, a: jnp.sum(jnp.where(m, a, 0.0), axis=1, keepdims=True)
    d_lo, d_hi, w_lo, w_hi = pick(is_lo, dest), pick(is_hi, dest), pick(is_lo, cw), pick(is_hi, cw)
    colv = jnp.where(lane == 0, d_lo, jnp.where(lane == 1, d_hi, jnp.where(lane == 2, w_lo, jnp.where(lane == 3, w_hi, 0.0))))
    col_ref[...] = colv
    row_ref[0] = colv.T[0:8, :]
    sub = lax.broadcasted_iota(I32, (8, LANES), 0)
    meta_ref[0] = jnp.where(sub == 0, off.astype(I32), jnp.where(sub == 1, cnt, 0))


def _moe_kernel(meta_ref, h_ref, col_ref, row_ref, wg_ref, wu_ref, wd_ref, o_ref, s_s, y_s, *, n, ns):
    t, e = pl.program_id(0), pl.program_id(1)

    @pl.when(e == 0)
    def _():
        for g in range(MOE_SUB):
            d_lo, d_hi = row_ref[g, 0:1, :], row_ref[g, 1:2, :]
            hg = h_ref[g * n:(g + 1) * n, :]
            for r in range(ns // LANES):
                si = (r * LANES + lax.broadcasted_iota(I32, (LANES, 1), 0)).astype(F32)
                perm = jnp.where((si == d_lo) | (si == d_hi), 1.0, 0.0).astype(BF16)
                s_s[g, r * LANES:(r + 1) * LANES, :] = jnp.dot(perm, hg, preferred_element_type=F32).astype(BF16)
        y_s[...] = jnp.zeros_like(y_s)

    def mlp(j, starts, store_ok):
        starts = [pl.multiple_of(r0, MOE_ALIGN) for r0 in starts]
        xs = jnp.concatenate([s_s[g, pl.ds(starts[g], MOE_CH), :] for g in range(MOE_SUB)], axis=0)
        gate = jnp.dot(xs, wg_ref[j], preferred_element_type=F32)
        up = jnp.dot(xs, wu_ref[j], preferred_element_type=F32)
        hid = (_silu(gate) * up).astype(BF16)
        y = jnp.dot(hid, wd_ref[j], preferred_element_type=F32).astype(BF16)
        for g in range(MOE_SUB):
            def store(g=g):
                y_s[g, pl.ds(starts[g], MOE_CH), :] = y[g * MOE_CH:(g + 1) * MOE_CH, :]
            if store_ok is None:
                store()
            else:
                pl.when(store_ok[g])(store)

    def meta(g, j, what):
        return meta_ref[(t * MOE_SUB + g) * 2 * MOE_EXPERTS + what * MOE_EXPERTS + e * MOE_EPS + j]

    offs = [[meta(g, j, 0) for g in range(MOE_SUB)] for j in range(MOE_EPS)]
    cnts = [[meta(g, j, 1) for g in range(MOE_SUB)] for j in range(MOE_EPS)]
    for j in range(MOE_EPS):
        mlp(j, offs[j], None)
    for j in range(MOE_EPS):
        ends = [offs[j][g] + ((cnts[j][g] + MOE_ALIGN - 1) // MOE_ALIGN) * MOE_ALIGN for g in range(MOE_SUB)]
        nch = [(cnts[j][g] + MOE_CH - 1) // MOE_CH for g in range(MOE_SUB)]

        def body(i, carry, j=j, ends=ends, nch=nch):
            more = [i < nch[g] for g in range(MOE_SUB)]
            starts = [jnp.where(more[g], jnp.minimum(offs[j][g] + i * MOE_CH, ends[g] - MOE_CH), offs[j][g])
                      for g in range(MOE_SUB)]
            mlp(j, starts, more)
            return carry

        lax.fori_loop(1, functools.reduce(jnp.maximum, nch), body, 0)

    @pl.when(e == pl.num_programs(1) - 1)
    def _():
        rows = 256
        kc = ns // next(k for k in (3, 2, 1) if ns % (k * LANES) == 0)
        for g in range(MOE_SUB):
            for c in range(n // rows):
                r0 = g * n + c * rows
                cv = col_ref[r0:r0 + rows, :]
                d_lo, d_hi, w_lo, w_hi = cv[:, 0:1], cv[:, 1:2], cv[:, 2:3], cv[:, 3:4]
                acc = jnp.zeros((rows, D_MODEL), F32)
                for r in range(ns // kc):
                    si = (r * kc + lax.broadcasted_iota(I32, (1, kc), 1)).astype(F32)
                    pw = (jnp.where(si == d_lo, w_lo, 0.0) + jnp.where(si == d_hi, w_hi, 0.0)).astype(BF16)
                    acc = acc + jnp.dot(pw, y_s[g, r * kc:(r + 1) * kc, :], preferred_element_type=F32)
                o_ref[r0:r0 + rows, :] = acc.astype(BF16)


def _moe(h2, cw, rs, w_gate, w_up, w_down):
    t = h2.shape[0]
    n, ns = MOE_N, MOE_NS
    nt = t // n
    meta, col, row = pl.pallas_call(
        functools.partial(_moe_plan_kernel, n=n),
        grid=(nt,),
        in_specs=[pl.BlockSpec((n, LANES), lambda i: (i, 0)), pl.BlockSpec((n, LANES), lambda i: (i, 0))],
        out_specs=[pl.BlockSpec((1, 8, LANES), lambda i: (i, 0, 0)), pl.BlockSpec((n, LANES), lambda i: (i, 0)),
                   pl.BlockSpec((1, 8, n), lambda i: (i, 0, 0))],
        out_shape=[jax.ShapeDtypeStruct((nt, 8, LANES), I32), jax.ShapeDtypeStruct((t, LANES), F32),
                   jax.ShapeDtypeStruct((nt, 8, n), F32)],
        compiler_params=_cparams(("parallel",)),
        name="moe_plan",
    )(rs, cw)
    meta_flat = meta[:, 0:2, 0:MOE_EXPERTS].reshape(-1)
    wg, wu, wd = w_gate.astype(BF16), w_up.astype(BF16), w_down.astype(BF16)
    grid_spec = pltpu.PrefetchScalarGridSpec(
        num_scalar_prefetch=1,
        grid=(nt // MOE_SUB, MOE_EXPERTS // MOE_EPS),
        in_specs=[pl.BlockSpec((MOE_SUB * n, D_MODEL), lambda i, e, m: (i, 0)),
                  pl.BlockSpec((MOE_SUB * n, LANES), lambda i, e, m: (i, 0)),
                  pl.BlockSpec((MOE_SUB, 8, n), lambda i, e, m: (i, 0, 0)),
                  pl.BlockSpec((MOE_EPS, D_MODEL, MOE_FF), lambda i, e, m: (e, 0, 0)),
                  pl.BlockSpec((MOE_EPS, D_MODEL, MOE_FF), lambda i, e, m: (e, 0, 0)),
                  pl.BlockSpec((MOE_EPS, MOE_FF, D_MODEL), lambda i, e, m: (e, 0, 0))],
        out_specs=pl.BlockSpec((MOE_SUB * n, D_MODEL), lambda i, e, m: (i, 0)),
        scratch_shapes=[pltpu.VMEM((MOE_SUB, ns, D_MODEL), BF16), pltpu.VMEM((MOE_SUB, ns, D_MODEL), BF16)],
    )
    return pl.pallas_call(
        functools.partial(_moe_kernel, n=n, ns=ns),
        grid_spec=grid_spec,
        out_shape=jax.ShapeDtypeStruct((t, D_MODEL), BF16),
        compiler_params=_cparams(("parallel", "arbitrary")),
        name="moe_experts",
    )(meta_flat, h2, col, row, wg, wu, wd)


def _final_kernel(x_ref, y_ref, g_ref, o_ref):
    o_ref[...] = _rms(x_ref[...] + y_ref[...].astype(F32), g_ref[...])


def _final(x, y, g):
    t = x.shape[0]
    tok = pl.BlockSpec((TM, D_MODEL), lambda i: (i, 0))
    return pl.pallas_call(
        _final_kernel,
        grid=(t // TM,),
        in_specs=[tok, tok, pl.BlockSpec((1, D_MODEL), lambda i: (0, 0))],
        out_specs=tok,
        out_shape=jax.ShapeDtypeStruct((t, D_MODEL), F32),
        compiler_params=_cparams(("parallel",)),
        name="final_norm",
    )(x, y, g)


def kernel(x, positions, norm_mix, w_in, ssd_conv_w, ssd_conv_b, ssd_dt_bias, ssd_a_log, ssd_d, ssd_norm, mla_q_norm, mla_w_uq, mla_kv_norm, mla_w_ukv, fox_f_bias, w_out, norm_ffn, router_group_w, router_group_b, router_expert_w, router_expert_b, expert_w_gate, expert_w_up, expert_w_down, final_norm):
    b, s, d = x.shape
    assert d == D_MODEL and s % FA_B == 0 and (b * s) % (MOE_N * MOE_SUB) == 0 and TM == DSA_SC
    depth = w_in.shape[0]
    n_sel = min(IDX_TOPK_MAX, s // 4)
    tab = _rope_tables(positions)
    xf = x.reshape(b * s, d)
    y_ffn = None
    for l in range(depth):
        xf, p = _inproj(xf, y_ffn, norm_mix[l].reshape(1, d), _pack_w_in(w_in[l]), tab, fox_f_bias[l],
                        mla_q_norm[l], mla_kv_norm[l], mla_w_uq[l], mla_w_ukv[l], s)
        y_ssd = _ssd(p["zx"], p["gt"], ssd_conv_w[l], ssd_conv_b[l], ssd_dt_bias[l], ssd_a_log[l], ssd_d[l],
                     ssd_norm[l], b, s)
        y_dsa = _dsa(p["iq"], p["gt"], p["dq"], p["ik"], p["dk"], p["dv"], b, s, n_sel)
        y_mla = _flash(p["mq"], p["mk"], p["mv"], b, s, "flash_mla")
        y_fox = _flash(p["fq"], p["fk"], p["fv"], b, s, "flash_fox")
        xf, h2, cw, rs = _outproj(xf, (y_ssd, y_dsa, y_mla, y_fox), w_out[l], norm_ffn[l].reshape(1, d),
                                  router_group_w[l], router_group_b[l], router_expert_w[l], router_expert_b[l])
        y_ffn = _moe(h2, cw, rs, expert_w_gate[l], expert_w_up[l], expert_w_down[l])
    return _final(xf, y_ffn, final_norm.reshape(1, d)).reshape(b, s, d)
```

```python
import functools
import math

import jax
import jax.numpy as jnp
from jax import lax
from jax.experimental import pallas as pl
from jax.experimental.pallas import tpu as pltpu

F32 = jnp.float32
BF16 = jnp.bfloat16
I32 = jnp.int32
EPS = 1e-6
ROPE_THETA = 500000.0

D_MODEL = 1024
SSD_HEADS, SSD_HEAD_DIM, SSD_GROUPS, SSD_STATE, SSD_CONV = 4, 64, 2, 128, 4
SSD_D_INNER = SSD_HEADS * SSD_HEAD_DIM
SSD_CONV_DIM = SSD_D_INNER + 2 * SSD_GROUPS * SSD_STATE
DSA_HEADS, DSA_HEAD_DIM, IDX_HEADS, IDX_DIM, IDX_TOPK_MAX = 4, 64, 8, 64, 256
MLA_HEADS, MLA_Q_RANK, MLA_KV_RANK, MLA_NOPE, MLA_ROPE, MLA_V = 4, 256, 128, 64, 32, 64
FOX_HEADS, FOX_HEAD_DIM = 4, 64
MOE_GROUPS, MOE_EPG, MOE_EXPERTS, MOE_FF = 4, 8, 32, 256

LANES = 128
VMEM_LIMIT = 56 * 1024 * 1024
INT_MIN = -(2 ** 31)
NEG_BIG = -1e30

TM = 512
SSD_Q = 256
DSA_BQ = 256
DSA_SC = 512
FA_B = 512
MOE_N = 512
MOE_ALIGN = 16
MOE_CH = 64
MOE_EPS = 2
MOE_SUB = 4
MOE_NS = -(-(2 * MOE_N + MOE_EXPERTS * MOE_ALIGN + MOE_CH) // LANES) * LANES

C_ZX, C_GT, C_DQ, C_KV, C_IK, C_IQ, C_MC, C_FQ, C_FK, C_FV, C_END = (
    0, 1024, 1152, 1408, 1536, 1664, 2176, 2688, 3200, 3712, 4224)
G_FOX, G_DT, G_IW = 0, 4, 8
LOG2E = math.log2(math.e)
HEAD_W = 128
FA_HEADS = 4
VT_ROWS = 80
FOX_ONE_Q, FOX_F_Q, FOX_F_K, FOX_ONE_K = 64, 67, 64, 67


def _cparams(sem):
    return pltpu.CompilerParams(dimension_semantics=sem, vmem_limit_bytes=VMEM_LIMIT)


def _rms(xf, g):
    return xf * lax.rsqrt(jnp.mean(xf * xf, axis=-1, keepdims=True) + EPS) * g


def _softplus(x):
    return jnp.maximum(x, 0.0) + jnp.log1p(jnp.exp(-jnp.abs(x)))


def _silu(x):
    return x * jax.nn.sigmoid(x)


def _rope_apply(a, c, sa, sb, half):
    return a * c + pltpu.roll(a, LANES - half, 1) * sa + pltpu.roll(a, half, 1) * sb


def _dot_nt(a, b):
    return lax.dot_general(a, b, (((1,), (1,)), ((), ())), preferred_element_type=F32)


def _split3(x):
    hi = x.astype(BF16)
    r = x - hi.astype(F32)
    mid = r.astype(BF16)
    return hi, mid, (r - mid.astype(F32)).astype(BF16)


def _dot_f32_by_mask(mask, x, mask_left):
    m = jnp.where(mask, 1.0, 0.0).astype(BF16)
    out = None
    for piece in _split3(x):
        d = jnp.dot(m, piece, preferred_element_type=F32) if mask_left else jnp.dot(piece, m, preferred_element_type=F32)
        out = d if out is None else out + d
    return out


def _rope_tab_kernel(pos_ref, inv_ref, o_ref):
    pos = pos_ref[...].astype(F32)
    lane = lax.broadcasted_iota(I32, (1, LANES), 1)
    ang = pos * inv_ref[0:1, :]
    c, s = jnp.cos(ang), jnp.sin(ang)
    jd = lane % 64
    o_ref[:, 0:128] = c
    o_ref[:, 128:256] = jnp.where(jd < 8, -s, 0.0)
    o_ref[:, 256:384] = jnp.where((jd >= 8) & (jd < 16), s, 0.0)
    ang = pos * inv_ref[1:2, :]
    c, s = jnp.cos(ang), jnp.sin(ang)
    jm = lane - 64
    o_ref[:, 384:512] = c
    o_ref[:, 512:640] = jnp.where((jm >= 0) & (jm < 16), -s, 0.0)
    o_ref[:, 640:768] = jnp.where((jm >= 16) & (jm < 32), s, 0.0)


def _rope_tables(positions):
    t = positions.size
    half_d, half_m = DSA_HEAD_DIM // 4 // 2, MLA_ROPE // 2
    inv_d = ROPE_THETA ** (-jnp.arange(half_d, dtype=F32) / half_d)
    inv_m = ROPE_THETA ** (-jnp.arange(half_m, dtype=F32) / half_m)
    lane = jnp.arange(LANES)
    jd = lane % 64
    row_d = jnp.where(jd < 2 * half_d, inv_d[jd % half_d], 0.0)
    jm = lane - 64
    row_m = jnp.where((jm >= 0) & (jm < 2 * half_m), inv_m[jm % half_m], 0.0)
    inv = jnp.zeros((8, LANES), F32).at[0].set(row_d).at[1].set(row_m)
    tm = 1024
    return pl.pallas_call(
        _rope_tab_kernel,
        grid=(t // tm,),
        in_specs=[pl.BlockSpec((tm, 1), lambda i: (i, 0)), pl.BlockSpec((8, LANES), lambda i: (0, 0))],
        out_specs=pl.BlockSpec((tm, 768), lambda i: (i, 0)),
        out_shape=jax.ShapeDtypeStruct((t, 768), F32),
        compiler_params=_cparams(("parallel",)),
        name="rope_tables",
    )(positions.reshape(t, 1), inv)


def _inproj_kernel(*refs, has_add, tiles_per_seq):
    refs = list(refs)
    x_ref = refs.pop(0)
    yp_ref = refs.pop(0) if has_add else None
    g_ref, w_ref, tab_ref, fb_ref, qn_ref, kn_ref, wq_ref, wk_ref, wv_ref = refs[:9]
    refs = refs[9:]
    xo_ref = refs.pop(0) if has_add else None
    (zx_ref, gt_ref, dq_ref, dk_ref, dv_ref, ik_ref, iq_ref, mq_ref, mk_ref, mv_ref, fq_ref, fk_ref, fv_ref,
     carry_s) = refs
    x = x_ref[...]
    if has_add:
        x = x + yp_ref[...].astype(F32)
        xo_ref[...] = x
    h = _rms(x, g_ref[...]).astype(BF16)

    def proj(a, b):
        return jnp.dot(h, w_ref[:, a:b], preferred_element_type=F32)

    zx_ref[...] = proj(C_ZX, C_GT)
    gates = proj(C_GT, C_DQ)
    gt_ref[...] = gates
    c_d, sa_d, sb_d = tab_ref[:, 0:128], tab_ref[:, 128:256], tab_ref[:, 256:384]
    c_m, sa_m, sb_m = tab_ref[:, 384:512], tab_ref[:, 512:640], tab_ref[:, 640:768]
    q = proj(C_DQ, C_KV)
    for i in range(2):
        dq_ref[:, i * 128:(i + 1) * 128] = _rope_apply(q[:, i * 128:(i + 1) * 128], c_d, sa_d, sb_d, 8).astype(BF16)
    kv = proj(C_KV, C_IK)
    is_k = lax.broadcasted_iota(I32, (1, LANES), 1) < 64
    kvr = _rope_apply(kv, jnp.where(is_k, c_d, 1.0), jnp.where(is_k, sa_d, 0.0), jnp.where(is_k, sb_d, 0.0), 8)
    dk_ref[...] = kvr[:, 0:64].astype(BF16)
    dv_ref[0, 0:64, :] = kvr.astype(BF16).astype(F32).T[64:128, :].astype(BF16)
    dv_ref[0, 64:VT_ROWS, :] = jnp.ones((VT_ROWS - 64, kvr.shape[0]), BF16)
    ikb = proj(C_IK, C_IQ)
    ik_ref[...] = _rope_apply(ikb, c_d, sa_d, sb_d, 8)[:, 0:64].astype(BF16)
    iq = proj(C_IQ, C_MC)
    for i in range(4):
        iq_ref[:, i * 128:(i + 1) * 128] = _rope_apply(iq[:, i * 128:(i + 1) * 128], c_d, sa_d, sb_d, 8).astype(BF16)
    ones_hi = jnp.where(lax.broadcasted_iota(I32, (1, FA_HEADS * HEAD_W), 1) % HEAD_W >= 64, 1.0, 0.0)

    mc = proj(C_MC, C_FQ)
    cq = _rms(mc[:, 0:256], qn_ref[...]).astype(BF16)
    ckv = _rms(mc[:, 256:384], kn_ref[...]).astype(BF16)
    kr = _rope_apply(mc[:, 384:512], c_m, sa_m, sb_m, 16)
    mq = jnp.dot(cq, wq_ref[...], preferred_element_type=F32)
    mk = jnp.dot(ckv, wk_ref[...], preferred_element_type=F32)
    for hd in range(MLA_HEADS):
        sl = slice(hd * HEAD_W, (hd + 1) * HEAD_W)
        mq_ref[:, sl] = _rope_apply(mq[:, sl], c_m, sa_m, sb_m, 16).astype(BF16)
        mk_ref[:, sl] = (mk[:, sl] + kr).astype(BF16)
    mv_ref[...] = (jnp.dot(ckv, wv_ref[...], preferred_element_type=F32) + ones_hi).astype(BF16)

    @pl.when(pl.program_id(0) % tiles_per_seq == 0)
    def _():
        carry_s[...] = jnp.zeros_like(carry_s)

    tm = x.shape[0]
    log_f = -_softplus(-(gates + fb_ref[...]))
    ri = lax.broadcasted_iota(I32, (tm, tm), 0)
    ci = lax.broadcasted_iota(I32, (tm, tm), 1)
    cs = _dot_f32_by_mask(ri >= ci, log_f, True) + carry_s[0:1, :]
    carry_s[0:1, :] = cs[tm - 1:tm, :]
    f2 = cs * LOG2E
    lane = lax.broadcasted_iota(I32, (1, HEAD_W), 1)

    def pieces(f, base):
        hi, mid, lo = _split3(f)
        return jnp.where(lane == base, hi.astype(F32),
                         jnp.where(lane == base + 1, mid.astype(F32), jnp.where(lane == base + 2, lo.astype(F32), 0.0)))

    def ones(base):
        return jnp.where((lane >= base) & (lane < base + 3), 1.0, 0.0)

    fq = proj(C_FQ, C_FK)
    fk = proj(C_FK, C_FV)
    for hd in range(FOX_HEADS):
        fh = f2[:, G_FOX + hd:G_FOX + hd + 1]
        sl = slice(hd * HEAD_W, (hd + 1) * HEAD_W)
        fq_ref[:, sl] = (fq[:, sl] + pieces(fh, FOX_F_Q) + ones(FOX_ONE_Q)).astype(BF16)
        fk_ref[:, sl] = (fk[:, sl] - pieces(fh, FOX_F_K) + ones(FOX_ONE_K)).astype(BF16)
    fv_ref[...] = (proj(C_FV, C_END) + ones_hi).astype(BF16)


def _pack_w_in(w):
    d = w.shape[0]
    z = lambda n: jnp.zeros((d, n), F32)
    o = 0
    parts = {}
    for name, width in (("z", 256), ("xbc", 768), ("dt", 4), ("dq", 256), ("dk", 64), ("dv", 64), ("iq", 512),
                        ("ik", 64), ("iw", 8), ("cq", 256), ("ckv", 128), ("kr", 32), ("fq", 256), ("fk", 256),
                        ("fv", 256), ("ff", 4)):
        parts[name] = w[:, o:o + width]
        o += width
    scale = DSA_HEAD_DIM ** -0.5

    def per_head(a):
        return jnp.pad(a.reshape(d, FA_HEADS, 64), ((0, 0), (0, 0), (0, HEAD_W - 64))).reshape(d, FA_HEADS * HEAD_W)

    cat = jnp.concatenate([
        parts["z"], parts["xbc"],
        parts["ff"], parts["dt"], parts["iw"], z(112),
        parts["dq"] * (scale * LOG2E),
        parts["dk"], parts["dv"],
        parts["ik"], z(64),
        parts["iq"],
        parts["cq"], parts["ckv"], z(64), parts["kr"], z(32),
        per_head(parts["fq"] * (FOX_HEAD_DIM ** -0.5 * LOG2E)), per_head(parts["fk"]), per_head(parts["fv"])], axis=1)
    return cat.astype(BF16)


def _mla_weights(w_uq, w_ukv):
    dqk = MLA_NOPE + MLA_ROPE
    wq = jnp.pad(w_uq.reshape(MLA_Q_RANK, MLA_HEADS, dqk) * (dqk ** -0.5 * LOG2E), ((0, 0), (0, 0), (0, HEAD_W - dqk)))
    wkv = w_ukv.reshape(MLA_KV_RANK, MLA_HEADS, MLA_NOPE + MLA_V)
    wk = jnp.pad(wkv[:, :, :MLA_NOPE], ((0, 0), (0, 0), (0, HEAD_W - MLA_NOPE)))
    wv = jnp.pad(wkv[:, :, MLA_NOPE:], ((0, 0), (0, 0), (0, HEAD_W - MLA_V)))
    flat = lambda a: a.reshape(a.shape[0], MLA_HEADS * HEAD_W).astype(BF16)
    return flat(wq), flat(wk), flat(wv)


def _inproj(x, y_prev, g, w_cat, tab, f_bias, q_norm, kv_norm, w_uq, w_ukv, s):
    t = x.shape[0]
    has_add = y_prev is not None
    tok = lambda n: pl.BlockSpec((TM, n), lambda i: (i, 0))
    full = lambda a: pl.BlockSpec(a.shape, lambda i: (0,) * a.ndim)
    fb = jnp.zeros((1, LANES), F32).at[0, G_FOX:G_FOX + FOX_HEADS].set(f_bias)
    params = [g, w_cat, tab, fb, q_norm.reshape(1, -1), kv_norm.reshape(1, -1), *_mla_weights(w_uq, w_ukv)]
    ins = [x] + ([y_prev] if has_add else []) + params
    in_specs = ([tok(D_MODEL)] + ([tok(D_MODEL)] if has_add else [])
                + [tok(768) if a is tab else full(a) for a in params])
    outs = ([("x", D_MODEL, F32)] if has_add else []) + [
        ("zx", 1024, F32), ("gt", 128, F32), ("dq", 256, BF16), ("dk", 64, BF16), ("dv", 64, BF16),
        ("ik", 64, BF16), ("iq", 512, BF16), ("mq", 512, BF16), ("mk", 512, BF16), ("mv", 512, BF16),
        ("fq", 512, BF16), ("fk", 512, BF16), ("fv", 512, BF16)]
    vt_rows = {"dv": VT_ROWS}
    res = pl.pallas_call(
        functools.partial(_inproj_kernel, has_add=has_add, tiles_per_seq=s // TM),
        grid=(t // TM,),
        in_specs=in_specs,
        out_specs=[pl.BlockSpec((1, vt_rows[nm], TM), lambda i: (i, 0, 0)) if nm in vt_rows else tok(n)
                   for nm, n, _ in outs],
        out_shape=[jax.ShapeDtypeStruct((t // TM, vt_rows[nm], TM) if nm in vt_rows else (t, n), dt)
                   for nm, n, dt in outs],
        scratch_shapes=[pltpu.VMEM((8, LANES), F32)],
        compiler_params=_cparams(("arbitrary",)),
        name="inproj",
    )(*ins)
    res = list(res)
    x_new = res.pop(0) if has_add else x
    return x_new, dict(zip([n for n, _, _ in outs if n != "x"], res))


def _ssd_kernel(zx_ref, gt_ref, cw_ref, cb_ref, pr_ref, pc_ref, drow_ref, nw_ref, o_ref, ext_s, st_s, y_s, *, q):
    @pl.when(pl.program_id(1) == 0)
    def _():
        ext_s[0:8, :] = jnp.zeros((8, SSD_CONV_DIM), F32)
        st_s[...] = jnp.zeros_like(st_s)

    raw = zx_ref[:, 256:1024]
    ext_s[8:8 + q, :] = raw
    acc = jnp.broadcast_to(cb_ref[...], (q, SSD_CONV_DIM))
    for j in range(SSD_CONV):
        acc = acc + cw_ref[j:j + 1, :] * ext_s[5 + j:5 + j + q, :]
    ext_s[0:8, :] = raw[q - 8:q, :]
    xbc = _silu(acc)
    xs = xbc[:, 0:SSD_D_INNER]

    g = gt_ref[...]
    lane = lax.broadcasted_iota(I32, (1, LANES), 1)
    dtc = _softplus(g + pr_ref[0:1, :])
    a_r = jnp.where((lane >= G_DT) & (lane < G_DT + SSD_HEADS), -jnp.exp(pr_ref[1:2, :]), 0.0)
    ri = lax.broadcasted_iota(I32, (q, q), 0)
    ci = lax.broadcasted_iota(I32, (q, q), 1)
    tri = ri >= ci
    acs_c = _dot_f32_by_mask(tri, dtc * a_r, True)
    sub = lax.broadcasted_iota(I32, (LANES, 1), 0)
    dtr = _softplus(g.T + pc_ref[:, 0:1])
    a_c = jnp.where((sub >= G_DT) & (sub < G_DT + SSD_HEADS), -jnp.exp(pc_ref[:, 1:2]), 0.0)
    acs_r = _dot_f32_by_mask(ri <= ci, dtr * a_c, False)

    rep = SSD_HEADS // SSD_GROUPS
    for gi in range(SSD_GROUPS):
        bg = xbc[:, SSD_D_INNER + gi * SSD_STATE:SSD_D_INNER + (gi + 1) * SSD_STATE]
        cg = xbc[:, SSD_D_INNER + (SSD_GROUPS + gi) * SSD_STATE:SSD_D_INNER + (SSD_GROUPS + gi + 1) * SSD_STATE]
        bt = bg.T.astype(BF16)
        cb16 = cg.astype(BF16)
        cbm = jnp.dot(cb16, bt, preferred_element_type=F32)
        for hh in range(rep):
            h = gi * rep + hh
            ac = acs_c[:, G_DT + h:G_DT + h + 1]
            ar = acs_r[G_DT + h:G_DT + h + 1, :]
            seg = jnp.where(tri, jnp.exp(ac - ar), 0.0)
            xh = xs[:, h * SSD_HEAD_DIM:(h + 1) * SSD_HEAD_DIM]
            xdt = xh * dtc[:, G_DT + h:G_DT + h + 1]
            yd = jnp.dot((cbm * seg).astype(BF16), xdt.astype(BF16), preferred_element_type=F32)
            aend = ac[q - 1:q, :]
            st = st_s[h]
            yo = jnp.dot(cb16, st.astype(BF16), preferred_element_type=F32) * jnp.exp(ac)
            st_s[h] = st * jnp.exp(aend) + jnp.dot(bt, (xdt * jnp.exp(aend - ac)).astype(BF16),
                                                   preferred_element_type=F32)
            y_s[:, h * SSD_HEAD_DIM:(h + 1) * SSD_HEAD_DIM] = yd + yo
    y = y_s[...] + drow_ref[...] * xs
    y = y * _silu(zx_ref[:, 0:SSD_D_INNER])
    o_ref[...] = _rms(y, nw_ref[...]).astype(BF16)


def _ssd(zx, gt, conv_w, conv_b, dt_bias, a_log, d_skip, norm_w, b, s):
    q = SSD_Q
    nc = s // q
    pr = jnp.zeros((8, LANES), F32).at[0, G_DT:G_DT + SSD_HEADS].set(dt_bias).at[1, G_DT:G_DT + SSD_HEADS].set(a_log)
    pc = pr.T
    drow = jnp.repeat(d_skip, SSD_HEAD_DIM).reshape(1, SSD_D_INNER)
    cb = conv_b.reshape(1, -1)
    nw = norm_w.reshape(1, -1)
    full = lambda a: pl.BlockSpec(a.shape, lambda i, j: (0,) * a.ndim)
    return pl.pallas_call(
        functools.partial(_ssd_kernel, q=q),
        grid=(b, nc),
        in_specs=[pl.BlockSpec((q, 1024), lambda i, j: (i * nc + j, 0)), pl.BlockSpec((q, LANES), lambda i, j: (i * nc + j, 0)),
                  full(conv_w), full(cb), full(pr), full(pc), full(drow), full(nw)],
        out_specs=pl.BlockSpec((q, SSD_D_INNER), lambda i, j: (i * nc + j, 0)),
        out_shape=jax.ShapeDtypeStruct((b * s, SSD_D_INNER), BF16),
        scratch_shapes=[pltpu.VMEM((8 + q, SSD_CONV_DIM), F32), pltpu.VMEM((SSD_HEADS, SSD_STATE, SSD_HEAD_DIM), F32),
                        pltpu.VMEM((q, SSD_D_INNER), F32)],
        compiler_params=_cparams(("parallel", "arbitrary")),
        name="ssd",
    )(zx, gt, conv_w, cb, pr, pc, drow, nw)


def _key_to_f32(k):
    return pltpu.bitcast(jnp.where(k < 0, k ^ jnp.int32(0x7FFFFFFF), k), F32)


def _dsa_kernel(iq_ref, gt_ref, q_ref, ik_ref, k_ref, vt_ref, o_ref, sc_s, qi_s, qs_s, thr_s, s16_s,
                *, bq, sc_w, n_sel):
    qb = pl.program_id(1)
    n_sc = ((qb + 1) * bq + sc_w - 1) // sc_w
    for j in range(IDX_HEADS):
        qi_s[j // 2, (j % 2) * bq:(j % 2 + 1) * bq, :] = iq_ref[:, j * IDX_DIM:(j + 1) * IDX_DIM]
    for h in range(DSA_HEADS):
        qs_s[h * bq:(h + 1) * bq, :] = q_ref[:, h * DSA_HEAD_DIM:(h + 1) * DSA_HEAD_DIM]
    w_t = (gt_ref[...] * ((IDX_HEADS * IDX_DIM) ** -0.5)).T
    qpos = qb * bq + lax.broadcasted_iota(I32, (1, bq), 1)

    def score_body(sc, carry):
        k0 = pl.multiple_of(sc * sc_w, sc_w)
        kidx = ik_ref[pl.ds(k0, sc_w), :]
        sco = jnp.zeros((sc_w, bq), F32)
        for jp in range(IDX_HEADS // 2):
            a = _dot_nt(kidx, qi_s[jp])
            for u in range(2):
                j = 2 * jp + u
                sco = sco + jnp.maximum(a[:, u * bq:(u + 1) * bq], 0.0) * w_t[G_IW + j:G_IW + j + 1, :]
        kpos = k0 + lax.broadcasted_iota(I32, (sc_w, 1), 0)
        sco = jnp.where(kpos <= qpos, sco, -jnp.inf)
        sc_s[sc] = sco
        s16_s[sc] = sco.astype(BF16)
        return carry

    lax.fori_loop(0, n_sc, score_body, 0)

    def count(pred):
        def body(sc, acc):
            for r in range(sc_w // 64):
                acc = acc + jnp.where(pred(sc_s[sc, r * 64:(r + 1) * 64, :]), 1.0, 0.0)
            return acc
        acc = lax.fori_loop(0, n_sc, body, jnp.zeros((64, bq), F32))
        return jnp.sum(acc, axis=0, keepdims=True)

    thr_s[0:1, :] = jnp.full((1, bq), -jnp.finfo(jnp.float32).max, F32)

    @pl.when((qb + 1) * bq > n_sel)
    def _():
        active = qpos + 1 > n_sel

        one16, zero16 = jnp.ones((), BF16), jnp.zeros((), BF16)

        def count16(c16):
            def body(sc, acc):
                a16 = jnp.zeros((64, bq), BF16)
                for r in range(sc_w // 64):
                    a16 = a16 + jnp.where(s16_s[sc, r * 64:(r + 1) * 64, :] >= c16, one16, zero16)
                return acc + a16.astype(F32)
            acc = lax.fori_loop(0, n_sc, body, jnp.zeros((64, bq), F32))
            return jnp.sum(acc, axis=0, keepdims=True)

        def coarse(it, lo):
            cand = lo ^ jnp.left_shift(jnp.int32(1), 31 - it)
            return jnp.where(count16(_key_to_f32(cand).astype(BF16)) >= n_sel, cand, lo)

        lo16 = lax.fori_loop(0, 16, coarse, jnp.full((1, bq), INT_MIN, I32))
        base = jnp.maximum(lo16, jnp.int32(INT_MIN + (1 << 16))) - jnp.int32(1 << 16)

        def cond(st):
            it, off, cnt_lo = st
            pending = active & (cnt_lo != n_sel)
            return (it < 32) & (jnp.max(jnp.where(pending, 1.0, 0.0)) > 0.0)

        def fine(it, off, cnt_lo):
            cand = off | jnp.left_shift(jnp.int32(1), 31 - it)
            thr_c = _key_to_f32(base + cand)
            cnt = count(lambda t: t >= thr_c)
            ok = cnt >= n_sel
            return jnp.where(ok, cand, off), jnp.where(ok, cnt, cnt_lo)

        def body(st):
            it, off, cnt_lo = st
            for u in range(4):
                off, cnt_lo = fine(it + u, off, cnt_lo)
            return it + 4, off, cnt_lo

        off, cnt_lo = fine(15, jnp.zeros((1, bq), I32), jnp.full((1, bq), 3.0e38, F32))
        _, off, cnt_lo = lax.while_loop(cond, body, (jnp.int32(16), off, cnt_lo))
        lo = base + off
        thr = _key_to_f32(lo)
        tie = active & (cnt_lo > n_sel)

        @pl.when(jnp.max(jnp.where(tie, 1.0, 0.0)) > 0.0)
        def _():
            need = n_sel - count(lambda t: t > thr)
            ri = lax.broadcasted_iota(I32, (sc_w, sc_w), 0)
            ci = lax.broadcasted_iota(I32, (sc_w, sc_w), 1)
            lower = jnp.where(ri >= ci, 1.0, 0.0).astype(BF16)

            def tie_body(sc, run):
                t = sc_s[sc]
                eq = (t == thr) & tie
                pre = jnp.dot(lower, jnp.where(eq, 1.0, 0.0).astype(BF16), preferred_element_type=F32)
                sc_s[sc] = jnp.where(eq & (run + pre > need), -jnp.inf, t)
                return run + pre[sc_w - 1:sc_w, :]

            lax.fori_loop(0, n_sc, tie_body, jnp.zeros((1, bq), F32))

        thr_s[0:1, :] = jnp.where(active, thr, thr_s[0:1, :])

    thr = thr_s[0:1, :]

    hq = DSA_HEADS * bq

    def att_body(sc, carry):
        m_prev, acc = carry
        k0 = pl.multiple_of(sc * sc_w, sc_w)
        st = _dot_nt(k_ref[pl.ds(k0, sc_w), :], qs_s[...])
        drop = jnp.where(sc_s[sc] >= thr, 0.0, 2 * NEG_BIG)
        st = st + jnp.tile(drop, (1, DSA_HEADS))
        m_cur = jnp.max(jnp.max(st.reshape(sc_w // 64, 64, hq), axis=0), axis=0, keepdims=True)
        m_next = jnp.maximum(m_prev, m_cur)
        p = jnp.exp2(st - m_next)
        pv = jnp.dot(vt_ref[sc], p.astype(BF16), preferred_element_type=F32)
        return m_next, jnp.exp2(m_prev - m_next) * acc + pv

    def att_body2(i, carry):
        return att_body(2 * i + 1, att_body(2 * i, carry))

    carry = lax.fori_loop(0, n_sc // 2, att_body2, (jnp.full((1, hq), NEG_BIG, F32), jnp.zeros((VT_ROWS, hq), F32)))
    _, acc = lax.cond(n_sc % 2 == 1, lambda c: att_body(n_sc - 1, c), lambda c: c, carry)
    out_t = jnp.concatenate([acc[0:DSA_HEAD_DIM, :] / acc[DSA_HEAD_DIM:DSA_HEAD_DIM + 1, :],
                             jnp.zeros((LANES - DSA_HEAD_DIM, hq), F32)], axis=0).T
    for h in range(DSA_HEADS):
        o_ref[:, h * DSA_HEAD_DIM:(h + 1) * DSA_HEAD_DIM] = out_t[h * bq:(h + 1) * bq, 0:DSA_HEAD_DIM].astype(BF16)


def _dsa(iq, gt, dq, ik, dk, dvt, b, s, n_sel):
    bq, sc_w = DSA_BQ, DSA_SC
    nq = s // bq
    qblk = lambda n: pl.BlockSpec((bq, n), lambda i, j: (i * nq + j, 0))
    kblk = pl.BlockSpec((s, 64), lambda i, j: (i, 0))
    return pl.pallas_call(
        functools.partial(_dsa_kernel, bq=bq, sc_w=sc_w, n_sel=n_sel),
        grid=(b, nq),
        in_specs=[qblk(512), qblk(LANES), qblk(256), kblk, kblk,
                  pl.BlockSpec((s // sc_w, VT_ROWS, sc_w), lambda i, j: (i, 0, 0))],
        out_specs=qblk(256),
        out_shape=jax.ShapeDtypeStruct((b * s, 256), BF16),
        scratch_shapes=[pltpu.VMEM((s // sc_w, sc_w, bq), F32), pltpu.VMEM((IDX_HEADS // 2, 2 * bq, IDX_DIM), BF16),
                        pltpu.VMEM((DSA_HEADS * bq, DSA_HEAD_DIM), BF16), pltpu.VMEM((8, bq), F32),
                        pltpu.VMEM((s // sc_w, sc_w, bq), BF16)],
        compiler_params=_cparams(("parallel", "arbitrary")),
        name="dsa",
    )(iq, gt, dq, ik, dk, dvt)


def _flash_kernel(qt_ref, kt_ref, *refs, blk, sets):
    qkv, (o_ref, m_s, acc_s) = refs[:3 * sets], refs[3 * sets:]
    t = pl.program_id(1)
    qi, ki = qt_ref[t], kt_ref[t]

    @pl.when(ki == 0)
    def _():
        m_s[...] = jnp.full(m_s.shape, NEG_BIG, F32)
        acc_s[...] = jnp.zeros_like(acc_s)

    def step(diagonal):
        if diagonal:
            keep = lax.broadcasted_iota(I32, (blk, blk), 0) >= lax.broadcasted_iota(I32, (blk, blk), 1)
        for g in range(sets):
            q_ref, k_ref, v_ref = qkv[3 * g:3 * g + 3]
            for h in range(FA_HEADS):
                sl = slice(h * HEAD_W, (h + 1) * HEAD_W)
                hh = g * FA_HEADS + h
                s = _dot_nt(q_ref[:, sl], k_ref[:, sl])
                if diagonal:
                    s = jnp.where(keep, s, 2 * NEG_BIG)
                m_prev = m_s[hh]
                m_next = jnp.maximum(m_prev, jnp.max(s, axis=1, keepdims=True))
                p = jnp.exp2(s - jnp.tile(m_next, (1, blk // LANES)))
                pv = jnp.dot(p.astype(BF16), v_ref[:, sl], preferred_element_type=F32)
                acc_s[hh] = jnp.exp2(m_prev - m_next) * acc_s[hh] + pv
                m_s[hh] = m_next

    @pl.when(ki < qi)
    def _():
        step(False)

    @pl.when(ki == qi)
    def _():
        step(True)
        for hh in range(sets * FA_HEADS):
            acc = acc_s[hh]
            o_ref[:, hh * 64:(hh + 1) * 64] = (acc / pltpu.roll(acc, 64, 1))[:, 0:64].astype(BF16)


def _flash(qkv, b, s):
    blk = FA_B
    nb = s // blk
    pairs = [(i, j) for i in range(nb) for j in range(i + 1)]
    qt = jnp.asarray([p[0] for p in pairs], I32)
    kt = jnp.asarray([p[1] for p in pairs], I32)
    w = FA_HEADS * HEAD_W
    sets = len(qkv)
    q_spec = pl.BlockSpec((blk, w), lambda i, t, qt, kt: (i * nb + qt[t], 0))
    kv_spec = pl.BlockSpec((blk, w), lambda i, t, qt, kt: (i * nb + kt[t], 0))
    grid_spec = pltpu.PrefetchScalarGridSpec(
        num_scalar_prefetch=2,
        grid=(b, len(pairs)),
        in_specs=[q_spec, kv_spec, kv_spec] * sets,
        out_specs=pl.BlockSpec((blk, sets * FA_HEADS * 64), lambda i, t, qt, kt: (i * nb + qt[t], 0)),
        scratch_shapes=[pltpu.VMEM((sets * FA_HEADS, blk, LANES), F32), pltpu.VMEM((sets * FA_HEADS, blk, HEAD_W), F32)],
    )
    return pl.pallas_call(
        functools.partial(_flash_kernel, blk=blk, sets=sets),
        grid_spec=grid_spec,
        out_shape=jax.ShapeDtypeStruct((b * s, sets * FA_HEADS * 64), BF16),
        compiler_params=_cparams(("parallel", "arbitrary")),
        name="flash",
    )(qt, kt, *[a for triple in qkv for a in triple])


def _outproj_kernel(x_ref, *refs, n_mix):
    y_refs, (wo_ref, g_ref, wr_ref, br_ref, xo_ref, h_ref, cw_ref, rs_ref) = refs[:n_mix], refs[n_mix:]
    acc = x_ref[...]
    row = 0
    for y_ref in y_refs:
        acc = acc + jnp.dot(y_ref[...], wo_ref[row:row + y_ref.shape[1], :], preferred_element_type=F32)
        row += y_ref.shape[1]
    xo_ref[...] = acc
    h2 = _rms(acc, g_ref[...])
    h_ref[...] = h2.astype(BF16)
    h_hi, h_lo, _ = _split3(h2)
    lg = (jnp.dot(h_hi, wr_ref[0], preferred_element_type=F32) + jnp.dot(h_lo, wr_ref[0], preferred_element_type=F32)
          + jnp.dot(h_hi, wr_ref[1], preferred_element_type=F32) + br_ref[...])
    lane = lax.broadcasted_iota(I32, (1, LANES), 1)
    lanef = lane.astype(F32)
    is_g = (lane >= MOE_EXPERTS) & (lane < MOE_EXPERTS + MOE_GROUPS)
    gl = jnp.where(is_g, lg, -jnp.inf)
    gmax = jnp.max(gl, axis=1, keepdims=True)
    gidx = jnp.min(jnp.where(gl == gmax, lanef, 999.0), axis=1, keepdims=True) - MOE_EXPERTS
    g_p = 1.0 / jnp.sum(jnp.where(is_g, jnp.exp(gl - gmax), 0.0), axis=1, keepdims=True)
    in_g = (lane < MOE_EXPERTS) & (jnp.floor(lanef * (1.0 / MOE_EPG)) == gidx)
    el = jnp.where(in_g, lg, -jnp.inf)
    m1 = jnp.max(el, axis=1, keepdims=True)
    i1 = jnp.min(jnp.where(el == m1, lanef, 999.0), axis=1, keepdims=True)
    el2 = jnp.where(lanef == i1, -jnp.inf, el)
    m2 = jnp.max(el2, axis=1, keepdims=True)
    i2 = jnp.min(jnp.where(el2 == m2, lanef, 999.0), axis=1, keepdims=True)
    t = jnp.exp(m2 - m1)
    w1 = 1.0 / (1.0 + t)
    cw_ref[...] = jnp.where(lanef == i1, w1 * g_p, jnp.where(lanef == i2, t * w1 * g_p, 0.0))
    rs_ref[...] = jnp.where((lanef == i1) | (lanef == i2), 1.0, 0.0).astype(BF16)


def _outproj(x, ys, w_out, g, w_rg, b_rg, w_re, b_re):
    t = x.shape[0]
    wo = w_out.astype(BF16)
    wr = jnp.zeros((D_MODEL, LANES), F32).at[:, 0:MOE_EXPERTS].set(w_re).at[:, MOE_EXPERTS:MOE_EXPERTS + MOE_GROUPS].set(w_rg)
    wr_hi = wr.astype(BF16)
    wr = jnp.stack([wr_hi, (wr - wr_hi.astype(F32)).astype(BF16)])
    br =jnp.zeros((1, LANES), F32).at[0, 0:MOE_EXPERTS].set(b_re).at[0, MOE_EXPERTS:MOE_EXPERTS + MOE_GROUPS].set(b_rg)
    tok = lambda n: pl.BlockSpec((TM, n), lambda i: (i, 0))
    full = lambda a: pl.BlockSpec(a.shape, lambda i: (0,) * a.ndim)
    return pl.pallas_call(
        functools.partial(_outproj_kernel, n_mix=len(ys)),
        grid=(t // TM,),
        in_specs=[tok(D_MODEL)] + [tok(y.shape[1]) for y in ys] + [full(wo), full(g), full(wr), full(br)],
        out_specs=[tok(D_MODEL), tok(D_MODEL), tok(LANES), tok(LANES)],
        out_shape=[jax.ShapeDtypeStruct((t, D_MODEL), F32), jax.ShapeDtypeStruct((t, D_MODEL), BF16),
                   jax.ShapeDtypeStruct((t, LANES), F32), jax.ShapeDtypeStruct((t, LANES), BF16)],
        compiler_params=_cparams(("parallel",)),
        name="outproj_router",
    )(x, *ys, wo, g, wr, br)


def _moe_plan_kernel(rs_ref, cw_ref, meta_ref, col_ref, row_ref, *, n):
    sel = rs_ref[...]
    self32 = sel.astype(F32)
    ti = lax.broadcasted_iota(I32, (n, n), 0)
    tj = lax.broadcasted_iota(I32, (n, n), 1)
    rank = jnp.dot(jnp.where(ti > tj, 1.0, 0.0).astype(BF16), sel, preferred_element_type=F32)
    cnt = jnp.sum(self32, axis=0, keepdims=True).astype(I32)
    cpad = ((cnt + (MOE_ALIGN - 1)) // MOE_ALIGN) * MOE_ALIGN
    li = lax.broadcasted_iota(I32, (LANES, LANES), 0)
    lj = lax.broadcasted_iota(I32, (LANES, LANES), 1)
    off = _dot_f32_by_mask(li < lj, jnp.broadcast_to(cpad.astype(F32), (8, LANES)), False)[0:1, :]
    dest = off + rank
    lane = lax.broadcasted_iota(I32, (1, LANES), 1)
    lanef = lane.astype(F32)
    on = self32 > 0.5
    e_lo = jnp.min(jnp.where(on, lanef, 999.0), axis=1, keepdims=True)
    e_hi = jnp.max(jnp.where(on, lanef, -1.0), axis=1, keepdims=True)
    is_lo, is_hi = lanef == e_lo, lanef == e_hi
    cw = cw_ref[...]
    pick = lambda m, a: jnp.sum(jnp.where(m, a, 0.0), axis=1, keepdims=True)
    d_lo, d_hi, w_lo, w_hi = pick(is_lo, dest), pick(is_hi, dest), pick(is_lo, cw), pick(is_hi, cw)
    colv = jnp.where(lane == 0, d_lo, jnp.where(lane == 1, d_hi, jnp.where(lane == 2, w_lo, jnp.where(lane == 3, w_hi, 0.0))))
    col_ref[...] = colv
    row_ref[0] = colv.T[0:8, :]
    sub = lax.broadcasted_iota(I32, (8, LANES), 0)
    meta_ref[0] = jnp.where(sub == 0, off.astype(I32), jnp.where(sub == 1, cnt, 0))


def _moe_kernel(meta_ref, h_ref, col_ref, row_ref, wg_ref, wu_ref, wd_ref, o_ref, s_s, y_s, *, n, ns):
    t, e = pl.program_id(0), pl.program_id(1)

    @pl.when(e == 0)
    def _():
        for g in range(MOE_SUB):
            d_lo, d_hi = row_ref[g, 0:1, :], row_ref[g, 1:2, :]
            hg = h_ref[g * n:(g + 1) * n, :]
            for r in range(ns // LANES):
                si = (r * LANES + lax.broadcasted_iota(I32, (LANES, 1), 0)).astype(F32)
                perm = jnp.where((si == d_lo) | (si == d_hi), 1.0, 0.0).astype(BF16)
                s_s[g, r * LANES:(r + 1) * LANES, :] = jnp.dot(perm, hg, preferred_element_type=F32).astype(BF16)
        y_s[...] = jnp.zeros_like(y_s)

    def mlp(j, starts, store_ok):
        starts = [pl.multiple_of(r0, MOE_ALIGN) for r0 in starts]
        xs = jnp.concatenate([s_s[g, pl.ds(starts[g], MOE_CH), :] for g in range(MOE_SUB)], axis=0)
        gate = jnp.dot(xs, wg_ref[j], preferred_element_type=F32)
        up = jnp.dot(xs, wu_ref[j], preferred_element_type=F32)
        hid = (_silu(gate) * up).astype(BF16)
        y = jnp.dot(hid, wd_ref[j], preferred_element_type=F32).astype(BF16)
        for g in range(MOE_SUB):
            def store(g=g):
                y_s[g, pl.ds(starts[g], MOE_CH), :] = y[g * MOE_CH:(g + 1) * MOE_CH, :]
            if store_ok is None:
                store()
            else:
                pl.when(store_ok[g])(store)

    def meta(g, j, what):
        return meta_ref[(t * MOE_SUB + g) * 2 * MOE_EXPERTS + what * MOE_EXPERTS + e * MOE_EPS + j]

    offs = [[meta(g, j, 0) for g in range(MOE_SUB)] for j in range(MOE_EPS)]
    cnts = [[meta(g, j, 1) for g in range(MOE_SUB)] for j in range(MOE_EPS)]
    for j in range(MOE_EPS):
        mlp(j, offs[j], None)
    for j in range(MOE_EPS):
        ends = [offs[j][g] + ((cnts[j][g] + MOE_ALIGN - 1) // MOE_ALIGN) * MOE_ALIGN for g in range(MOE_SUB)]
        nch = [(cnts[j][g] + MOE_CH - 1) // MOE_CH for g in range(MOE_SUB)]

        def body(i, carry, j=j, ends=ends, nch=nch):
            more = [i < nch[g] for g in range(MOE_SUB)]
            starts = [jnp.where(more[g], jnp.minimum(offs[j][g] + i * MOE_CH, ends[g] - MOE_CH), offs[j][g])
                      for g in range(MOE_SUB)]
            mlp(j, starts, more)
            return carry

        lax.fori_loop(1, functools.reduce(jnp.maximum, nch), body, 0)

    @pl.when(e == pl.num_programs(1) - 1)
    def _():
        rows = 256
        kc = ns // next(k for k in (3, 2, 1) if ns % (k * LANES) == 0)
        for g in range(MOE_SUB):
            for c in range(n // rows):
                r0 = g * n + c * rows
                cv = col_ref[r0:r0 + rows, :]
                d_lo, d_hi, w_lo, w_hi = cv[:, 0:1], cv[:, 1:2], cv[:, 2:3], cv[:, 3:4]
                acc = jnp.zeros((rows, D_MODEL), F32)
                for r in range(ns // kc):
                    si = (r * kc + lax.broadcasted_iota(I32, (1, kc), 1)).astype(F32)
                    pw = (jnp.where(si == d_lo, w_lo, 0.0) + jnp.where(si == d_hi, w_hi, 0.0)).astype(BF16)
                    acc = acc + jnp.dot(pw, y_s[g, r * kc:(r + 1) * kc, :], preferred_element_type=F32)
                o_ref[r0:r0 + rows, :] = acc.astype(BF16)


def _moe(h2, cw, rs, w_gate, w_up, w_down):
    t = h2.shape[0]
    n, ns = MOE_N, MOE_NS
    nt = t // n
    meta, col, row = pl.pallas_call(
        functools.partial(_moe_plan_kernel, n=n),
        grid=(nt,),
        in_specs=[pl.BlockSpec((n, LANES), lambda i: (i, 0)), pl.BlockSpec((n, LANES), lambda i: (i, 0))],
        out_specs=[pl.BlockSpec((1, 8, LANES), lambda i: (i, 0, 0)), pl.BlockSpec((n, LANES), lambda i: (i, 0)),
                   pl.BlockSpec((1, 8, n), lambda i: (i, 0, 0))],
        out_shape=[jax.ShapeDtypeStruct((nt, 8, LANES), I32), jax.ShapeDtypeStruct((t, LANES), F32),
                   jax.ShapeDtypeStruct((nt, 8, n), F32)],
        compiler_params=_cparams(("parallel",)),
        name="moe_plan",
    )(rs, cw)
    meta_flat = meta[:, 0:2, 0:MOE_EXPERTS].reshape(-1)
    wg, wu, wd = w_gate.astype(BF16), w_up.astype(BF16), w_down.astype(BF16)
    grid_spec = pltpu.PrefetchScalarGridSpec(
        num_scalar_prefetch=1,
        grid=(nt // MOE_SUB, MOE_EXPERTS // MOE_EPS),
        in_specs=[pl.BlockSpec((MOE_SUB * n, D_MODEL), lambda i, e, m: (i, 0)),
                  pl.BlockSpec((MOE_SUB * n, LANES), lambda i, e, m: (i, 0)),
                  pl.BlockSpec((MOE_SUB, 8, n), lambda i, e, m: (i, 0, 0)),
                  pl.BlockSpec((MOE_EPS, D_MODEL, MOE_FF), lambda i, e, m: (e, 0, 0)),
                  pl.BlockSpec((MOE_EPS, D_MODEL, MOE_FF), lambda i, e, m: (e, 0, 0)),
                  pl.BlockSpec((MOE_EPS, MOE_FF, D_MODEL), lambda i, e, m: (e, 0, 0))],
        out_specs=pl.BlockSpec((MOE_SUB * n, D_MODEL), lambda i, e, m: (i, 0)),
        scratch_shapes=[pltpu.VMEM((MOE_SUB, ns, D_MODEL), BF16), pltpu.VMEM((MOE_SUB, ns, D_MODEL), BF16)],
    )
    return pl.pallas_call(
        functools.partial(_moe_kernel, n=n, ns=ns),
        grid_spec=grid_spec,
        out_shape=jax.ShapeDtypeStruct((t, D_MODEL), BF16),
        compiler_params=_cparams(("parallel", "arbitrary")),
        name="moe_experts",
    )(meta_flat, h2, col, row, wg, wu, wd)


def _final_kernel(x_ref, y_ref, g_ref, o_ref):
    o_ref[...] = _rms(x_ref[...] + y_ref[...].astype(F32), g_ref[...])


def _final(x, y, g):
    t = x.shape[0]
    tok = pl.BlockSpec((TM, D_MODEL), lambda i: (i, 0))
    return pl.pallas_call(
        _final_kernel,
        grid=(t // TM,),
        in_specs=[tok, tok, pl.BlockSpec((1, D_MODEL), lambda i: (0, 0))],
        out_specs=tok,
        out_shape=jax.ShapeDtypeStruct((t, D_MODEL), F32),
        compiler_params=_cparams(("parallel",)),
        name="final_norm",
    )(x, y, g)


def kernel(x, positions, norm_mix, w_in, ssd_conv_w, ssd_conv_b, ssd_dt_bias, ssd_a_log, ssd_d, ssd_norm, mla_q_norm, mla_w_uq, mla_kv_norm, mla_w_ukv, fox_f_bias, w_out, norm_ffn, router_group_w, router_group_b, router_expert_w, router_expert_b, expert_w_gate, expert_w_up, expert_w_down, final_norm):
    b, s, d = x.shape
    assert d == D_MODEL and s % FA_B == 0 and (b * s) % (MOE_N * MOE_SUB) == 0 and TM == DSA_SC
    depth = w_in.shape[0]
    n_sel = min(IDX_TOPK_MAX, s // 4)
    tab = _rope_tables(positions)
    xf = x.reshape(b * s, d)
    y_ffn = None
    for l in range(depth):
        xf, p = _inproj(xf, y_ffn, norm_mix[l].reshape(1, d), _pack_w_in(w_in[l]), tab, fox_f_bias[l],
                        mla_q_norm[l], mla_kv_norm[l], mla_w_uq[l], mla_w_ukv[l], s)
        y_ssd = _ssd(p["zx"], p["gt"], ssd_conv_w[l], ssd_conv_b[l], ssd_dt_bias[l], ssd_a_log[l], ssd_d[l],
                     ssd_norm[l], b, s)
        y_dsa = _dsa(p["iq"], p["gt"], p["dq"], p["ik"], p["dk"], p["dv"], b, s, n_sel)
        y_att = _flash([(p["mq"], p["mk"], p["mv"]), (p["fq"], p["fk"], p["fv"])], b, s)
        xf, h2, cw, rs = _outproj(xf, (y_ssd, y_dsa, y_att), w_out[l], norm_ffn[l].reshape(1, d),
                                  router_group_w[l], router_group_b[l], router_expert_w[l], router_expert_b[l])
        y_ffn = _moe(h2, cw, rs, expert_w_gate[l], expert_w_up[l], expert_w_down[l])
    return _final(xf, y_ffn, final_norm.reshape(1, d)).reshape(b, s, d)
```

```python
import functools
import math

import jax
import jax.numpy as jnp
from jax import lax
from jax.experimental import pallas as pl
from jax.experimental.pallas import tpu as pltpu

F32 = jnp.float32
BF16 = jnp.bfloat16
I32 = jnp.int32
EPS = 1e-6
ROPE_THETA = 500000.0

D_MODEL = 1024
SSD_HEADS, SSD_HEAD_DIM, SSD_GROUPS, SSD_STATE, SSD_CONV = 4, 64, 2, 128, 4
SSD_D_INNER = SSD_HEADS * SSD_HEAD_DIM
SSD_CONV_DIM = SSD_D_INNER + 2 * SSD_GROUPS * SSD_STATE
DSA_HEADS, DSA_HEAD_DIM, IDX_HEADS, IDX_DIM, IDX_TOPK_MAX = 4, 64, 8, 64, 256
MLA_HEADS, MLA_Q_RANK, MLA_KV_RANK, MLA_NOPE, MLA_ROPE, MLA_V = 4, 256, 128, 64, 32, 64
FOX_HEADS, FOX_HEAD_DIM = 4, 64
MOE_GROUPS, MOE_EPG, MOE_EXPERTS, MOE_FF = 4, 8, 32, 256

LANES = 128
VMEM_LIMIT = 56 * 1024 * 1024
INT_MIN = -(2 ** 31)
NEG_BIG = -1e30

TM = 512
SSD_Q = 256
DSA_BQ = 256
DSA_SC = 512
FA_B = 512
MOE_N = 512
MOE_ALIGN = 16
MOE_CH = 64
MOE_EPS = 2
MOE_SUB = 4
MOE_NS = -(-(2 * MOE_N + MOE_EXPERTS * MOE_ALIGN + MOE_CH) // LANES) * LANES

C_ZX, C_GT, C_DQ, C_KV, C_IK, C_IQ, C_MC, C_FQ, C_FK, C_FV, C_END = (
    0, 1024, 1152, 1408, 1536, 1664, 2176, 2688, 3200, 3712, 4224)
G_FOX, G_DT, G_IW = 0, 4, 8
LOG2E = math.log2(math.e)
HEAD_W = 128
FA_HEADS = 4
VT_ROWS = 80
FOX_ONE_Q, FOX_F_Q, FOX_F_K, FOX_ONE_K = 64, 67, 64, 67


def _cparams(sem):
    return pltpu.CompilerParams(dimension_semantics=sem, vmem_limit_bytes=VMEM_LIMIT)


def _rms(xf, g):
    return xf * lax.rsqrt(jnp.mean(xf * xf, axis=-1, keepdims=True) + EPS) * g


def _softplus(x):
    return jnp.maximum(x, 0.0) + jnp.log1p(jnp.exp(-jnp.abs(x)))


def _silu(x):
    return x * jax.nn.sigmoid(x)


def _rope_apply(a, c, sa, sb, half):
    return a * c + pltpu.roll(a, LANES - half, 1) * sa + pltpu.roll(a, half, 1) * sb


def _dot_nt(a, b):
    return lax.dot_general(a, b, (((1,), (1,)), ((), ())), preferred_element_type=F32)


def _split3(x):
    hi = x.astype(BF16)
    r = x - hi.astype(F32)
    mid = r.astype(BF16)
    return hi, mid, (r - mid.astype(F32)).astype(BF16)


def _dot_f32_by_mask(mask, x, mask_left):
    m = jnp.where(mask, 1.0, 0.0).astype(BF16)
    out = None
    for piece in _split3(x):
        d = jnp.dot(m, piece, preferred_element_type=F32) if mask_left else jnp.dot(piece, m, preferred_element_type=F32)
        out = d if out is None else out + d
    return out


def _rope_tab_kernel(pos_ref, inv_ref, o_ref):
    pos = pos_ref[...].astype(F32)
    lane = lax.broadcasted_iota(I32, (1, LANES), 1)
    ang = pos * inv_ref[0:1, :]
    c, s = jnp.cos(ang), jnp.sin(ang)
    jd = lane % 64
    o_ref[:, 0:128] = c
    o_ref[:, 128:256] = jnp.where(jd < 8, -s, 0.0)
    o_ref[:, 256:384] = jnp.where((jd >= 8) & (jd < 16), s, 0.0)
    ang = pos * inv_ref[1:2, :]
    c, s = jnp.cos(ang), jnp.sin(ang)
    jm = lane - 64
    o_ref[:, 384:512] = c
    o_ref[:, 512:640] = jnp.where((jm >= 0) & (jm < 16), -s, 0.0)
    o_ref[:, 640:768] = jnp.where((jm >= 16) & (jm < 32), s, 0.0)


def _rope_tables(positions):
    t = positions.size
    half_d, half_m = DSA_HEAD_DIM // 4 // 2, MLA_ROPE // 2
    inv_d = ROPE_THETA ** (-jnp.arange(half_d, dtype=F32) / half_d)
    inv_m = ROPE_THETA ** (-jnp.arange(half_m, dtype=F32) / half_m)
    lane = jnp.arange(LANES)
    jd = lane % 64
    row_d = jnp.where(jd < 2 * half_d, inv_d[jd % half_d], 0.0)
    jm = lane - 64
    row_m = jnp.where((jm >= 0) & (jm < 2 * half_m), inv_m[jm % half_m], 0.0)
    inv = jnp.zeros((8, LANES), F32).at[0].set(row_d).at[1].set(row_m)
    tm = 1024
    return pl.pallas_call(
        _rope_tab_kernel,
        grid=(t // tm,),
        in_specs=[pl.BlockSpec((tm, 1), lambda i: (i, 0)), pl.BlockSpec((8, LANES), lambda i: (0, 0))],
        out_specs=pl.BlockSpec((tm, 768), lambda i: (i, 0)),
        out_shape=jax.ShapeDtypeStruct((t, 768), F32),
        compiler_params=_cparams(("parallel",)),
        name="rope_tables",
    )(positions.reshape(t, 1), inv)


def _inproj_kernel(*refs, has_add, tiles_per_seq):
    refs = list(refs)
    x_ref = refs.pop(0)
    yp_ref = refs.pop(0) if has_add else None
    g_ref, w_ref, tab_ref, fb_ref, qn_ref, kn_ref, wq_ref, wk_ref, wv_ref = refs[:9]
    refs = refs[9:]
    xo_ref = refs.pop(0) if has_add else None
    (zx_ref, gt_ref, dq_ref, dk_ref, dv_ref, ik_ref, iq_ref, mq_ref, mk_ref, mv_ref, fq_ref, fk_ref, fv_ref,
     carry_s) = refs
    x = x_ref[...]
    if has_add:
        x = x + yp_ref[...].astype(F32)
        xo_ref[...] = x
    h = _rms(x, g_ref[...]).astype(BF16)

    def proj(a, b):
        return jnp.dot(h, w_ref[:, a:b], preferred_element_type=F32)

    zx_ref[...] = proj(C_ZX, C_GT)
    gates = proj(C_GT, C_DQ)
    gt_ref[...] = gates
    c_d, sa_d, sb_d = tab_ref[:, 0:128], tab_ref[:, 128:256], tab_ref[:, 256:384]
    c_m, sa_m, sb_m = tab_ref[:, 384:512], tab_ref[:, 512:640], tab_ref[:, 640:768]
    q = proj(C_DQ, C_KV)
    for i in range(2):
        dq_ref[:, i * 128:(i + 1) * 128] = _rope_apply(q[:, i * 128:(i + 1) * 128], c_d, sa_d, sb_d, 8).astype(BF16)
    kv = proj(C_KV, C_IK)
    is_k = lax.broadcasted_iota(I32, (1, LANES), 1) < 64
    kvr = _rope_apply(kv, jnp.where(is_k, c_d, 1.0), jnp.where(is_k, sa_d, 0.0), jnp.where(is_k, sb_d, 0.0), 8)
    dk_ref[...] = kvr[:, 0:64].astype(BF16)
    dv_ref[0, 0:64, :] = kvr.astype(BF16).astype(F32).T[64:128, :].astype(BF16)
    dv_ref[0, 64:VT_ROWS, :] = jnp.ones((VT_ROWS - 64, kvr.shape[0]), BF16)
    ikb = proj(C_IK, C_IQ)
    ik_ref[...] = _rope_apply(ikb, c_d, sa_d, sb_d, 8)[:, 0:64].astype(BF16)
    iq = proj(C_IQ, C_MC)
    for i in range(4):
        iq_ref[:, i * 128:(i + 1) * 128] = _rope_apply(iq[:, i * 128:(i + 1) * 128], c_d, sa_d, sb_d, 8).astype(BF16)
    ones_hi = jnp.where(lax.broadcasted_iota(I32, (1, FA_HEADS * HEAD_W), 1) % HEAD_W >= 64, 1.0, 0.0)

    mc = proj(C_MC, C_FQ)
    cq = _rms(mc[:, 0:256], qn_ref[...]).astype(BF16)
    ckv = _rms(mc[:, 256:384], kn_ref[...]).astype(BF16)
    kr = _rope_apply(mc[:, 384:512], c_m, sa_m, sb_m, 16)
    mq = jnp.dot(cq, wq_ref[...], preferred_element_type=F32)
    mk = jnp.dot(ckv, wk_ref[...], preferred_element_type=F32)
    for hd in range(MLA_HEADS):
        sl = slice(hd * HEAD_W, (hd + 1) * HEAD_W)
        mq_ref[:, sl] = _rope_apply(mq[:, sl], c_m, sa_m, sb_m, 16).astype(BF16)
        mk_ref[:, sl] = (mk[:, sl] + kr).astype(BF16)
    mv_ref[...] = (jnp.dot(ckv, wv_ref[...], preferred_element_type=F32) + ones_hi).astype(BF16)

    @pl.when(pl.program_id(0) % tiles_per_seq == 0)
    def _():
        carry_s[...] = jnp.zeros_like(carry_s)

    tm = x.shape[0]
    log_f = -_softplus(-(gates + fb_ref[...]))
    ri = lax.broadcasted_iota(I32, (tm, tm), 0)
    ci = lax.broadcasted_iota(I32, (tm, tm), 1)
    cs = _dot_f32_by_mask(ri >= ci, log_f, True) + carry_s[0:1, :]
    carry_s[0:1, :] = cs[tm - 1:tm, :]
    f2 = cs * LOG2E
    lane = lax.broadcasted_iota(I32, (1, HEAD_W), 1)

    def pieces(f, base):
        hi, mid, lo = _split3(f)
        return jnp.where(lane == base, hi.astype(F32),
                         jnp.where(lane == base + 1, mid.astype(F32), jnp.where(lane == base + 2, lo.astype(F32), 0.0)))

    def ones(base):
        return jnp.where((lane >= base) & (lane < base + 3), 1.0, 0.0)

    fq = proj(C_FQ, C_FK)
    fk = proj(C_FK, C_FV)
    for hd in range(FOX_HEADS):
        fh = f2[:, G_FOX + hd:G_FOX + hd + 1]
        sl = slice(hd * HEAD_W, (hd + 1) * HEAD_W)
        fq_ref[:, sl] = (fq[:, sl] + pieces(fh, FOX_F_Q) + ones(FOX_ONE_Q)).astype(BF16)
        fk_ref[:, sl] = (fk[:, sl] - pieces(fh, FOX_F_K) + ones(FOX_ONE_K)).astype(BF16)
    fv_ref[...] = (proj(C_FV, C_END) + ones_hi).astype(BF16)


def _pack_w_in(w):
    d = w.shape[0]
    z = lambda n: jnp.zeros((d, n), F32)
    o = 0
    parts = {}
    for name, width in (("z", 256), ("xbc", 768), ("dt", 4), ("dq", 256), ("dk", 64), ("dv", 64), ("iq", 512),
                        ("ik", 64), ("iw", 8), ("cq", 256), ("ckv", 128), ("kr", 32), ("fq", 256), ("fk", 256),
                        ("fv", 256), ("ff", 4)):
        parts[name] = w[:, o:o + width]
        o += width
    scale = DSA_HEAD_DIM ** -0.5

    def per_head(a):
        return jnp.pad(a.reshape(d, FA_HEADS, 64), ((0, 0), (0, 0), (0, HEAD_W - 64))).reshape(d, FA_HEADS * HEAD_W)

    cat = jnp.concatenate([
        parts["z"], parts["xbc"],
        parts["ff"], parts["dt"], parts["iw"], z(112),
        parts["dq"] * (scale * LOG2E),
        parts["dk"], parts["dv"],
        parts["ik"], z(64),
        parts["iq"],
        parts["cq"], parts["ckv"], z(64), parts["kr"], z(32),
        per_head(parts["fq"] * (FOX_HEAD_DIM ** -0.5 * LOG2E)), per_head(parts["fk"]), per_head(parts["fv"])], axis=1)
    return cat.astype(BF16)


def _mla_weights(w_uq, w_ukv):
    dqk = MLA_NOPE + MLA_ROPE
    wq = jnp.pad(w_uq.reshape(MLA_Q_RANK, MLA_HEADS, dqk) * (dqk ** -0.5 * LOG2E), ((0, 0), (0, 0), (0, HEAD_W - dqk)))
    wkv = w_ukv.reshape(MLA_KV_RANK, MLA_HEADS, MLA_NOPE + MLA_V)
    wk = jnp.pad(wkv[:, :, :MLA_NOPE], ((0, 0), (0, 0), (0, HEAD_W - MLA_NOPE)))
    wv = jnp.pad(wkv[:, :, MLA_NOPE:], ((0, 0), (0, 0), (0, HEAD_W - MLA_V)))
    flat = lambda a: a.reshape(a.shape[0], MLA_HEADS * HEAD_W).astype(BF16)
    return flat(wq), flat(wk), flat(wv)


def _inproj(x, y_prev, g, w_cat, tab, f_bias, q_norm, kv_norm, w_uq, w_ukv, s):
    t = x.shape[0]
    has_add = y_prev is not None
    tok = lambda n: pl.BlockSpec((TM, n), lambda i: (i, 0))
    full = lambda a: pl.BlockSpec(a.shape, lambda i: (0,) * a.ndim)
    fb = jnp.zeros((1, LANES), F32).at[0, G_FOX:G_FOX + FOX_HEADS].set(f_bias)
    params = [g, w_cat, tab, fb, q_norm.reshape(1, -1), kv_norm.reshape(1, -1), *_mla_weights(w_uq, w_ukv)]
    ins = [x] + ([y_prev] if has_add else []) + params
    in_specs = ([tok(D_MODEL)] + ([tok(D_MODEL)] if has_add else [])
                + [tok(768) if a is tab else full(a) for a in params])
    outs = ([("x", D_MODEL, F32)] if has_add else []) + [
        ("zx", 1024, F32), ("gt", 128, F32), ("dq", 256, BF16), ("dk", 64, BF16), ("dv", 64, BF16),
        ("ik", 64, BF16), ("iq", 512, BF16), ("mq", 512, BF16), ("mk", 512, BF16), ("mv", 512, BF16),
        ("fq", 512, BF16), ("fk", 512, BF16), ("fv", 512, BF16)]
    vt_rows = {"dv": VT_ROWS}
    res = pl.pallas_call(
        functools.partial(_inproj_kernel, has_add=has_add, tiles_per_seq=s // TM),
        grid=(t // TM,),
        in_specs=in_specs,
        out_specs=[pl.BlockSpec((1, vt_rows[nm], TM), lambda i: (i, 0, 0)) if nm in vt_rows else tok(n)
                   for nm, n, _ in outs],
        out_shape=[jax.ShapeDtypeStruct((t // TM, vt_rows[nm], TM) if nm in vt_rows else (t, n), dt)
                   for nm, n, dt in outs],
        scratch_shapes=[pltpu.VMEM((8, LANES), F32)],
        compiler_params=_cparams(("arbitrary",)),
        name="inproj",
    )(*ins)
    res = list(res)
    x_new = res.pop(0) if has_add else x
    return x_new, dict(zip([n for n, _, _ in outs if n != "x"], res))


def _ssd_kernel(zx_ref, gt_ref, cw_ref, cb_ref, pr_ref, pc_ref, drow_ref, nw_ref, o_ref, ext_s, st_s, y_s, *, q):
    @pl.when(pl.program_id(1) == 0)
    def _():
        ext_s[0:8, :] = jnp.zeros((8, SSD_CONV_DIM), F32)
        st_s[...] = jnp.zeros_like(st_s)

    raw = zx_ref[:, 256:1024]
    ext_s[8:8 + q, :] = raw
    acc = jnp.broadcast_to(cb_ref[...], (q, SSD_CONV_DIM))
    for j in range(SSD_CONV):
        acc = acc + cw_ref[j:j + 1, :] * ext_s[5 + j:5 + j + q, :]
    ext_s[0:8, :] = raw[q - 8:q, :]
    xbc = _silu(acc)
    xs = xbc[:, 0:SSD_D_INNER]

    g = gt_ref[...]
    lane = lax.broadcasted_iota(I32, (1, LANES), 1)
    dtc = _softplus(g + pr_ref[0:1, :])
    a_r = jnp.where((lane >= G_DT) & (lane < G_DT + SSD_HEADS), -jnp.exp(pr_ref[1:2, :]), 0.0)
    ri = lax.broadcasted_iota(I32, (q, q), 0)
    ci = lax.broadcasted_iota(I32, (q, q), 1)
    tri = ri >= ci
    acs_c = _dot_f32_by_mask(tri, dtc * a_r, True)
    sub = lax.broadcasted_iota(I32, (LANES, 1), 0)
    dtr = _softplus(g.T + pc_ref[:, 0:1])
    a_c = jnp.where((sub >= G_DT) & (sub < G_DT + SSD_HEADS), -jnp.exp(pc_ref[:, 1:2]), 0.0)
    acs_r = _dot_f32_by_mask(ri <= ci, dtr * a_c, False)

    rep = SSD_HEADS // SSD_GROUPS
    for gi in range(SSD_GROUPS):
        bg = xbc[:, SSD_D_INNER + gi * SSD_STATE:SSD_D_INNER + (gi + 1) * SSD_STATE]
        cg = xbc[:, SSD_D_INNER + (SSD_GROUPS + gi) * SSD_STATE:SSD_D_INNER + (SSD_GROUPS + gi + 1) * SSD_STATE]
        bt = bg.T.astype(BF16)
        cb16 = cg.astype(BF16)
        cbm = jnp.dot(cb16, bt, preferred_element_type=F32)
        for hh in range(rep):
            h = gi * rep + hh
            ac = acs_c[:, G_DT + h:G_DT + h + 1]
            ar = acs_r[G_DT + h:G_DT + h + 1, :]
            seg = jnp.where(tri, jnp.exp(ac - ar), 0.0)
            xh = xs[:, h * SSD_HEAD_DIM:(h + 1) * SSD_HEAD_DIM]
            xdt = xh * dtc[:, G_DT + h:G_DT + h + 1]
            yd = jnp.dot((cbm * seg).astype(BF16), xdt.astype(BF16), preferred_element_type=F32)
            aend = ac[q - 1:q, :]
            st = st_s[h]
            yo = jnp.dot(cb16, st.astype(BF16), preferred_element_type=F32) * jnp.exp(ac)
            st_s[h] = st * jnp.exp(aend) + jnp.dot(bt, (xdt * jnp.exp(aend - ac)).astype(BF16),
                                                   preferred_element_type=F32)
            y_s[:, h * SSD_HEAD_DIM:(h + 1) * SSD_HEAD_DIM] = yd + yo
    y = y_s[...] + drow_ref[...] * xs
    y = y * _silu(zx_ref[:, 0:SSD_D_INNER])
    o_ref[...] = _rms(y, nw_ref[...]).astype(BF16)


def _ssd(zx, gt, conv_w, conv_b, dt_bias, a_log, d_skip, norm_w, b, s):
    q = SSD_Q
    nc = s // q
    pr = jnp.zeros((8, LANES), F32).at[0, G_DT:G_DT + SSD_HEADS].set(dt_bias).at[1, G_DT:G_DT + SSD_HEADS].set(a_log)
    pc = pr.T
    drow = jnp.repeat(d_skip, SSD_HEAD_DIM).reshape(1, SSD_D_INNER)
    cb = conv_b.reshape(1, -1)
    nw = norm_w.reshape(1, -1)
    full = lambda a: pl.BlockSpec(a.shape, lambda i, j: (0,) * a.ndim)
    return pl.pallas_call(
        functools.partial(_ssd_kernel, q=q),
        grid=(b, nc),
        in_specs=[pl.BlockSpec((q, 1024), lambda i, j: (i * nc + j, 0)), pl.BlockSpec((q, LANES), lambda i, j: (i * nc + j, 0)),
                  full(conv_w), full(cb), full(pr), full(pc), full(drow), full(nw)],
        out_specs=pl.BlockSpec((q, SSD_D_INNER), lambda i, j: (i * nc + j, 0)),
        out_shape=jax.ShapeDtypeStruct((b * s, SSD_D_INNER), BF16),
        scratch_shapes=[pltpu.VMEM((8 + q, SSD_CONV_DIM), F32), pltpu.VMEM((SSD_HEADS, SSD_STATE, SSD_HEAD_DIM), F32),
                        pltpu.VMEM((q, SSD_D_INNER), F32)],
        compiler_params=_cparams(("parallel", "arbitrary")),
        name="ssd",
    )(zx, gt, conv_w, cb, pr, pc, drow, nw)


def _key_to_f32(k):
    return pltpu.bitcast(jnp.where(k < 0, k ^ jnp.int32(0x7FFFFFFF), k), F32)


def _dsa_kernel(iq_ref, gt_ref, q_ref, ik_ref, k_ref, vt_ref, o_ref, sc_s, qi_s, qs_s, thr_s, s16_s,
                *, bq, sc_w, n_sel):
    qb = pl.program_id(1)
    n_sc = ((qb + 1) * bq + sc_w - 1) // sc_w
    for j in range(IDX_HEADS):
        qi_s[j // 2, (j % 2) * bq:(j % 2 + 1) * bq, :] = iq_ref[:, j * IDX_DIM:(j + 1) * IDX_DIM]
    for h in range(DSA_HEADS):
        qs_s[h * bq:(h + 1) * bq, :] = q_ref[:, h * DSA_HEAD_DIM:(h + 1) * DSA_HEAD_DIM]
    w_t = (gt_ref[...] * ((IDX_HEADS * IDX_DIM) ** -0.5)).T
    qpos = qb * bq + lax.broadcasted_iota(I32, (1, bq), 1)

    def score_body(sc, carry):
        k0 = pl.multiple_of(sc * sc_w, sc_w)
        kidx = ik_ref[pl.ds(k0, sc_w), :]
        sco = jnp.zeros((sc_w, bq), F32)
        for jp in range(IDX_HEADS // 2):
            a = _dot_nt(kidx, qi_s[jp])
            for u in range(2):
                j = 2 * jp + u
                sco = sco + jnp.maximum(a[:, u * bq:(u + 1) * bq], 0.0) * w_t[G_IW + j:G_IW + j + 1, :]
        kpos = k0 + lax.broadcasted_iota(I32, (sc_w, 1), 0)
        sco = jnp.where(kpos <= qpos, sco, -jnp.inf)
        sc_s[sc] = sco
        s16_s[sc] = sco.astype(BF16)
        return carry

    lax.fori_loop(0, n_sc, score_body, 0)

    def count(pred):
        def body(sc, acc):
            for r in range(sc_w // 64):
                acc = acc + jnp.where(pred(sc_s[sc, r * 64:(r + 1) * 64, :]), 1.0, 0.0)
            return acc
        acc = lax.fori_loop(0, n_sc, body, jnp.zeros((64, bq), F32))
        return jnp.sum(acc, axis=0, keepdims=True)

    thr_s[0:1, :] = jnp.full((1, bq), -jnp.finfo(jnp.float32).max, F32)

    @pl.when((qb + 1) * bq > n_sel)
    def _():
        active = qpos + 1 > n_sel

        one16, zero16 = jnp.ones((), BF16), jnp.zeros((), BF16)

        def count16(c16):
            def body(sc, acc):
                a16 = jnp.zeros((64, bq), BF16)
                for r in range(sc_w // 64):
                    a16 = a16 + jnp.where(s16_s[sc, r * 64:(r + 1) * 64, :] >= c16, one16, zero16)
                return acc + a16.astype(F32)
            acc = lax.fori_loop(0, n_sc, body, jnp.zeros((64, bq), F32))
            return jnp.sum(acc, axis=0, keepdims=True)

        def coarse(it, lo):
            cand = lo ^ jnp.left_shift(jnp.int32(1), 31 - it)
            return jnp.where(count16(_key_to_f32(cand).astype(BF16)) >= n_sel, cand, lo)

        lo16 = lax.fori_loop(0, 16, coarse, jnp.full((1, bq), INT_MIN, I32))
        base = jnp.maximum(lo16, jnp.int32(INT_MIN + (1 << 16))) - jnp.int32(1 << 16)

        def cond(st):
            it, off, cnt_lo = st
            pending = active & (cnt_lo != n_sel)
            return (it < 32) & (jnp.max(jnp.where(pending, 1.0, 0.0)) > 0.0)

        def fine(it, off, cnt_lo):
            cand = off | jnp.left_shift(jnp.int32(1), 31 - it)
            thr_c = _key_to_f32(base + cand)
            cnt = count(lambda t: t >= thr_c)
            ok = cnt >= n_sel
            return jnp.where(ok, cand, off), jnp.where(ok, cnt, cnt_lo)

        def body(st):
            it, off, cnt_lo = st
            for u in range(4):
                off, cnt_lo = fine(it + u, off, cnt_lo)
            return it + 4, off, cnt_lo

        off, cnt_lo = fine(15, jnp.zeros((1, bq), I32), jnp.full((1, bq), 3.0e38, F32))
        _, off, cnt_lo = lax.while_loop(cond, body, (jnp.int32(16), off, cnt_lo))
        lo = base + off
        thr = _key_to_f32(lo)
        tie = active & (cnt_lo > n_sel)

        @pl.when(jnp.max(jnp.where(tie, 1.0, 0.0)) > 0.0)
        def _():
            need = n_sel - count(lambda t: t > thr)
            ri = lax.broadcasted_iota(I32, (sc_w, sc_w), 0)
            ci = lax.broadcasted_iota(I32, (sc_w, sc_w), 1)
            lower = jnp.where(ri >= ci, 1.0, 0.0).astype(BF16)

            def tie_body(sc, run):
                t = sc_s[sc]
                eq = (t == thr) & tie
                pre = jnp.dot(lower, jnp.where(eq, 1.0, 0.0).astype(BF16), preferred_element_type=F32)
                sc_s[sc] = jnp.where(eq & (run + pre > need), -jnp.inf, t)
                return run + pre[sc_w - 1:sc_w, :]

            lax.fori_loop(0, n_sc, tie_body, jnp.zeros((1, bq), F32))

        thr_s[0:1, :] = jnp.where(active, thr, thr_s[0:1, :])

    thr = thr_s[0:1, :]

    hq = DSA_HEADS * bq

    def att_body(sc, carry):
        m_prev, acc = carry
        k0 = pl.multiple_of(sc * sc_w, sc_w)
        st = _dot_nt(k_ref[pl.ds(k0, sc_w), :], qs_s[...])
        drop = jnp.where(sc_s[sc] >= thr, 0.0, 2 * NEG_BIG)
        st = st + jnp.tile(drop, (1, DSA_HEADS))
        m_cur = jnp.max(jnp.max(st.reshape(sc_w // 64, 64, hq), axis=0), axis=0, keepdims=True)
        m_next = jnp.maximum(m_prev, m_cur)
        p = jnp.exp2(st - m_next)
        pv = jnp.dot(vt_ref[sc], p.astype(BF16), preferred_element_type=F32)
        return m_next, jnp.exp2(m_prev - m_next) * acc + pv

    def att_body2(i, carry):
        return att_body(2 * i + 1, att_body(2 * i, carry))

    carry = lax.fori_loop(0, n_sc // 2, att_body2, (jnp.full((1, hq), NEG_BIG, F32), jnp.zeros((VT_ROWS, hq), F32)))
    _, acc = lax.cond(n_sc % 2 == 1, lambda c: att_body(n_sc - 1, c), lambda c: c, carry)
    out_t = jnp.concatenate([acc[0:DSA_HEAD_DIM, :] / acc[DSA_HEAD_DIM:DSA_HEAD_DIM + 1, :],
                             jnp.zeros((LANES - DSA_HEAD_DIM, hq), F32)], axis=0).T
    for h in range(DSA_HEADS):
        o_ref[:, h * DSA_HEAD_DIM:(h + 1) * DSA_HEAD_DIM] = out_t[h * bq:(h + 1) * bq, 0:DSA_HEAD_DIM].astype(BF16)


def _dsa(iq, gt, dq, ik, dk, dvt, b, s, n_sel):
    bq, sc_w = DSA_BQ, DSA_SC
    nq = s // bq
    qblk = lambda n: pl.BlockSpec((bq, n), lambda i, j: (i * nq + j, 0))
    kblk = pl.BlockSpec((s, 64), lambda i, j: (i, 0))
    return pl.pallas_call(
        functools.partial(_dsa_kernel, bq=bq, sc_w=sc_w, n_sel=n_sel),
        grid=(b, nq),
        in_specs=[qblk(512), qblk(LANES), qblk(256), kblk, kblk,
                  pl.BlockSpec((s // sc_w, VT_ROWS, sc_w), lambda i, j: (i, 0, 0))],
        out_specs=qblk(256),
        out_shape=jax.ShapeDtypeStruct((b * s, 256), BF16),
        scratch_shapes=[pltpu.VMEM((s // sc_w, sc_w, bq), F32), pltpu.VMEM((IDX_HEADS // 2, 2 * bq, IDX_DIM), BF16),
                        pltpu.VMEM((DSA_HEADS * bq, DSA_HEAD_DIM), BF16), pltpu.VMEM((8, bq), F32),
                        pltpu.VMEM((s // sc_w, sc_w, bq), BF16)],
        compiler_params=_cparams(("parallel", "arbitrary")),
        name="dsa",
    )(iq, gt, dq, ik, dk, dvt)


def _flash_kernel(qt_ref, kt_ref, *refs, blk, sets):
    qkv, (o_ref, m_s, acc_s) = refs[:3 * sets], refs[3 * sets:]
    t = pl.program_id(1)
    qi, ki = qt_ref[t], kt_ref[t]

    @pl.when(ki == 0)
    def _():
        m_s[...] = jnp.full(m_s.shape, NEG_BIG, F32)
        acc_s[...] = jnp.zeros_like(acc_s)

    def step(diagonal):
        if diagonal:
            keep = lax.broadcasted_iota(I32, (blk, blk), 0) >= lax.broadcasted_iota(I32, (blk, blk), 1)
        for g in range(sets):
            q_ref, k_ref, v_ref = qkv[3 * g:3 * g + 3]
            for h in range(FA_HEADS):
                sl = slice(h * HEAD_W, (h + 1) * HEAD_W)
                hh = g * FA_HEADS + h
                s = _dot_nt(q_ref[:, sl], k_ref[:, sl])
                if diagonal:
                    s = jnp.where(keep, s, 2 * NEG_BIG)
                m_prev = m_s[hh]
                m_next = jnp.maximum(m_prev, jnp.max(s, axis=1, keepdims=True))
                p = jnp.exp2(s - jnp.tile(m_next, (1, blk // LANES)))
                pv = jnp.dot(p.astype(BF16), v_ref[:, sl], preferred_element_type=F32)
                acc_s[hh] = jnp.exp2(m_prev - m_next) * acc_s[hh] + pv
                m_s[hh] = m_next

    @pl.when(ki < qi)
    def _():
        step(False)

    @pl.when(ki == qi)
    def _():
        step(True)
        for hh in range(sets * FA_HEADS):
            acc = acc_s[hh]
            o_ref[:, hh * 64:(hh + 1) * 64] = (acc / pltpu.roll(acc, 64, 1))[:, 0:64].astype(BF16)


def _flash(qkv, b, s):
    blk = FA_B
    nb = s // blk
    pairs = [(i, j) for i in range(nb) for j in range(i + 1)]
    qt = jnp.asarray([p[0] for p in pairs], I32)
    kt = jnp.asarray([p[1] for p in pairs], I32)
    w = FA_HEADS * HEAD_W
    sets = len(qkv)
    q_spec = pl.BlockSpec((blk, w), lambda i, t, qt, kt: (i * nb + qt[t], 0))
    kv_spec = pl.BlockSpec((blk, w), lambda i, t, qt, kt: (i * nb + kt[t], 0))
    grid_spec = pltpu.PrefetchScalarGridSpec(
        num_scalar_prefetch=2,
        grid=(b, len(pairs)),
        in_specs=[q_spec, kv_spec, kv_spec] * sets,
        out_specs=pl.BlockSpec((blk, sets * FA_HEADS * 64), lambda i, t, qt, kt: (i * nb + qt[t], 0)),
        scratch_shapes=[pltpu.VMEM((sets * FA_HEADS, blk, LANES), F32), pltpu.VMEM((sets * FA_HEADS, blk, HEAD_W), F32)],
    )
    return pl.pallas_call(
        functools.partial(_flash_kernel, blk=blk, sets=sets),
        grid_spec=grid_spec,
        out_shape=jax.ShapeDtypeStruct((b * s, sets * FA_HEADS * 64), BF16),
        compiler_params=_cparams(("parallel", "arbitrary")),
        name="flash",
    )(qt, kt, *[a for triple in qkv for a in triple])


def _outproj_kernel(x_ref, *refs, n_mix):
    y_refs, (wo_ref, g_ref, wr_ref, br_ref, xo_ref, h_ref, cw_ref, rs_ref) = refs[:n_mix], refs[n_mix:]
    acc = x_ref[...]
    row = 0
    for y_ref in y_refs:
        acc = acc + jnp.dot(y_ref[...], wo_ref[row:row + y_ref.shape[1], :], preferred_element_type=F32)
        row += y_ref.shape[1]
    xo_ref[...] = acc
    h2 = _rms(acc, g_ref[...])
    h_ref[...] = h2.astype(BF16)
    h_hi, h_lo, _ = _split3(h2)
    lg = (jnp.dot(h_hi, wr_ref[0], preferred_element_type=F32) + jnp.dot(h_lo, wr_ref[0], preferred_element_type=F32)
          + jnp.dot(h_hi, wr_ref[1], preferred_element_type=F32) + br_ref[...])
    lane = lax.broadcasted_iota(I32, (1, LANES), 1)
    lanef = lane.astype(F32)
    is_g = (lane >= MOE_EXPERTS) & (lane < MOE_EXPERTS + MOE_GROUPS)
    gl = jnp.where(is_g, lg, -jnp.inf)
    gmax = jnp.max(gl, axis=1, keepdims=True)
    gidx = jnp.min(jnp.where(gl == gmax, lanef, 999.0), axis=1, keepdims=True) - MOE_EXPERTS
    g_p = 1.0 / jnp.sum(jnp.where(is_g, jnp.exp(gl - gmax), 0.0), axis=1, keepdims=True)
    in_g = (lane < MOE_EXPERTS) & (jnp.floor(lanef * (1.0 / MOE_EPG)) == gidx)
    el = jnp.where(in_g, lg, -jnp.inf)
    m1 = jnp.max(el, axis=1, keepdims=True)
    i1 = jnp.min(jnp.where(el == m1, lanef, 999.0), axis=1, keepdims=True)
    el2 = jnp.where(lanef == i1, -jnp.inf, el)
    m2 = jnp.max(el2, axis=1, keepdims=True)
    i2 = jnp.min(jnp.where(el2 == m2, lanef, 999.0), axis=1, keepdims=True)
    t = jnp.exp(m2 - m1)
    w1 = 1.0 / (1.0 + t)
    cw_ref[...] = jnp.where(lanef == i1, w1 * g_p, jnp.where(lanef == i2, t * w1 * g_p, 0.0))
    rs_ref[...] = jnp.where((lanef == i1) | (lanef == i2), 1.0, 0.0).astype(BF16)


def _outproj(x, ys, w_out, g, w_rg, b_rg, w_re, b_re):
    t = x.shape[0]
    wo = w_out.astype(BF16)
    wr = jnp.zeros((D_MODEL, LANES), F32).at[:, 0:MOE_EXPERTS].set(w_re).at[:, MOE_EXPERTS:MOE_EXPERTS + MOE_GROUPS].set(w_rg)
    wr_hi = wr.astype(BF16)
    wr = jnp.stack([wr_hi, (wr - wr_hi.astype(F32)).astype(BF16)])
    br =jnp.zeros((1, LANES), F32).at[0, 0:MOE_EXPERTS].set(b_re).at[0, MOE_EXPERTS:MOE_EXPERTS + MOE_GROUPS].set(b_rg)
    tok = lambda n: pl.BlockSpec((TM, n), lambda i: (i, 0))
    full = lambda a: pl.BlockSpec(a.shape, lambda i: (0,) * a.ndim)
    return pl.pallas_call(
        functools.partial(_outproj_kernel, n_mix=len(ys)),
        grid=(t // TM,),
        in_specs=[tok(D_MODEL)] + [tok(y.shape[1]) for y in ys] + [full(wo), full(g), full(wr), full(br)],
        out_specs=[tok(D_MODEL), tok(D_MODEL), tok(LANES), tok(LANES)],
        out_shape=[jax.ShapeDtypeStruct((t, D_MODEL), F32), jax.ShapeDtypeStruct((t, D_MODEL), BF16),
                   jax.ShapeDtypeStruct((t, LANES), F32), jax.ShapeDtypeStruct((t, LANES), BF16)],
        compiler_params=_cparams(("parallel",)),
        name="outproj_router",
    )(x, *ys, wo, g, wr, br)


def _moe_plan_kernel(rs_ref, cw_ref, meta_ref, col_ref, row_ref, *, n):
    sel = rs_ref[...]
    self32 = sel.astype(F32)
    ti = lax.broadcasted_iota(I32, (n, n), 0)
    tj = lax.broadcasted_iota(I32, (n, n), 1)
    rank = jnp.dot(jnp.where(ti > tj, 1.0, 0.0).astype(BF16), sel, preferred_element_type=F32)
    cnt = jnp.sum(self32, axis=0, keepdims=True).astype(I32)
    cpad = ((cnt + (MOE_ALIGN - 1)) // MOE_ALIGN) * MOE_ALIGN
    li = lax.broadcasted_iota(I32, (LANES, LANES), 0)
    lj = lax.broadcasted_iota(I32, (LANES, LANES), 1)
    off = _dot_f32_by_mask(li < lj, jnp.broadcast_to(cpad.astype(F32), (8, LANES)), False)[0:1, :]
    dest = off + rank
    lane = lax.broadcasted_iota(I32, (1, LANES), 1)
    lanef = lane.astype(F32)
    on = self32 > 0.5
    e_lo = jnp.min(jnp.where(on, lanef, 999.0), axis=1, keepdims=True)
    e_hi = jnp.max(jnp.where(on, lanef, -1.0), axis=1, keepdims=True)
    is_lo, is_hi = lanef == e_lo, lanef == e_hi
    cw = cw_ref[...]
    pick = lambda m, a: jnp.sum(jnp.where(m, a, 0.0), axis=1, keepdims=True)
    d_lo, d_hi, w_lo, w_hi = pick(is_lo, dest), pick(is_hi, dest), pick(is_lo, cw), pick(is_hi, cw)
    colv = jnp.where(lane == 0, d_lo, jnp.where(lane == 1, d_hi, jnp.where(lane == 2, w_lo, jnp.where(lane == 3, w_hi, 0.0))))
    col_ref[...] = colv
    row_ref[0] = colv.T[0:8, :]
    sub = lax.broadcasted_iota(I32, (8, LANES), 0)
    meta_ref[0] = jnp.where(sub == 0, off.astype(I32), jnp.where(sub == 1, cnt, 0))


def _moe_kernel(meta_ref, h_ref, col_ref, row_ref, wg_ref, wu_ref, wd_ref, o_ref, s_s, y_s, *, n, ns):
    t, e = pl.program_id(0), pl.program_id(1)

    @pl.when(e == 0)
    def _():
        for g in range(MOE_SUB):
            d_lo, d_hi = row_ref[g, 0:1, :], row_ref[g, 1:2, :]
            hg = h_ref[g * n:(g + 1) * n, :]
            for r in range(ns // LANES):
                si = (r * LANES + lax.broadcasted_iota(I32, (LANES, 1), 0)).astype(F32)
                perm = jnp.where((si == d_lo) | (si == d_hi), 1.0, 0.0).astype(BF16)
                s_s[g, r * LANES:(r + 1) * LANES, :] = jnp.dot(perm, hg, preferred_element_type=F32).astype(BF16)
        y_s[...] = jnp.zeros_like(y_s)

    def mlp(j, starts, store_ok):
        starts = [pl.multiple_of(r0, MOE_ALIGN) for r0 in starts]
        xs = jnp.concatenate([s_s[g, pl.ds(starts[g], MOE_CH), :] for g in range(MOE_SUB)], axis=0)
        gate = jnp.dot(xs, wg_ref[j], preferred_element_type=F32)
        up = jnp.dot(xs, wu_ref[j], preferred_element_type=F32)
        hid = (_silu(gate) * up).astype(BF16)
        y = jnp.dot(hid, wd_ref[j], preferred_element_type=F32).astype(BF16)
        for g in range(MOE_SUB):
            def store(g=g):
                y_s[g, pl.ds(starts[g], MOE_CH), :] = y[g * MOE_CH:(g + 1) * MOE_CH, :]
            if store_ok is None:
                store()
            else:
                pl.when(store_ok[g])(store)

    def meta(g, j, what):
        return meta_ref[(t * MOE_SUB + g) * 2 * MOE_EXPERTS + what * MOE_EXPERTS + e * MOE_EPS + j]

    offs = [[meta(g, j, 0) for g in range(MOE_SUB)] for j in range(MOE_EPS)]
    cnts = [[meta(g, j, 1) for g in range(MOE_SUB)] for j in range(MOE_EPS)]
    for j in range(MOE_EPS):
        mlp(j, offs[j], None)
    for j in range(MOE_EPS):
        ends = [offs[j][g] + ((cnts[j][g] + MOE_ALIGN - 1) // MOE_ALIGN) * MOE_ALIGN for g in range(MOE_SUB)]
        nch = [(cnts[j][g] + MOE_CH - 1) // MOE_CH for g in range(MOE_SUB)]

        def body(i, carry, j=j, ends=ends, nch=nch):
            more = [i < nch[g] for g in range(MOE_SUB)]
            starts = [jnp.where(more[g], jnp.minimum(offs[j][g] + i * MOE_CH, ends[g] - MOE_CH), offs[j][g])
                      for g in range(MOE_SUB)]
            mlp(j, starts, more)
            return carry

        lax.fori_loop(1, functools.reduce(jnp.maximum, nch), body, 0)

    @pl.when(e == pl.num_programs(1) - 1)
    def _():
        rows = 256
        kc = ns // next(k for k in (3, 2, 1) if ns % (k * LANES) == 0)
        for g in range(MOE_SUB):
            for c in range(n // rows):
                r0 = g * n + c * rows
                cv = col_ref[r0:r0 + rows, :]
                d_lo, d_hi, w_lo, w_hi = cv[:, 0:1], cv[:, 1:2], cv[:, 2:3], cv[:, 3:4]
                acc = jnp.zeros((rows, D_MODEL), F32)
                for r in range(ns // kc):
                    si = (r * kc + lax.broadcasted_iota(I32, (1, kc), 1)).astype(F32)
                    pw = (jnp.where(si == d_lo, w_lo, 0.0) + jnp.where(si == d_hi, w_hi, 0.0)).astype(BF16)
                    acc = acc + jnp.dot(pw, y_s[g, r * kc:(r + 1) * kc, :], preferred_element_type=F32)
                o_ref[r0:r0 + rows, :] = acc.astype(BF16)


def _moe(h2, cw, rs, wg, wu, wd, layer):
    t = h2.shape[0]
    n, ns = MOE_N, MOE_NS
    nt = t // n
    meta, col, row = pl.pallas_call(
        functools.partial(_moe_plan_kernel, n=n),
        grid=(nt,),
        in_specs=[pl.BlockSpec((n, LANES), lambda i: (i, 0)), pl.BlockSpec((n, LANES), lambda i: (i, 0))],
        out_specs=[pl.BlockSpec((1, 8, LANES), lambda i: (i, 0, 0)), pl.BlockSpec((n, LANES), lambda i: (i, 0)),
                   pl.BlockSpec((1, 8, n), lambda i: (i, 0, 0))],
        out_shape=[jax.ShapeDtypeStruct((nt, 8, LANES), I32), jax.ShapeDtypeStruct((t, LANES), F32),
                   jax.ShapeDtypeStruct((nt, 8, n), F32)],
        compiler_params=_cparams(("parallel",)),
        name="moe_plan",
    )(rs, cw)
    meta_flat = meta[:, 0:2, 0:MOE_EXPERTS].reshape(-1)
    grid_spec = pltpu.PrefetchScalarGridSpec(
        num_scalar_prefetch=1,
        grid=(nt // MOE_SUB, MOE_EXPERTS // MOE_EPS),
        in_specs=[pl.BlockSpec((MOE_SUB * n, D_MODEL), lambda i, e, m: (i, 0)),
                  pl.BlockSpec((MOE_SUB * n, LANES), lambda i, e, m: (i, 0)),
                  pl.BlockSpec((MOE_SUB, 8, n), lambda i, e, m: (i, 0, 0)),
                  pl.BlockSpec((None, MOE_EPS, D_MODEL, MOE_FF), lambda i, e, m: (layer, e, 0, 0)),
                  pl.BlockSpec((None, MOE_EPS, D_MODEL, MOE_FF), lambda i, e, m: (layer, e, 0, 0)),
                  pl.BlockSpec((None, MOE_EPS, MOE_FF, D_MODEL), lambda i, e, m: (layer, e, 0, 0))],
        out_specs=pl.BlockSpec((MOE_SUB * n, D_MODEL), lambda i, e, m: (i, 0)),
        scratch_shapes=[pltpu.VMEM((MOE_SUB, ns, D_MODEL), BF16), pltpu.VMEM((MOE_SUB, ns, D_MODEL), BF16)],
    )
    return pl.pallas_call(
        functools.partial(_moe_kernel, n=n, ns=ns),
        grid_spec=grid_spec,
        out_shape=jax.ShapeDtypeStruct((t, D_MODEL), BF16),
        compiler_params=_cparams(("parallel", "arbitrary")),
        name="moe_experts",
    )(meta_flat, h2, col, row, wg, wu, wd)


def _final_kernel(x_ref, y_ref, g_ref, o_ref):
    o_ref[...] = _rms(x_ref[...] + y_ref[...].astype(F32), g_ref[...])


def _final(x, y, g):
    t = x.shape[0]
    tok = pl.BlockSpec((TM, D_MODEL), lambda i: (i, 0))
    return pl.pallas_call(
        _final_kernel,
        grid=(t // TM,),
        in_specs=[tok, tok, pl.BlockSpec((1, D_MODEL), lambda i: (0, 0))],
        out_specs=tok,
        out_shape=jax.ShapeDtypeStruct((t, D_MODEL), F32),
        compiler_params=_cparams(("parallel",)),
        name="final_norm",
    )(x, y, g)


def kernel(x, positions, norm_mix, w_in, ssd_conv_w, ssd_conv_b, ssd_dt_bias, ssd_a_log, ssd_d, ssd_norm, mla_q_norm, mla_w_uq, mla_kv_norm, mla_w_ukv, fox_f_bias, w_out, norm_ffn, router_group_w, router_group_b, router_expert_w, router_expert_b, expert_w_gate, expert_w_up, expert_w_down, final_norm):
    b, s, d = x.shape
    assert d == D_MODEL and s % FA_B == 0 and (b * s) % (MOE_N * MOE_SUB) == 0 and TM == DSA_SC
    depth = w_in.shape[0]
    n_sel = min(IDX_TOPK_MAX, s // 4)
    tab = _rope_tables(positions)
    xf = x.reshape(b * s, d)
    wg, wu, wd = expert_w_gate.astype(BF16), expert_w_up.astype(BF16), expert_w_down.astype(BF16)
    y_ffn = None
    for l in range(depth):
        xf, p = _inproj(xf, y_ffn, norm_mix[l].reshape(1, d), _pack_w_in(w_in[l]), tab, fox_f_bias[l],
                        mla_q_norm[l], mla_kv_norm[l], mla_w_uq[l], mla_w_ukv[l], s)
        y_ssd = _ssd(p["zx"], p["gt"], ssd_conv_w[l], ssd_conv_b[l], ssd_dt_bias[l], ssd_a_log[l], ssd_d[l],
                     ssd_norm[l], b, s)
        y_dsa = _dsa(p["iq"], p["gt"], p["dq"], p["ik"], p["dk"], p["dv"], b, s, n_sel)
        y_att = _flash([(p["mq"], p["mk"], p["mv"]), (p["fq"], p["fk"], p["fv"])], b, s)
        xf, h2, cw, rs = _outproj(xf, (y_ssd, y_dsa, y_att), w_out[l], norm_ffn[l].reshape(1, d),
                                  router_group_w[l], router_group_b[l], router_expert_w[l], router_expert_b[l])
        y_ffn = _moe(h2, cw, rs, wg, wu, wd, l)
    return _final(xf, y_ffn, final_norm.reshape(1, d)).reshape(b, s, d)
```

```python
import functools
import math

import jax
import jax.numpy as jnp
from jax import lax
from jax.experimental import pallas as pl
from jax.experimental.pallas import tpu as pltpu

F32 = jnp.float32
BF16 = jnp.bfloat16
I32 = jnp.int32
EPS = 1e-6
ROPE_THETA = 500000.0

D_MODEL = 1024
SSD_HEADS, SSD_HEAD_DIM, SSD_GROUPS, SSD_STATE, SSD_CONV = 4, 64, 2, 128, 4
SSD_D_INNER = SSD_HEADS * SSD_HEAD_DIM
SSD_CONV_DIM = SSD_D_INNER + 2 * SSD_GROUPS * SSD_STATE
DSA_HEADS, DSA_HEAD_DIM, IDX_HEADS, IDX_DIM, IDX_TOPK_MAX = 4, 64, 8, 64, 256
MLA_HEADS, MLA_Q_RANK, MLA_KV_RANK, MLA_NOPE, MLA_ROPE, MLA_V = 4, 256, 128, 64, 32, 64
FOX_HEADS, FOX_HEAD_DIM = 4, 64
MOE_GROUPS, MOE_EPG, MOE_EXPERTS, MOE_FF = 4, 8, 32, 256

LANES = 128
VMEM_LIMIT = 56 * 1024 * 1024
INT_MIN = -(2 ** 31)
NEG_BIG = -1e30

TM = 512
SSD_Q = 256
DSA_BQ = 256
DSA_SC = 512
FA_B = 512
MOE_N = 512
MOE_ALIGN = 16
MOE_CH = 64
MOE_EPS = 2
MOE_SUB = 4
MOE_NS = -(-(2 * MOE_N + MOE_EXPERTS * MOE_ALIGN + MOE_CH) // LANES) * LANES

C_ZX, C_GT, C_DQ, C_KV, C_IK, C_IQ, C_MC, C_FQ, C_FK, C_FV, C_END = (
    0, 1024, 1152, 1408, 1536, 1664, 2176, 2688, 3200, 3712, 4224)
G_FOX, G_DT, G_IW = 0, 4, 8
LOG2E = math.log2(math.e)
HEAD_W = 128
FA_HEADS = 4
VT_ROWS = 80
FOX_ONE_Q, FOX_F_Q, FOX_F_K, FOX_ONE_K = 64, 67, 64, 67


def _cparams(sem):
    return pltpu.CompilerParams(dimension_semantics=sem, vmem_limit_bytes=VMEM_LIMIT)


def _rms(xf, g):
    return xf * lax.rsqrt(jnp.mean(xf * xf, axis=-1, keepdims=True) + EPS) * g


def _softplus(x):
    return jnp.maximum(x, 0.0) + jnp.log1p(jnp.exp(-jnp.abs(x)))


def _silu(x):
    return x * jax.nn.sigmoid(x)


def _rope_apply(a, c, sa, sb, half):
    return a * c + pltpu.roll(a, LANES - half, 1) * sa + pltpu.roll(a, half, 1) * sb


def _dot_nt(a, b):
    return lax.dot_general(a, b, (((1,), (1,)), ((), ())), preferred_element_type=F32)


def _split3(x):
    hi = x.astype(BF16)
    r = x - hi.astype(F32)
    mid = r.astype(BF16)
    return hi, mid, (r - mid.astype(F32)).astype(BF16)


def _dot_f32_by_mask(mask, x, mask_left):
    m = jnp.where(mask, 1.0, 0.0).astype(BF16)
    out = None
    for piece in _split3(x):
        d = jnp.dot(m, piece, preferred_element_type=F32) if mask_left else jnp.dot(piece, m, preferred_element_type=F32)
        out = d if out is None else out + d
    return out


def _rope_tab_kernel(pos_ref, inv_ref, o_ref):
    pos = pos_ref[...].astype(F32)
    lane = lax.broadcasted_iota(I32, (1, LANES), 1)
    ang = pos * inv_ref[0:1, :]
    c, s = jnp.cos(ang), jnp.sin(ang)
    jm = lane - 64
    first = (lane < 8) | ((jm >= 0) & (jm < 16))
    second = ((lane >= 8) & (lane < 16)) | ((jm >= 16) & (jm < 32))
    o_ref[:, 0:128] = c
    o_ref[:, 128:256] = jnp.where(first, -s, 0.0)
    o_ref[:, 256:384] = jnp.where(second, s, 0.0)


def _rope_tables(positions):
    t = positions.size
    half_d, half_m = DSA_HEAD_DIM // 4 // 2, MLA_ROPE // 2
    inv_d = ROPE_THETA ** (-jnp.arange(half_d, dtype=F32) / half_d)
    inv_m = ROPE_THETA ** (-jnp.arange(half_m, dtype=F32) / half_m)
    lane = jnp.arange(LANES)
    jm = lane - 64
    row = jnp.where(lane < 2 * half_d, inv_d[lane % half_d],
                    jnp.where((jm >= 0) & (jm < 2 * half_m), inv_m[jm % half_m], 0.0))
    inv = jnp.zeros((8, LANES), F32).at[0].set(row)
    tm = 1024
    return pl.pallas_call(
        _rope_tab_kernel,
        grid=(t // tm,),
        in_specs=[pl.BlockSpec((tm, 1), lambda i: (i, 0)), pl.BlockSpec((8, LANES), lambda i: (0, 0))],
        out_specs=pl.BlockSpec((tm, 3 * LANES), lambda i: (i, 0)),
        out_shape=jax.ShapeDtypeStruct((t, 3 * LANES), F32),
        compiler_params=_cparams(("parallel",)),
        name="rope_tables",
    )(positions.reshape(t, 1), inv)


def _rope_patterns(tab_ref):
    lane = lax.broadcasted_iota(I32, (1, LANES), 1)
    in_d0, in_d1 = lane < 16, (lane >= 64) & (lane < 80)
    in_m = (lane >= 64) & (lane < 96)
    pat_d, pat_m = [], []
    for i, neutral in enumerate((1.0, 0.0, 0.0)):
        t = tab_ref[:, i * LANES:(i + 1) * LANES]
        pat_d.append(jnp.where(in_d0, t, jnp.where(in_d1, pltpu.roll(t, 64, 1), neutral)))
        pat_m.append(jnp.where(in_m, t, neutral))
    return pat_d, pat_m


def _inproj_kernel(*refs, has_add, tiles_per_seq):
    refs = list(refs)
    x_ref = refs.pop(0)
    yp_ref = refs.pop(0) if has_add else None
    g_ref, w_ref, tab_ref, fb_ref, qn_ref, kn_ref, wq_ref, wk_ref, wv_ref = refs[:9]
    refs = refs[9:]
    xo_ref = refs.pop(0) if has_add else None
    (zx_ref, gt_ref, dq_ref, dk_ref, dv_ref, ik_ref, iq_ref, mq_ref, mk_ref, mv_ref, fq_ref, fk_ref, fv_ref,
     carry_s) = refs
    x = x_ref[...]
    if has_add:
        x = x + yp_ref[...].astype(F32)
        xo_ref[...] = x
    h = _rms(x, g_ref[...]).astype(BF16)

    def proj(a, b):
        return jnp.dot(h, w_ref[:, a:b], preferred_element_type=F32)

    zx_ref[...] = proj(C_ZX, C_GT)
    gates = proj(C_GT, C_DQ)
    gt_ref[...] = gates
    (c_d, sa_d, sb_d), (c_m, sa_m, sb_m) = _rope_patterns(tab_ref)
    q = proj(C_DQ, C_KV)
    for i in range(2):
        dq_ref[:, i * 128:(i + 1) * 128] = _rope_apply(q[:, i * 128:(i + 1) * 128], c_d, sa_d, sb_d, 8).astype(BF16)
    kv = proj(C_KV, C_IK)
    is_k = lax.broadcasted_iota(I32, (1, LANES), 1) < 64
    kvr = _rope_apply(kv, jnp.where(is_k, c_d, 1.0), jnp.where(is_k, sa_d, 0.0), jnp.where(is_k, sb_d, 0.0), 8)
    dk_ref[...] = kvr[:, 0:64].astype(BF16)
    dv_ref[0, 0:64, :] = kvr.astype(BF16).astype(F32).T[64:128, :].astype(BF16)
    dv_ref[0, 64:VT_ROWS, :] = jnp.ones((VT_ROWS - 64, kvr.shape[0]), BF16)
    ikb = proj(C_IK, C_IQ)
    ik_ref[...] = _rope_apply(ikb, c_d, sa_d, sb_d, 8)[:, 0:64].astype(BF16)
    iq = proj(C_IQ, C_MC)
    for i in range(4):
        iq_ref[:, i * 128:(i + 1) * 128] = _rope_apply(iq[:, i * 128:(i + 1) * 128], c_d, sa_d, sb_d, 8).astype(BF16)
    ones_hi = jnp.where(lax.broadcasted_iota(I32, (1, FA_HEADS * HEAD_W), 1) % HEAD_W >= 64, 1.0, 0.0)

    mc = proj(C_MC, C_FQ)
    cq = _rms(mc[:, 0:256], qn_ref[...]).astype(BF16)
    ckv = _rms(mc[:, 256:384], kn_ref[...]).astype(BF16)
    kr = _rope_apply(mc[:, 384:512], c_m, sa_m, sb_m, 16)
    mq = jnp.dot(cq, wq_ref[...], preferred_element_type=F32)
    mk = jnp.dot(ckv, wk_ref[...], preferred_element_type=F32)
    for hd in range(MLA_HEADS):
        sl = slice(hd * HEAD_W, (hd + 1) * HEAD_W)
        mq_ref[:, sl] = _rope_apply(mq[:, sl], c_m, sa_m, sb_m, 16).astype(BF16)
        mk_ref[:, sl] = (mk[:, sl] + kr).astype(BF16)
    mv_ref[...] = (jnp.dot(ckv, wv_ref[...], preferred_element_type=F32) + ones_hi).astype(BF16)

    @pl.when(pl.program_id(0) % tiles_per_seq == 0)
    def _():
        carry_s[...] = jnp.zeros_like(carry_s)

    tm = x.shape[0]
    log_f = -_softplus(-(gates + fb_ref[...]))
    ri = lax.broadcasted_iota(I32, (tm, tm), 0)
    ci = lax.broadcasted_iota(I32, (tm, tm), 1)
    cs = _dot_f32_by_mask(ri >= ci, log_f, True) + carry_s[0:1, :]
    carry_s[0:1, :] = cs[tm - 1:tm, :]
    f2 = cs * LOG2E
    lane = lax.broadcasted_iota(I32, (1, HEAD_W), 1)

    def pieces(f, base):
        hi, mid, lo = _split3(f)
        return jnp.where(lane == base, hi.astype(F32),
                         jnp.where(lane == base + 1, mid.astype(F32), jnp.where(lane == base + 2, lo.astype(F32), 0.0)))

    def ones(base):
        return jnp.where((lane >= base) & (lane < base + 3), 1.0, 0.0)

    fq = proj(C_FQ, C_FK)
    fk = proj(C_FK, C_FV)
    for hd in range(FOX_HEADS):
        fh = f2[:, G_FOX + hd:G_FOX + hd + 1]
        sl = slice(hd * HEAD_W, (hd + 1) * HEAD_W)
        fq_ref[:, sl] = (fq[:, sl] + pieces(fh, FOX_F_Q) + ones(FOX_ONE_Q)).astype(BF16)
        fk_ref[:, sl] = (fk[:, sl] - pieces(fh, FOX_F_K) + ones(FOX_ONE_K)).astype(BF16)
    fv_ref[...] = (proj(C_FV, C_END) + ones_hi).astype(BF16)


def _pack_w_in(w):
    d = w.shape[0]
    z = lambda n: jnp.zeros((d, n), F32)
    o = 0
    parts = {}
    for name, width in (("z", 256), ("xbc", 768), ("dt", 4), ("dq", 256), ("dk", 64), ("dv", 64), ("iq", 512),
                        ("ik", 64), ("iw", 8), ("cq", 256), ("ckv", 128), ("kr", 32), ("fq", 256), ("fk", 256),
                        ("fv", 256), ("ff", 4)):
        parts[name] = w[:, o:o + width]
        o += width
    scale = DSA_HEAD_DIM ** -0.5

    def per_head(a):
        return jnp.pad(a.reshape(d, FA_HEADS, 64), ((0, 0), (0, 0), (0, HEAD_W - 64))).reshape(d, FA_HEADS * HEAD_W)

    cat = jnp.concatenate([
        parts["z"], parts["xbc"],
        parts["ff"], parts["dt"], parts["iw"], z(112),
        parts["dq"] * (scale * LOG2E),
        parts["dk"], parts["dv"],
        parts["ik"], z(64),
        parts["iq"],
        parts["cq"], parts["ckv"], z(64), parts["kr"], z(32),
        per_head(parts["fq"] * (FOX_HEAD_DIM ** -0.5 * LOG2E)), per_head(parts["fk"]), per_head(parts["fv"])], axis=1)
    return cat.astype(BF16)


def _mla_weights(w_uq, w_ukv):
    dqk = MLA_NOPE + MLA_ROPE
    wq = jnp.pad(w_uq.reshape(MLA_Q_RANK, MLA_HEADS, dqk) * (dqk ** -0.5 * LOG2E), ((0, 0), (0, 0), (0, HEAD_W - dqk)))
    wkv = w_ukv.reshape(MLA_KV_RANK, MLA_HEADS, MLA_NOPE + MLA_V)
    wk = jnp.pad(wkv[:, :, :MLA_NOPE], ((0, 0), (0, 0), (0, HEAD_W - MLA_NOPE)))
    wv = jnp.pad(wkv[:, :, MLA_NOPE:], ((0, 0), (0, 0), (0, HEAD_W - MLA_V)))
    flat = lambda a: a.reshape(a.shape[0], MLA_HEADS * HEAD_W).astype(BF16)
    return flat(wq), flat(wk), flat(wv)


def _inproj(x, y_prev, g, w_cat, tab, f_bias, q_norm, kv_norm, w_uq, w_ukv, s):
    t = x.shape[0]
    has_add = y_prev is not None
    tok = lambda n: pl.BlockSpec((TM, n), lambda i: (i, 0))
    full = lambda a: pl.BlockSpec(a.shape, lambda i: (0,) * a.ndim)
    fb = jnp.zeros((1, LANES), F32).at[0, G_FOX:G_FOX + FOX_HEADS].set(f_bias)
    params = [g, w_cat, tab, fb, q_norm.reshape(1, -1), kv_norm.reshape(1, -1), *_mla_weights(w_uq, w_ukv)]
    ins = [x] + ([y_prev] if has_add else []) + params
    in_specs = ([tok(D_MODEL)] + ([tok(D_MODEL)] if has_add else [])
                + [tok(3 * LANES) if a is tab else full(a) for a in params])
    outs = ([("x", D_MODEL, F32)] if has_add else []) + [
        ("zx", 1024, F32), ("gt", 128, F32), ("dq", 256, BF16), ("dk", 64, BF16), ("dv", 64, BF16),
        ("ik", 64, BF16), ("iq", 512, BF16), ("mq", 512, BF16), ("mk", 512, BF16), ("mv", 512, BF16),
        ("fq", 512, BF16), ("fk", 512, BF16), ("fv", 512, BF16)]
    vt_rows = {"dv": VT_ROWS}
    res = pl.pallas_call(
        functools.partial(_inproj_kernel, has_add=has_add, tiles_per_seq=s // TM),
        grid=(t // TM,),
        in_specs=in_specs,
        out_specs=[pl.BlockSpec((1, vt_rows[nm], TM), lambda i: (i, 0, 0)) if nm in vt_rows else tok(n)
                   for nm, n, _ in outs],
        out_shape=[jax.ShapeDtypeStruct((t // TM, vt_rows[nm], TM) if nm in vt_rows else (t, n), dt)
                   for nm, n, dt in outs],
        scratch_shapes=[pltpu.VMEM((8, LANES), F32)],
        compiler_params=_cparams(("arbitrary",)),
        name="inproj",
    )(*ins)
    res = list(res)
    x_new = res.pop(0) if has_add else x
    return x_new, dict(zip([n for n, _, _ in outs if n != "x"], res))


def _ssd_kernel(zx_ref, gt_ref, cw_ref, cb_ref, pr_ref, pc_ref, drow_ref, nw_ref, o_ref, ext_s, st_s, y_s, *, q):
    @pl.when(pl.program_id(1) == 0)
    def _():
        ext_s[0:8, :] = jnp.zeros((8, SSD_CONV_DIM), F32)
        st_s[...] = jnp.zeros_like(st_s)

    raw = zx_ref[:, 256:1024]
    ext_s[8:8 + q, :] = raw
    acc = jnp.broadcast_to(cb_ref[...], (q, SSD_CONV_DIM))
    for j in range(SSD_CONV):
        acc = acc + cw_ref[j:j + 1, :] * ext_s[5 + j:5 + j + q, :]
    ext_s[0:8, :] = raw[q - 8:q, :]
    xbc = _silu(acc)
    xs = xbc[:, 0:SSD_D_INNER]

    g = gt_ref[...]
    lane = lax.broadcasted_iota(I32, (1, LANES), 1)
    dtc = _softplus(g + pr_ref[0:1, :])
    a_r = jnp.where((lane >= G_DT) & (lane < G_DT + SSD_HEADS), -jnp.exp(pr_ref[1:2, :]), 0.0)
    ri = lax.broadcasted_iota(I32, (q, q), 0)
    ci = lax.broadcasted_iota(I32, (q, q), 1)
    tri = ri >= ci
    acs_c = _dot_f32_by_mask(tri, dtc * a_r, True)
    sub = lax.broadcasted_iota(I32, (LANES, 1), 0)
    dtr = _softplus(g.T + pc_ref[:, 0:1])
    a_c = jnp.where((sub >= G_DT) & (sub < G_DT + SSD_HEADS), -jnp.exp(pc_ref[:, 1:2]), 0.0)
    acs_r = _dot_f32_by_mask(ri <= ci, dtr * a_c, False)

    rep = SSD_HEADS // SSD_GROUPS
    for gi in range(SSD_GROUPS):
        bg = xbc[:, SSD_D_INNER + gi * SSD_STATE:SSD_D_INNER + (gi + 1) * SSD_STATE]
        cg = xbc[:, SSD_D_INNER + (SSD_GROUPS + gi) * SSD_STATE:SSD_D_INNER + (SSD_GROUPS + gi + 1) * SSD_STATE]
        bt = bg.T.astype(BF16)
        cb16 = cg.astype(BF16)
        cbm = jnp.dot(cb16, bt, preferred_element_type=F32)
        for hh in range(rep):
            h = gi * rep + hh
            ac = acs_c[:, G_DT + h:G_DT + h + 1]
            ar = acs_r[G_DT + h:G_DT + h + 1, :]
            seg = jnp.where(tri, jnp.exp(ac - ar), 0.0)
            xh = xs[:, h * SSD_HEAD_DIM:(h + 1) * SSD_HEAD_DIM]
            xdt = xh * dtc[:, G_DT + h:G_DT + h + 1]
            yd = jnp.dot((cbm * seg).astype(BF16), xdt.astype(BF16), preferred_element_type=F32)
            aend = ac[q - 1:q, :]
            st = st_s[h]
            yo = jnp.dot(cb16, st.astype(BF16), preferred_element_type=F32) * jnp.exp(ac)
            st_s[h] = st * jnp.exp(aend) + jnp.dot(bt, (xdt * jnp.exp(aend - ac)).astype(BF16),
                                                   preferred_element_type=F32)
            y_s[:, h * SSD_HEAD_DIM:(h + 1) * SSD_HEAD_DIM] = yd + yo
    y = y_s[...] + drow_ref[...] * xs
    y = y * _silu(zx_ref[:, 0:SSD_D_INNER])
    o_ref[...] = _rms(y, nw_ref[...]).astype(BF16)


def _ssd(zx, gt, conv_w, conv_b, dt_bias, a_log, d_skip, norm_w, b, s):
    q = SSD_Q
    nc = s // q
    pr = jnp.zeros((8, LANES), F32).at[0, G_DT:G_DT + SSD_HEADS].set(dt_bias).at[1, G_DT:G_DT + SSD_HEADS].set(a_log)
    pc = pr.T
    drow = jnp.repeat(d_skip, SSD_HEAD_DIM).reshape(1, SSD_D_INNER)
    cb = conv_b.reshape(1, -1)
    nw = norm_w.reshape(1, -1)
    full = lambda a: pl.BlockSpec(a.shape, lambda i, j: (0,) * a.ndim)
    return pl.pallas_call(
        functools.partial(_ssd_kernel, q=q),
        grid=(b, nc),
        in_specs=[pl.BlockSpec((q, 1024), lambda i, j: (i * nc + j, 0)), pl.BlockSpec((q, LANES), lambda i, j: (i * nc + j, 0)),
                  full(conv_w), full(cb), full(pr), full(pc), full(drow), full(nw)],
        out_specs=pl.BlockSpec((q, SSD_D_INNER), lambda i, j: (i * nc + j, 0)),
        out_shape=jax.ShapeDtypeStruct((b * s, SSD_D_INNER), BF16),
        scratch_shapes=[pltpu.VMEM((8 + q, SSD_CONV_DIM), F32), pltpu.VMEM((SSD_HEADS, SSD_STATE, SSD_HEAD_DIM), F32),
                        pltpu.VMEM((q, SSD_D_INNER), F32)],
        compiler_params=_cparams(("parallel", "arbitrary")),
        name="ssd",
    )(zx, gt, conv_w, cb, pr, pc, drow, nw)


def _key_to_f32(k):
    return pltpu.bitcast(jnp.where(k < 0, k ^ jnp.int32(0x7FFFFFFF), k), F32)


def _dsa_kernel(iq_ref, gt_ref, q_ref, ik_ref, k_ref, vt_ref, o_ref, sc_s, qi_s, qs_s, thr_s, s16_s,
                *, bq, sc_w, n_sel):
    qb = pl.program_id(1)
    n_sc = ((qb + 1) * bq + sc_w - 1) // sc_w
    for j in range(IDX_HEADS):
        qi_s[j // 2, (j % 2) * bq:(j % 2 + 1) * bq, :] = iq_ref[:, j * IDX_DIM:(j + 1) * IDX_DIM]
    for h in range(DSA_HEADS):
        qs_s[h * bq:(h + 1) * bq, :] = q_ref[:, h * DSA_HEAD_DIM:(h + 1) * DSA_HEAD_DIM]
    w_t = (gt_ref[...] * ((IDX_HEADS * IDX_DIM) ** -0.5)).T
    qpos = qb * bq + lax.broadcasted_iota(I32, (1, bq), 1)

    def score_body(sc, carry):
        k0 = pl.multiple_of(sc * sc_w, sc_w)
        kidx = ik_ref[pl.ds(k0, sc_w), :]
        sco = jnp.zeros((sc_w, bq), F32)
        for jp in range(IDX_HEADS // 2):
            a = _dot_nt(kidx, qi_s[jp])
            for u in range(2):
                j = 2 * jp + u
                sco = sco + jnp.maximum(a[:, u * bq:(u + 1) * bq], 0.0) * w_t[G_IW + j:G_IW + j + 1, :]
        kpos = k0 + lax.broadcasted_iota(I32, (sc_w, 1), 0)
        sco = jnp.where(kpos <= qpos, sco, -jnp.inf)
        sc_s[sc] = sco
        s16_s[sc] = sco.astype(BF16)
        return carry

    lax.fori_loop(0, n_sc, score_body, 0)

    def count(pred):
        def body(sc, acc):
            for r in range(sc_w // 64):
                acc = acc + jnp.where(pred(sc_s[sc, r * 64:(r + 1) * 64, :]), 1.0, 0.0)
            return acc
        acc = lax.fori_loop(0, n_sc, body, jnp.zeros((64, bq), F32))
        return jnp.sum(acc, axis=0, keepdims=True)

    thr_s[0:1, :] = jnp.full((1, bq), -jnp.finfo(jnp.float32).max, F32)

    @pl.when((qb + 1) * bq > n_sel)
    def _():
        active = qpos + 1 > n_sel

        one16, zero16 = jnp.ones((), BF16), jnp.zeros((), BF16)

        def count16(c16):
            def body(sc, acc):
                a16 = jnp.zeros((64, bq), BF16)
                for r in range(sc_w // 64):
                    a16 = a16 + jnp.where(s16_s[sc, r * 64:(r + 1) * 64, :] >= c16, one16, zero16)
                return acc + a16.astype(F32)
            acc = lax.fori_loop(0, n_sc, body, jnp.zeros((64, bq), F32))
            return jnp.sum(acc, axis=0, keepdims=True)

        def coarse(it, lo):
            cand = lo ^ jnp.left_shift(jnp.int32(1), 31 - it)
            return jnp.where(count16(_key_to_f32(cand).astype(BF16)) >= n_sel, cand, lo)

        lo16 = lax.fori_loop(0, 16, coarse, jnp.full((1, bq), INT_MIN, I32))
        base = jnp.maximum(lo16, jnp.int32(INT_MIN + (1 << 16))) - jnp.int32(1 << 16)

        def cond(st):
            it, off, cnt_lo = st
            pending = active & (cnt_lo != n_sel)
            return (it < 32) & (jnp.max(jnp.where(pending, 1.0, 0.0)) > 0.0)

        def fine(it, off, cnt_lo):
            cand = off | jnp.left_shift(jnp.int32(1), 31 - it)
            thr_c = _key_to_f32(base + cand)
            cnt = count(lambda t: t >= thr_c)
            ok = cnt >= n_sel
            return jnp.where(ok, cand, off), jnp.where(ok, cnt, cnt_lo)

        def body(st):
            it, off, cnt_lo = st
            for u in range(4):
                off, cnt_lo = fine(it + u, off, cnt_lo)
            return it + 4, off, cnt_lo

        off, cnt_lo = fine(15, jnp.zeros((1, bq), I32), jnp.full((1, bq), 3.0e38, F32))
        _, off, cnt_lo = lax.while_loop(cond, body, (jnp.int32(16), off, cnt_lo))
        lo = base + off
        thr = _key_to_f32(lo)
        tie = active & (cnt_lo > n_sel)

        @pl.when(jnp.max(jnp.where(tie, 1.0, 0.0)) > 0.0)
        def _():
            need = n_sel - count(lambda t: t > thr)
            ri = lax.broadcasted_iota(I32, (sc_w, sc_w), 0)
            ci = lax.broadcasted_iota(I32, (sc_w, sc_w), 1)
            lower = jnp.where(ri >= ci, 1.0, 0.0).astype(BF16)

            def tie_body(sc, run):
                t = sc_s[sc]
                eq = (t == thr) & tie
                pre = jnp.dot(lower, jnp.where(eq, 1.0, 0.0).astype(BF16), preferred_element_type=F32)
                sc_s[sc] = jnp.where(eq & (run + pre > need), -jnp.inf, t)
                return run + pre[sc_w - 1:sc_w, :]

            lax.fori_loop(0, n_sc, tie_body, jnp.zeros((1, bq), F32))

        thr_s[0:1, :] = jnp.where(active, thr, thr_s[0:1, :])

    thr = thr_s[0:1, :]

    hq = DSA_HEADS * bq

    def att_body(sc, carry):
        m_prev, acc = carry
        k0 = pl.multiple_of(sc * sc_w, sc_w)
        st = _dot_nt(k_ref[pl.ds(k0, sc_w), :], qs_s[...])
        drop = jnp.where(sc_s[sc] >= thr, 0.0, 2 * NEG_BIG)
        st = st + jnp.tile(drop, (1, DSA_HEADS))
        m_cur = jnp.max(jnp.max(st.reshape(sc_w // 64, 64, hq), axis=0), axis=0, keepdims=True)
        m_next = jnp.maximum(m_prev, m_cur)
        p = jnp.exp2(st - m_next)
        pv = jnp.dot(vt_ref[sc], p.astype(BF16), preferred_element_type=F32)
        return m_next, jnp.exp2(m_prev - m_next) * acc + pv

    def att_body2(i, carry):
        return att_body(2 * i + 1, att_body(2 * i, carry))

    carry = lax.fori_loop(0, n_sc // 2, att_body2, (jnp.full((1, hq), NEG_BIG, F32), jnp.zeros((VT_ROWS, hq), F32)))
    _, acc = lax.cond(n_sc % 2 == 1, lambda c: att_body(n_sc - 1, c), lambda c: c, carry)
    out_t = jnp.concatenate([acc[0:DSA_HEAD_DIM, :] / acc[DSA_HEAD_DIM:DSA_HEAD_DIM + 1, :],
                             jnp.zeros((LANES - DSA_HEAD_DIM, hq), F32)], axis=0).T
    for h in range(DSA_HEADS):
        o_ref[:, h * DSA_HEAD_DIM:(h + 1) * DSA_HEAD_DIM] = out_t[h * bq:(h + 1) * bq, 0:DSA_HEAD_DIM].astype(BF16)


def _dsa(iq, gt, dq, ik, dk, dvt, b, s, n_sel):
    bq, sc_w = DSA_BQ, DSA_SC
    nq = s // bq
    qblk = lambda n: pl.BlockSpec((bq, n), lambda i, j: (i * nq + j, 0))
    kblk = pl.BlockSpec((s, 64), lambda i, j: (i, 0))
    return pl.pallas_call(
        functools.partial(_dsa_kernel, bq=bq, sc_w=sc_w, n_sel=n_sel),
        grid=(b, nq),
        in_specs=[qblk(512), qblk(LANES), qblk(256), kblk, kblk,
                  pl.BlockSpec((s // sc_w, VT_ROWS, sc_w), lambda i, j: (i, 0, 0))],
        out_specs=qblk(256),
        out_shape=jax.ShapeDtypeStruct((b * s, 256), BF16),
        scratch_shapes=[pltpu.VMEM((s // sc_w, sc_w, bq), F32), pltpu.VMEM((IDX_HEADS // 2, 2 * bq, IDX_DIM), BF16),
                        pltpu.VMEM((DSA_HEADS * bq, DSA_HEAD_DIM), BF16), pltpu.VMEM((8, bq), F32),
                        pltpu.VMEM((s // sc_w, sc_w, bq), BF16)],
        compiler_params=_cparams(("parallel", "arbitrary")),
        name="dsa",
    )(iq, gt, dq, ik, dk, dvt)


def _flash_kernel(qt_ref, kt_ref, *refs, blk, sets):
    qkv, (o_ref, m_s, acc_s) = refs[:3 * sets], refs[3 * sets:]
    t = pl.program_id(1)
    qi, ki = qt_ref[t], kt_ref[t]

    @pl.when(ki == 0)
    def _():
        m_s[...] = jnp.full(m_s.shape, NEG_BIG, F32)
        acc_s[...] = jnp.zeros_like(acc_s)

    def step(diagonal):
        if diagonal:
            keep = lax.broadcasted_iota(I32, (blk, blk), 0) >= lax.broadcasted_iota(I32, (blk, blk), 1)
        for g in range(sets):
            q_ref, k_ref, v_ref = qkv[3 * g:3 * g + 3]
            for h in range(FA_HEADS):
                sl = slice(h * HEAD_W, (h + 1) * HEAD_W)
                hh = g * FA_HEADS + h
                s = _dot_nt(q_ref[:, sl], k_ref[:, sl])
                if diagonal:
                    s = jnp.where(keep, s, 2 * NEG_BIG)
                m_prev = m_s[hh]
                m_next = jnp.maximum(m_prev, jnp.max(s, axis=1, keepdims=True))
                p = jnp.exp2(s - jnp.tile(m_next, (1, blk // LANES)))
                pv = jnp.dot(p.astype(BF16), v_ref[:, sl], preferred_element_type=F32)
                acc_s[hh] = jnp.exp2(m_prev - m_next) * acc_s[hh] + pv
                m_s[hh] = m_next

    @pl.when(ki < qi)
    def _():
        step(False)

    @pl.when(ki == qi)
    def _():
        step(True)
        for hh in range(sets * FA_HEADS):
            acc = acc_s[hh]
            o_ref[:, hh * 64:(hh + 1) * 64] = (acc / pltpu.roll(acc, 64, 1))[:, 0:64].astype(BF16)


def _flash(qkv, b, s):
    blk = FA_B
    nb = s // blk
    pairs = [(i, j) for i in range(nb) for j in range(i + 1)]
    qt = jnp.asarray([p[0] for p in pairs], I32)
    kt = jnp.asarray([p[1] for p in pairs], I32)
    w = FA_HEADS * HEAD_W
    sets = len(qkv)
    q_spec = pl.BlockSpec((blk, w), lambda i, t, qt, kt: (i * nb + qt[t], 0))
    kv_spec = pl.BlockSpec((blk, w), lambda i, t, qt, kt: (i * nb + kt[t], 0))
    grid_spec = pltpu.PrefetchScalarGridSpec(
        num_scalar_prefetch=2,
        grid=(b, len(pairs)),
        in_specs=[q_spec, kv_spec, kv_spec] * sets,
        out_specs=pl.BlockSpec((blk, sets * FA_HEADS * 64), lambda i, t, qt, kt: (i * nb + qt[t], 0)),
        scratch_shapes=[pltpu.VMEM((sets * FA_HEADS, blk, LANES), F32), pltpu.VMEM((sets * FA_HEADS, blk, HEAD_W), F32)],
    )
    return pl.pallas_call(
        functools.partial(_flash_kernel, blk=blk, sets=sets),
        grid_spec=grid_spec,
        out_shape=jax.ShapeDtypeStruct((b * s, sets * FA_HEADS * 64), BF16),
        compiler_params=_cparams(("parallel", "arbitrary")),
        name="flash",
    )(qt, kt, *[a for triple in qkv for a in triple])


def _outproj_kernel(x_ref, *refs, n_mix):
    y_refs, (wo_ref, g_ref, wr_ref, br_ref, xo_ref, h_ref, cw_ref, rs_ref) = refs[:n_mix], refs[n_mix:]
    acc = x_ref[...]
    row = 0
    for y_ref in y_refs:
        acc = acc + jnp.dot(y_ref[...], wo_ref[row:row + y_ref.shape[1], :], preferred_element_type=F32)
        row += y_ref.shape[1]
    xo_ref[...] = acc
    h2 = _rms(acc, g_ref[...])
    h_ref[...] = h2.astype(BF16)
    h_hi, h_lo, _ = _split3(h2)
    lg = (jnp.dot(h_hi, wr_ref[0], preferred_element_type=F32) + jnp.dot(h_lo, wr_ref[0], preferred_element_type=F32)
          + jnp.dot(h_hi, wr_ref[1], preferred_element_type=F32) + br_ref[...])
    lane = lax.broadcasted_iota(I32, (1, LANES), 1)
    lanef = lane.astype(F32)
    is_g = (lane >= MOE_EXPERTS) & (lane < MOE_EXPERTS + MOE_GROUPS)
    gl = jnp.where(is_g, lg, -jnp.inf)
    gmax = jnp.max(gl, axis=1, keepdims=True)
    gidx = jnp.min(jnp.where(gl == gmax, lanef, 999.0), axis=1, keepdims=True) - MOE_EXPERTS
    g_p = 1.0 / jnp.sum(jnp.where(is_g, jnp.exp(gl - gmax), 0.0), axis=1, keepdims=True)
    in_g = (lane < MOE_EXPERTS) & (jnp.floor(lanef * (1.0 / MOE_EPG)) == gidx)
    el = jnp.where(in_g, lg, -jnp.inf)
    m1 = jnp.max(el, axis=1, keepdims=True)
    i1 = jnp.min(jnp.where(el == m1, lanef, 999.0), axis=1, keepdims=True)
    el2 = jnp.where(lanef == i1, -jnp.inf, el)
    m2 = jnp.max(el2, axis=1, keepdims=True)
    i2 = jnp.min(jnp.where(el2 == m2, lanef, 999.0), axis=1, keepdims=True)
    t = jnp.exp(m2 - m1)
    w1 = 1.0 / (1.0 + t)
    cw_ref[...] = jnp.where(lanef == i1, w1 * g_p, jnp.where(lanef == i2, t * w1 * g_p, 0.0))
    rs_ref[...] = jnp.where((lanef == i1) | (lanef == i2), 1.0, 0.0).astype(BF16)


def _outproj(x, ys, w_out, g, w_rg, b_rg, w_re, b_re):
    t = x.shape[0]
    wo = w_out.astype(BF16)
    wr = jnp.zeros((D_MODEL, LANES), F32).at[:, 0:MOE_EXPERTS].set(w_re).at[:, MOE_EXPERTS:MOE_EXPERTS + MOE_GROUPS].set(w_rg)
    wr_hi = wr.astype(BF16)
    wr = jnp.stack([wr_hi, (wr - wr_hi.astype(F32)).astype(BF16)])
    br =jnp.zeros((1, LANES), F32).at[0, 0:MOE_EXPERTS].set(b_re).at[0, MOE_EXPERTS:MOE_EXPERTS + MOE_GROUPS].set(b_rg)
    tok = lambda n: pl.BlockSpec((TM, n), lambda i: (i, 0))
    full = lambda a: pl.BlockSpec(a.shape, lambda i: (0,) * a.ndim)
    return pl.pallas_call(
        functools.partial(_outproj_kernel, n_mix=len(ys)),
        grid=(t // TM,),
        in_specs=[tok(D_MODEL)] + [tok(y.shape[1]) for y in ys] + [full(wo), full(g), full(wr), full(br)],
        out_specs=[tok(D_MODEL), tok(D_MODEL), tok(LANES), tok(LANES)],
        out_shape=[jax.ShapeDtypeStruct((t, D_MODEL), F32), jax.ShapeDtypeStruct((t, D_MODEL), BF16),
                   jax.ShapeDtypeStruct((t, LANES), F32), jax.ShapeDtypeStruct((t, LANES), BF16)],
        compiler_params=_cparams(("parallel",)),
        name="outproj_router",
    )(x, *ys, wo, g, wr, br)


def _moe_plan_kernel(rs_ref, cw_ref, meta_ref, col_ref, row_ref, *, n):
    sel = rs_ref[...]
    self32 = sel.astype(F32)
    ti = lax.broadcasted_iota(I32, (n, n), 0)
    tj = lax.broadcasted_iota(I32, (n, n), 1)
    rank = jnp.dot(jnp.where(ti > tj, 1.0, 0.0).astype(BF16), sel, preferred_element_type=F32)
    cnt = jnp.sum(self32, axis=0, keepdims=True).astype(I32)
    cpad = ((cnt + (MOE_ALIGN - 1)) // MOE_ALIGN) * MOE_ALIGN
    li = lax.broadcasted_iota(I32, (LANES, LANES), 0)
    lj = lax.broadcasted_iota(I32, (LANES, LANES), 1)
    off = _dot_f32_by_mask(li < lj, jnp.broadcast_to(cpad.astype(F32), (8, LANES)), False)[0:1, :]
    dest = off + rank
    lane = lax.broadcasted_iota(I32, (1, LANES), 1)
    lanef = lane.astype(F32)
    on = self32 > 0.5
    e_lo = jnp.min(jnp.where(on, lanef, 999.0), axis=1, keepdims=True)
    e_hi = jnp.max(jnp.where(on, lanef, -1.0), axis=1, keepdims=True)
    is_lo, is_hi = lanef == e_lo, lanef == e_hi
    cw = cw_ref[...]
    pick = lambda m, a: jnp.sum(jnp.where(m, a, 0.0), axis=1, keepdims=True)
    d_lo, d_hi, w_lo, w_hi = pick(is_lo, dest), pick(is_hi, dest), pick(is_lo, cw), pick(is_hi, cw)
    colv = jnp.where(lane == 0, d_lo, jnp.where(lane == 1, d_hi, jnp.where(lane == 2, w_lo, jnp.where(lane == 3, w_hi, 0.0))))
    col_ref[...] = colv
    row_ref[0] = colv.T[0:8, :]
    sub = lax.broadcasted_iota(I32, (8, LANES), 0)
    meta_ref[0] = jnp.where(sub == 0, off.astype(I32), jnp.where(sub == 1, cnt, 0))


def _moe_kernel(meta_ref, h_ref, col_ref, row_ref, wg_ref, wu_ref, wd_ref, o_ref, s_s, y_s, *, n, ns):
    t, e = pl.program_id(0), pl.program_id(1)

    @pl.when(e == 0)
    def _():
        for g in range(MOE_SUB):
            d_lo, d_hi = row_ref[g, 0:1, :], row_ref[g, 1:2, :]
            hg = h_ref[g * n:(g + 1) * n, :]
            for r in range(ns // LANES):
                si = (r * LANES + lax.broadcasted_iota(I32, (LANES, 1), 0)).astype(F32)
                perm = jnp.where((si == d_lo) | (si == d_hi), 1.0, 0.0).astype(BF16)
                s_s[g, r * LANES:(r + 1) * LANES, :] = jnp.dot(perm, hg, preferred_element_type=F32).astype(BF16)
        y_s[...] = jnp.zeros_like(y_s)

    def mlp(j, starts, store_ok):
        starts = [pl.multiple_of(r0, MOE_ALIGN) for r0 in starts]
        xs = jnp.concatenate([s_s[g, pl.ds(starts[g], MOE_CH), :] for g in range(MOE_SUB)], axis=0)
        gate = jnp.dot(xs, wg_ref[j], preferred_element_type=F32)
        up = jnp.dot(xs, wu_ref[j], preferred_element_type=F32)
        hid = (_silu(gate) * up).astype(BF16)
        y = jnp.dot(hid, wd_ref[j], preferred_element_type=F32).astype(BF16)
        for g in range(MOE_SUB):
            def store(g=g):
                y_s[g, pl.ds(starts[g], MOE_CH), :] = y[g * MOE_CH:(g + 1) * MOE_CH, :]
            if store_ok is None:
                store()
            else:
                pl.when(store_ok[g])(store)

    def meta(g, j, what):
        return meta_ref[(t * MOE_SUB + g) * 2 * MOE_EXPERTS + what * MOE_EXPERTS + e * MOE_EPS + j]

    offs = [[meta(g, j, 0) for g in range(MOE_SUB)] for j in range(MOE_EPS)]
    cnts = [[meta(g, j, 1) for g in range(MOE_SUB)] for j in range(MOE_EPS)]
    for j in range(MOE_EPS):
        mlp(j, offs[j], None)
    for j in range(MOE_EPS):
        ends = [offs[j][g] + ((cnts[j][g] + MOE_ALIGN - 1) // MOE_ALIGN) * MOE_ALIGN for g in range(MOE_SUB)]
        nch = [(cnts[j][g] + MOE_CH - 1) // MOE_CH for g in range(MOE_SUB)]

        def body(i, carry, j=j, ends=ends, nch=nch):
            more = [i < nch[g] for g in range(MOE_SUB)]
            starts = [jnp.where(more[g], jnp.minimum(offs[j][g] + i * MOE_CH, ends[g] - MOE_CH), offs[j][g])
                      for g in range(MOE_SUB)]
            mlp(j, starts, more)
            return carry

        lax.fori_loop(1, functools.reduce(jnp.maximum, nch), body, 0)

    @pl.when(e == pl.num_programs(1) - 1)
    def _():
        rows = 256
        kc = ns // next(k for k in (3, 2, 1) if ns % (k * LANES) == 0)
        for g in range(MOE_SUB):
            for c in range(n // rows):
                r0 = g * n + c * rows
                cv = col_ref[r0:r0 + rows, :]
                d_lo, d_hi, w_lo, w_hi = cv[:, 0:1], cv[:, 1:2], cv[:, 2:3], cv[:, 3:4]
                acc = jnp.zeros((rows, D_MODEL), F32)
                for r in range(ns // kc):
                    si = (r * kc + lax.broadcasted_iota(I32, (1, kc), 1)).astype(F32)
                    pw = (jnp.where(si == d_lo, w_lo, 0.0) + jnp.where(si == d_hi, w_hi, 0.0)).astype(BF16)
                    acc = acc + jnp.dot(pw, y_s[g, r * kc:(r + 1) * kc, :], preferred_element_type=F32)
                o_ref[r0:r0 + rows, :] = acc.astype(BF16)


def _moe(h2, cw, rs, wg, wu, wd, layer):
    t = h2.shape[0]
    n, ns = MOE_N, MOE_NS
    nt = t // n
    meta, col, row = pl.pallas_call(
        functools.partial(_moe_plan_kernel, n=n),
        grid=(nt,),
        in_specs=[pl.BlockSpec((n, LANES), lambda i: (i, 0)), pl.BlockSpec((n, LANES), lambda i: (i, 0))],
        out_specs=[pl.BlockSpec((1, 8, LANES), lambda i: (i, 0, 0)), pl.BlockSpec((n, LANES), lambda i: (i, 0)),
                   pl.BlockSpec((1, 8, n), lambda i: (i, 0, 0))],
        out_shape=[jax.ShapeDtypeStruct((nt, 8, LANES), I32), jax.ShapeDtypeStruct((t, LANES), F32),
                   jax.ShapeDtypeStruct((nt, 8, n), F32)],
        compiler_params=_cparams(("parallel",)),
        name="moe_plan",
    )(rs, cw)
    meta_flat = meta[:, 0:2, 0:MOE_EXPERTS].reshape(-1)
    grid_spec = pltpu.PrefetchScalarGridSpec(
        num_scalar_prefetch=1,
        grid=(nt // MOE_SUB, MOE_EXPERTS // MOE_EPS),
        in_specs=[pl.BlockSpec((MOE_SUB * n, D_MODEL), lambda i, e, m: (i, 0)),
                  pl.BlockSpec((MOE_SUB * n, LANES), lambda i, e, m: (i, 0)),
                  pl.BlockSpec((MOE_SUB, 8, n), lambda i, e, m: (i, 0, 0)),
                  pl.BlockSpec((None, MOE_EPS, D_MODEL, MOE_FF), lambda i, e, m: (layer, e, 0, 0)),
                  pl.BlockSpec((None, MOE_EPS, D_MODEL, MOE_FF), lambda i, e, m: (layer, e, 0, 0)),
                  pl.BlockSpec((None, MOE_EPS, MOE_FF, D_MODEL), lambda i, e, m: (layer, e, 0, 0))],
        out_specs=pl.BlockSpec((MOE_SUB * n, D_MODEL), lambda i, e, m: (i, 0)),
        scratch_shapes=[pltpu.VMEM((MOE_SUB, ns, D_MODEL), BF16), pltpu.VMEM((MOE_SUB, ns, D_MODEL), BF16)],
    )
    return pl.pallas_call(
        functools.partial(_moe_kernel, n=n, ns=ns),
        grid_spec=grid_spec,
        out_shape=jax.ShapeDtypeStruct((t, D_MODEL), BF16),
        compiler_params=_cparams(("parallel", "arbitrary")),
        name="moe_experts",
    )(meta_flat, h2, col, row, wg, wu, wd)


def _final_kernel(x_ref, y_ref, g_ref, o_ref):
    o_ref[...] = _rms(x_ref[...] + y_ref[...].astype(F32), g_ref[...])


def _final(x, y, g):
    t = x.shape[0]
    tok = pl.BlockSpec((TM, D_MODEL), lambda i: (i, 0))
    return pl.pallas_call(
        _final_kernel,
        grid=(t // TM,),
        in_specs=[tok, tok, pl.BlockSpec((1, D_MODEL), lambda i: (0, 0))],
        out_specs=tok,
        out_shape=jax.ShapeDtypeStruct((t, D_MODEL), F32),
        compiler_params=_cparams(("parallel",)),
        name="final_norm",
    )(x, y, g)


def kernel(x, positions, norm_mix, w_in, ssd_conv_w, ssd_conv_b, ssd_dt_bias, ssd_a_log, ssd_d, ssd_norm, mla_q_norm, mla_w_uq, mla_kv_norm, mla_w_ukv, fox_f_bias, w_out, norm_ffn, router_group_w, router_group_b, router_expert_w, router_expert_b, expert_w_gate, expert_w_up, expert_w_down, final_norm):
    b, s, d = x.shape
    assert d == D_MODEL and s % FA_B == 0 and (b * s) % (MOE_N * MOE_SUB) == 0 and TM == DSA_SC
    depth = w_in.shape[0]
    n_sel = min(IDX_TOPK_MAX, s // 4)
    tab = _rope_tables(positions)
    xf = x.reshape(b * s, d)
    wg, wu, wd = expert_w_gate.astype(BF16), expert_w_up.astype(BF16), expert_w_down.astype(BF16)
    y_ffn = None
    for l in range(depth):
        xf, p = _inproj(xf, y_ffn, norm_mix[l].reshape(1, d), _pack_w_in(w_in[l]), tab, fox_f_bias[l],
                        mla_q_norm[l], mla_kv_norm[l], mla_w_uq[l], mla_w_ukv[l], s)
        y_ssd = _ssd(p["zx"], p["gt"], ssd_conv_w[l], ssd_conv_b[l], ssd_dt_bias[l], ssd_a_log[l], ssd_d[l],
                     ssd_norm[l], b, s)
        y_dsa = _dsa(p["iq"], p["gt"], p["dq"], p["ik"], p["dk"], p["dv"], b, s, n_sel)
        y_att = _flash([(p["mq"], p["mk"], p["mv"]), (p["fq"], p["fk"], p["fv"])], b, s)
        xf, h2, cw, rs = _outproj(xf, (y_ssd, y_dsa, y_att), w_out[l], norm_ffn[l].reshape(1, d),
                                  router_group_w[l], router_group_b[l], router_expert_w[l], router_expert_b[l])
        y_ffn = _moe(h2, cw, rs, wg, wu, wd, l)
    return _final(xf, y_ffn, final_norm.reshape(1, d)).reshape(b, s, d)
```

```python
import functools
import math

import jax
import jax.numpy as jnp
from jax import lax
from jax.experimental import pallas as pl
from jax.experimental.pallas import tpu as pltpu

F32 = jnp.float32
BF16 = jnp.bfloat16
I32 = jnp.int32
EPS = 1e-6
ROPE_THETA = 500000.0

D_MODEL = 1024
SSD_HEADS, SSD_HEAD_DIM, SSD_GROUPS, SSD_STATE, SSD_CONV = 4, 64, 2, 128, 4
SSD_D_INNER = SSD_HEADS * SSD_HEAD_DIM
SSD_CONV_DIM = SSD_D_INNER + 2 * SSD_GROUPS * SSD_STATE
DSA_HEADS, DSA_HEAD_DIM, IDX_HEADS, IDX_DIM, IDX_TOPK_MAX = 4, 64, 8, 64, 256
MLA_HEADS, MLA_Q_RANK, MLA_KV_RANK, MLA_NOPE, MLA_ROPE, MLA_V = 4, 256, 128, 64, 32, 64
FOX_HEADS, FOX_HEAD_DIM = 4, 64
MOE_GROUPS, MOE_EPG, MOE_EXPERTS, MOE_FF = 4, 8, 32, 256

LANES = 128
VMEM_LIMIT = 56 * 1024 * 1024
INT_MIN = -(2 ** 31)
NEG_BIG = -1e30

TM = 512
SSD_Q = 256
DSA_BQ = 256
DSA_SC = 512
FA_B = 512
MOE_N = 512
MOE_ALIGN = 16
MOE_CH = 64
MOE_EPS = 2
MOE_SUB = 4
MOE_NS = -(-(2 * MOE_N + MOE_EXPERTS * MOE_ALIGN + MOE_CH) // LANES) * LANES

C_ZX, C_GT, C_DQ, C_KV, C_IK, C_IQ, C_MC, C_FQ, C_FK, C_FV, C_END = (
    0, 1024, 1152, 1408, 1536, 1664, 2176, 2688, 3200, 3712, 4224)
G_FOX, G_DT, G_IW = 0, 4, 8
LOG2E = math.log2(math.e)
HEAD_W = 128
FA_HEADS = 4
VT_ROWS = 80
FOX_ONE_Q, FOX_F_Q, FOX_F_K, FOX_ONE_K = 64, 67, 64, 67


def _cparams(sem):
    return pltpu.CompilerParams(dimension_semantics=sem, vmem_limit_bytes=VMEM_LIMIT)


def _rms(xf, g):
    return xf * lax.rsqrt(jnp.mean(xf * xf, axis=-1, keepdims=True) + EPS) * g


def _softplus(x):
    return jnp.maximum(x, 0.0) + jnp.log1p(jnp.exp(-jnp.abs(x)))


def _silu(x):
    return x * jax.nn.sigmoid(x)


def _rope_apply(a, c, sa, sb, half):
    return a * c + pltpu.roll(a, LANES - half, 1) * sa + pltpu.roll(a, half, 1) * sb


def _dot_nt(a, b):
    return lax.dot_general(a, b, (((1,), (1,)), ((), ())), preferred_element_type=F32)


def _split3(x):
    hi = x.astype(BF16)
    r = x - hi.astype(F32)
    mid = r.astype(BF16)
    return hi, mid, (r - mid.astype(F32)).astype(BF16)


def _dot_f32_by_mask(mask, x, mask_left):
    m = jnp.where(mask, 1.0, 0.0).astype(BF16)
    out = None
    for piece in _split3(x):
        d = jnp.dot(m, piece, preferred_element_type=F32) if mask_left else jnp.dot(piece, m, preferred_element_type=F32)
        out = d if out is None else out + d
    return out


def _rope_tab_kernel(pos_ref, inv_ref, o_ref):
    pos = pos_ref[...].astype(F32)
    lane = lax.broadcasted_iota(I32, (1, LANES), 1)
    ang = pos * inv_ref[0:1, :]
    c, s = jnp.cos(ang), jnp.sin(ang)
    jm = lane - 64
    first = (lane < 8) | ((jm >= 0) & (jm < 16))
    second = ((lane >= 8) & (lane < 16)) | ((jm >= 16) & (jm < 32))
    o_ref[:, 0:128] = c
    o_ref[:, 128:256] = jnp.where(first, -s, 0.0)
    o_ref[:, 256:384] = jnp.where(second, s, 0.0)


def _rope_tables(positions):
    t = positions.size
    half_d, half_m = DSA_HEAD_DIM // 4 // 2, MLA_ROPE // 2
    inv_d = ROPE_THETA ** (-jnp.arange(half_d, dtype=F32) / half_d)
    inv_m = ROPE_THETA ** (-jnp.arange(half_m, dtype=F32) / half_m)
    lane = jnp.arange(LANES)
    jm = lane - 64
    row = jnp.where(lane < 2 * half_d, inv_d[lane % half_d],
                    jnp.where((jm >= 0) & (jm < 2 * half_m), inv_m[jm % half_m], 0.0))
    inv = jnp.zeros((8, LANES), F32).at[0].set(row)
    tm = 1024
    return pl.pallas_call(
        _rope_tab_kernel,
        grid=(t // tm,),
        in_specs=[pl.BlockSpec((tm, 1), lambda i: (i, 0)), pl.BlockSpec((8, LANES), lambda i: (0, 0))],
        out_specs=pl.BlockSpec((tm, 3 * LANES), lambda i: (i, 0)),
        out_shape=jax.ShapeDtypeStruct((t, 3 * LANES), F32),
        compiler_params=_cparams(("parallel",)),
        name="rope_tables",
    )(positions.reshape(t, 1), inv)


def _rope_patterns(tab_ref):
    lane = lax.broadcasted_iota(I32, (1, LANES), 1)
    in_d0, in_d1 = lane < 16, (lane >= 64) & (lane < 80)
    in_m = (lane >= 64) & (lane < 96)
    pat_d, pat_m = [], []
    for i, neutral in enumerate((1.0, 0.0, 0.0)):
        t = tab_ref[:, i * LANES:(i + 1) * LANES]
        pat_d.append(jnp.where(in_d0, t, jnp.where(in_d1, pltpu.roll(t, 64, 1), neutral)))
        pat_m.append(jnp.where(in_m, t, neutral))
    return pat_d, pat_m


def _inproj_kernel(*refs, has_add, tiles_per_seq):
    refs = list(refs)
    x_ref = refs.pop(0)
    yp_ref = refs.pop(0) if has_add else None
    g_ref, w_ref, tab_ref, fb_ref, qn_ref, kn_ref, wq_ref, wk_ref, wv_ref = refs[:9]
    refs = refs[9:]
    xo_ref = refs.pop(0) if has_add else None
    (zx_ref, gt_ref, dq_ref, dk_ref, dv_ref, ik_ref, iq_ref, mq_ref, mk_ref, mv_ref, fq_ref, fk_ref, fv_ref,
     carry_s) = refs
    x = x_ref[...]
    if has_add:
        x = x + yp_ref[...].astype(F32)
        xo_ref[...] = x
    h = _rms(x, g_ref[...]).astype(BF16)

    def proj(a, b):
        return jnp.dot(h, w_ref[:, a:b], preferred_element_type=F32)

    zx_ref[...] = proj(C_ZX, C_GT)
    gates = proj(C_GT, C_DQ)
    gt_ref[...] = gates
    (c_d, sa_d, sb_d), (c_m, sa_m, sb_m) = _rope_patterns(tab_ref)
    q = proj(C_DQ, C_KV)
    for i in range(2):
        dq_ref[:, i * 128:(i + 1) * 128] = _rope_apply(q[:, i * 128:(i + 1) * 128], c_d, sa_d, sb_d, 8).astype(BF16)
    kv = proj(C_KV, C_IK)
    is_k = lax.broadcasted_iota(I32, (1, LANES), 1) < 64
    kvr = _rope_apply(kv, jnp.where(is_k, c_d, 1.0), jnp.where(is_k, sa_d, 0.0), jnp.where(is_k, sb_d, 0.0), 8)
    dk_ref[...] = kvr[:, 0:64].astype(BF16)
    dv_ref[0, 0:64, :] = kvr.astype(BF16).astype(F32).T[64:128, :].astype(BF16)
    dv_ref[0, 64:VT_ROWS, :] = jnp.ones((VT_ROWS - 64, kvr.shape[0]), BF16)
    ikb = proj(C_IK, C_IQ)
    ik_ref[...] = _rope_apply(ikb, c_d, sa_d, sb_d, 8)[:, 0:64].astype(BF16)
    iq = proj(C_IQ, C_MC)
    for i in range(4):
        iq_ref[:, i * 128:(i + 1) * 128] = _rope_apply(iq[:, i * 128:(i + 1) * 128], c_d, sa_d, sb_d, 8).astype(BF16)
    ones_hi = jnp.where(lax.broadcasted_iota(I32, (1, FA_HEADS * HEAD_W), 1) % HEAD_W >= 64, 1.0, 0.0)

    mc = proj(C_MC, C_FQ)
    cq = _rms(mc[:, 0:256], qn_ref[...]).astype(BF16)
    ckv = _rms(mc[:, 256:384], kn_ref[...]).astype(BF16)
    kr = _rope_apply(mc[:, 384:512], c_m, sa_m, sb_m, 16)
    mq = jnp.dot(cq, wq_ref[...], preferred_element_type=F32)
    mk = jnp.dot(ckv, wk_ref[...], preferred_element_type=F32)
    for hd in range(MLA_HEADS):
        sl = slice(hd * HEAD_W, (hd + 1) * HEAD_W)
        mq_ref[:, sl] = _rope_apply(mq[:, sl], c_m, sa_m, sb_m, 16).astype(BF16)
        mk_ref[:, sl] = (mk[:, sl] + kr).astype(BF16)
    mv_ref[...] = (jnp.dot(ckv, wv_ref[...], preferred_element_type=F32) + ones_hi).astype(BF16)

    @pl.when(pl.program_id(0) % tiles_per_seq == 0)
    def _():
        carry_s[...] = jnp.zeros_like(carry_s)

    tm = x.shape[0]
    log_f = -_softplus(-(gates + fb_ref[...]))
    ri = lax.broadcasted_iota(I32, (tm, tm), 0)
    ci = lax.broadcasted_iota(I32, (tm, tm), 1)
    cs = _dot_f32_by_mask(ri >= ci, log_f, True) + carry_s[0:1, :]
    carry_s[0:1, :] = cs[tm - 1:tm, :]
    f2 = cs * LOG2E
    lane = lax.broadcasted_iota(I32, (1, HEAD_W), 1)

    def pieces(f, base):
        hi, mid, lo = _split3(f)
        return jnp.where(lane == base, hi.astype(F32),
                         jnp.where(lane == base + 1, mid.astype(F32), jnp.where(lane == base + 2, lo.astype(F32), 0.0)))

    def ones(base):
        return jnp.where((lane >= base) & (lane < base + 3), 1.0, 0.0)

    fq = proj(C_FQ, C_FK)
    fk = proj(C_FK, C_FV)
    for hd in range(FOX_HEADS):
        fh = f2[:, G_FOX + hd:G_FOX + hd + 1]
        sl = slice(hd * HEAD_W, (hd + 1) * HEAD_W)
        fq_ref[:, sl] = (fq[:, sl] + pieces(fh, FOX_F_Q) + ones(FOX_ONE_Q)).astype(BF16)
        fk_ref[:, sl] = (fk[:, sl] - pieces(fh, FOX_F_K) + ones(FOX_ONE_K)).astype(BF16)
    fv_ref[...] = (proj(C_FV, C_END) + ones_hi).astype(BF16)


def _pack_w_in(w):
    d = w.shape[0]
    z = lambda n: jnp.zeros((d, n), F32)
    o = 0
    parts = {}
    for name, width in (("z", 256), ("xbc", 768), ("dt", 4), ("dq", 256), ("dk", 64), ("dv", 64), ("iq", 512),
                        ("ik", 64), ("iw", 8), ("cq", 256), ("ckv", 128), ("kr", 32), ("fq", 256), ("fk", 256),
                        ("fv", 256), ("ff", 4)):
        parts[name] = w[:, o:o + width]
        o += width
    scale = DSA_HEAD_DIM ** -0.5

    def per_head(a):
        return jnp.pad(a.reshape(d, FA_HEADS, 64), ((0, 0), (0, 0), (0, HEAD_W - 64))).reshape(d, FA_HEADS * HEAD_W)

    cat = jnp.concatenate([
        parts["z"], parts["xbc"],
        parts["ff"], parts["dt"], parts["iw"], z(112),
        parts["dq"] * (scale * LOG2E),
        parts["dk"], parts["dv"],
        parts["ik"], z(64),
        parts["iq"],
        parts["cq"], parts["ckv"], z(64), parts["kr"], z(32),
        per_head(parts["fq"] * (FOX_HEAD_DIM ** -0.5 * LOG2E)), per_head(parts["fk"]), per_head(parts["fv"])], axis=1)
    return cat.astype(BF16)


def _mla_weights(w_uq, w_ukv):
    dqk = MLA_NOPE + MLA_ROPE
    wq = jnp.pad(w_uq.reshape(MLA_Q_RANK, MLA_HEADS, dqk) * (dqk ** -0.5 * LOG2E), ((0, 0), (0, 0), (0, HEAD_W - dqk)))
    wkv = w_ukv.reshape(MLA_KV_RANK, MLA_HEADS, MLA_NOPE + MLA_V)
    wk = jnp.pad(wkv[:, :, :MLA_NOPE], ((0, 0), (0, 0), (0, HEAD_W - MLA_NOPE)))
    wv = jnp.pad(wkv[:, :, MLA_NOPE:], ((0, 0), (0, 0), (0, HEAD_W - MLA_V)))
    flat = lambda a: a.reshape(a.shape[0], MLA_HEADS * HEAD_W).astype(BF16)
    return flat(wq), flat(wk), flat(wv)


def _inproj(x, y_prev, g, w_cat, tab, f_bias, q_norm, kv_norm, w_uq, w_ukv, s):
    t = x.shape[0]
    has_add = y_prev is not None
    tok = lambda n: pl.BlockSpec((TM, n), lambda i: (i, 0))
    full = lambda a: pl.BlockSpec(a.shape, lambda i: (0,) * a.ndim)
    fb = jnp.zeros((1, LANES), F32).at[0, G_FOX:G_FOX + FOX_HEADS].set(f_bias)
    params = [g, w_cat, tab, fb, q_norm.reshape(1, -1), kv_norm.reshape(1, -1), *_mla_weights(w_uq, w_ukv)]
    ins = [x] + ([y_prev] if has_add else []) + params
    in_specs = ([tok(D_MODEL)] + ([tok(D_MODEL)] if has_add else [])
                + [tok(3 * LANES) if a is tab else full(a) for a in params])
    outs = ([("x", D_MODEL, F32)] if has_add else []) + [
        ("zx", 1024, F32), ("gt", 128, F32), ("dq", 256, BF16), ("dk", 64, BF16), ("dv", 64, BF16),
        ("ik", 64, BF16), ("iq", 512, BF16), ("mq", 512, BF16), ("mk", 512, BF16), ("mv", 512, BF16),
        ("fq", 512, BF16), ("fk", 512, BF16), ("fv", 512, BF16)]
    vt_rows = {"dv": VT_ROWS}
    res = pl.pallas_call(
        functools.partial(_inproj_kernel, has_add=has_add, tiles_per_seq=s // TM),
        grid=(t // TM,),
        in_specs=in_specs,
        out_specs=[pl.BlockSpec((1, vt_rows[nm], TM), lambda i: (i, 0, 0)) if nm in vt_rows else tok(n)
                   for nm, n, _ in outs],
        out_shape=[jax.ShapeDtypeStruct((t // TM, vt_rows[nm], TM) if nm in vt_rows else (t, n), dt)
                   for nm, n, dt in outs],
        scratch_shapes=[pltpu.VMEM((8, LANES), F32)],
        compiler_params=_cparams(("arbitrary",)),
        name="inproj",
    )(*ins)
    res = list(res)
    x_new = res.pop(0) if has_add else x
    return x_new, dict(zip([n for n, _, _ in outs if n != "x"], res))


def _ssd_kernel(zx_ref, gt_ref, cw_ref, cb_ref, pr_ref, pc_ref, drow_ref, nw_ref, o_ref, ext_s, st_s, y_s, *, q):
    @pl.when(pl.program_id(1) == 0)
    def _():
        ext_s[0:8, :] = jnp.zeros((8, SSD_CONV_DIM), F32)
        st_s[...] = jnp.zeros_like(st_s)

    raw = zx_ref[:, 256:1024]
    ext_s[8:8 + q, :] = raw
    acc = jnp.broadcast_to(cb_ref[...], (q, SSD_CONV_DIM))
    for j in range(SSD_CONV):
        acc = acc + cw_ref[j:j + 1, :] * ext_s[5 + j:5 + j + q, :]
    ext_s[0:8, :] = raw[q - 8:q, :]
    xbc = _silu(acc)
    xs = xbc[:, 0:SSD_D_INNER]

    g = gt_ref[...]
    lane = lax.broadcasted_iota(I32, (1, LANES), 1)
    dtc = _softplus(g + pr_ref[0:1, :])
    a_r = jnp.where((lane >= G_DT) & (lane < G_DT + SSD_HEADS), -jnp.exp(pr_ref[1:2, :]), 0.0)
    ri = lax.broadcasted_iota(I32, (q, q), 0)
    ci = lax.broadcasted_iota(I32, (q, q), 1)
    tri = ri >= ci
    acs_c = _dot_f32_by_mask(tri, dtc * a_r, True)
    sub = lax.broadcasted_iota(I32, (LANES, 1), 0)
    dtr = _softplus(g.T + pc_ref[:, 0:1])
    a_c = jnp.where((sub >= G_DT) & (sub < G_DT + SSD_HEADS), -jnp.exp(pc_ref[:, 1:2]), 0.0)
    acs_r = _dot_f32_by_mask(ri <= ci, dtr * a_c, False)

    rep = SSD_HEADS // SSD_GROUPS
    for gi in range(SSD_GROUPS):
        bg = xbc[:, SSD_D_INNER + gi * SSD_STATE:SSD_D_INNER + (gi + 1) * SSD_STATE]
        cg = xbc[:, SSD_D_INNER + (SSD_GROUPS + gi) * SSD_STATE:SSD_D_INNER + (SSD_GROUPS + gi + 1) * SSD_STATE]
        bt = bg.T.astype(BF16)
        cb16 = cg.astype(BF16)
        cbm = jnp.dot(cb16, bt, preferred_element_type=F32)
        for hh in range(rep):
            h = gi * rep + hh
            ac = acs_c[:, G_DT + h:G_DT + h + 1]
            ar = acs_r[G_DT + h:G_DT + h + 1, :]
            seg = jnp.where(tri, jnp.exp(ac - ar), 0.0)
            xh = xs[:, h * SSD_HEAD_DIM:(h + 1) * SSD_HEAD_DIM]
            xdt = xh * dtc[:, G_DT + h:G_DT + h + 1]
            yd = jnp.dot((cbm * seg).astype(BF16), xdt.astype(BF16), preferred_element_type=F32)
            aend = ac[q - 1:q, :]
            st = st_s[h]
            yo = jnp.dot(cb16, st.astype(BF16), preferred_element_type=F32) * jnp.exp(ac)
            st_s[h] = st * jnp.exp(aend) + jnp.dot(bt, (xdt * jnp.exp(aend - ac)).astype(BF16),
                                                   preferred_element_type=F32)
            y_s[:, h * SSD_HEAD_DIM:(h + 1) * SSD_HEAD_DIM] = yd + yo
    y = y_s[...] + drow_ref[...] * xs
    y = y * _silu(zx_ref[:, 0:SSD_D_INNER])
    o_ref[...] = _rms(y, nw_ref[...]).astype(BF16)


def _ssd(zx, gt, conv_w, conv_b, dt_bias, a_log, d_skip, norm_w, b, s):
    q = SSD_Q
    nc = s // q
    pr = jnp.zeros((8, LANES), F32).at[0, G_DT:G_DT + SSD_HEADS].set(dt_bias).at[1, G_DT:G_DT + SSD_HEADS].set(a_log)
    pc = pr.T
    drow = jnp.repeat(d_skip, SSD_HEAD_DIM).reshape(1, SSD_D_INNER)
    cb = conv_b.reshape(1, -1)
    nw = norm_w.reshape(1, -1)
    full = lambda a: pl.BlockSpec(a.shape, lambda i, j: (0,) * a.ndim)
    return pl.pallas_call(
        functools.partial(_ssd_kernel, q=q),
        grid=(b, nc),
        in_specs=[pl.BlockSpec((q, 1024), lambda i, j: (i * nc + j, 0)), pl.BlockSpec((q, LANES), lambda i, j: (i * nc + j, 0)),
                  full(conv_w), full(cb), full(pr), full(pc), full(drow), full(nw)],
        out_specs=pl.BlockSpec((q, SSD_D_INNER), lambda i, j: (i * nc + j, 0)),
        out_shape=jax.ShapeDtypeStruct((b * s, SSD_D_INNER), BF16),
        scratch_shapes=[pltpu.VMEM((8 + q, SSD_CONV_DIM), F32), pltpu.VMEM((SSD_HEADS, SSD_STATE, SSD_HEAD_DIM), F32),
                        pltpu.VMEM((q, SSD_D_INNER), F32)],
        compiler_params=_cparams(("parallel", "arbitrary")),
        name="ssd",
    )(zx, gt, conv_w, cb, pr, pc, drow, nw)


def _key_to_f32(k):
    return pltpu.bitcast(jnp.where(k < 0, k ^ jnp.int32(0x7FFFFFFF), k), F32)


def _dsa_kernel(iq_ref, gt_ref, q_ref, ik_ref, k_ref, vt_ref, o_ref, sc_s, qi_s, qs_s, thr_s, s16_s,
                *, bq, sc_w, n_sel):
    qb = pl.program_id(1)
    n_sc = ((qb + 1) * bq + sc_w - 1) // sc_w
    for j in range(IDX_HEADS):
        qi_s[j // 2, (j % 2) * bq:(j % 2 + 1) * bq, :] = iq_ref[:, j * IDX_DIM:(j + 1) * IDX_DIM]
    for h in range(DSA_HEADS):
        qs_s[h * bq:(h + 1) * bq, :] = q_ref[:, h * DSA_HEAD_DIM:(h + 1) * DSA_HEAD_DIM]
    w_t = (gt_ref[...] * ((IDX_HEADS * IDX_DIM) ** -0.5)).T
    qpos = qb * bq + lax.broadcasted_iota(I32, (1, bq), 1)

    def score_body(sc, carry):
        k0 = pl.multiple_of(sc * sc_w, sc_w)
        kidx = ik_ref[pl.ds(k0, sc_w), :]
        sco = jnp.zeros((sc_w, bq), F32)
        for jp in range(IDX_HEADS // 2):
            a = _dot_nt(kidx, qi_s[jp])
            for u in range(2):
                j = 2 * jp + u
                sco = sco + jnp.maximum(a[:, u * bq:(u + 1) * bq], 0.0) * w_t[G_IW + j:G_IW + j + 1, :]
        kpos = k0 + lax.broadcasted_iota(I32, (sc_w, 1), 0)
        sco = jnp.where(kpos <= qpos, sco, -jnp.inf)
        sc_s[sc] = sco
        s16_s[sc] = sco.astype(BF16)
        return carry

    lax.fori_loop(0, n_sc, score_body, 0)

    def count(pred):
        def body(sc, acc):
            for r in range(sc_w // 64):
                acc = acc + jnp.where(pred(sc_s[sc, r * 64:(r + 1) * 64, :]), 1.0, 0.0)
            return acc
        acc = lax.fori_loop(0, n_sc, body, jnp.zeros((64, bq), F32))
        return jnp.sum(acc, axis=0, keepdims=True)

    thr_s[0:1, :] = jnp.full((1, bq), -jnp.finfo(jnp.float32).max, F32)

    @pl.when((qb + 1) * bq > n_sel)
    def _():
        active = qpos + 1 > n_sel

        one16, zero16 = jnp.ones((), BF16), jnp.zeros((), BF16)

        def count16(c16):
            def body(sc, acc):
                a16 = jnp.zeros((64, bq), BF16)
                for r in range(sc_w // 64):
                    a16 = a16 + jnp.where(s16_s[sc, r * 64:(r + 1) * 64, :] >= c16, one16, zero16)
                return acc + a16.astype(F32)
            acc = lax.fori_loop(0, n_sc, body, jnp.zeros((64, bq), F32))
            return jnp.sum(acc, axis=0, keepdims=True)

        def coarse(it, lo):
            cand = lo ^ jnp.left_shift(jnp.int32(1), 31 - it)
            return jnp.where(count16(_key_to_f32(cand).astype(BF16)) >= n_sel, cand, lo)

        lo16 = lax.fori_loop(0, 16, coarse, jnp.full((1, bq), INT_MIN, I32))
        base = jnp.maximum(lo16, jnp.int32(INT_MIN + (1 << 16))) - jnp.int32(1 << 16)

        def cond(st):
            it, off, cnt_lo = st
            pending = active & (cnt_lo != n_sel)
            return (it < 32) & (jnp.max(jnp.where(pending, 1.0, 0.0)) > 0.0)

        def fine(it, off, cnt_lo):
            cand = off | jnp.left_shift(jnp.int32(1), 31 - it)
            thr_c = _key_to_f32(base + cand)
            cnt = count(lambda t: t >= thr_c)
            ok = cnt >= n_sel
            return jnp.where(ok, cand, off), jnp.where(ok, cnt, cnt_lo)

        def body(st):
            it, off, cnt_lo = st
            for u in range(4):
                off, cnt_lo = fine(it + u, off, cnt_lo)
            return it + 4, off, cnt_lo

        off, cnt_lo = fine(15, jnp.zeros((1, bq), I32), jnp.full((1, bq), 3.0e38, F32))
        _, off, cnt_lo = lax.while_loop(cond, body, (jnp.int32(16), off, cnt_lo))
        lo = base + off
        thr = _key_to_f32(lo)
        tie = active & (cnt_lo > n_sel)

        @pl.when(jnp.max(jnp.where(tie, 1.0, 0.0)) > 0.0)
        def _():
            need = n_sel - count(lambda t: t > thr)
            ri = lax.broadcasted_iota(I32, (sc_w, sc_w), 0)
            ci = lax.broadcasted_iota(I32, (sc_w, sc_w), 1)
            lower = jnp.where(ri >= ci, 1.0, 0.0).astype(BF16)

            def tie_body(sc, run):
                t = sc_s[sc]
                eq = (t == thr) & tie
                pre = jnp.dot(lower, jnp.where(eq, 1.0, 0.0).astype(BF16), preferred_element_type=F32)
                sc_s[sc] = jnp.where(eq & (run + pre > need), -jnp.inf, t)
                return run + pre[sc_w - 1:sc_w, :]

            lax.fori_loop(0, n_sc, tie_body, jnp.zeros((1, bq), F32))

        thr_s[0:1, :] = jnp.where(active, thr, thr_s[0:1, :])

    thr = thr_s[0:1, :]

    hq = DSA_HEADS * bq

    def att_body(sc, carry):
        m_prev, acc = carry
        k0 = pl.multiple_of(sc * sc_w, sc_w)
        st = _dot_nt(k_ref[pl.ds(k0, sc_w), :], qs_s[...])
        drop = jnp.where(sc_s[sc] >= thr, 0.0, 2 * NEG_BIG)
        st = st + jnp.tile(drop, (1, DSA_HEADS))
        m_cur = jnp.max(jnp.max(st.reshape(sc_w // 64, 64, hq), axis=0), axis=0, keepdims=True)
        m_next = jnp.maximum(m_prev, m_cur)
        p = jnp.exp2(st - m_next)
        pv = jnp.dot(vt_ref[sc], p.astype(BF16), preferred_element_type=F32)
        return m_next, jnp.exp2(m_prev - m_next) * acc + pv

    def att_body2(i, carry):
        return att_body(2 * i + 1, att_body(2 * i, carry))

    carry = lax.fori_loop(0, n_sc // 2, att_body2, (jnp.full((1, hq), NEG_BIG, F32), jnp.zeros((VT_ROWS, hq), F32)))
    _, acc = lax.cond(n_sc % 2 == 1, lambda c: att_body(n_sc - 1, c), lambda c: c, carry)
    out_t = jnp.concatenate([acc[0:DSA_HEAD_DIM, :] / acc[DSA_HEAD_DIM:DSA_HEAD_DIM + 1, :],
                             jnp.zeros((LANES - DSA_HEAD_DIM, hq), F32)], axis=0).T
    for h in range(DSA_HEADS):
        o_ref[:, h * DSA_HEAD_DIM:(h + 1) * DSA_HEAD_DIM] = out_t[h * bq:(h + 1) * bq, 0:DSA_HEAD_DIM].astype(BF16)


def _dsa(iq, gt, dq, ik, dk, dvt, b, s, n_sel):
    bq, sc_w = DSA_BQ, DSA_SC
    nq = s // bq
    qblk = lambda n: pl.BlockSpec((bq, n), lambda i, j: (i * nq + j, 0))
    kblk = pl.BlockSpec((s, 64), lambda i, j: (i, 0))
    return pl.pallas_call(
        functools.partial(_dsa_kernel, bq=bq, sc_w=sc_w, n_sel=n_sel),
        grid=(b, nq),
        in_specs=[qblk(512), qblk(LANES), qblk(256), kblk, kblk,
                  pl.BlockSpec((s // sc_w, VT_ROWS, sc_w), lambda i, j: (i, 0, 0))],
        out_specs=qblk(256),
        out_shape=jax.ShapeDtypeStruct((b * s, 256), BF16),
        scratch_shapes=[pltpu.VMEM((s // sc_w, sc_w, bq), F32), pltpu.VMEM((IDX_HEADS // 2, 2 * bq, IDX_DIM), BF16),
                        pltpu.VMEM((DSA_HEADS * bq, DSA_HEAD_DIM), BF16), pltpu.VMEM((8, bq), F32),
                        pltpu.VMEM((s // sc_w, sc_w, bq), BF16)],
        compiler_params=_cparams(("parallel", "arbitrary")),
        name="dsa",
    )(iq, gt, dq, ik, dk, dvt)


def _flash_kernel(qt_ref, kt_ref, *refs, blk, sets):
    qkv, (o_ref, m_s, acc_s) = refs[:3 * sets], refs[3 * sets:]
    t = pl.program_id(1)
    qi, ki = qt_ref[t], kt_ref[t]

    @pl.when(ki == 0)
    def _():
        m_s[...] = jnp.full(m_s.shape, NEG_BIG, F32)
        acc_s[...] = jnp.zeros_like(acc_s)

    def step(diagonal):
        if diagonal:
            keep = lax.broadcasted_iota(I32, (blk, blk), 0) >= lax.broadcasted_iota(I32, (blk, blk), 1)
        for g in range(sets):
            q_ref, k_ref, v_ref = qkv[3 * g:3 * g + 3]
            for h in range(FA_HEADS):
                sl = slice(h * HEAD_W, (h + 1) * HEAD_W)
                hh = g * FA_HEADS + h
                s = _dot_nt(q_ref[:, sl], k_ref[:, sl])
                if diagonal:
                    s = jnp.where(keep, s, 2 * NEG_BIG)
                m_prev = m_s[hh]
                m_next = jnp.maximum(m_prev, jnp.max(s, axis=1, keepdims=True))
                p = jnp.exp2(s - jnp.tile(m_next, (1, blk // LANES)))
                pv = jnp.dot(p.astype(BF16), v_ref[:, sl], preferred_element_type=F32)
                acc_s[hh] = jnp.exp2(m_prev - m_next) * acc_s[hh] + pv
                m_s[hh] = m_next

    @pl.when(ki < qi)
    def _():
        step(False)

    @pl.when(ki == qi)
    def _():
        step(True)
        for hh in range(sets * FA_HEADS):
            acc = acc_s[hh]
            o_ref[:, hh * 64:(hh + 1) * 64] = (acc / pltpu.roll(acc, 64, 1))[:, 0:64].astype(BF16)


def _flash(qkv, b, s):
    blk = FA_B
    nb = s // blk
    pairs = [(i, j) for i in range(nb) for j in range(i + 1)]
    qt = jnp.asarray([p[0] for p in pairs], I32)
    kt = jnp.asarray([p[1] for p in pairs], I32)
    w = FA_HEADS * HEAD_W
    sets = len(qkv)
    q_spec = pl.BlockSpec((blk, w), lambda i, t, qt, kt: (i * nb + qt[t], 0))
    kv_spec = pl.BlockSpec((blk, w), lambda i, t, qt, kt: (i * nb + kt[t], 0))
    grid_spec = pltpu.PrefetchScalarGridSpec(
        num_scalar_prefetch=2,
        grid=(b, len(pairs)),
        in_specs=[q_spec, kv_spec, kv_spec] * sets,
        out_specs=pl.BlockSpec((blk, sets * FA_HEADS * 64), lambda i, t, qt, kt: (i * nb + qt[t], 0)),
        scratch_shapes=[pltpu.VMEM((sets * FA_HEADS, blk, LANES), F32), pltpu.VMEM((sets * FA_HEADS, blk, HEAD_W), F32)],
    )
    return pl.pallas_call(
        functools.partial(_flash_kernel, blk=blk, sets=sets),
        grid_spec=grid_spec,
        out_shape=jax.ShapeDtypeStruct((b * s, sets * FA_HEADS * 64), BF16),
        compiler_params=_cparams(("parallel", "arbitrary")),
        name="flash",
    )(qt, kt, *[a for triple in qkv for a in triple])


def _outproj_kernel(x_ref, *refs, n_mix):
    y_refs, (wo_ref, g_ref, wr_ref, br_ref, xo_ref, h_ref, meta_ref, col_ref, row_ref) = refs[:n_mix], refs[n_mix:]
    acc = x_ref[...]
    row = 0
    for y_ref in y_refs:
        acc = acc + jnp.dot(y_ref[...], wo_ref[row:row + y_ref.shape[1], :], preferred_element_type=F32)
        row += y_ref.shape[1]
    xo_ref[...] = acc
    h2 = _rms(acc, g_ref[...])
    h_ref[...] = h2.astype(BF16)
    h_hi, h_lo, _ = _split3(h2)
    lg = (jnp.dot(h_hi, wr_ref[0], preferred_element_type=F32) + jnp.dot(h_lo, wr_ref[0], preferred_element_type=F32)
          + jnp.dot(h_hi, wr_ref[1], preferred_element_type=F32) + br_ref[...])
    lane = lax.broadcasted_iota(I32, (1, LANES), 1)
    lanef = lane.astype(F32)
    is_g = (lane >= MOE_EXPERTS) & (lane < MOE_EXPERTS + MOE_GROUPS)
    gl = jnp.where(is_g, lg, -jnp.inf)
    gmax = jnp.max(gl, axis=1, keepdims=True)
    gidx = jnp.min(jnp.where(gl == gmax, lanef, 999.0), axis=1, keepdims=True) - MOE_EXPERTS
    g_p = 1.0 / jnp.sum(jnp.where(is_g, jnp.exp(gl - gmax), 0.0), axis=1, keepdims=True)
    in_g = (lane < MOE_EXPERTS) & (jnp.floor(lanef * (1.0 / MOE_EPG)) == gidx)
    el = jnp.where(in_g, lg, -jnp.inf)
    m1 = jnp.max(el, axis=1, keepdims=True)
    i1 = jnp.min(jnp.where(el == m1, lanef, 999.0), axis=1, keepdims=True)
    el2 = jnp.where(lanef == i1, -jnp.inf, el)
    m2 = jnp.max(el2, axis=1, keepdims=True)
    i2 = jnp.min(jnp.where(el2 == m2, lanef, 999.0), axis=1, keepdims=True)
    t = jnp.exp(m2 - m1)
    w1 = 1.0 / (1.0 + t)
    gate_w = jnp.where(lanef == i1, w1 * g_p, jnp.where(lanef == i2, t * w1 * g_p, 0.0))
    chosen = jnp.where((lanef == i1) | (lanef == i2), 1.0, 0.0).astype(BF16)
    meta_ref[0], col_ref[...], row_ref[0] = _moe_plan(chosen, gate_w)


def _outproj(x, ys, w_out, g, w_rg, b_rg, w_re, b_re):
    t = x.shape[0]
    wo = w_out.astype(BF16)
    wr = jnp.zeros((D_MODEL, LANES), F32).at[:, 0:MOE_EXPERTS].set(w_re).at[:, MOE_EXPERTS:MOE_EXPERTS + MOE_GROUPS].set(w_rg)
    wr_hi = wr.astype(BF16)
    wr = jnp.stack([wr_hi, (wr - wr_hi.astype(F32)).astype(BF16)])
    br =jnp.zeros((1, LANES), F32).at[0, 0:MOE_EXPERTS].set(b_re).at[0, MOE_EXPERTS:MOE_EXPERTS + MOE_GROUPS].set(b_rg)
    tok = lambda n: pl.BlockSpec((TM, n), lambda i: (i, 0))
    full = lambda a: pl.BlockSpec(a.shape, lambda i: (0,) * a.ndim)
    return pl.pallas_call(
        functools.partial(_outproj_kernel, n_mix=len(ys)),
        grid=(t // TM,),
        in_specs=[tok(D_MODEL)] + [tok(y.shape[1]) for y in ys] + [full(wo), full(g), full(wr), full(br)],
        out_specs=[tok(D_MODEL), tok(D_MODEL), pl.BlockSpec((1, 8, LANES), lambda i: (i, 0, 0)), tok(LANES),
                   pl.BlockSpec((1, 8, TM), lambda i: (i, 0, 0))],
        out_shape=[jax.ShapeDtypeStruct((t, D_MODEL), F32), jax.ShapeDtypeStruct((t, D_MODEL), BF16),
                   jax.ShapeDtypeStruct((t // TM, 8, LANES), I32), jax.ShapeDtypeStruct((t, LANES), F32),
                   jax.ShapeDtypeStruct((t // TM, 8, TM), F32)],
        compiler_params=_cparams(("parallel",)),
        name="outproj_router",
    )(x, *ys, wo, g, wr, br)


def _moe_plan(sel, cw):
    n = sel.shape[0]
    self32 = sel.astype(F32)
    ti = lax.broadcasted_iota(I32, (n, n), 0)
    tj = lax.broadcasted_iota(I32, (n, n), 1)
    rank = jnp.dot(jnp.where(ti > tj, 1.0, 0.0).astype(BF16), sel, preferred_element_type=F32)
    cnt = jnp.sum(self32, axis=0, keepdims=True).astype(I32)
    cpad = ((cnt + (MOE_ALIGN - 1)) // MOE_ALIGN) * MOE_ALIGN
    li = lax.broadcasted_iota(I32, (LANES, LANES), 0)
    lj = lax.broadcasted_iota(I32, (LANES, LANES), 1)
    off = _dot_f32_by_mask(li < lj, jnp.broadcast_to(cpad.astype(F32), (8, LANES)), False)[0:1, :]
    dest = off + rank
    lane = lax.broadcasted_iota(I32, (1, LANES), 1)
    lanef = lane.astype(F32)
    on = self32 > 0.5
    e_lo = jnp.min(jnp.where(on, lanef, 999.0), axis=1, keepdims=True)
    e_hi = jnp.max(jnp.where(on, lanef, -1.0), axis=1, keepdims=True)
    is_lo, is_hi = lanef == e_lo, lanef == e_hi
    pick = lambda m, a: jnp.sum(jnp.where(m, a, 0.0), axis=1, keepdims=True)
    d_lo, d_hi, w_lo, w_hi = pick(is_lo, dest), pick(is_hi, dest), pick(is_lo, cw), pick(is_hi, cw)
    colv = jnp.where(lane == 0, d_lo, jnp.where(lane == 1, d_hi, jnp.where(lane == 2, w_lo, jnp.where(lane == 3, w_hi, 0.0))))
    sub = lax.broadcasted_iota(I32, (8, LANES), 0)
    meta = jnp.where(sub == 0, off.astype(I32), jnp.where(sub == 1, cnt, 0))
    return meta, colv, colv.T[0:8, :]


def _moe_kernel(meta_ref, h_ref, col_ref, row_ref, wg_ref, wu_ref, wd_ref, o_ref, s_s, y_s, *, n, ns):
    t, e = pl.program_id(0), pl.program_id(1)

    @pl.when(e == 0)
    def _():
        for g in range(MOE_SUB):
            d_lo, d_hi = row_ref[g, 0:1, :], row_ref[g, 1:2, :]
            hg = h_ref[g * n:(g + 1) * n, :]
            for r in range(ns // LANES):
                si = (r * LANES + lax.broadcasted_iota(I32, (LANES, 1), 0)).astype(F32)
                perm = jnp.where((si == d_lo) | (si == d_hi), 1.0, 0.0).astype(BF16)
                s_s[g, r * LANES:(r + 1) * LANES, :] = jnp.dot(perm, hg, preferred_element_type=F32).astype(BF16)
        y_s[...] = jnp.zeros_like(y_s)

    def mlp(j, starts, store_ok):
        starts = [pl.multiple_of(r0, MOE_ALIGN) for r0 in starts]
        xs = jnp.concatenate([s_s[g, pl.ds(starts[g], MOE_CH), :] for g in range(MOE_SUB)], axis=0)
        gate = jnp.dot(xs, wg_ref[j], preferred_element_type=F32)
        up = jnp.dot(xs, wu_ref[j], preferred_element_type=F32)
        hid = (_silu(gate) * up).astype(BF16)
        y = jnp.dot(hid, wd_ref[j], preferred_element_type=F32).astype(BF16)
        for g in range(MOE_SUB):
            def store(g=g):
                y_s[g, pl.ds(starts[g], MOE_CH), :] = y[g * MOE_CH:(g + 1) * MOE_CH, :]
            if store_ok is None:
                store()
            else:
                pl.when(store_ok[g])(store)

    def meta(g, j, what):
        return meta_ref[(t * MOE_SUB + g) * 2 * MOE_EXPERTS + what * MOE_EXPERTS + e * MOE_EPS + j]

    offs = [[meta(g, j, 0) for g in range(MOE_SUB)] for j in range(MOE_EPS)]
    cnts = [[meta(g, j, 1) for g in range(MOE_SUB)] for j in range(MOE_EPS)]
    for j in range(MOE_EPS):
        mlp(j, offs[j], None)
    for j in range(MOE_EPS):
        ends = [offs[j][g] + ((cnts[j][g] + MOE_ALIGN - 1) // MOE_ALIGN) * MOE_ALIGN for g in range(MOE_SUB)]
        nch = [(cnts[j][g] + MOE_CH - 1) // MOE_CH for g in range(MOE_SUB)]

        def body(i, carry, j=j, ends=ends, nch=nch):
            more = [i < nch[g] for g in range(MOE_SUB)]
            starts = [jnp.where(more[g], jnp.minimum(offs[j][g] + i * MOE_CH, ends[g] - MOE_CH), offs[j][g])
                      for g in range(MOE_SUB)]
            mlp(j, starts, more)
            return carry

        lax.fori_loop(1, functools.reduce(jnp.maximum, nch), body, 0)

    @pl.when(e == pl.num_programs(1) - 1)
    def _():
        rows = 256
        kc = ns // next(k for k in (3, 2, 1) if ns % (k * LANES) == 0)
        for g in range(MOE_SUB):
            for c in range(n // rows):
                r0 = g * n + c * rows
                cv = col_ref[r0:r0 + rows, :]
                d_lo, d_hi, w_lo, w_hi = cv[:, 0:1], cv[:, 1:2], cv[:, 2:3], cv[:, 3:4]
                acc = jnp.zeros((rows, D_MODEL), F32)
                for r in range(ns // kc):
                    si = (r * kc + lax.broadcasted_iota(I32, (1, kc), 1)).astype(F32)
                    pw = (jnp.where(si == d_lo, w_lo, 0.0) + jnp.where(si == d_hi, w_hi, 0.0)).astype(BF16)
                    acc = acc + jnp.dot(pw, y_s[g, r * kc:(r + 1) * kc, :], preferred_element_type=F32)
                o_ref[r0:r0 + rows, :] = acc.astype(BF16)


def _moe(h2, meta, col, row, wg, wu, wd, layer):
    t = h2.shape[0]
    n, ns = MOE_N, MOE_NS
    nt = t // n
    meta_flat = meta[:, 0:2, 0:MOE_EXPERTS].reshape(-1)
    grid_spec = pltpu.PrefetchScalarGridSpec(
        num_scalar_prefetch=1,
        grid=(nt // MOE_SUB, MOE_EXPERTS // MOE_EPS),
        in_specs=[pl.BlockSpec((MOE_SUB * n, D_MODEL), lambda i, e, m: (i, 0)),
                  pl.BlockSpec((MOE_SUB * n, LANES), lambda i, e, m: (i, 0)),
                  pl.BlockSpec((MOE_SUB, 8, n), lambda i, e, m: (i, 0, 0)),
                  pl.BlockSpec((None, MOE_EPS, D_MODEL, MOE_FF), lambda i, e, m: (layer, e, 0, 0)),
                  pl.BlockSpec((None, MOE_EPS, D_MODEL, MOE_FF), lambda i, e, m: (layer, e, 0, 0)),
                  pl.BlockSpec((None, MOE_EPS, MOE_FF, D_MODEL), lambda i, e, m: (layer, e, 0, 0))],
        out_specs=pl.BlockSpec((MOE_SUB * n, D_MODEL), lambda i, e, m: (i, 0)),
        scratch_shapes=[pltpu.VMEM((MOE_SUB, ns, D_MODEL), BF16), pltpu.VMEM((MOE_SUB, ns, D_MODEL), BF16)],
    )
    return pl.pallas_call(
        functools.partial(_moe_kernel, n=n, ns=ns),
        grid_spec=grid_spec,
        out_shape=jax.ShapeDtypeStruct((t, D_MODEL), BF16),
        compiler_params=_cparams(("parallel", "arbitrary")),
        name="moe_experts",
    )(meta_flat, h2, col, row, wg, wu, wd)


def _final_kernel(x_ref, y_ref, g_ref, o_ref):
    o_ref[...] = _rms(x_ref[...] + y_ref[...].astype(F32), g_ref[...])


def _final(x, y, g):
    t = x.shape[0]
    tok = pl.BlockSpec((TM, D_MODEL), lambda i: (i, 0))
    return pl.pallas_call(
        _final_kernel,
        grid=(t // TM,),
        in_specs=[tok, tok, pl.BlockSpec((1, D_MODEL), lambda i: (0, 0))],
        out_specs=tok,
        out_shape=jax.ShapeDtypeStruct((t, D_MODEL), F32),
        compiler_params=_cparams(("parallel",)),
        name="final_norm",
    )(x, y, g)


def kernel(x, positions, norm_mix, w_in, ssd_conv_w, ssd_conv_b, ssd_dt_bias, ssd_a_log, ssd_d, ssd_norm, mla_q_norm, mla_w_uq, mla_kv_norm, mla_w_ukv, fox_f_bias, w_out, norm_ffn, router_group_w, router_group_b, router_expert_w, router_expert_b, expert_w_gate, expert_w_up, expert_w_down, final_norm):
    b, s, d = x.shape
    assert d == D_MODEL and s % FA_B == 0 and (b * s) % (MOE_N * MOE_SUB) == 0 and TM == DSA_SC and TM == MOE_N
    depth = w_in.shape[0]
    n_sel = min(IDX_TOPK_MAX, s // 4)
    tab = _rope_tables(positions)
    xf = x.reshape(b * s, d)
    wg, wu, wd = expert_w_gate.astype(BF16), expert_w_up.astype(BF16), expert_w_down.astype(BF16)
    y_ffn = None
    for l in range(depth):
        xf, p = _inproj(xf, y_ffn, norm_mix[l].reshape(1, d), _pack_w_in(w_in[l]), tab, fox_f_bias[l],
                        mla_q_norm[l], mla_kv_norm[l], mla_w_uq[l], mla_w_ukv[l], s)
        y_ssd = _ssd(p["zx"], p["gt"], ssd_conv_w[l], ssd_conv_b[l], ssd_dt_bias[l], ssd_a_log[l], ssd_d[l],
                     ssd_norm[l], b, s)
        y_dsa = _dsa(p["iq"], p["gt"], p["dq"], p["ik"], p["dk"], p["dv"], b, s, n_sel)
        y_att = _flash([(p["mq"], p["mk"], p["mv"]), (p["fq"], p["fk"], p["fv"])], b, s)
        xf, h2, meta, col, row = _outproj(xf, (y_ssd, y_dsa, y_att), w_out[l], norm_ffn[l].reshape(1, d),
                                          router_group_w[l], router_group_b[l], router_expert_w[l], router_expert_b[l])
        y_ffn = _moe(h2, meta, col, row, wg, wu, wd, l)
    return _final(xf, y_ffn, final_norm.reshape(1, d)).reshape(b, s, d)
```

```python
import functools
import math

import jax
import jax.numpy as jnp
from jax import lax
from jax.experimental import pallas as pl
from jax.experimental.pallas import tpu as pltpu

F32 = jnp.float32
BF16 = jnp.bfloat16
I32 = jnp.int32
EPS = 1e-6
ROPE_THETA = 500000.0

D_MODEL = 1024
SSD_HEADS, SSD_HEAD_DIM, SSD_GROUPS, SSD_STATE, SSD_CONV = 4, 64, 2, 128, 4
SSD_D_INNER = SSD_HEADS * SSD_HEAD_DIM
SSD_CONV_DIM = SSD_D_INNER + 2 * SSD_GROUPS * SSD_STATE
DSA_HEADS, DSA_HEAD_DIM, IDX_HEADS, IDX_DIM, IDX_TOPK_MAX = 4, 64, 8, 64, 256
MLA_HEADS, MLA_Q_RANK, MLA_KV_RANK, MLA_NOPE, MLA_ROPE, MLA_V = 4, 256, 128, 64, 32, 64
FOX_HEADS, FOX_HEAD_DIM = 4, 64
MOE_GROUPS, MOE_EPG, MOE_EXPERTS, MOE_FF = 4, 8, 32, 256

LANES = 128
VMEM_LIMIT = 56 * 1024 * 1024
INT_MIN = -(2 ** 31)
NEG_BIG = -1e30

TM = 512
SSD_Q = 256
DSA_BQ = 256
DSA_SC = 512
FA_B = 512
MOE_N = 512
MOE_ALIGN = 16
MOE_CH = 64
MOE_EPS = 2
MOE_SUB = 4
MOE_NS = -(-(2 * MOE_N + MOE_EXPERTS * MOE_ALIGN + MOE_CH) // LANES) * LANES

C_ZX, C_GT, C_DQ, C_KV, C_IK, C_IQ, C_MC, C_FQ, C_FK, C_FV, C_END = (
    0, 1024, 1152, 1408, 1536, 1664, 2176, 2688, 3200, 3712, 4224)
G_FOX, G_DT, G_IW = 0, 4, 8
LOG2E = math.log2(math.e)
HEAD_W = 128
FA_HEADS = 4
VT_ROWS = 80
FOX_ONE_Q, FOX_F_Q, FOX_F_K, FOX_ONE_K = 64, 67, 64, 67


def _cparams(sem):
    return pltpu.CompilerParams(dimension_semantics=sem, vmem_limit_bytes=VMEM_LIMIT)


def _rms(xf, g):
    return xf * lax.rsqrt(jnp.mean(xf * xf, axis=-1, keepdims=True) + EPS) * g


def _softplus(x):
    return jnp.maximum(x, 0.0) + jnp.log1p(jnp.exp(-jnp.abs(x)))


def _silu(x):
    return x * jax.nn.sigmoid(x)


def _rope_apply(a, c, sa, sb, half):
    return a * c + pltpu.roll(a, LANES - half, 1) * sa + pltpu.roll(a, half, 1) * sb


def _dot_nt(a, b):
    return lax.dot_general(a, b, (((1,), (1,)), ((), ())), preferred_element_type=F32)


def _split3(x):
    hi = x.astype(BF16)
    r = x - hi.astype(F32)
    mid = r.astype(BF16)
    return hi, mid, (r - mid.astype(F32)).astype(BF16)


def _dot_f32_by_mask(mask, x, mask_left):
    m = jnp.where(mask, 1.0, 0.0).astype(BF16)
    out = None
    for piece in _split3(x):
        d = jnp.dot(m, piece, preferred_element_type=F32) if mask_left else jnp.dot(piece, m, preferred_element_type=F32)
        out = d if out is None else out + d
    return out


def _rope_tab_kernel(pos_ref, inv_ref, o_ref):
    pos = pos_ref[...].astype(F32)
    lane = lax.broadcasted_iota(I32, (1, LANES), 1)
    ang = pos * inv_ref[0:1, :]
    c, s = jnp.cos(ang), jnp.sin(ang)
    jm = lane - 64
    first = (lane < 8) | ((jm >= 0) & (jm < 16))
    second = ((lane >= 8) & (lane < 16)) | ((jm >= 16) & (jm < 32))
    o_ref[:, 0:128] = c
    o_ref[:, 128:256] = jnp.where(first, -s, 0.0)
    o_ref[:, 256:384] = jnp.where(second, s, 0.0)


def _rope_tables(positions):
    t = positions.size
    half_d, half_m = DSA_HEAD_DIM // 4 // 2, MLA_ROPE // 2
    inv_d = ROPE_THETA ** (-jnp.arange(half_d, dtype=F32) / half_d)
    inv_m = ROPE_THETA ** (-jnp.arange(half_m, dtype=F32) / half_m)
    lane = jnp.arange(LANES)
    jm = lane - 64
    row = jnp.where(lane < 2 * half_d, inv_d[lane % half_d],
                    jnp.where((jm >= 0) & (jm < 2 * half_m), inv_m[jm % half_m], 0.0))
    inv = jnp.zeros((8, LANES), F32).at[0].set(row)
    tm = 1024
    return pl.pallas_call(
        _rope_tab_kernel,
        grid=(t // tm,),
        in_specs=[pl.BlockSpec((tm, 1), lambda i: (i, 0)), pl.BlockSpec((8, LANES), lambda i: (0, 0))],
        out_specs=pl.BlockSpec((tm, 3 * LANES), lambda i: (i, 0)),
        out_shape=jax.ShapeDtypeStruct((t, 3 * LANES), F32),
        compiler_params=_cparams(("parallel",)),
        name="rope_tables",
    )(positions.reshape(t, 1), inv)


def _rope_patterns(tab_ref):
    lane = lax.broadcasted_iota(I32, (1, LANES), 1)
    in_d0, in_d1 = lane < 16, (lane >= 64) & (lane < 80)
    in_m = (lane >= 64) & (lane < 96)
    pat_d, pat_m = [], []
    for i, neutral in enumerate((1.0, 0.0, 0.0)):
        t = tab_ref[:, i * LANES:(i + 1) * LANES]
        pat_d.append(jnp.where(in_d0, t, jnp.where(in_d1, pltpu.roll(t, 64, 1), neutral)))
        pat_m.append(jnp.where(in_m, t, neutral))
    return pat_d, pat_m


def _inproj_kernel(*refs, has_add, tiles_per_seq):
    refs = list(refs)
    x_ref = refs.pop(0)
    yp_ref = refs.pop(0) if has_add else None
    g_ref, w_ref, tab_ref, fb_ref, qn_ref, kn_ref, wq_ref, wk_ref, wv_ref = refs[:9]
    refs = refs[9:]
    xo_ref = refs.pop(0) if has_add else None
    (zx_ref, gt_ref, dq_ref, dk_ref, dv_ref, ik_ref, iq_ref, mq_ref, mk_ref, mv_ref, fq_ref, fk_ref, fv_ref,
     carry_s) = refs
    x = x_ref[...]
    if has_add:
        x = x + yp_ref[...].astype(F32)
        xo_ref[...] = x
    h = _rms(x, g_ref[...]).astype(BF16)

    def proj(a, b):
        return jnp.dot(h, w_ref[:, a:b], preferred_element_type=F32)

    zx_ref[...] = proj(C_ZX, C_GT)
    gates = proj(C_GT, C_DQ)
    gt_ref[...] = gates
    (c_d, sa_d, sb_d), (c_m, sa_m, sb_m) = _rope_patterns(tab_ref)
    q = proj(C_DQ, C_KV)
    for i in range(2):
        dq_ref[:, i * 128:(i + 1) * 128] = _rope_apply(q[:, i * 128:(i + 1) * 128], c_d, sa_d, sb_d, 8).astype(BF16)
    kv = proj(C_KV, C_IK)
    is_k = lax.broadcasted_iota(I32, (1, LANES), 1) < 64
    kvr = _rope_apply(kv, jnp.where(is_k, c_d, 1.0), jnp.where(is_k, sa_d, 0.0), jnp.where(is_k, sb_d, 0.0), 8)
    dk_ref[...] = kvr[:, 0:64].astype(BF16)
    dv_ref[0, 0:64, :] = kvr.astype(BF16).astype(F32).T[64:128, :].astype(BF16)
    dv_ref[0, 64:VT_ROWS, :] = jnp.ones((VT_ROWS - 64, kvr.shape[0]), BF16)
    ikb = proj(C_IK, C_IQ)
    ik_ref[...] = _rope_apply(ikb, c_d, sa_d, sb_d, 8)[:, 0:64].astype(BF16)
    iq = proj(C_IQ, C_MC)
    for i in range(4):
        iq_ref[:, i * 128:(i + 1) * 128] = _rope_apply(iq[:, i * 128:(i + 1) * 128], c_d, sa_d, sb_d, 8).astype(BF16)
    ones_hi = jnp.where(lax.broadcasted_iota(I32, (1, FA_HEADS * HEAD_W), 1) % HEAD_W >= 64, 1.0, 0.0)

    mc = proj(C_MC, C_FQ)
    cq = _rms(mc[:, 0:256], qn_ref[...]).astype(BF16)
    ckv = _rms(mc[:, 256:384], kn_ref[...]).astype(BF16)
    kr = _rope_apply(mc[:, 384:512], c_m, sa_m, sb_m, 16)
    mq = jnp.dot(cq, wq_ref[...], preferred_element_type=F32)
    mk = jnp.dot(ckv, wk_ref[...], preferred_element_type=F32)
    for hd in range(MLA_HEADS):
        sl = slice(hd * HEAD_W, (hd + 1) * HEAD_W)
        mq_ref[:, sl] = _rope_apply(mq[:, sl], c_m, sa_m, sb_m, 16).astype(BF16)
        mk_ref[:, sl] = (mk[:, sl] + kr).astype(BF16)
    mv_ref[...] = (jnp.dot(ckv, wv_ref[...], preferred_element_type=F32) + ones_hi).astype(BF16)

    @pl.when(pl.program_id(0) % tiles_per_seq == 0)
    def _():
        carry_s[...] = jnp.zeros_like(carry_s)

    tm = x.shape[0]
    log_f = -_softplus(-(gates + fb_ref[...]))
    ri = lax.broadcasted_iota(I32, (LANES, LANES), 0)
    ci = lax.broadcasted_iota(I32, (LANES, LANES), 1)
    run = carry_s[0:1, :]
    parts = []
    for r in range(tm // LANES):
        blk = _dot_f32_by_mask(ri >= ci, log_f[r * LANES:(r + 1) * LANES, :], True) + run
        run = blk[LANES - 1:LANES, :]
        parts.append(blk)
    cs = jnp.concatenate(parts, axis=0)
    carry_s[0:1, :] = run
    f2 = cs * LOG2E
    lane = lax.broadcasted_iota(I32, (1, HEAD_W), 1)

    def pieces(f, base):
        hi, mid, lo = _split3(f)
        return jnp.where(lane == base, hi.astype(F32),
                         jnp.where(lane == base + 1, mid.astype(F32), jnp.where(lane == base + 2, lo.astype(F32), 0.0)))

    def ones(base):
        return jnp.where((lane >= base) & (lane < base + 3), 1.0, 0.0)

    fq = proj(C_FQ, C_FK)
    fk = proj(C_FK, C_FV)
    for hd in range(FOX_HEADS):
        fh = f2[:, G_FOX + hd:G_FOX + hd + 1]
        sl = slice(hd * HEAD_W, (hd + 1) * HEAD_W)
        fq_ref[:, sl] = (fq[:, sl] + pieces(fh, FOX_F_Q) + ones(FOX_ONE_Q)).astype(BF16)
        fk_ref[:, sl] = (fk[:, sl] - pieces(fh, FOX_F_K) + ones(FOX_ONE_K)).astype(BF16)
    fv_ref[...] = (proj(C_FV, C_END) + ones_hi).astype(BF16)


def _pack_w_in(w):
    d = w.shape[0]
    z = lambda n: jnp.zeros((d, n), F32)
    o = 0
    parts = {}
    for name, width in (("z", 256), ("xbc", 768), ("dt", 4), ("dq", 256), ("dk", 64), ("dv", 64), ("iq", 512),
                        ("ik", 64), ("iw", 8), ("cq", 256), ("ckv", 128), ("kr", 32), ("fq", 256), ("fk", 256),
                        ("fv", 256), ("ff", 4)):
        parts[name] = w[:, o:o + width]
        o += width
    scale = DSA_HEAD_DIM ** -0.5

    def per_head(a):
        return jnp.pad(a.reshape(d, FA_HEADS, 64), ((0, 0), (0, 0), (0, HEAD_W - 64))).reshape(d, FA_HEADS * HEAD_W)

    cat = jnp.concatenate([
        parts["z"], parts["xbc"],
        parts["ff"], parts["dt"], parts["iw"], z(112),
        parts["dq"] * (scale * LOG2E),
        parts["dk"], parts["dv"],
        parts["ik"], z(64),
        parts["iq"],
        parts["cq"], parts["ckv"], z(64), parts["kr"], z(32),
        per_head(parts["fq"] * (FOX_HEAD_DIM ** -0.5 * LOG2E)), per_head(parts["fk"]), per_head(parts["fv"])], axis=1)
    return cat.astype(BF16)


def _mla_weights(w_uq, w_ukv):
    dqk = MLA_NOPE + MLA_ROPE
    wq = jnp.pad(w_uq.reshape(MLA_Q_RANK, MLA_HEADS, dqk) * (dqk ** -0.5 * LOG2E), ((0, 0), (0, 0), (0, HEAD_W - dqk)))
    wkv = w_ukv.reshape(MLA_KV_RANK, MLA_HEADS, MLA_NOPE + MLA_V)
    wk = jnp.pad(wkv[:, :, :MLA_NOPE], ((0, 0), (0, 0), (0, HEAD_W - MLA_NOPE)))
    wv = jnp.pad(wkv[:, :, MLA_NOPE:], ((0, 0), (0, 0), (0, HEAD_W - MLA_V)))
    flat = lambda a: a.reshape(a.shape[0], MLA_HEADS * HEAD_W).astype(BF16)
    return flat(wq), flat(wk), flat(wv)


def _inproj(x, y_prev, g, w_cat, tab, f_bias, q_norm, kv_norm, w_uq, w_ukv, s):
    t = x.shape[0]
    has_add = y_prev is not None
    tok = lambda n: pl.BlockSpec((TM, n), lambda i: (i, 0))
    full = lambda a: pl.BlockSpec(a.shape, lambda i: (0,) * a.ndim)
    fb = jnp.zeros((1, LANES), F32).at[0, G_FOX:G_FOX + FOX_HEADS].set(f_bias)
    params = [g, w_cat, tab, fb, q_norm.reshape(1, -1), kv_norm.reshape(1, -1), *_mla_weights(w_uq, w_ukv)]
    ins = [x] + ([y_prev] if has_add else []) + params
    in_specs = ([tok(D_MODEL)] + ([tok(D_MODEL)] if has_add else [])
                + [tok(3 * LANES) if a is tab else full(a) for a in params])
    outs = ([("x", D_MODEL, F32)] if has_add else []) + [
        ("zx", 1024, F32), ("gt", 128, F32), ("dq", 256, BF16), ("dk", 64, BF16), ("dv", 64, BF16),
        ("ik", 64, BF16), ("iq", 512, BF16), ("mq", 512, BF16), ("mk", 512, BF16), ("mv", 512, BF16),
        ("fq", 512, BF16), ("fk", 512, BF16), ("fv", 512, BF16)]
    vt_rows = {"dv": VT_ROWS}
    res = pl.pallas_call(
        functools.partial(_inproj_kernel, has_add=has_add, tiles_per_seq=s // TM),
        grid=(t // TM,),
        in_specs=in_specs,
        out_specs=[pl.BlockSpec((1, vt_rows[nm], TM), lambda i: (i, 0, 0)) if nm in vt_rows else tok(n)
                   for nm, n, _ in outs],
        out_shape=[jax.ShapeDtypeStruct((t // TM, vt_rows[nm], TM) if nm in vt_rows else (t, n), dt)
                   for nm, n, dt in outs],
        scratch_shapes=[pltpu.VMEM((8, LANES), F32)],
        compiler_params=_cparams(("arbitrary",)),
        name="inproj",
    )(*ins)
    res = list(res)
    x_new = res.pop(0) if has_add else x
    return x_new, dict(zip([n for n, _, _ in outs if n != "x"], res))


def _ssd_kernel(zx_ref, gt_ref, cw_ref, cb_ref, pr_ref, pc_ref, drow_ref, nw_ref, o_ref, ext_s, st_s, y_s, *, q):
    @pl.when(pl.program_id(1) == 0)
    def _():
        ext_s[0:8, :] = jnp.zeros((8, SSD_CONV_DIM), F32)
        st_s[...] = jnp.zeros_like(st_s)

    raw = zx_ref[:, 256:1024]
    ext_s[8:8 + q, :] = raw
    acc = jnp.broadcast_to(cb_ref[...], (q, SSD_CONV_DIM))
    for j in range(SSD_CONV):
        acc = acc + cw_ref[j:j + 1, :] * ext_s[5 + j:5 + j + q, :]
    ext_s[0:8, :] = raw[q - 8:q, :]
    xbc = _silu(acc)
    xs = xbc[:, 0:SSD_D_INNER]

    g = gt_ref[...]
    lane = lax.broadcasted_iota(I32, (1, LANES), 1)
    dtc = _softplus(g + pr_ref[0:1, :])
    a_r = jnp.where((lane >= G_DT) & (lane < G_DT + SSD_HEADS), -jnp.exp(pr_ref[1:2, :]), 0.0)
    ri = lax.broadcasted_iota(I32, (q, q), 0)
    ci = lax.broadcasted_iota(I32, (q, q), 1)
    tri = ri >= ci
    acs_c = _dot_f32_by_mask(tri, dtc * a_r, True)
    sub = lax.broadcasted_iota(I32, (LANES, 1), 0)
    dtr = _softplus(g.T + pc_ref[:, 0:1])
    a_c = jnp.where((sub >= G_DT) & (sub < G_DT + SSD_HEADS), -jnp.exp(pc_ref[:, 1:2]), 0.0)
    acs_r = _dot_f32_by_mask(ri <= ci, dtr * a_c, False)

    rep = SSD_HEADS // SSD_GROUPS
    for gi in range(SSD_GROUPS):
        bg = xbc[:, SSD_D_INNER + gi * SSD_STATE:SSD_D_INNER + (gi + 1) * SSD_STATE]
        cg = xbc[:, SSD_D_INNER + (SSD_GROUPS + gi) * SSD_STATE:SSD_D_INNER + (SSD_GROUPS + gi + 1) * SSD_STATE]
        bt = bg.T.astype(BF16)
        cb16 = cg.astype(BF16)
        cbm = jnp.dot(cb16, bt, preferred_element_type=F32)
        for hh in range(rep):
            h = gi * rep + hh
            ac = acs_c[:, G_DT + h:G_DT + h + 1]
            ar = acs_r[G_DT + h:G_DT + h + 1, :]
            seg = jnp.where(tri, jnp.exp(ac - ar), 0.0)
            xh = xs[:, h * SSD_HEAD_DIM:(h + 1) * SSD_HEAD_DIM]
            xdt = xh * dtc[:, G_DT + h:G_DT + h + 1]
            yd = jnp.dot((cbm * seg).astype(BF16), xdt.astype(BF16), preferred_element_type=F32)
            aend = ac[q - 1:q, :]
            st = st_s[h]
            yo = jnp.dot(cb16, st.astype(BF16), preferred_element_type=F32) * jnp.exp(ac)
            st_s[h] = st * jnp.exp(aend) + jnp.dot(bt, (xdt * jnp.exp(aend - ac)).astype(BF16),
                                                   preferred_element_type=F32)
            y_s[:, h * SSD_HEAD_DIM:(h + 1) * SSD_HEAD_DIM] = yd + yo
    y = y_s[...] + drow_ref[...] * xs
    y = y * _silu(zx_ref[:, 0:SSD_D_INNER])
    o_ref[...] = _rms(y, nw_ref[...]).astype(BF16)


def _ssd(zx, gt, conv_w, conv_b, dt_bias, a_log, d_skip, norm_w, b, s):
    q = SSD_Q
    nc = s // q
    pr = jnp.zeros((8, LANES), F32).at[0, G_DT:G_DT + SSD_HEADS].set(dt_bias).at[1, G_DT:G_DT + SSD_HEADS].set(a_log)
    pc = pr.T
    drow = jnp.repeat(d_skip, SSD_HEAD_DIM).reshape(1, SSD_D_INNER)
    cb = conv_b.reshape(1, -1)
    nw = norm_w.reshape(1, -1)
    full = lambda a: pl.BlockSpec(a.shape, lambda i, j: (0,) * a.ndim)
    return pl.pallas_call(
        functools.partial(_ssd_kernel, q=q),
        grid=(b, nc),
        in_specs=[pl.BlockSpec((q, 1024), lambda i, j: (i * nc + j, 0)), pl.BlockSpec((q, LANES), lambda i, j: (i * nc + j, 0)),
                  full(conv_w), full(cb), full(pr), full(pc), full(drow), full(nw)],
        out_specs=pl.BlockSpec((q, SSD_D_INNER), lambda i, j: (i * nc + j, 0)),
        out_shape=jax.ShapeDtypeStruct((b * s, SSD_D_INNER), BF16),
        scratch_shapes=[pltpu.VMEM((8 + q, SSD_CONV_DIM), F32), pltpu.VMEM((SSD_HEADS, SSD_STATE, SSD_HEAD_DIM), F32),
                        pltpu.VMEM((q, SSD_D_INNER), F32)],
        compiler_params=_cparams(("parallel", "arbitrary")),
        name="ssd",
    )(zx, gt, conv_w, cb, pr, pc, drow, nw)


def _key_to_f32(k):
    return pltpu.bitcast(jnp.where(k < 0, k ^ jnp.int32(0x7FFFFFFF), k), F32)


def _dsa_kernel(iq_ref, gt_ref, q_ref, ik_ref, k_ref, vt_ref, o_ref, sc_s, qi_s, qs_s, thr_s, s16_s,
                *, bq, sc_w, n_sel):
    qb = pl.program_id(1)
    n_sc = ((qb + 1) * bq + sc_w - 1) // sc_w
    for j in range(IDX_HEADS):
        qi_s[j // 2, (j % 2) * bq:(j % 2 + 1) * bq, :] = iq_ref[:, j * IDX_DIM:(j + 1) * IDX_DIM]
    for h in range(DSA_HEADS):
        qs_s[h * bq:(h + 1) * bq, :] = q_ref[:, h * DSA_HEAD_DIM:(h + 1) * DSA_HEAD_DIM]
    w_t = (gt_ref[...] * ((IDX_HEADS * IDX_DIM) ** -0.5)).T
    qpos = qb * bq + lax.broadcasted_iota(I32, (1, bq), 1)

    def score_body(sc, carry):
        k0 = pl.multiple_of(sc * sc_w, sc_w)
        kidx = ik_ref[pl.ds(k0, sc_w), :]
        sco = jnp.zeros((sc_w, bq), F32)
        for jp in range(IDX_HEADS // 2):
            a = _dot_nt(kidx, qi_s[jp])
            for u in range(2):
                j = 2 * jp + u
                sco = sco + jnp.maximum(a[:, u * bq:(u + 1) * bq], 0.0) * w_t[G_IW + j:G_IW + j + 1, :]
        kpos = k0 + lax.broadcasted_iota(I32, (sc_w, 1), 0)
        sco = jnp.where(kpos <= qpos, sco, -jnp.inf)
        sc_s[sc] = sco
        s16_s[sc] = sco.astype(BF16)
        return carry

    lax.fori_loop(0, n_sc, score_body, 0)

    def count(pred):
        def body(sc, acc):
            for r in range(sc_w // 64):
                acc = acc + jnp.where(pred(sc_s[sc, r * 64:(r + 1) * 64, :]), 1.0, 0.0)
            return acc
        acc = lax.fori_loop(0, n_sc, body, jnp.zeros((64, bq), F32))
        return jnp.sum(acc, axis=0, keepdims=True)

    thr_s[0:1, :] = jnp.full((1, bq), -jnp.finfo(jnp.float32).max, F32)

    @pl.when((qb + 1) * bq > n_sel)
    def _():
        active = qpos + 1 > n_sel

        one16, zero16 = jnp.ones((), BF16), jnp.zeros((), BF16)

        def count16(c16):
            def body(sc, acc):
                a16 = jnp.zeros((64, bq), BF16)
                for r in range(sc_w // 64):
                    a16 = a16 + jnp.where(s16_s[sc, r * 64:(r + 1) * 64, :] >= c16, one16, zero16)
                return acc + a16.astype(F32)
            acc = lax.fori_loop(0, n_sc, body, jnp.zeros((64, bq), F32))
            return jnp.sum(acc, axis=0, keepdims=True)

        def coarse(it, lo):
            cand = lo ^ jnp.left_shift(jnp.int32(1), 31 - it)
            return jnp.where(count16(_key_to_f32(cand).astype(BF16)) >= n_sel, cand, lo)

        lo16 = lax.fori_loop(0, 16, coarse, jnp.full((1, bq), INT_MIN, I32))
        base = jnp.maximum(lo16, jnp.int32(INT_MIN + (1 << 16))) - jnp.int32(1 << 16)

        def cond(st):
            it, off, cnt_lo = st
            pending = active & (cnt_lo != n_sel)
            return (it < 32) & (jnp.max(jnp.where(pending, 1.0, 0.0)) > 0.0)

        def fine(it, off, cnt_lo):
            cand = off | jnp.left_shift(jnp.int32(1), 31 - it)
            thr_c = _key_to_f32(base + cand)
            cnt = count(lambda t: t >= thr_c)
            ok = cnt >= n_sel
            return jnp.where(ok, cand, off), jnp.where(ok, cnt, cnt_lo)

        def body(st):
            it, off, cnt_lo = st
            for u in range(2):
                off, cnt_lo = fine(it + u, off, cnt_lo)
            return it + 2, off, cnt_lo

        off, cnt_lo = fine(15, jnp.zeros((1, bq), I32), jnp.full((1, bq), 3.0e38, F32))
        _, off, cnt_lo = lax.while_loop(cond, body, (jnp.int32(16), off, cnt_lo))
        lo = base + off
        thr = _key_to_f32(lo)
        tie = active & (cnt_lo > n_sel)

        @pl.when(jnp.max(jnp.where(tie, 1.0, 0.0)) > 0.0)
        def _():
            need = n_sel - count(lambda t: t > thr)
            ri = lax.broadcasted_iota(I32, (sc_w, sc_w), 0)
            ci = lax.broadcasted_iota(I32, (sc_w, sc_w), 1)
            lower = jnp.where(ri >= ci, 1.0, 0.0).astype(BF16)

            def tie_body(sc, run):
                t = sc_s[sc]
                eq = (t == thr) & tie
                pre = jnp.dot(lower, jnp.where(eq, 1.0, 0.0).astype(BF16), preferred_element_type=F32)
                sc_s[sc] = jnp.where(eq & (run + pre > need), -jnp.inf, t)
                return run + pre[sc_w - 1:sc_w, :]

            lax.fori_loop(0, n_sc, tie_body, jnp.zeros((1, bq), F32))

        thr_s[0:1, :] = jnp.where(active, thr, thr_s[0:1, :])

    thr = thr_s[0:1, :]

    hq = DSA_HEADS * bq

    def att_body(sc, carry):
        m_prev, acc = carry
        k0 = pl.multiple_of(sc * sc_w, sc_w)
        st = _dot_nt(k_ref[pl.ds(k0, sc_w), :], qs_s[...])
        drop = jnp.where(sc_s[sc] >= thr, 0.0, 2 * NEG_BIG)
        st = st + jnp.tile(drop, (1, DSA_HEADS))
        m_cur = jnp.max(jnp.max(st.reshape(sc_w // 64, 64, hq), axis=0), axis=0, keepdims=True)
        m_next = jnp.maximum(m_prev, m_cur)
        p = jnp.exp2(st - m_next)
        pv = jnp.dot(vt_ref[sc], p.astype(BF16), preferred_element_type=F32)
        return m_next, jnp.exp2(m_prev - m_next) * acc + pv

    def att_body2(i, carry):
        return att_body(2 * i + 1, att_body(2 * i, carry))

    carry = lax.fori_loop(0, n_sc // 2, att_body2, (jnp.full((1, hq), NEG_BIG, F32), jnp.zeros((VT_ROWS, hq), F32)))
    _, acc = lax.cond(n_sc % 2 == 1, lambda c: att_body(n_sc - 1, c), lambda c: c, carry)
    out_t = jnp.concatenate([acc[0:DSA_HEAD_DIM, :] / acc[DSA_HEAD_DIM:DSA_HEAD_DIM + 1, :],
                             jnp.zeros((LANES - DSA_HEAD_DIM, hq), F32)], axis=0).T
    for h in range(DSA_HEADS):
        o_ref[:, h * DSA_HEAD_DIM:(h + 1) * DSA_HEAD_DIM] = out_t[h * bq:(h + 1) * bq, 0:DSA_HEAD_DIM].astype(BF16)


def _dsa(iq, gt, dq, ik, dk, dvt, b, s, n_sel):
    bq, sc_w = DSA_BQ, DSA_SC
    nq = s // bq
    qblk = lambda n: pl.BlockSpec((bq, n), lambda i, j: (i * nq + j, 0))
    kblk = pl.BlockSpec((s, 64), lambda i, j: (i, 0))
    return pl.pallas_call(
        functools.partial(_dsa_kernel, bq=bq, sc_w=sc_w, n_sel=n_sel),
        grid=(b, nq),
        in_specs=[qblk(512), qblk(LANES), qblk(256), kblk, kblk,
                  pl.BlockSpec((s // sc_w, VT_ROWS, sc_w), lambda i, j: (i, 0, 0))],
        out_specs=qblk(256),
        out_shape=jax.ShapeDtypeStruct((b * s, 256), BF16),
        scratch_shapes=[pltpu.VMEM((s // sc_w, sc_w, bq), F32), pltpu.VMEM((IDX_HEADS // 2, 2 * bq, IDX_DIM), BF16),
                        pltpu.VMEM((DSA_HEADS * bq, DSA_HEAD_DIM), BF16), pltpu.VMEM((8, bq), F32),
                        pltpu.VMEM((s // sc_w, sc_w, bq), BF16)],
        compiler_params=_cparams(("parallel", "arbitrary")),
        name="dsa",
    )(iq, gt, dq, ik, dk, dvt)


def _flash_kernel(qt_ref, kt_ref, *refs, blk, sets):
    qkv, (o_ref, m_s, acc_s) = refs[:3 * sets], refs[3 * sets:]
    t = pl.program_id(1)
    qi, ki = qt_ref[t], kt_ref[t]

    @pl.when(ki == 0)
    def _():
        m_s[...] = jnp.full(m_s.shape, NEG_BIG, F32)
        acc_s[...] = jnp.zeros_like(acc_s)

    def step(diagonal):
        if diagonal:
            keep = lax.broadcasted_iota(I32, (blk, blk), 0) >= lax.broadcasted_iota(I32, (blk, blk), 1)
        for g in range(sets):
            q_ref, k_ref, v_ref = qkv[3 * g:3 * g + 3]
            for h in range(FA_HEADS):
                sl = slice(h * HEAD_W, (h + 1) * HEAD_W)
                hh = g * FA_HEADS + h
                s = _dot_nt(q_ref[:, sl], k_ref[:, sl])
                if diagonal:
                    s = jnp.where(keep, s, 2 * NEG_BIG)
                m_prev = m_s[hh]
                m_next = jnp.maximum(m_prev, jnp.max(s, axis=1, keepdims=True))
                p = jnp.exp2(s - jnp.tile(m_next, (1, blk // LANES)))
                pv = jnp.dot(p.astype(BF16), v_ref[:, sl], preferred_element_type=F32)
                acc_s[hh] = jnp.exp2(m_prev - m_next) * acc_s[hh] + pv
                m_s[hh] = m_next

    @pl.when(ki < qi)
    def _():
        step(False)

    @pl.when(ki == qi)
    def _():
        step(True)
        for hh in range(sets * FA_HEADS):
            acc = acc_s[hh]
            o_ref[:, hh * 64:(hh + 1) * 64] = (acc / pltpu.roll(acc, 64, 1))[:, 0:64].astype(BF16)


def _flash(qkv, b, s):
    blk = FA_B
    nb = s // blk
    pairs = [(i, j) for i in range(nb) for j in range(i + 1)]
    qt = jnp.asarray([p[0] for p in pairs], I32)
    kt = jnp.asarray([p[1] for p in pairs], I32)
    w = FA_HEADS * HEAD_W
    sets = len(qkv)
    q_spec = pl.BlockSpec((blk, w), lambda i, t, qt, kt: (i * nb + qt[t], 0))
    kv_spec = pl.BlockSpec((blk, w), lambda i, t, qt, kt: (i * nb + kt[t], 0))
    grid_spec = pltpu.PrefetchScalarGridSpec(
        num_scalar_prefetch=2,
        grid=(b, len(pairs)),
        in_specs=[q_spec, kv_spec, kv_spec] * sets,
        out_specs=pl.BlockSpec((blk, sets * FA_HEADS * 64), lambda i, t, qt, kt: (i * nb + qt[t], 0)),
        scratch_shapes=[pltpu.VMEM((sets * FA_HEADS, blk, LANES), F32), pltpu.VMEM((sets * FA_HEADS, blk, HEAD_W), F32)],
    )
    return pl.pallas_call(
        functools.partial(_flash_kernel, blk=blk, sets=sets),
        grid_spec=grid_spec,
        out_shape=jax.ShapeDtypeStruct((b * s, sets * FA_HEADS * 64), BF16),
        compiler_params=_cparams(("parallel", "arbitrary")),
        name="flash",
    )(qt, kt, *[a for triple in qkv for a in triple])


def _outproj_kernel(x_ref, *refs, n_mix):
    y_refs, (wo_ref, g_ref, wr_ref, br_ref, xo_ref, h_ref, meta_ref, col_ref, row_ref) = refs[:n_mix], refs[n_mix:]
    acc = x_ref[...]
    row = 0
    for y_ref in y_refs:
        acc = acc + jnp.dot(y_ref[...], wo_ref[row:row + y_ref.shape[1], :], preferred_element_type=F32)
        row += y_ref.shape[1]
    xo_ref[...] = acc
    h2 = _rms(acc, g_ref[...])
    h_ref[...] = h2.astype(BF16)
    h_hi, h_lo, _ = _split3(h2)
    lg = (jnp.dot(h_hi, wr_ref[0], preferred_element_type=F32) + jnp.dot(h_lo, wr_ref[0], preferred_element_type=F32)
          + jnp.dot(h_hi, wr_ref[1], preferred_element_type=F32) + br_ref[...])
    lane = lax.broadcasted_iota(I32, (1, LANES), 1)
    lanef = lane.astype(F32)
    is_g = (lane >= MOE_EXPERTS) & (lane < MOE_EXPERTS + MOE_GROUPS)
    gl = jnp.where(is_g, lg, -jnp.inf)
    gmax = jnp.max(gl, axis=1, keepdims=True)
    gidx = jnp.min(jnp.where(gl == gmax, lanef, 999.0), axis=1, keepdims=True) - MOE_EXPERTS
    g_p = 1.0 / jnp.sum(jnp.where(is_g, jnp.exp(gl - gmax), 0.0), axis=1, keepdims=True)
    in_g = (lane < MOE_EXPERTS) & (jnp.floor(lanef * (1.0 / MOE_EPG)) == gidx)
    el = jnp.where(in_g, lg, -jnp.inf)
    m1 = jnp.max(el, axis=1, keepdims=True)
    i1 = jnp.min(jnp.where(el == m1, lanef, 999.0), axis=1, keepdims=True)
    el2 = jnp.where(lanef == i1, -jnp.inf, el)
    m2 = jnp.max(el2, axis=1, keepdims=True)
    i2 = jnp.min(jnp.where(el2 == m2, lanef, 999.0), axis=1, keepdims=True)
    t = jnp.exp(m2 - m1)
    w1 = 1.0 / (1.0 + t)
    gate_w = jnp.where(lanef == i1, w1 * g_p, jnp.where(lanef == i2, t * w1 * g_p, 0.0))
    chosen = jnp.where((lanef == i1) | (lanef == i2), 1.0, 0.0).astype(BF16)
    meta_ref[0], col_ref[...], row_ref[0] = _moe_plan(chosen, gate_w)


def _outproj(x, ys, w_out, g, w_rg, b_rg, w_re, b_re):
    t = x.shape[0]
    wo = w_out.astype(BF16)
    wr = jnp.zeros((D_MODEL, LANES), F32).at[:, 0:MOE_EXPERTS].set(w_re).at[:, MOE_EXPERTS:MOE_EXPERTS + MOE_GROUPS].set(w_rg)
    wr_hi = wr.astype(BF16)
    wr = jnp.stack([wr_hi, (wr - wr_hi.astype(F32)).astype(BF16)])
    br =jnp.zeros((1, LANES), F32).at[0, 0:MOE_EXPERTS].set(b_re).at[0, MOE_EXPERTS:MOE_EXPERTS + MOE_GROUPS].set(b_rg)
    tok = lambda n: pl.BlockSpec((TM, n), lambda i: (i, 0))
    full = lambda a: pl.BlockSpec(a.shape, lambda i: (0,) * a.ndim)
    return pl.pallas_call(
        functools.partial(_outproj_kernel, n_mix=len(ys)),
        grid=(t // TM,),
        in_specs=[tok(D_MODEL)] + [tok(y.shape[1]) for y in ys] + [full(wo), full(g), full(wr), full(br)],
        out_specs=[tok(D_MODEL), tok(D_MODEL), pl.BlockSpec((1, 8, LANES), lambda i: (i, 0, 0)), tok(LANES),
                   pl.BlockSpec((1, 8, TM), lambda i: (i, 0, 0))],
        out_shape=[jax.ShapeDtypeStruct((t, D_MODEL), F32), jax.ShapeDtypeStruct((t, D_MODEL), BF16),
                   jax.ShapeDtypeStruct((t // TM, 8, LANES), I32), jax.ShapeDtypeStruct((t, LANES), F32),
                   jax.ShapeDtypeStruct((t // TM, 8, TM), F32)],
        compiler_params=_cparams(("parallel",)),
        name="outproj_router",
    )(x, *ys, wo, g, wr, br)


def _moe_plan(sel, cw):
    n = sel.shape[0]
    self32 = sel.astype(F32)
    ti = lax.broadcasted_iota(I32, (n, n), 0)
    tj = lax.broadcasted_iota(I32, (n, n), 1)
    rank = jnp.dot(jnp.where(ti > tj, 1.0, 0.0).astype(BF16), sel, preferred_element_type=F32)
    cnt = jnp.sum(self32, axis=0, keepdims=True).astype(I32)
    cpad = ((cnt + (MOE_ALIGN - 1)) // MOE_ALIGN) * MOE_ALIGN
    li = lax.broadcasted_iota(I32, (LANES, LANES), 0)
    lj = lax.broadcasted_iota(I32, (LANES, LANES), 1)
    off = _dot_f32_by_mask(li < lj, jnp.broadcast_to(cpad.astype(F32), (8, LANES)), False)[0:1, :]
    dest = off + rank
    lane = lax.broadcasted_iota(I32, (1, LANES), 1)
    lanef = lane.astype(F32)
    on = self32 > 0.5
    e_lo = jnp.min(jnp.where(on, lanef, 999.0), axis=1, keepdims=True)
    e_hi = jnp.max(jnp.where(on, lanef, -1.0), axis=1, keepdims=True)
    is_lo, is_hi = lanef == e_lo, lanef == e_hi
    pick = lambda m, a: jnp.sum(jnp.where(m, a, 0.0), axis=1, keepdims=True)
    d_lo, d_hi, w_lo, w_hi = pick(is_lo, dest), pick(is_hi, dest), pick(is_lo, cw), pick(is_hi, cw)
    colv = jnp.where(lane == 0, d_lo, jnp.where(lane == 1, d_hi, jnp.where(lane == 2, w_lo, jnp.where(lane == 3, w_hi, 0.0))))
    sub = lax.broadcasted_iota(I32, (8, LANES), 0)
    meta = jnp.where(sub == 0, off.astype(I32), jnp.where(sub == 1, cnt, 0))
    return meta, colv, colv.T[0:8, :]


def _moe_kernel(meta_ref, h_ref, col_ref, row_ref, wg_ref, wu_ref, wd_ref, o_ref, s_s, y_s, *, n, ns):
    t, e = pl.program_id(0), pl.program_id(1)

    @pl.when(e == 0)
    def _():
        for g in range(MOE_SUB):
            d_lo, d_hi = row_ref[g, 0:1, :], row_ref[g, 1:2, :]
            hg = h_ref[g * n:(g + 1) * n, :]
            for r in range(ns // LANES):
                si = (r * LANES + lax.broadcasted_iota(I32, (LANES, 1), 0)).astype(F32)
                perm = jnp.where((si == d_lo) | (si == d_hi), 1.0, 0.0).astype(BF16)
                s_s[g, r * LANES:(r + 1) * LANES, :] = jnp.dot(perm, hg, preferred_element_type=F32).astype(BF16)
        y_s[...] = jnp.zeros_like(y_s)

    def mlp(j, starts, store_ok):
        starts = [pl.multiple_of(r0, MOE_ALIGN) for r0 in starts]
        xs = jnp.concatenate([s_s[g, pl.ds(starts[g], MOE_CH), :] for g in range(MOE_SUB)], axis=0)
        gate = jnp.dot(xs, wg_ref[j], preferred_element_type=F32)
        up = jnp.dot(xs, wu_ref[j], preferred_element_type=F32)
        hid = (_silu(gate) * up).astype(BF16)
        y = jnp.dot(hid, wd_ref[j], preferred_element_type=F32).astype(BF16)
        for g in range(MOE_SUB):
            def store(g=g):
                y_s[g, pl.ds(starts[g], MOE_CH), :] = y[g * MOE_CH:(g + 1) * MOE_CH, :]
            if store_ok is None:
                store()
            else:
                pl.when(store_ok[g])(store)

    def meta(g, j, what):
        return meta_ref[(t * MOE_SUB + g) * 2 * MOE_EXPERTS + what * MOE_EXPERTS + e * MOE_EPS + j]

    offs = [[meta(g, j, 0) for g in range(MOE_SUB)] for j in range(MOE_EPS)]
    cnts = [[meta(g, j, 1) for g in range(MOE_SUB)] for j in range(MOE_EPS)]
    for j in range(MOE_EPS):
        mlp(j, offs[j], None)
    for j in range(MOE_EPS):
        ends = [offs[j][g] + ((cnts[j][g] + MOE_ALIGN - 1) // MOE_ALIGN) * MOE_ALIGN for g in range(MOE_SUB)]
        nch = [(cnts[j][g] + MOE_CH - 1) // MOE_CH for g in range(MOE_SUB)]

        def body(i, carry, j=j, ends=ends, nch=nch):
            more = [i < nch[g] for g in range(MOE_SUB)]
            starts = [jnp.where(more[g], jnp.minimum(offs[j][g] + i * MOE_CH, ends[g] - MOE_CH), offs[j][g])
                      for g in range(MOE_SUB)]
            mlp(j, starts, more)
            return carry

        lax.fori_loop(1, functools.reduce(jnp.maximum, nch), body, 0)

    @pl.when(e == pl.num_programs(1) - 1)
    def _():
        rows = 256
        kc = ns // next(k for k in (3, 2, 1) if ns % (k * LANES) == 0)
        for g in range(MOE_SUB):
            for c in range(n // rows):
                r0 = g * n + c * rows
                cv = col_ref[r0:r0 + rows, :]
                d_lo, d_hi, w_lo, w_hi = cv[:, 0:1], cv[:, 1:2], cv[:, 2:3], cv[:, 3:4]
                acc = jnp.zeros((rows, D_MODEL), F32)
                for r in range(ns // kc):
                    si = (r * kc + lax.broadcasted_iota(I32, (1, kc), 1)).astype(F32)
                    pw = (jnp.where(si == d_lo, w_lo, 0.0) + jnp.where(si == d_hi, w_hi, 0.0)).astype(BF16)
                    acc = acc + jnp.dot(pw, y_s[g, r * kc:(r + 1) * kc, :], preferred_element_type=F32)
                o_ref[r0:r0 + rows, :] = acc.astype(BF16)


def _moe(h2, meta, col, row, wg, wu, wd, layer):
    t = h2.shape[0]
    n, ns = MOE_N, MOE_NS
    nt = t // n
    meta_flat = meta[:, 0:2, 0:MOE_EXPERTS].reshape(-1)
    grid_spec = pltpu.PrefetchScalarGridSpec(
        num_scalar_prefetch=1,
        grid=(nt // MOE_SUB, MOE_EXPERTS // MOE_EPS),
        in_specs=[pl.BlockSpec((MOE_SUB * n, D_MODEL), lambda i, e, m: (i, 0)),
                  pl.BlockSpec((MOE_SUB * n, LANES), lambda i, e, m: (i, 0)),
                  pl.BlockSpec((MOE_SUB, 8, n), lambda i, e, m: (i, 0, 0)),
                  pl.BlockSpec((None, MOE_EPS, D_MODEL, MOE_FF), lambda i, e, m: (layer, e, 0, 0)),
                  pl.BlockSpec((None, MOE_EPS, D_MODEL, MOE_FF), lambda i, e, m: (layer, e, 0, 0)),
                  pl.BlockSpec((None, MOE_EPS, MOE_FF, D_MODEL), lambda i, e, m: (layer, e, 0, 0))],
        out_specs=pl.BlockSpec((MOE_SUB * n, D_MODEL), lambda i, e, m: (i, 0)),
        scratch_shapes=[pltpu.VMEM((MOE_SUB, ns, D_MODEL), BF16), pltpu.VMEM((MOE_SUB, ns, D_MODEL), BF16)],
    )
    return pl.pallas_call(
        functools.partial(_moe_kernel, n=n, ns=ns),
        grid_spec=grid_spec,
        out_shape=jax.ShapeDtypeStruct((t, D_MODEL), BF16),
        compiler_params=_cparams(("parallel", "arbitrary")),
        name="moe_experts",
    )(meta_flat, h2, col, row, wg, wu, wd)


def _final_kernel(x_ref, y_ref, g_ref, o_ref):
    o_ref[...] = _rms(x_ref[...] + y_ref[...].astype(F32), g_ref[...])


def _final(x, y, g):
    t = x.shape[0]
    tok = pl.BlockSpec((TM, D_MODEL), lambda i: (i, 0))
    return pl.pallas_call(
        _final_kernel,
        grid=(t // TM,),
        in_specs=[tok, tok, pl.BlockSpec((1, D_MODEL), lambda i: (0, 0))],
        out_specs=tok,
        out_shape=jax.ShapeDtypeStruct((t, D_MODEL), F32),
        compiler_params=_cparams(("parallel",)),
        name="final_norm",
    )(x, y, g)


def kernel(x, positions, norm_mix, w_in, ssd_conv_w, ssd_conv_b, ssd_dt_bias, ssd_a_log, ssd_d, ssd_norm, mla_q_norm, mla_w_uq, mla_kv_norm, mla_w_ukv, fox_f_bias, w_out, norm_ffn, router_group_w, router_group_b, router_expert_w, router_expert_b, expert_w_gate, expert_w_up, expert_w_down, final_norm):
    b, s, d = x.shape
    assert d == D_MODEL and s % FA_B == 0 and (b * s) % (MOE_N * MOE_SUB) == 0 and TM == DSA_SC and TM == MOE_N
    depth = w_in.shape[0]
    n_sel = min(IDX_TOPK_MAX, s // 4)
    tab = _rope_tables(positions)
    xf = x.reshape(b * s, d)
    wg, wu, wd = expert_w_gate.astype(BF16), expert_w_up.astype(BF16), expert_w_down.astype(BF16)
    y_ffn = None
    for l in range(depth):
        xf, p = _inproj(xf, y_ffn, norm_mix[l].reshape(1, d), _pack_w_in(w_in[l]), tab, fox_f_bias[l],
                        mla_q_norm[l], mla_kv_norm[l], mla_w_uq[l], mla_w_ukv[l], s)
        y_ssd = _ssd(p["zx"], p["gt"], ssd_conv_w[l], ssd_conv_b[l], ssd_dt_bias[l], ssd_a_log[l], ssd_d[l],
                     ssd_norm[l], b, s)
        y_dsa = _dsa(p["iq"], p["gt"], p["dq"], p["ik"], p["dk"], p["dv"], b, s, n_sel)
        y_att = _flash([(p["mq"], p["mk"], p["mv"]), (p["fq"], p["fk"], p["fv"])], b, s)
        xf, h2, meta, col, row = _outproj(xf, (y_ssd, y_dsa, y_att), w_out[l], norm_ffn[l].reshape(1, d),
                                          router_group_w[l], router_group_b[l], router_expert_w[l], router_expert_b[l])
        y_ffn = _moe(h2, meta, col, row, wg, wu, wd, l)
    return _final(xf, y_ffn, final_norm.reshape(1, d)).reshape(b, s, d)
```

```python
import functools
import math

import jax
import jax.numpy as jnp
from jax import lax
from jax.experimental import pallas as pl
from jax.experimental.pallas import tpu as pltpu

F32 = jnp.float32
BF16 = jnp.bfloat16
I32 = jnp.int32
EPS = 1e-6
ROPE_THETA = 500000.0

D_MODEL = 1024
SSD_HEADS, SSD_HEAD_DIM, SSD_GROUPS, SSD_STATE, SSD_CONV = 4, 64, 2, 128, 4
SSD_D_INNER = SSD_HEADS * SSD_HEAD_DIM
SSD_CONV_DIM = SSD_D_INNER + 2 * SSD_GROUPS * SSD_STATE
DSA_HEADS, DSA_HEAD_DIM, IDX_HEADS, IDX_DIM, IDX_TOPK_MAX = 4, 64, 8, 64, 256
MLA_HEADS, MLA_Q_RANK, MLA_KV_RANK, MLA_NOPE, MLA_ROPE, MLA_V = 4, 256, 128, 64, 32, 64
FOX_HEADS, FOX_HEAD_DIM = 4, 64
MOE_GROUPS, MOE_EPG, MOE_EXPERTS, MOE_FF = 4, 8, 32, 256

LANES = 128
VMEM_LIMIT = 56 * 1024 * 1024
INT_MIN = -(2 ** 31)
NEG_BIG = -1e30

TM = 512
SSD_Q = 256
DSA_BQ = 256
DSA_SC = 512
FA_B = 512
MOE_N = 512
MOE_ALIGN = 16
MOE_CH = 64
MOE_EPS = 2
MOE_SUB = 4
MOE_NS = -(-(2 * MOE_N + MOE_EXPERTS * MOE_ALIGN + MOE_CH) // LANES) * LANES

C_ZX, C_GT, C_DQ, C_KV, C_IK, C_IQ, C_MC, C_FQ, C_FK, C_FV, C_END = (
    0, 1024, 1152, 1408, 1536, 1664, 2176, 2688, 3200, 3712, 4224)
G_FOX, G_DT, G_IW = 0, 4, 8
LOG2E = math.log2(math.e)
HEAD_W = 128
FA_HEADS = 4
VT_ROWS = 80
FOX_ONE_Q, FOX_F_Q, FOX_F_K, FOX_ONE_K = 64, 67, 64, 67


def _cparams(sem):
    return pltpu.CompilerParams(dimension_semantics=sem, vmem_limit_bytes=VMEM_LIMIT)


def _rms(xf, g):
    return xf * lax.rsqrt(jnp.mean(xf * xf, axis=-1, keepdims=True) + EPS) * g


def _softplus(x):
    return jnp.maximum(x, 0.0) + jnp.log1p(jnp.exp(-jnp.abs(x)))


def _silu(x):
    return x * jax.nn.sigmoid(x)


def _rope_apply(a, c, sa, sb, half):
    return a * c + pltpu.roll(a, LANES - half, 1) * sa + pltpu.roll(a, half, 1) * sb


def _dot_nt(a, b):
    return lax.dot_general(a, b, (((1,), (1,)), ((), ())), preferred_element_type=F32)


def _split3(x):
    hi = x.astype(BF16)
    r = x - hi.astype(F32)
    mid = r.astype(BF16)
    return hi, mid, (r - mid.astype(F32)).astype(BF16)


def _dot_f32_by_mask(mask, x, mask_left):
    m = jnp.where(mask, 1.0, 0.0).astype(BF16)
    out = None
    for piece in _split3(x):
        d = jnp.dot(m, piece, preferred_element_type=F32) if mask_left else jnp.dot(piece, m, preferred_element_type=F32)
        out = d if out is None else out + d
    return out


def _rope_tab_kernel(pos_ref, inv_ref, o_ref):
    pos = pos_ref[...].astype(F32)
    lane = lax.broadcasted_iota(I32, (1, LANES), 1)
    ang = pos * inv_ref[0:1, :]
    c, s = jnp.cos(ang), jnp.sin(ang)
    jm = lane - 64
    first = (lane < 8) | ((jm >= 0) & (jm < 16))
    second = ((lane >= 8) & (lane < 16)) | ((jm >= 16) & (jm < 32))
    o_ref[:, 0:128] = c
    o_ref[:, 128:256] = jnp.where(first, -s, 0.0)
    o_ref[:, 256:384] = jnp.where(second, s, 0.0)


def _rope_tables(positions):
    t = positions.size
    half_d, half_m = DSA_HEAD_DIM // 4 // 2, MLA_ROPE // 2
    inv_d = ROPE_THETA ** (-jnp.arange(half_d, dtype=F32) / half_d)
    inv_m = ROPE_THETA ** (-jnp.arange(half_m, dtype=F32) / half_m)
    lane = jnp.arange(LANES)
    jm = lane - 64
    row = jnp.where(lane < 2 * half_d, inv_d[lane % half_d],
                    jnp.where((jm >= 0) & (jm < 2 * half_m), inv_m[jm % half_m], 0.0))
    inv = jnp.zeros((8, LANES), F32).at[0].set(row)
    tm = 1024
    return pl.pallas_call(
        _rope_tab_kernel,
        grid=(t // tm,),
        in_specs=[pl.BlockSpec((tm, 1), lambda i: (i, 0)), pl.BlockSpec((8, LANES), lambda i: (0, 0))],
        out_specs=pl.BlockSpec((tm, 3 * LANES), lambda i: (i, 0)),
        out_shape=jax.ShapeDtypeStruct((t, 3 * LANES), F32),
        compiler_params=_cparams(("parallel",)),
        name="rope_tables",
    )(positions.reshape(t, 1), inv)


def _rope_patterns(tab_ref):
    lane = lax.broadcasted_iota(I32, (1, LANES), 1)
    in_d0, in_d1 = lane < 16, (lane >= 64) & (lane < 80)
    in_m = (lane >= 64) & (lane < 96)
    pat_d, pat_m = [], []
    for i, neutral in enumerate((1.0, 0.0, 0.0)):
        t = tab_ref[:, i * LANES:(i + 1) * LANES]
        pat_d.append(jnp.where(in_d0, t, jnp.where(in_d1, pltpu.roll(t, 64, 1), neutral)))
        pat_m.append(jnp.where(in_m, t, neutral))
    return pat_d, pat_m


def _inproj_kernel(*refs, has_add, tiles_per_seq):
    refs = list(refs)
    x_ref = refs.pop(0)
    yp_ref = refs.pop(0) if has_add else None
    g_ref, w_ref, tab_ref, fb_ref, qn_ref, kn_ref, wq_ref, wk_ref, wv_ref = refs[:9]
    refs = refs[9:]
    xo_ref = refs.pop(0) if has_add else None
    (zx_ref, gt_ref, dq_ref, dk_ref, dv_ref, ik_ref, iq_ref, mq_ref, mk_ref, mv_ref, fq_ref, fk_ref, fv_ref,
     carry_s) = refs
    x = x_ref[...]
    if has_add:
        x = x + yp_ref[...].astype(F32)
        xo_ref[...] = x
    h = _rms(x, g_ref[...]).astype(BF16)

    def proj(a, b):
        return jnp.dot(h, w_ref[:, a:b], preferred_element_type=F32)

    zx_ref[...] = proj(C_ZX, C_GT)
    gates = proj(C_GT, C_DQ)
    gt_ref[...] = gates
    (c_d, sa_d, sb_d), (c_m, sa_m, sb_m) = _rope_patterns(tab_ref)
    q = proj(C_DQ, C_KV)
    for i in range(2):
        dq_ref[:, i * 128:(i + 1) * 128] = _rope_apply(q[:, i * 128:(i + 1) * 128], c_d, sa_d, sb_d, 8).astype(BF16)
    kv = proj(C_KV, C_IK)
    is_k = lax.broadcasted_iota(I32, (1, LANES), 1) < 64
    kvr = _rope_apply(kv, jnp.where(is_k, c_d, 1.0), jnp.where(is_k, sa_d, 0.0), jnp.where(is_k, sb_d, 0.0), 8)
    dk_ref[...] = kvr[:, 0:64].astype(BF16)
    dv_ref[0, 0:64, :] = kvr.astype(BF16).astype(F32).T[64:128, :].astype(BF16)
    dv_ref[0, 64:VT_ROWS, :] = jnp.ones((VT_ROWS - 64, kvr.shape[0]), BF16)
    ikb = proj(C_IK, C_IQ)
    ik_ref[...] = _rope_apply(ikb, c_d, sa_d, sb_d, 8)[:, 0:64].astype(BF16)
    iq = proj(C_IQ, C_MC)
    for i in range(4):
        iq_ref[:, i * 128:(i + 1) * 128] = _rope_apply(iq[:, i * 128:(i + 1) * 128], c_d, sa_d, sb_d, 8).astype(BF16)
    ones_hi = jnp.where(lax.broadcasted_iota(I32, (1, FA_HEADS * HEAD_W), 1) % HEAD_W >= 64, 1.0, 0.0)

    mc = proj(C_MC, C_FQ)
    cq = _rms(mc[:, 0:256], qn_ref[...]).astype(BF16)
    ckv = _rms(mc[:, 256:384], kn_ref[...]).astype(BF16)
    kr = _rope_apply(mc[:, 384:512], c_m, sa_m, sb_m, 16)
    mq = jnp.dot(cq, wq_ref[...], preferred_element_type=F32)
    mk = jnp.dot(ckv, wk_ref[...], preferred_element_type=F32)
    for hd in range(MLA_HEADS):
        sl = slice(hd * HEAD_W, (hd + 1) * HEAD_W)
        mq_ref[:, sl] = _rope_apply(mq[:, sl], c_m, sa_m, sb_m, 16).astype(BF16)
        mk_ref[:, sl] = (mk[:, sl] + kr).astype(BF16)
    mv_ref[...] = (jnp.dot(ckv, wv_ref[...], preferred_element_type=F32) + ones_hi).astype(BF16)

    @pl.when(pl.program_id(0) % tiles_per_seq == 0)
    def _():
        carry_s[...] = jnp.zeros_like(carry_s)

    tm = x.shape[0]
    log_f = -_softplus(-(gates + fb_ref[...]))
    ri = lax.broadcasted_iota(I32, (LANES, LANES), 0)
    ci = lax.broadcasted_iota(I32, (LANES, LANES), 1)
    run = carry_s[0:1, :]
    parts = []
    for r in range(tm // LANES):
        blk = _dot_f32_by_mask(ri >= ci, log_f[r * LANES:(r + 1) * LANES, :], True) + run
        run = blk[LANES - 1:LANES, :]
        parts.append(blk)
    cs = jnp.concatenate(parts, axis=0)
    carry_s[0:1, :] = run
    f2 = cs * LOG2E
    lane = lax.broadcasted_iota(I32, (1, HEAD_W), 1)

    def pieces(f, base):
        hi, mid, lo = _split3(f)
        return jnp.where(lane == base, hi.astype(F32),
                         jnp.where(lane == base + 1, mid.astype(F32), jnp.where(lane == base + 2, lo.astype(F32), 0.0)))

    def ones(base):
        return jnp.where((lane >= base) & (lane < base + 3), 1.0, 0.0)

    fq = proj(C_FQ, C_FK)
    fk = proj(C_FK, C_FV)
    for hd in range(FOX_HEADS):
        fh = f2[:, G_FOX + hd:G_FOX + hd + 1]
        sl = slice(hd * HEAD_W, (hd + 1) * HEAD_W)
        fq_ref[:, sl] = (fq[:, sl] + pieces(fh, FOX_F_Q) + ones(FOX_ONE_Q)).astype(BF16)
        fk_ref[:, sl] = (fk[:, sl] - pieces(fh, FOX_F_K) + ones(FOX_ONE_K)).astype(BF16)
    fv_ref[...] = (proj(C_FV, C_END) + ones_hi).astype(BF16)


def _pack_w_in(w):
    d = w.shape[0]
    z = lambda n: jnp.zeros((d, n), F32)
    o = 0
    parts = {}
    for name, width in (("z", 256), ("xbc", 768), ("dt", 4), ("dq", 256), ("dk", 64), ("dv", 64), ("iq", 512),
                        ("ik", 64), ("iw", 8), ("cq", 256), ("ckv", 128), ("kr", 32), ("fq", 256), ("fk", 256),
                        ("fv", 256), ("ff", 4)):
        parts[name] = w[:, o:o + width]
        o += width
    scale = DSA_HEAD_DIM ** -0.5

    def per_head(a):
        return jnp.pad(a.reshape(d, FA_HEADS, 64), ((0, 0), (0, 0), (0, HEAD_W - 64))).reshape(d, FA_HEADS * HEAD_W)

    cat = jnp.concatenate([
        parts["z"], parts["xbc"],
        parts["ff"], parts["dt"], parts["iw"], z(112),
        parts["dq"] * (scale * LOG2E),
        parts["dk"], parts["dv"],
        parts["ik"], z(64),
        parts["iq"],
        parts["cq"], parts["ckv"], z(64), parts["kr"], z(32),
        per_head(parts["fq"] * (FOX_HEAD_DIM ** -0.5 * LOG2E)), per_head(parts["fk"]), per_head(parts["fv"])], axis=1)
    return cat.astype(BF16)


def _mla_weights(w_uq, w_ukv):
    dqk = MLA_NOPE + MLA_ROPE
    wq = jnp.pad(w_uq.reshape(MLA_Q_RANK, MLA_HEADS, dqk) * (dqk ** -0.5 * LOG2E), ((0, 0), (0, 0), (0, HEAD_W - dqk)))
    wkv = w_ukv.reshape(MLA_KV_RANK, MLA_HEADS, MLA_NOPE + MLA_V)
    wk = jnp.pad(wkv[:, :, :MLA_NOPE], ((0, 0), (0, 0), (0, HEAD_W - MLA_NOPE)))
    wv = jnp.pad(wkv[:, :, MLA_NOPE:], ((0, 0), (0, 0), (0, HEAD_W - MLA_V)))
    flat = lambda a: a.reshape(a.shape[0], MLA_HEADS * HEAD_W).astype(BF16)
    return flat(wq), flat(wk), flat(wv)


def _inproj(x, y_prev, g, w_cat, tab, f_bias, q_norm, kv_norm, w_uq, w_ukv, s):
    t = x.shape[0]
    has_add = y_prev is not None
    tok = lambda n: pl.BlockSpec((TM, n), lambda i: (i, 0))
    full = lambda a: pl.BlockSpec(a.shape, lambda i: (0,) * a.ndim)
    fb = jnp.zeros((1, LANES), F32).at[0, G_FOX:G_FOX + FOX_HEADS].set(f_bias)
    params = [g, w_cat, tab, fb, q_norm.reshape(1, -1), kv_norm.reshape(1, -1), *_mla_weights(w_uq, w_ukv)]
    ins = [x] + ([y_prev] if has_add else []) + params
    in_specs = ([tok(D_MODEL)] + ([tok(D_MODEL)] if has_add else [])
                + [tok(3 * LANES) if a is tab else full(a) for a in params])
    outs = ([("x", D_MODEL, F32)] if has_add else []) + [
        ("zx", 1024, F32), ("gt", 128, F32), ("dq", 256, BF16), ("dk", 64, BF16), ("dv", 64, BF16),
        ("ik", 64, BF16), ("iq", 512, BF16), ("mq", 512, BF16), ("mk", 512, BF16), ("mv", 512, BF16),
        ("fq", 512, BF16), ("fk", 512, BF16), ("fv", 512, BF16)]
    vt_rows = {"dv": VT_ROWS}
    res = pl.pallas_call(
        functools.partial(_inproj_kernel, has_add=has_add, tiles_per_seq=s // TM),
        grid=(t // TM,),
        in_specs=in_specs,
        out_specs=[pl.BlockSpec((1, vt_rows[nm], TM), lambda i: (i, 0, 0)) if nm in vt_rows else tok(n)
                   for nm, n, _ in outs],
        out_shape=[jax.ShapeDtypeStruct((t // TM, vt_rows[nm], TM) if nm in vt_rows else (t, n), dt)
                   for nm, n, dt in outs],
        scratch_shapes=[pltpu.VMEM((8, LANES), F32)],
        compiler_params=_cparams(("arbitrary",)),
        name="inproj",
    )(*ins)
    res = list(res)
    x_new = res.pop(0) if has_add else x
    return x_new, dict(zip([n for n, _, _ in outs if n != "x"], res))


def _ssd_kernel(zx_ref, gt_ref, cw_ref, cb_ref, pr_ref, pc_ref, drow_ref, nw_ref, o_ref, ext_s, st_s, y_s, *, q):
    @pl.when(pl.program_id(1) == 0)
    def _():
        ext_s[0:8, :] = jnp.zeros((8, SSD_CONV_DIM), F32)
        st_s[...] = jnp.zeros_like(st_s)

    raw = zx_ref[:, 256:1024]
    ext_s[8:8 + q, :] = raw
    acc = jnp.broadcast_to(cb_ref[...], (q, SSD_CONV_DIM))
    for j in range(SSD_CONV):
        acc = acc + cw_ref[j:j + 1, :] * ext_s[5 + j:5 + j + q, :]
    ext_s[0:8, :] = raw[q - 8:q, :]
    xbc = _silu(acc)
    xs = xbc[:, 0:SSD_D_INNER]

    g = gt_ref[...]
    lane = lax.broadcasted_iota(I32, (1, LANES), 1)
    dtc = _softplus(g + pr_ref[0:1, :])
    a_r = jnp.where((lane >= G_DT) & (lane < G_DT + SSD_HEADS), -jnp.exp(pr_ref[1:2, :]), 0.0)
    ri = lax.broadcasted_iota(I32, (q, q), 0)
    ci = lax.broadcasted_iota(I32, (q, q), 1)
    tri = ri >= ci
    acs_c = _dot_f32_by_mask(tri, dtc * a_r, True)
    sub = lax.broadcasted_iota(I32, (LANES, 1), 0)
    dtr = _softplus(g.T + pc_ref[:, 0:1])
    a_c = jnp.where((sub >= G_DT) & (sub < G_DT + SSD_HEADS), -jnp.exp(pc_ref[:, 1:2]), 0.0)
    acs_r = _dot_f32_by_mask(ri <= ci, dtr * a_c, False)

    rep = SSD_HEADS // SSD_GROUPS
    for gi in range(SSD_GROUPS):
        bg = xbc[:, SSD_D_INNER + gi * SSD_STATE:SSD_D_INNER + (gi + 1) * SSD_STATE]
        cg = xbc[:, SSD_D_INNER + (SSD_GROUPS + gi) * SSD_STATE:SSD_D_INNER + (SSD_GROUPS + gi + 1) * SSD_STATE]
        bt = bg.T.astype(BF16)
        cb16 = cg.astype(BF16)
        cbm = jnp.dot(cb16, bt, preferred_element_type=F32)
        for hh in range(rep):
            h = gi * rep + hh
            ac = acs_c[:, G_DT + h:G_DT + h + 1]
            ar = acs_r[G_DT + h:G_DT + h + 1, :]
            seg = jnp.where(tri, jnp.exp(ac - ar), 0.0)
            xh = xs[:, h * SSD_HEAD_DIM:(h + 1) * SSD_HEAD_DIM]
            xdt = xh * dtc[:, G_DT + h:G_DT + h + 1]
            yd = jnp.dot((cbm * seg).astype(BF16), xdt.astype(BF16), preferred_element_type=F32)
            aend = ac[q - 1:q, :]
            st = st_s[h]
            yo = jnp.dot(cb16, st.astype(BF16), preferred_element_type=F32) * jnp.exp(ac)
            st_s[h] = st * jnp.exp(aend) + jnp.dot(bt, (xdt * jnp.exp(aend - ac)).astype(BF16),
                                                   preferred_element_type=F32)
            y_s[:, h * SSD_HEAD_DIM:(h + 1) * SSD_HEAD_DIM] = yd + yo
    y = y_s[...] + drow_ref[...] * xs
    y = y * _silu(zx_ref[:, 0:SSD_D_INNER])
    o_ref[...] = _rms(y, nw_ref[...]).astype(BF16)


def _ssd(zx, gt, conv_w, conv_b, dt_bias, a_log, d_skip, norm_w, b, s):
    q = SSD_Q
    nc = s // q
    pr = jnp.zeros((8, LANES), F32).at[0, G_DT:G_DT + SSD_HEADS].set(dt_bias).at[1, G_DT:G_DT + SSD_HEADS].set(a_log)
    pc = pr.T
    drow = jnp.repeat(d_skip, SSD_HEAD_DIM).reshape(1, SSD_D_INNER)
    cb = conv_b.reshape(1, -1)
    nw = norm_w.reshape(1, -1)
    full = lambda a: pl.BlockSpec(a.shape, lambda i, j: (0,) * a.ndim)
    return pl.pallas_call(
        functools.partial(_ssd_kernel, q=q),
        grid=(b, nc),
        in_specs=[pl.BlockSpec((q, 1024), lambda i, j: (i * nc + j, 0)), pl.BlockSpec((q, LANES), lambda i, j: (i * nc + j, 0)),
                  full(conv_w), full(cb), full(pr), full(pc), full(drow), full(nw)],
        out_specs=pl.BlockSpec((q, SSD_D_INNER), lambda i, j: (i * nc + j, 0)),
        out_shape=jax.ShapeDtypeStruct((b * s, SSD_D_INNER), BF16),
        scratch_shapes=[pltpu.VMEM((8 + q, SSD_CONV_DIM), F32), pltpu.VMEM((SSD_HEADS, SSD_STATE, SSD_HEAD_DIM), F32),
                        pltpu.VMEM((q, SSD_D_INNER), F32)],
        compiler_params=_cparams(("parallel", "arbitrary")),
        name="ssd",
    )(zx, gt, conv_w, cb, pr, pc, drow, nw)


def _key_to_f32(k):
    return pltpu.bitcast(jnp.where(k < 0, k ^ jnp.int32(0x7FFFFFFF), k), F32)


def _dsa_kernel(iq_ref, gt_ref, q_ref, ik_ref, k_ref, vt_ref, o_ref, sc_s, qi_s, qs_s, thr_s, s16_s,
                *, bq, sc_w, n_sel):
    qb = pl.program_id(1)
    n_sc = ((qb + 1) * bq + sc_w - 1) // sc_w
    for j in range(IDX_HEADS):
        qi_s[j // 2, (j % 2) * bq:(j % 2 + 1) * bq, :] = iq_ref[:, j * IDX_DIM:(j + 1) * IDX_DIM]
    for h in range(DSA_HEADS):
        qs_s[h * bq:(h + 1) * bq, :] = q_ref[:, h * DSA_HEAD_DIM:(h + 1) * DSA_HEAD_DIM]
    w_t = (gt_ref[...] * ((IDX_HEADS * IDX_DIM) ** -0.5)).T
    qpos = qb * bq + lax.broadcasted_iota(I32, (1, bq), 1)

    def score_body(sc, carry):
        k0 = pl.multiple_of(sc * sc_w, sc_w)
        kidx = ik_ref[pl.ds(k0, sc_w), :]
        sco = jnp.zeros((sc_w, bq), F32)
        for jp in range(IDX_HEADS // 2):
            a = _dot_nt(kidx, qi_s[jp])
            for u in range(2):
                j = 2 * jp + u
                sco = sco + jnp.maximum(a[:, u * bq:(u + 1) * bq], 0.0) * w_t[G_IW + j:G_IW + j + 1, :]
        kpos = k0 + lax.broadcasted_iota(I32, (sc_w, 1), 0)
        sco = jnp.where(kpos <= qpos, sco, -jnp.inf)
        sc_s[sc] = sco
        s16_s[sc] = sco.astype(BF16)
        return carry

    lax.fori_loop(0, n_sc, score_body, 0)

    def count(pred):
        def body(sc, acc):
            for r in range(sc_w // 64):
                acc = acc + jnp.where(pred(sc_s[sc, r * 64:(r + 1) * 64, :]), 1.0, 0.0)
            return acc
        acc = lax.fori_loop(0, n_sc, body, jnp.zeros((64, bq), F32))
        return jnp.sum(acc, axis=0, keepdims=True)

    thr_s[0:1, :] = jnp.full((1, bq), -jnp.finfo(jnp.float32).max, F32)

    @pl.when((qb + 1) * bq > n_sel)
    def _():
        active = qpos + 1 > n_sel

        one16, zero16 = jnp.ones((), BF16), jnp.zeros((), BF16)

        def count16(c16):
            def body(sc, acc):
                a16 = jnp.zeros((64, bq), BF16)
                for r in range(sc_w // 64):
                    a16 = a16 + jnp.where(s16_s[sc, r * 64:(r + 1) * 64, :] >= c16, one16, zero16)
                return acc + a16.astype(F32)
            acc = lax.fori_loop(0, n_sc, body, jnp.zeros((64, bq), F32))
            return jnp.sum(acc, axis=0, keepdims=True)

        def coarse(it, lo):
            cand = lo ^ jnp.left_shift(jnp.int32(1), 31 - it)
            return jnp.where(count16(_key_to_f32(cand).astype(BF16)) >= n_sel, cand, lo)

        lo16 = lax.fori_loop(0, 16, coarse, jnp.full((1, bq), INT_MIN, I32))
        base = jnp.maximum(lo16, jnp.int32(INT_MIN + (1 << 16))) - jnp.int32(1 << 16)

        def cond(st):
            it, off, cnt_lo = st
            pending = active & (cnt_lo != n_sel)
            return (it < 32) & (jnp.max(jnp.where(pending, 1.0, 0.0)) > 0.0)

        def fine(it, off, cnt_lo):
            cand = off | jnp.left_shift(jnp.int32(1), 31 - it)
            thr_c = _key_to_f32(base + cand)
            cnt = count(lambda t: t >= thr_c)
            ok = cnt >= n_sel
            return jnp.where(ok, cand, off), jnp.where(ok, cnt, cnt_lo)

        def body(st):
            it, off, cnt_lo = st
            for u in range(4):
                off, cnt_lo = fine(it + u, off, cnt_lo)
            return it + 4, off, cnt_lo

        off, cnt_lo = fine(15, jnp.zeros((1, bq), I32), jnp.full((1, bq), 3.0e38, F32))
        _, off, cnt_lo = lax.while_loop(cond, body, (jnp.int32(16), off, cnt_lo))
        lo = base + off
        thr = _key_to_f32(lo)
        tie = active & (cnt_lo > n_sel)

        @pl.when(jnp.max(jnp.where(tie, 1.0, 0.0)) > 0.0)
        def _():
            need = n_sel - count(lambda t: t > thr)
            ri = lax.broadcasted_iota(I32, (sc_w, sc_w), 0)
            ci = lax.broadcasted_iota(I32, (sc_w, sc_w), 1)
            lower = jnp.where(ri >= ci, 1.0, 0.0).astype(BF16)

            def tie_body(sc, run):
                t = sc_s[sc]
                eq = (t == thr) & tie
                pre = jnp.dot(lower, jnp.where(eq, 1.0, 0.0).astype(BF16), preferred_element_type=F32)
                sc_s[sc] = jnp.where(eq & (run + pre > need), -jnp.inf, t)
                return run + pre[sc_w - 1:sc_w, :]

            lax.fori_loop(0, n_sc, tie_body, jnp.zeros((1, bq), F32))

        thr_s[0:1, :] = jnp.where(active, thr, thr_s[0:1, :])

    thr = thr_s[0:1, :]

    hq = DSA_HEADS * bq

    def att_body(sc, carry):
        m_prev, acc = carry
        k0 = pl.multiple_of(sc * sc_w, sc_w)
        st = _dot_nt(k_ref[pl.ds(k0, sc_w), :], qs_s[...])
        drop = jnp.where(sc_s[sc] >= thr, 0.0, 2 * NEG_BIG)
        st = st + jnp.tile(drop, (1, DSA_HEADS))
        m_cur = jnp.max(jnp.max(st.reshape(sc_w // 64, 64, hq), axis=0), axis=0, keepdims=True)
        m_next = jnp.maximum(m_prev, m_cur)
        p = jnp.exp2(st - m_next)
        pv = jnp.dot(vt_ref[sc], p.astype(BF16), preferred_element_type=F32)
        return m_next, jnp.exp2(m_prev - m_next) * acc + pv

    def att_body2(i, carry):
        return att_body(2 * i + 1, att_body(2 * i, carry))

    carry = lax.fori_loop(0, n_sc // 2, att_body2, (jnp.full((1, hq), NEG_BIG, F32), jnp.zeros((VT_ROWS, hq), F32)))
    _, acc = lax.cond(n_sc % 2 == 1, lambda c: att_body(n_sc - 1, c), lambda c: c, carry)
    out_t = jnp.concatenate([acc[0:DSA_HEAD_DIM, :] / acc[DSA_HEAD_DIM:DSA_HEAD_DIM + 1, :],
                             jnp.zeros((LANES - DSA_HEAD_DIM, hq), F32)], axis=0).T
    for h in range(DSA_HEADS):
        o_ref[:, h * DSA_HEAD_DIM:(h + 1) * DSA_HEAD_DIM] = out_t[h * bq:(h + 1) * bq, 0:DSA_HEAD_DIM].astype(BF16)


def _dsa(iq, gt, dq, ik, dk, dvt, b, s, n_sel):
    bq, sc_w = DSA_BQ, DSA_SC
    nq = s // bq
    qblk = lambda n: pl.BlockSpec((bq, n), lambda i, j: (i * nq + j, 0))
    kblk = pl.BlockSpec((s, 64), lambda i, j: (i, 0))
    return pl.pallas_call(
        functools.partial(_dsa_kernel, bq=bq, sc_w=sc_w, n_sel=n_sel),
        grid=(b, nq),
        in_specs=[qblk(512), qblk(LANES), qblk(256), kblk, kblk,
                  pl.BlockSpec((s // sc_w, VT_ROWS, sc_w), lambda i, j: (i, 0, 0))],
        out_specs=qblk(256),
        out_shape=jax.ShapeDtypeStruct((b * s, 256), BF16),
        scratch_shapes=[pltpu.VMEM((s // sc_w, sc_w, bq), F32), pltpu.VMEM((IDX_HEADS // 2, 2 * bq, IDX_DIM), BF16),
                        pltpu.VMEM((DSA_HEADS * bq, DSA_HEAD_DIM), BF16), pltpu.VMEM((8, bq), F32),
                        pltpu.VMEM((s // sc_w, sc_w, bq), BF16)],
        compiler_params=_cparams(("parallel", "arbitrary")),
        name="dsa",
    )(iq, gt, dq, ik, dk, dvt)


def _flash_kernel(qt_ref, kt_ref, *refs, blk, sets):
    qkv, (o_ref, m_s, acc_s) = refs[:3 * sets], refs[3 * sets:]
    t = pl.program_id(1)
    qi, ki = qt_ref[t], kt_ref[t]

    @pl.when(ki == 0)
    def _():
        m_s[...] = jnp.full(m_s.shape, NEG_BIG, F32)
        acc_s[...] = jnp.zeros_like(acc_s)

    def step(diagonal):
        if diagonal:
            keep = lax.broadcasted_iota(I32, (blk, blk), 0) >= lax.broadcasted_iota(I32, (blk, blk), 1)
        for g in range(sets):
            q_ref, k_ref, v_ref = qkv[3 * g:3 * g + 3]
            for h in range(FA_HEADS):
                sl = slice(h * HEAD_W, (h + 1) * HEAD_W)
                hh = g * FA_HEADS + h
                s = _dot_nt(q_ref[:, sl], k_ref[:, sl])
                if diagonal:
                    s = jnp.where(keep, s, 2 * NEG_BIG)
                m_prev = m_s[hh]
                m_next = jnp.maximum(m_prev, jnp.max(s, axis=1, keepdims=True))
                p = jnp.exp2(s - jnp.tile(m_next, (1, blk // LANES)))
                pv = jnp.dot(p.astype(BF16), v_ref[:, sl], preferred_element_type=F32)
                acc_s[hh] = jnp.exp2(m_prev - m_next) * acc_s[hh] + pv
                m_s[hh] = m_next

    @pl.when(ki < qi)
    def _():
        step(False)

    @pl.when(ki == qi)
    def _():
        step(True)
        for hh in range(sets * FA_HEADS):
            acc = acc_s[hh]
            o_ref[:, hh * 64:(hh + 1) * 64] = (acc / pltpu.roll(acc, 64, 1))[:, 0:64].astype(BF16)


def _flash(qkv, b, s):
    blk = FA_B
    nb = s // blk
    pairs = [(i, j) for i in range(nb) for j in range(i + 1)]
    qt = jnp.asarray([p[0] for p in pairs], I32)
    kt = jnp.asarray([p[1] for p in pairs], I32)
    w = FA_HEADS * HEAD_W
    sets = len(qkv)
    q_spec = pl.BlockSpec((blk, w), lambda i, t, qt, kt: (i * nb + qt[t], 0))
    kv_spec = pl.BlockSpec((blk, w), lambda i, t, qt, kt: (i * nb + kt[t], 0))
    grid_spec = pltpu.PrefetchScalarGridSpec(
        num_scalar_prefetch=2,
        grid=(b, len(pairs)),
        in_specs=[q_spec, kv_spec, kv_spec] * sets,
        out_specs=pl.BlockSpec((blk, sets * FA_HEADS * 64), lambda i, t, qt, kt: (i * nb + qt[t], 0)),
        scratch_shapes=[pltpu.VMEM((sets * FA_HEADS, blk, LANES), F32), pltpu.VMEM((sets * FA_HEADS, blk, HEAD_W), F32)],
    )
    return pl.pallas_call(
        functools.partial(_flash_kernel, blk=blk, sets=sets),
        grid_spec=grid_spec,
        out_shape=jax.ShapeDtypeStruct((b * s, sets * FA_HEADS * 64), BF16),
        compiler_params=_cparams(("parallel", "arbitrary")),
        name="flash",
    )(qt, kt, *[a for triple in qkv for a in triple])


def _outproj_kernel(x_ref, *refs, n_mix):
    y_refs, (wo_ref, g_ref, wr_ref, br_ref, xo_ref, h_ref, meta_ref, col_ref, row_ref) = refs[:n_mix], refs[n_mix:]
    acc = x_ref[...]
    row = 0
    for y_ref in y_refs:
        acc = acc + jnp.dot(y_ref[...], wo_ref[row:row + y_ref.shape[1], :], preferred_element_type=F32)
        row += y_ref.shape[1]
    xo_ref[...] = acc
    h2 = _rms(acc, g_ref[...])
    h_ref[...] = h2.astype(BF16)
    h_hi, h_lo, _ = _split3(h2)
    lg = (jnp.dot(h_hi, wr_ref[0], preferred_element_type=F32) + jnp.dot(h_lo, wr_ref[0], preferred_element_type=F32)
          + jnp.dot(h_hi, wr_ref[1], preferred_element_type=F32) + br_ref[...])
    lane = lax.broadcasted_iota(I32, (1, LANES), 1)
    lanef = lane.astype(F32)
    is_g = (lane >= MOE_EXPERTS) & (lane < MOE_EXPERTS + MOE_GROUPS)
    gl = jnp.where(is_g, lg, -jnp.inf)
    gmax = jnp.max(gl, axis=1, keepdims=True)
    gidx = jnp.min(jnp.where(gl == gmax, lanef, 999.0), axis=1, keepdims=True) - MOE_EXPERTS
    g_p = 1.0 / jnp.sum(jnp.where(is_g, jnp.exp(gl - gmax), 0.0), axis=1, keepdims=True)
    in_g = (lane < MOE_EXPERTS) & (jnp.floor(lanef * (1.0 / MOE_EPG)) == gidx)
    el = jnp.where(in_g, lg, -jnp.inf)
    m1 = jnp.max(el, axis=1, keepdims=True)
    i1 = jnp.min(jnp.where(el == m1, lanef, 999.0), axis=1, keepdims=True)
    el2 = jnp.where(lanef == i1, -jnp.inf, el)
    m2 = jnp.max(el2, axis=1, keepdims=True)
    i2 = jnp.min(jnp.where(el2 == m2, lanef, 999.0), axis=1, keepdims=True)
    t = jnp.exp(m2 - m1)
    w1 = 1.0 / (1.0 + t)
    gate_w = jnp.where(lanef == i1, w1 * g_p, jnp.where(lanef == i2, t * w1 * g_p, 0.0))
    chosen = jnp.where((lanef == i1) | (lanef == i2), 1.0, 0.0).astype(BF16)
    meta_ref[0], col_ref[...], row_ref[0] = _moe_plan(chosen, gate_w)


def _outproj(x, ys, w_out, g, w_rg, b_rg, w_re, b_re):
    t = x.shape[0]
    wo = w_out.astype(BF16)
    wr = jnp.zeros((D_MODEL, LANES), F32).at[:, 0:MOE_EXPERTS].set(w_re).at[:, MOE_EXPERTS:MOE_EXPERTS + MOE_GROUPS].set(w_rg)
    wr_hi = wr.astype(BF16)
    wr = jnp.stack([wr_hi, (wr - wr_hi.astype(F32)).astype(BF16)])
    br =jnp.zeros((1, LANES), F32).at[0, 0:MOE_EXPERTS].set(b_re).at[0, MOE_EXPERTS:MOE_EXPERTS + MOE_GROUPS].set(b_rg)
    tok = lambda n: pl.BlockSpec((TM, n), lambda i: (i, 0))
    full = lambda a: pl.BlockSpec(a.shape, lambda i: (0,) * a.ndim)
    return pl.pallas_call(
        functools.partial(_outproj_kernel, n_mix=len(ys)),
        grid=(t // TM,),
        in_specs=[tok(D_MODEL)] + [tok(y.shape[1]) for y in ys] + [full(wo), full(g), full(wr), full(br)],
        out_specs=[tok(D_MODEL), tok(D_MODEL), pl.BlockSpec((1, 8, LANES), lambda i: (i, 0, 0)), tok(LANES),
                   pl.BlockSpec((1, 8, TM), lambda i: (i, 0, 0))],
        out_shape=[jax.ShapeDtypeStruct((t, D_MODEL), F32), jax.ShapeDtypeStruct((t, D_MODEL), BF16),
                   jax.ShapeDtypeStruct((t // TM, 8, LANES), I32), jax.ShapeDtypeStruct((t, LANES), F32),
                   jax.ShapeDtypeStruct((t // TM, 8, TM), F32)],
        compiler_params=_cparams(("parallel",)),
        name="outproj_router",
    )(x, *ys, wo, g, wr, br)


def _moe_plan(sel, cw):
    n = sel.shape[0]
    self32 = sel.astype(F32)
    ti = lax.broadcasted_iota(I32, (n, n), 0)
    tj = lax.broadcasted_iota(I32, (n, n), 1)
    rank = jnp.dot(jnp.where(ti > tj, 1.0, 0.0).astype(BF16), sel, preferred_element_type=F32)
    cnt = jnp.sum(self32, axis=0, keepdims=True).astype(I32)
    cpad = ((cnt + (MOE_ALIGN - 1)) // MOE_ALIGN) * MOE_ALIGN
    li = lax.broadcasted_iota(I32, (LANES, LANES), 0)
    lj = lax.broadcasted_iota(I32, (LANES, LANES), 1)
    off = _dot_f32_by_mask(li < lj, jnp.broadcast_to(cpad.astype(F32), (8, LANES)), False)[0:1, :]
    dest = off + rank
    lane = lax.broadcasted_iota(I32, (1, LANES), 1)
    lanef = lane.astype(F32)
    on = self32 > 0.5
    e_lo = jnp.min(jnp.where(on, lanef, 999.0), axis=1, keepdims=True)
    e_hi = jnp.max(jnp.where(on, lanef, -1.0), axis=1, keepdims=True)
    is_lo, is_hi = lanef == e_lo, lanef == e_hi
    pick = lambda m, a: jnp.sum(jnp.where(m, a, 0.0), axis=1, keepdims=True)
    d_lo, d_hi, w_lo, w_hi = pick(is_lo, dest), pick(is_hi, dest), pick(is_lo, cw), pick(is_hi, cw)
    colv = jnp.where(lane == 0, d_lo, jnp.where(lane == 1, d_hi, jnp.where(lane == 2, w_lo, jnp.where(lane == 3, w_hi, 0.0))))
    sub = lax.broadcasted_iota(I32, (8, LANES), 0)
    meta = jnp.where(sub == 0, off.astype(I32), jnp.where(sub == 1, cnt, 0))
    return meta, colv, colv.T[0:8, :]


def _moe_kernel(meta_ref, h_ref, col_ref, row_ref, wg_ref, wu_ref, wd_ref, o_ref, s_s, y_s, *, n, ns):
    t, e = pl.program_id(0), pl.program_id(1)

    @pl.when(e == 0)
    def _():
        for g in range(MOE_SUB):
            d_lo, d_hi = row_ref[g, 0:1, :], row_ref[g, 1:2, :]
            hg = h_ref[g * n:(g + 1) * n, :]
            for r in range(ns // LANES):
                si = (r * LANES + lax.broadcasted_iota(I32, (LANES, 1), 0)).astype(F32)
                perm = jnp.where((si == d_lo) | (si == d_hi), 1.0, 0.0).astype(BF16)
                s_s[g, r * LANES:(r + 1) * LANES, :] = jnp.dot(perm, hg, preferred_element_type=F32).astype(BF16)
        y_s[...] = jnp.zeros_like(y_s)

    def mlp(j, starts, store_ok):
        starts = [pl.multiple_of(r0, MOE_ALIGN) for r0 in starts]
        xs = jnp.concatenate([s_s[g, pl.ds(starts[g], MOE_CH), :] for g in range(MOE_SUB)], axis=0)
        gate = jnp.dot(xs, wg_ref[j], preferred_element_type=F32)
        up = jnp.dot(xs, wu_ref[j], preferred_element_type=F32)
        hid = (_silu(gate) * up).astype(BF16)
        y = jnp.dot(hid, wd_ref[j], preferred_element_type=F32).astype(BF16)
        for g in range(MOE_SUB):
            def store(g=g):
                y_s[g, pl.ds(starts[g], MOE_CH), :] = y[g * MOE_CH:(g + 1) * MOE_CH, :]
            if store_ok is None:
                store()
            else:
                pl.when(store_ok[g])(store)

    def meta(g, j, what):
        return meta_ref[(t * MOE_SUB + g) * 2 * MOE_EXPERTS + what * MOE_EXPERTS + e * MOE_EPS + j]

    offs = [[meta(g, j, 0) for g in range(MOE_SUB)] for j in range(MOE_EPS)]
    cnts = [[meta(g, j, 1) for g in range(MOE_SUB)] for j in range(MOE_EPS)]
    for j in range(MOE_EPS):
        mlp(j, offs[j], None)
    for j in range(MOE_EPS):
        ends = [offs[j][g] + ((cnts[j][g] + MOE_ALIGN - 1) // MOE_ALIGN) * MOE_ALIGN for g in range(MOE_SUB)]
        nch = [(cnts[j][g] + MOE_CH - 1) // MOE_CH for g in range(MOE_SUB)]

        def body(i, carry, j=j, ends=ends, nch=nch):
            more = [i < nch[g] for g in range(MOE_SUB)]
            starts = [jnp.where(more[g], jnp.minimum(offs[j][g] + i * MOE_CH, ends[g] - MOE_CH), offs[j][g])
                      for g in range(MOE_SUB)]
            mlp(j, starts, more)
            return carry

        lax.fori_loop(1, functools.reduce(jnp.maximum, nch), body, 0)

    @pl.when(e == pl.num_programs(1) - 1)
    def _():
        rows = 256
        kc = ns // next(k for k in (3, 2, 1) if ns % (k * LANES) == 0)
        for g in range(MOE_SUB):
            for c in range(n // rows):
                r0 = g * n + c * rows
                cv = col_ref[r0:r0 + rows, :]
                d_lo, d_hi, w_lo, w_hi = cv[:, 0:1], cv[:, 1:2], cv[:, 2:3], cv[:, 3:4]
                acc = jnp.zeros((rows, D_MODEL), F32)
                for r in range(ns // kc):
                    si = (r * kc + lax.broadcasted_iota(I32, (1, kc), 1)).astype(F32)
                    pw = (jnp.where(si == d_lo, w_lo, 0.0) + jnp.where(si == d_hi, w_hi, 0.0)).astype(BF16)
                    acc = acc + jnp.dot(pw, y_s[g, r * kc:(r + 1) * kc, :], preferred_element_type=F32)
                o_ref[r0:r0 + rows, :] = acc.astype(BF16)


def _moe(h2, meta, col, row, wg, wu, wd, layer):
    t = h2.shape[0]
    n, ns = MOE_N, MOE_NS
    nt = t // n
    meta_flat = meta[:, 0:2, 0:MOE_EXPERTS].reshape(-1)
    grid_spec = pltpu.PrefetchScalarGridSpec(
        num_scalar_prefetch=1,
        grid=(nt // MOE_SUB, MOE_EXPERTS // MOE_EPS),
        in_specs=[pl.BlockSpec((MOE_SUB * n, D_MODEL), lambda i, e, m: (i, 0)),
                  pl.BlockSpec((MOE_SUB * n, LANES), lambda i, e, m: (i, 0)),
                  pl.BlockSpec((MOE_SUB, 8, n), lambda i, e, m: (i, 0, 0)),
                  pl.BlockSpec((None, MOE_EPS, D_MODEL, MOE_FF), lambda i, e, m: (layer, e, 0, 0)),
                  pl.BlockSpec((None, MOE_EPS, D_MODEL, MOE_FF), lambda i, e, m: (layer, e, 0, 0)),
                  pl.BlockSpec((None, MOE_EPS, MOE_FF, D_MODEL), lambda i, e, m: (layer, e, 0, 0))],
        out_specs=pl.BlockSpec((MOE_SUB * n, D_MODEL), lambda i, e, m: (i, 0)),
        scratch_shapes=[pltpu.VMEM((MOE_SUB, ns, D_MODEL), BF16), pltpu.VMEM((MOE_SUB, ns, D_MODEL), BF16)],
    )
    return pl.pallas_call(
        functools.partial(_moe_kernel, n=n, ns=ns),
        grid_spec=grid_spec,
        out_shape=jax.ShapeDtypeStruct((t, D_MODEL), BF16),
        compiler_params=_cparams(("parallel", "arbitrary")),
        name="moe_experts",
    )(meta_flat, h2, col, row, wg, wu, wd)


def _final_kernel(x_ref, y_ref, g_ref, o_ref):
    o_ref[...] = _rms(x_ref[...] + y_ref[...].astype(F32), g_ref[...])


def _final(x, y, g):
    t = x.shape[0]
    tok = pl.BlockSpec((TM, D_MODEL), lambda i: (i, 0))
    return pl.pallas_call(
        _final_kernel,
        grid=(t // TM,),
        in_specs=[tok, tok, pl.BlockSpec((1, D_MODEL), lambda i: (0, 0))],
        out_specs=tok,
        out_shape=jax.ShapeDtypeStruct((t, D_MODEL), F32),
        compiler_params=_cparams(("parallel",)),
        name="final_norm",
    )(x, y, g)


def kernel(x, positions, norm_mix, w_in, ssd_conv_w, ssd_conv_b, ssd_dt_bias, ssd_a_log, ssd_d, ssd_norm, mla_q_norm, mla_w_uq, mla_kv_norm, mla_w_ukv, fox_f_bias, w_out, norm_ffn, router_group_w, router_group_b, router_expert_w, router_expert_b, expert_w_gate, expert_w_up, expert_w_down, final_norm):
    b, s, d = x.shape
    assert d == D_MODEL and s % FA_B == 0 and (b * s) % (MOE_N * MOE_SUB) == 0 and TM == DSA_SC and TM == MOE_N
    depth = w_in.shape[0]
    n_sel = min(IDX_TOPK_MAX, s // 4)
    tab = _rope_tables(positions)
    xf = x.reshape(b * s, d)
    wg, wu, wd = expert_w_gate.astype(BF16), expert_w_up.astype(BF16), expert_w_down.astype(BF16)
    y_ffn = None
    for l in range(depth):
        xf, p = _inproj(xf, y_ffn, norm_mix[l].reshape(1, d), _pack_w_in(w_in[l]), tab, fox_f_bias[l],
                        mla_q_norm[l], mla_kv_norm[l], mla_w_uq[l], mla_w_ukv[l], s)
        y_ssd = _ssd(p["zx"], p["gt"], ssd_conv_w[l], ssd_conv_b[l], ssd_dt_bias[l], ssd_a_log[l], ssd_d[l],
                     ssd_norm[l], b, s)
        y_dsa = _dsa(p["iq"], p["gt"], p["dq"], p["ik"], p["dk"], p["dv"], b, s, n_sel)
        y_att = _flash([(p["mq"], p["mk"], p["mv"]), (p["fq"], p["fk"], p["fv"])], b, s)
        xf, h2, meta, col, row = _outproj(xf, (y_ssd, y_dsa, y_att), w_out[l], norm_ffn[l].reshape(1, d),
                                          router_group_w[l], router_group_b[l], router_expert_w[l], router_expert_b[l])
        y_ffn = _moe(h2, meta, col, row, wg, wu, wd, l)
    return _final(xf, y_ffn, final_norm.reshape(1, d)).reshape(b, s, d)
```
